```python
import math
import jax, jax.numpy as jnp
from jax import lax
import numpy as np


D_MODEL = 1024
BATCH = 8
SEQ = 2048
DEPTH = 2
DEC_BATCH = 128
DEC_SEQ = 4
PAST_LEN = 16384
PAGE_SIZE = 128

D_MIX = D_MODEL
D_SSD = D_MIX // 2
SSD_HEAD_DIM = 64
SSD_HEADS = D_SSD // SSD_HEAD_DIM
SSD_GROUPS = 2
D_STATE = 128
CONV_WIDTH = 4
SSD_CHUNK = 128
CONV_DIM = D_SSD + 2 * SSD_GROUPS * D_STATE
D_GMLP = D_MIX - D_SSD
GMLP_HEADS = 8
GMLP_HEAD_DIM = D_GMLP // GMLP_HEADS
GMLP_CHUNK = 128
IN_DIM = D_SSD + CONV_DIM + SSD_HEADS + 2 * D_GMLP
N_EXPERT_GROUPS = 4
EXPERTS_PER_GROUP = 4
N_EXPERTS = N_EXPERT_GROUPS * EXPERTS_PER_GROUP
TOP_K_GROUP = 1
TOP_K_EXPERT = 2
D_EXPERT = 512
D_PLE = 256
EPS = 1e-6

kernel_name = 'hymba_ssd_gmlp_hiermoe_step'


def rms_norm(x, g):
    xf = x.astype(jnp.float32)
    y = xf * lax.rsqrt(jnp.mean(xf * xf, axis=-1, keepdims=True) + EPS)
    return (y * g.astype(jnp.float32)).astype(x.dtype)


def gated_group_rms_norm(y, z, g):
    yf = y.astype(jnp.float32) * jax.nn.silu(z.astype(jnp.float32))
    shp = yf.shape
    yg = yf.reshape(shp[:-1] + (SSD_GROUPS, shp[-1] // SSD_GROUPS))
    yg = yg * lax.rsqrt(jnp.mean(yg * yg, axis=-1, keepdims=True) + EPS)
    return (yg.reshape(shp) * g.astype(jnp.float32)).astype(y.dtype)


def segsum(a):
    t = a.shape[-1]
    xe = jnp.broadcast_to(a[..., :, None], a.shape + (t,))
    strict = jnp.tril(jnp.ones((t, t), dtype=bool), -1)
    xs = jnp.cumsum(jnp.where(strict, xe, 0.0), axis=-2)
    incl = jnp.tril(jnp.ones((t, t), dtype=bool), 0)
    return jnp.where(incl, xs, -jnp.inf)


def causal_depthwise_conv(x_ext, w, b):
    c = x_ext.shape[-1]
    y = lax.conv_general_dilated(x_ext, w[:, None, :].astype(x_ext.dtype), window_strides=(1,),
                                 padding='VALID', dimension_numbers=('NWC', 'WIO', 'NWC'),
                                 feature_group_count=c)
    return y + b.astype(y.dtype)


def ssd_scan(x, dt, a, b_in, c_in, init_state):
    f32 = jnp.float32
    bt, seq_len, n_heads, head_dim = x.shape
    n_groups, d_state = b_in.shape[-2:]
    rep = n_heads // n_groups
    q = min(SSD_CHUNK, seq_len)
    n_chunks = -(-seq_len // q)
    pad = n_chunks * q - seq_len
    xf = x.astype(f32) * dt[..., None]
    da = dt * a
    bf = b_in.astype(f32)
    cf = c_in.astype(f32)
    if pad:
        padt = lambda t: jnp.pad(t, [(0, 0), (0, pad)] + [(0, 0)] * (t.ndim - 2))
        xf, da, bf, cf = padt(xf), padt(da), padt(bf), padt(cf)
    xf = xf.reshape(bt, n_chunks, q, n_groups, rep, head_dim)
    bf = bf.reshape(bt, n_chunks, q, n_groups, d_state)
    cf = cf.reshape(bt, n_chunks, q, n_groups, d_state)
    da = jnp.moveaxis(da.reshape(bt, n_chunks, q, n_groups, rep), 2, -1)
    cum = jnp.cumsum(da, axis=-1)
    decay = jnp.exp(segsum(da))
    cb = jnp.einsum('bclgn,bcsgn->bcgls', cf, bf)
    wts = cb[:, :, :, None] * decay
    y_diag = jnp.einsum('bcgrls,bcsgrp->bclgrp', wts, xf)
    decay_states = jnp.exp(cum[..., -1:] - cum)
    chunk_states = jnp.einsum('bclgn,bcgrl,bclgrp->bcgrpn', bf, decay_states, xf)
    init = init_state.astype(f32).reshape(bt, n_groups, rep, head_dim, d_state)
    states = jnp.concatenate([init[:, None], chunk_states], axis=1)
    tot = jnp.pad(jnp.moveaxis(cum[..., -1], 1, -1), [(0, 0), (0, 0), (0, 0), (1, 0)])
    dchunk = jnp.exp(segsum(tot))
    new_states = jnp.einsum('bgrzc,bcgrpn->bzgrpn', dchunk, states)
    y_off = jnp.einsum('bclgn,bcgrpn,bcgrl->bclgrp', cf, new_states[:, :-1], jnp.exp(cum))
    y = (y_diag + y_off).reshape(bt, n_chunks * q, n_heads, head_dim)[:, :seq_len]
    final = new_states[:, -1].reshape(bt, n_heads, head_dim, d_state)
    return y, final


def chunk_spatial_gating(u, v, w_s, b_s):
    bt, seq_len, n_h, d_h = v.shape
    n_chunks = -(-seq_len // GMLP_CHUNK)
    pad = n_chunks * GMLP_CHUNK - seq_len
    if pad:
        v = jnp.pad(v, [(0, 0), (0, pad), (0, 0), (0, 0)])
    vc = v.reshape(bt, n_chunks, GMLP_CHUNK, n_h, d_h)
    causal = jnp.tril(jnp.ones((GMLP_CHUNK, GMLP_CHUNK), dtype=bool))
    ws = jnp.where(causal, w_s, 0.0).astype(v.dtype)
    s = jnp.einsum('hts,bcshd->bcthd', ws, vc) + b_s.T.astype(v.dtype)[:, :, None]
    s = s.reshape(bt, n_chunks * GMLP_CHUNK, n_h, d_h)[:, :seq_len]
    return u * s


def hierarchical_moe(m, w_rg, b_rg, w_re, b_re, w_gate, w_up, w_down):
    shp = m.shape
    t = m.reshape(-1, shp[-1])
    n = t.shape[0]
    rows = jnp.arange(n)
    f32 = jnp.float32
    lg = (t @ w_rg).astype(f32) + b_rg.astype(f32)
    pg = jax.nn.softmax(lg, axis=-1)
    _, g_sel = lax.top_k(lg, TOP_K_GROUP)
    g_sel = g_sel[:, 0]
    p_sel = pg[rows, g_sel][:, None]
    le = ((t @ w_re).astype(f32) + b_re.astype(f32)).reshape(n, N_EXPERT_GROUPS, EXPERTS_PER_GROUP)
    le_g = le[rows, g_sel]
    top_v, top_i = lax.top_k(le_g, TOP_K_EXPERT)
    gates = jax.nn.softmax(top_v, axis=-1) * p_sel
    e_idx = g_sel[:, None] * EXPERTS_PER_GROUP + top_i
    comb = jnp.sum(jax.nn.one_hot(e_idx, N_EXPERTS, dtype=f32) * gates[..., None], axis=1)
    comb = comb.astype(t.dtype)
    y = jnp.zeros_like(t)
    for e in range(N_EXPERTS):
        hid = jax.nn.silu(t @ w_gate[e]) * (t @ w_up[e])
        y = y + comb[:, e:e + 1] * (hid @ w_down[e])
    return y.reshape(shp)


def decoder_layer(h, p_i, conv_state, ssm_state, norm_mix, w_in, conv_w, conv_b, dt_bias, a_log,
                  d_skip, ssd_norm, gmlp_norm, w_spatial, b_spatial, w_out, norm_ffn,
                  w_router_group, b_router_group, w_router_expert, b_router_expert,
                  w_gate, w_up, w_down, w_ple, norm_ple, norm_pg, w_pg):
    bt, seq_len, _ = h.shape
    a = rms_norm(h, norm_mix)
    proj = a @ w_in
    z, xbc, dt_raw, uv = jnp.split(proj, [D_SSD, D_SSD + CONV_DIM, D_SSD + CONV_DIM + SSD_HEADS], axis=-1)
    xbc_ext = jnp.concatenate([conv_state.astype(xbc.dtype), xbc], axis=1)
    new_conv = xbc_ext[:, -(CONV_WIDTH - 1):]
    xbc_c = jax.nn.silu(causal_depthwise_conv(xbc_ext, conv_w, conv_b))
    xs, bs, cs = jnp.split(xbc_c, [D_SSD, D_SSD + SSD_GROUPS * D_STATE], axis=-1)
    dt = jax.nn.softplus(dt_raw.astype(jnp.float32) + dt_bias.astype(jnp.float32))
    a_neg = -jnp.exp(a_log.astype(jnp.float32))
    xs_h = xs.reshape(bt, seq_len, SSD_HEADS, SSD_HEAD_DIM)
    y_ssd, new_ssm = ssd_scan(xs_h, dt, a_neg,
                              bs.reshape(bt, seq_len, SSD_GROUPS, D_STATE),
                              cs.reshape(bt, seq_len, SSD_GROUPS, D_STATE), ssm_state)
    y_ssd = y_ssd + d_skip.astype(jnp.float32)[:, None] * xs_h.astype(jnp.float32)
    y_ssd = gated_group_rms_norm(y_ssd.reshape(bt, seq_len, D_SSD).astype(h.dtype), z, ssd_norm)
    uv = jax.nn.gelu(uv, approximate=False)
    u, v = jnp.split(uv, 2, axis=-1)
    v = rms_norm(v, gmlp_norm)
    y_gm = chunk_spatial_gating(u.reshape(bt, seq_len, GMLP_HEADS, GMLP_HEAD_DIM),
                                v.reshape(bt, seq_len, GMLP_HEADS, GMLP_HEAD_DIM),
                                w_spatial, b_spatial).reshape(bt, seq_len, D_GMLP)
    h = h + jnp.concatenate([y_ssd, y_gm], axis=-1) @ w_out
    h = h + hierarchical_moe(rms_norm(h, norm_ffn), w_router_group, b_router_group,
                             w_router_expert, b_router_expert, w_gate, w_up, w_down)
    e = rms_norm(p_i.astype(h.dtype) @ w_ple, norm_ple)
    gate = jax.nn.sigmoid(rms_norm(h, norm_pg) @ w_pg)
    h = h + gate * e
    return h, new_conv, new_ssm.astype(ssm_state.dtype), v


def trunk(x, p, conv_states, ssm_states, norm_mix, w_in, conv_w, conv_b, dt_bias, a_log, d_skip,
          ssd_norm, gmlp_norm, w_spatial, b_spatial, w_out, norm_ffn, w_router_group,
          b_router_group, w_router_expert, b_router_expert, w_gate, w_up, w_down, w_ple,
          norm_ple, norm_pg, w_pg, norm_final):
    h = x
    convs, ssms, vs = [], [], []
    for i in range(DEPTH):
        h, c_i, s_i, v_i = decoder_layer(
            h, p[i], conv_states[i], ssm_states[i], norm_mix[i], w_in[i], conv_w[i], conv_b[i],
            dt_bias[i], a_log[i], d_skip[i], ssd_norm[i], gmlp_norm[i], w_spatial[i], b_spatial[i],
            w_out[i], norm_ffn[i], w_router_group[i], b_router_group[i], w_router_expert[i],
            b_router_expert[i], w_gate[i], w_up[i], w_down[i], w_ple[i], norm_ple[i], norm_pg[i],
            w_pg[i])
        convs.append(c_i)
        ssms.append(s_i)
        vs.append(v_i)
    y = rms_norm(h, norm_final)
    return y, jnp.stack(convs), jnp.stack(ssms), jnp.stack(vs)


def setup_inputs(seed: int = 0) -> dict:
    key = jax.random.key(seed)
    ks = jax.random.split(key, 40)
    f32 = jnp.float32
    nrm = lambda k, shape, scale: jax.random.normal(k, shape, f32) * scale
    gain = lambda k, shape: 1.0 + 0.05 * jax.random.normal(k, shape, f32)
    dt0 = jnp.exp(jax.random.uniform(ks[10], (DEPTH, SSD_HEADS), f32, math.log(1e-3), math.log(1e-1)))
    dt_bias = dt0 + jnp.log(-jnp.expm1(-dt0))
    a_log = jnp.log(jax.random.uniform(ks[11], (DEPTH, SSD_HEADS), f32, 1.0, 16.0))
    return {
        'x_prompt': nrm(ks[0], (BATCH, SEQ, D_MODEL), 1.0),
        'x_sample': nrm(ks[1], (DEC_BATCH, DEC_SEQ, D_MODEL), 1.0),
        'state_conv': nrm(ks[2], (DEPTH, DEC_BATCH, CONV_WIDTH - 1, CONV_DIM), 1.0),
        'state_ssm': nrm(ks[3], (DEPTH, DEC_BATCH, SSD_HEADS, SSD_HEAD_DIM, D_STATE), 0.1),
        'p_prompt': nrm(ks[4], (DEPTH, BATCH, SEQ, D_PLE), 1.0),
        'p_sample': nrm(ks[5], (DEPTH, DEC_BATCH, DEC_SEQ, D_PLE), 1.0),
        'norm_mix': gain(ks[6], (DEPTH, D_MODEL)),
        'w_in': nrm(ks[7], (DEPTH, D_MODEL, IN_DIM), D_MODEL ** -0.5),
        'conv_w': nrm(ks[8], (DEPTH, CONV_WIDTH, CONV_DIM), CONV_WIDTH ** -0.5),
        'conv_b': nrm(ks[9], (DEPTH, CONV_DIM), 0.02),
        'dt_bias': dt_bias,
        'a_log': a_log,
        'd_skip': gain(ks[12], (DEPTH, SSD_HEADS)),
        'ssd_norm': gain(ks[13], (DEPTH, D_SSD)),
        'gmlp_norm': gain(ks[14], (DEPTH, D_GMLP)),
        'w_spatial': nrm(ks[15], (DEPTH, GMLP_HEADS, GMLP_CHUNK, GMLP_CHUNK), GMLP_CHUNK ** -0.5),
        'b_spatial': 1.0 + nrm(ks[16], (DEPTH, GMLP_HEADS, GMLP_CHUNK), 0.02),
        'w_out': nrm(ks[17], (DEPTH, D_MIX, D_MODEL), D_MIX ** -0.5),
        'norm_ffn': gain(ks[18], (DEPTH, D_MODEL)),
        'w_router_group': nrm(ks[19], (DEPTH, D_MODEL, N_EXPERT_GROUPS), D_MODEL ** -0.5),
        'b_router_group': nrm(ks[20], (DEPTH, N_EXPERT_GROUPS), 0.01),
        'w_router_expert': nrm(ks[21], (DEPTH, D_MODEL, N_EXPERTS), D_MODEL ** -0.5),
        'b_router_expert': nrm(ks[22], (DEPTH, N_EXPERTS), 0.01),
        'w_gate': nrm(ks[23], (DEPTH, N_EXPERTS, D_MODEL, D_EXPERT), D_MODEL ** -0.5),
        'w_up': nrm(ks[24], (DEPTH, N_EXPERTS, D_MODEL, D_EXPERT), D_MODEL ** -0.5),
        'w_down': nrm(ks[25], (DEPTH, N_EXPERTS, D_EXPERT, D_MODEL), D_EXPERT ** -0.5),
        'w_ple': nrm(ks[26], (DEPTH, D_PLE, D_MODEL), D_PLE ** -0.5),
        'norm_ple': gain(ks[27], (DEPTH, D_MODEL)),
        'norm_pg': gain(ks[28], (DEPTH, D_MODEL)),
        'w_pg': nrm(ks[29], (DEPTH, D_MODEL, D_MODEL), D_MODEL ** -0.5),
        'norm_final': gain(ks[30], (D_MODEL,)),
    }


def reference(x_prompt, x_sample, state_conv, state_ssm, p_prompt, p_sample, norm_mix, w_in,
              conv_w, conv_b, dt_bias, a_log, d_skip, ssd_norm, gmlp_norm, w_spatial, b_spatial,
              w_out, norm_ffn, w_router_group, b_router_group, w_router_expert, b_router_expert,
              w_gate, w_up, w_down, w_ple, norm_ple, norm_pg, w_pg, norm_final):
    weights = (norm_mix, w_in, conv_w, conv_b, dt_bias, a_log, d_skip, ssd_norm, gmlp_norm,
               w_spatial, b_spatial, w_out, norm_ffn, w_router_group, b_router_group,
               w_router_expert, b_router_expert, w_gate, w_up, w_down, w_ple, norm_ple,
               norm_pg, w_pg, norm_final)
    n_prompt = x_prompt.shape[0]
    zero_conv = jnp.zeros((DEPTH, n_prompt, CONV_WIDTH - 1, CONV_DIM), x_prompt.dtype)
    zero_ssm = jnp.zeros((DEPTH, n_prompt) + state_ssm.shape[2:], state_ssm.dtype)
    y_prompt, new_conv_prompt, new_ssm_prompt, _ = trunk(x_prompt, p_prompt, zero_conv, zero_ssm, *weights)
    y_sample, new_conv_sample, new_ssm_sample, new_gmlp_v_sample = trunk(
        x_sample, p_sample, state_conv, state_ssm, *weights)
    return (y_prompt, y_sample, new_conv_prompt, new_ssm_prompt, new_conv_sample, new_ssm_sample, new_gmlp_v_sample)
```

```python
import functools
import math

import jax
import jax.numpy as jnp
from jax import lax
from jax.experimental import pallas as pl
from jax.experimental.pallas import tpu as pltpu

F32 = jnp.float32
BF16 = jnp.bfloat16

LANES = 128
VMEM_LIMIT_BYTES = 56 * 1024 * 1024

CONV_WIDTH = 4
SSD_HEADS = 8
SSD_HEAD_DIM = 64
SSD_GROUPS = 2
D_STATE = 128
CHUNK = 128
GMLP_HEADS = 8
N_EXPERT_GROUPS = 4
EXPERTS_PER_GROUP = 4
N_PAIRS = 6
N_BUCKETS = N_EXPERT_GROUPS * N_PAIRS
EPS = 1e-6

GROUP = 8
DEC_SEQ_ROWS = 4
TOK_TILE = 512
MOE_TILE = 256
ROUTE_ROWS = 32
NEG_BIG = -1e30


def _dot(a, b):
    return jnp.dot(a, b, preferred_element_type=F32)


def _dot_nt(a, b):
    return lax.dot_general(a, b, (((1,), (1,)), ((), ())), preferred_element_type=F32)


def _rms(x, g):
    ms = jnp.mean(x * x, axis=-1, keepdims=True)
    return (x * lax.rsqrt(ms + EPS)) * g


def _gelu(x):
    return 0.5 * x * (1.0 + lax.erf(x * (1.0 / math.sqrt(2.0))))


def _softplus(x):
    return jnp.maximum(x, 0.0) + jnp.log1p(jnp.exp(-jnp.abs(x)))


def _params(n_grid):
    return pltpu.CompilerParams(dimension_semantics=("arbitrary",) * n_grid,
                                vmem_limit_bytes=VMEM_LIMIT_BYTES)


def _inproj_kernel(h_ref, nm_ref, wz_ref, wx_ref, wdt_ref, wu_ref, wv_ref, dtb_ref, gn_ref,
                   z_ref, xbc_ref, dt_ref, u_ref, v_ref):
    a = _rms(h_ref[...], nm_ref[...]).astype(BF16)
    z_ref[...] = _dot(a, wz_ref[...])
    xbc_ref[...] = _dot(a, wx_ref[...])
    dt_ref[...] = _softplus(_dot(a, wdt_ref[...]) + dtb_ref[...])
    u_ref[...] = _gelu(_dot(a, wu_ref[...]))
    v_ref[...] = _rms(_gelu(_dot(a, wv_ref[...])), gn_ref[...])


def _inproj(h, nm, wz, wx, wdt, wu, wv, dtb, gn):
    t, d = h.shape
    widths = (wz.shape[1], wx.shape[1], wdt.shape[1], wu.shape[1], wv.shape[1])
    row = lambda w: pl.BlockSpec((TOK_TILE, w), lambda i: (i, 0))
    full = lambda a: pl.BlockSpec(a.shape, lambda i: (0,) * a.ndim)
    return pl.pallas_call(
        _inproj_kernel,
        grid=(t // TOK_TILE,),
        in_specs=[row(d), full(nm), full(wz), full(wx), full(wdt), full(wu), full(wv), full(dtb), full(gn)],
        out_specs=[row(w) for w in widths],
        out_shape=[jax.ShapeDtypeStruct((t, w), F32) for w in widths],
        compiler_params=_params(1),
        name="inproj",
    )(h, nm, wz, wx, wdt, wu, wv, dtb, gn)


def _seg_cumsum(x, seg, rowmod):
    d = 1
    while d < seg:
        x = x + jnp.where(rowmod >= d, pltpu.roll(x, d, axis=0), 0.0)
        d *= 2
    return x


def _seg_rev_cumsum(x, seg, rowmod):
    n = x.shape[0]
    d = 1
    while d < seg:
        x = x + jnp.where(rowmod + d < seg, pltpu.roll(x, n - d, axis=0), 0.0)
        d *= 2
    return x


def _expand_heads(m, lane_lt_half):
    parts = []
    for j in range(SSD_HEADS // 2):
        parts.append(jnp.where(lane_lt_half, m[:, 2 * j:2 * j + 1], m[:, 2 * j + 1:2 * j + 2]))
    return jnp.concatenate(parts, axis=1)


def _merge_head_pairs(per_head, lane_lt_half):
    parts = [jnp.where(lane_lt_half, per_head[2 * j], per_head[2 * j + 1]) for j in range(len(per_head) // 2)]
    return jnp.concatenate(parts, axis=1)


def _mixer_kernel(*refs, sample):
    if sample:
        (z_ref, xbc_ref, dt_ref, u_ref, v_ref, cs_ref, s0_ref, cw_ref, cb_ref, arow_ref, dsk_ref, sn_ref,
         wsp_ref, bsp_ref, y_ref, sout_ref, ext_ref, yoff_ref) = refs
        seg = GROUP
        first = pl.program_id(0) == 0
    else:
        (z_ref, xbc_ref, dt_ref, u_ref, v_ref, cw_ref, cb_ref, arow_ref, dsk_ref, sn_ref,
         wsp_ref, bsp_ref, y_ref, sout_ref, ext_ref) = refs
        seg = CHUNK
        first = pl.program_id(1) == 0
    d_ssd = SSD_HEADS * SSD_HEAD_DIM
    gw = d_ssd // SSD_GROUPS
    hpg = SSD_HEADS // SSD_GROUPS
    cs_first = GROUP - DEC_SEQ_ROWS - (CONV_WIDTH - 1)

    rows = lax.broadcasted_iota(jnp.int32, (CHUNK, LANES), 0)
    cols = lax.broadcasted_iota(jnp.int32, (CHUNK, LANES), 1)
    rowmod = rows & (seg - 1)
    lane_lt_half = cols < SSD_HEAD_DIM

    xbc = xbc_ref[...]
    if sample:
        rm = lax.broadcasted_iota(jnp.int32, xbc.shape, 0) & (GROUP - 1)
        xbc = jnp.where((rm >= cs_first) & (rm < cs_first + CONV_WIDTH - 1), cs_ref[...], xbc)

    @pl.when(first)
    def _():
        ext_ref[0:8, :] = jnp.zeros((8, xbc.shape[1]), F32)
        if not sample:
            sout_ref[...] = jnp.zeros(sout_ref.shape, F32)

    ext_ref[8:8 + CHUNK, :] = xbc
    acc = cb_ref[...] + cw_ref[CONV_WIDTH - 1:CONV_WIDTH, :] * xbc
    for j in range(1, CONV_WIDTH):
        acc = acc + cw_ref[CONV_WIDTH - 1 - j:CONV_WIDTH - j, :] * ext_ref[8 - j:8 - j + CHUNK, :]
    if not sample:
        ext_ref[0:8, :] = ext_ref[CHUNK:CHUNK + 8, :]
    xc = acc * jax.nn.sigmoid(acc)
    x = xc[:, :d_ssd]
    bb = xc[:, d_ssd:d_ssd + SSD_GROUPS * D_STATE].astype(BF16)
    cm = xc[:, d_ssd + SSD_GROUPS * D_STATE:]
    cbf = cm.astype(BF16)

    dtc = dt_ref[...]
    if sample:
        dtc = jnp.where(rowmod >= GROUP - DEC_SEQ_ROWS, dtc, 0.0)
    da = dtc * arow_ref[...]
    cum = _seg_cumsum(da, seg, rowmod)
    rev = _seg_rev_cumsum(da, seg, rowmod) - da
    cum_t = cum.T
    ecum = jnp.exp(cum)
    dt_e = _expand_heads(dtc, lane_lt_half)
    ecum_e = _expand_heads(ecum, lane_lt_half)
    erev_e = _expand_heads(jnp.exp(rev), lane_lt_half)

    xdt = x * dt_e
    xdt_bf = xdt.astype(BF16)
    causal = rows >= cols
    if sample:
        causal = causal & ((rows >> 3) == (cols >> 3))

    yd = []
    for g in range(SSD_GROUPS):
        cb_g = _dot_nt(cbf[:, g * D_STATE:(g + 1) * D_STATE], bb[:, g * D_STATE:(g + 1) * D_STATE])
        for hh in range(hpg):
            h = g * hpg + hh
            expo = cum[:, h:h + 1] - cum_t[h:h + 1, :]
            w = (cb_g * jnp.exp(jnp.where(causal, expo, NEG_BIG))).astype(BF16)
            j = h // 2
            yd.append(_dot(w, xdt_bf[:, j * LANES:(j + 1) * LANES]))
    y_diag = _merge_head_pairs(yd, lane_lt_half)

    if sample:
        for i in range(CHUNK // GROUP):
            for g in range(SSD_GROUPS):
                s_g = s0_ref[i, g * gw:(g + 1) * gw, :].astype(BF16)
                yoff_ref[i * GROUP:(i + 1) * GROUP, g * gw:(g + 1) * gw] = _dot_nt(
                    cm[i * GROUP:(i + 1) * GROUP, g * D_STATE:(g + 1) * D_STATE].astype(BF16), s_g)
        y_off = yoff_ref[...]
    else:
        y_off = jnp.concatenate(
            [_dot_nt(cbf[:, g * D_STATE:(g + 1) * D_STATE], sout_ref[0, g * gw:(g + 1) * gw, :].astype(BF16))
             for g in range(SSD_GROUPS)], axis=1)
    y = y_diag + y_off * ecum_e + dsk_ref[...] * x

    xd = xdt * erev_e
    for g in range(SSD_GROUPS):
        xd_t = xd[:, g * gw:(g + 1) * gw].T
        b_g = bb[:, g * D_STATE:(g + 1) * D_STATE]
        if sample:
            tcols = lax.broadcasted_iota(jnp.int32, xd_t.shape, 1) >> 3
            for i in range(CHUNK // GROUP):
                upd = _dot(jnp.where(tcols == i, xd_t, 0.0).astype(BF16), b_g)
                last = i * GROUP + GROUP - 1
                for hh in range(hpg):
                    h = g * hpg + hh
                    r0 = h * SSD_HEAD_DIM
                    sout_ref[i, r0:r0 + SSD_HEAD_DIM, :] = (
                        s0_ref[i, r0:r0 + SSD_HEAD_DIM, :] * ecum[last:last + 1, h:h + 1]
                        + upd[hh * SSD_HEAD_DIM:(hh + 1) * SSD_HEAD_DIM, :])
        else:
            upd = _dot(xd_t.astype(BF16), b_g)
            for hh in range(hpg):
                h = g * hpg + hh
                r0 = h * SSD_HEAD_DIM
                sout_ref[0, r0:r0 + SSD_HEAD_DIM, :] = (
                    sout_ref[0, r0:r0 + SSD_HEAD_DIM, :] * ecum[CHUNK - 1:CHUNK, h:h + 1]
                    + upd[hh * SSD_HEAD_DIM:(hh + 1) * SSD_HEAD_DIM, :])

    zf = z_ref[...]
    yf = y * (zf * jax.nn.sigmoid(zf))
    parts = []
    for g in range(SSD_GROUPS):
        part = yf[:, g * gw:(g + 1) * gw]
        ms = jnp.mean(part * part, axis=-1, keepdims=True)
        parts.append(part * lax.rsqrt(ms + EPS))
    y_ssd = jnp.concatenate(parts, axis=1) * sn_ref[...]

    vb = v_ref[...].astype(BF16)
    sg = [_dot(wsp_ref[h], vb[:, (h // 2) * LANES:(h // 2 + 1) * LANES]) for h in range(GMLP_HEADS)]
    s = _merge_head_pairs(sg, lane_lt_half) + bsp_ref[...]
    y_gm = u_ref[...] * s

    y_ref[:, :d_ssd] = y_ssd.astype(BF16)
    y_ref[:, d_ssd:] = y_gm.astype(BF16)


def _mixer_prompt(z, xbc, dt, u, v, cw, cb, arow, dsk, sn, wsp, bsp, n_seq, n_chunk):
    tok = lambda w: pl.BlockSpec((CHUNK, w), lambda b, c: (b * n_chunk + c, 0))
    full = lambda a: pl.BlockSpec(a.shape, lambda b, c: (0,) * a.ndim)
    d_ssd = z.shape[1]
    d_mix = d_ssd + u.shape[1]
    return pl.pallas_call(
        functools.partial(_mixer_kernel, sample=False),
        grid=(n_seq, n_chunk),
        in_specs=[tok(z.shape[1]), tok(xbc.shape[1]), tok(dt.shape[1]), tok(u.shape[1]), tok(v.shape[1]),
                  full(cw), full(cb), full(arow), full(dsk), full(sn), full(wsp), full(bsp)],
        out_specs=[pl.BlockSpec((CHUNK, d_mix), lambda b, c: (b * n_chunk + c, 0)),
                   pl.BlockSpec((1, d_ssd, D_STATE), lambda b, c: (b, 0, 0))],
        out_shape=[jax.ShapeDtypeStruct((n_seq * n_chunk * CHUNK, d_mix), BF16),
                   jax.ShapeDtypeStruct((n_seq, d_ssd, D_STATE), F32)],
        scratch_shapes=[pltpu.VMEM((CHUNK + 8, xbc.shape[1]), F32)],
        compiler_params=_params(2),
        name="mixer_prompt",
    )(z, xbc, dt, u, v, cw, cb, arow, dsk, sn, wsp, bsp)


def _mixer_sample(z, xbc, dt, u, v, cs, s0, cw, cb, arow, dsk, sn, wsp, bsp, row0, n_rows):
    blk0 = row0 // CHUNK
    n_seq_blk = CHUNK // GROUP
    tok = lambda w: pl.BlockSpec((CHUNK, w), lambda i: (blk0 + i, 0))
    full = lambda a: pl.BlockSpec(a.shape, lambda i: (0,) * a.ndim)
    d_ssd = z.shape[1]
    d_mix = d_ssd + u.shape[1]
    st = pl.BlockSpec((n_seq_blk, d_ssd, D_STATE), lambda i: (i, 0, 0))
    return pl.pallas_call(
        functools.partial(_mixer_kernel, sample=True),
        grid=(n_rows // CHUNK,),
        in_specs=[tok(z.shape[1]), tok(xbc.shape[1]), tok(dt.shape[1]), tok(u.shape[1]), tok(v.shape[1]),
                  pl.BlockSpec((CHUNK, cs.shape[1]), lambda i: (i, 0)), st,
                  full(cw), full(cb), full(arow), full(dsk), full(sn), full(wsp), full(bsp)],
        out_specs=[pl.BlockSpec((CHUNK, d_mix), lambda i: (i, 0)), st],
        out_shape=[jax.ShapeDtypeStruct((n_rows, d_mix), BF16),
                   jax.ShapeDtypeStruct(s0.shape, F32)],
        scratch_shapes=[pltpu.VMEM((CHUNK + 8, xbc.shape[1]), F32), pltpu.VMEM((CHUNK, d_ssd), F32)],
        compiler_params=_params(1),
        name="mixer_sample",
    )(z, xbc, dt, u, v, cs, s0, cw, cb, arow, dsk, sn, wsp, bsp)


def _first_argmax(vals):
    m = vals[0]
    for v in vals[1:]:
        m = jnp.maximum(m, v)
    idx = jnp.full(m.shape, len(vals) - 1, jnp.int32)
    for k in range(len(vals) - 2, -1, -1):
        idx = jnp.where(vals[k] >= m, k, idx)
    return m, idx


def _post_kernel(yp_ref, ys_ref, h_ref, wo_ref, nf_ref, wr_ref, br_ref,
                 haug_ref, route_ref, cnt_ref, carry_ref, *, n_prompt_tiles):
    i = pl.program_id(0)
    d = h_ref.shape[1]
    tm = h_ref.shape[0]

    @pl.when(i == 0)
    def _():
        carry_ref[...] = jnp.zeros(carry_ref.shape, F32)

    ym = jnp.where(i < n_prompt_tiles, yp_ref[...], ys_ref[...])
    h1 = h_ref[...] + _dot(ym, wo_ref[...])
    haug_ref[:, :d] = h1

    t = _rms(h1, nf_ref[...])
    logits = lax.dot_general(wr_ref[...], t, (((1,), (1,)), ((), ())),
                             precision=lax.Precision.HIGHEST, preferred_element_type=F32) + br_ref[...]
    lg = [logits[k:k + 1, :] for k in range(N_EXPERT_GROUPS)]
    m, g = _first_argmax(lg)
    ssum = jnp.exp(lg[0] - m)
    for k in range(1, N_EXPERT_GROUPS):
        ssum = ssum + jnp.exp(lg[k] - m)
    p_sel = 1.0 / ssum
    le = [logits[N_EXPERT_GROUPS + e:N_EXPERT_GROUPS + e + 1, :] for e in range(N_EXPERT_GROUPS * EXPERTS_PER_GROUP)]
    a = []
    for k in range(EXPERTS_PER_GROUP):
        sel = le[(N_EXPERT_GROUPS - 1) * EXPERTS_PER_GROUP + k]
        for gi in range(N_EXPERT_GROUPS - 2, -1, -1):
            sel = jnp.where(g == gi, le[gi * EXPERTS_PER_GROUP + k], sel)
        a.append(sel)
    v1, i1 = _first_argmax(a)
    a2 = [jnp.where(i1 == k, -jnp.inf, a[k]) for k in range(EXPERTS_PER_GROUP)]
    v2, i2 = _first_argmax(a2)
    e2 = jnp.exp(v2 - v1)
    den = 1.0 + e2
    g1 = (1.0 / den) * p_sel
    g2 = (e2 / den) * p_sel
    lo = jnp.minimum(i1, i2)
    hi = jnp.maximum(i1, i2)
    c_lo = jnp.where(i1 < i2, g1, g2)
    c_hi = jnp.where(i1 < i2, g2, g1)
    pair = jnp.where(lo == 0, hi - 1, jnp.where(lo == 1, hi + 1, N_PAIRS - 1))
    bucket = g * N_PAIRS + pair

    brow = lax.broadcasted_iota(jnp.int32, (ROUTE_ROWS, tm), 0)
    onehot = jnp.where(brow == bucket, 1.0, 0.0)
    tr = lax.broadcasted_iota(jnp.int32, (tm, tm), 0)
    tc = lax.broadcasted_iota(jnp.int32, (tm, tm), 1)
    earlier = jnp.where(tr < tc, 1.0, 0.0).astype(BF16)
    prefix = _dot(onehot.astype(BF16), earlier)
    carry = carry_ref[:, 0:1]
    rank = jnp.sum(onehot * (prefix + carry), axis=0, keepdims=True)
    carry = carry + jnp.sum(onehot, axis=1, keepdims=True)
    carry_b = jnp.broadcast_to(carry, carry_ref.shape)
    carry_ref[...] = carry_b
    cnt_ref[...] = carry_b

    rr = lax.broadcasted_iota(jnp.int32, (8, tm), 0)
    route_ref[...] = jnp.where(rr == 0, bucket.astype(F32), jnp.where(rr == 1, rank, 0.0))
    ar = lax.broadcasted_iota(jnp.int32, (LANES, tm), 0)
    aux = jnp.where(ar == 0, c_lo, jnp.where(ar == 1, c_hi, 0.0))
    haug_ref[:, d:] = aux.T


def _post(yp, ys, h, wo, nf, wr, br):
    t, d = h.shape
    n_prompt_tiles = yp.shape[0] // TOK_TILE
    full = lambda a: pl.BlockSpec(a.shape, lambda i: (0,) * a.ndim)
    return pl.pallas_call(
        functools.partial(_post_kernel, n_prompt_tiles=n_prompt_tiles),
        grid=(t // TOK_TILE,),
        in_specs=[pl.BlockSpec((TOK_TILE, d), lambda i: (jnp.minimum(i, n_prompt_tiles - 1), 0)),
                  pl.BlockSpec((TOK_TILE, d), lambda i: (jnp.maximum(i - n_prompt_tiles, 0), 0)),
                  pl.BlockSpec((TOK_TILE, d), lambda i: (i, 0)),
                  full(wo), full(nf), full(wr), full(br)],
        out_specs=[pl.BlockSpec((TOK_TILE, d + LANES), lambda i: (i, 0)),
                   pl.BlockSpec((8, TOK_TILE), lambda i: (0, i)),
                   pl.BlockSpec((ROUTE_ROWS, LANES), lambda i: (0, 0))],
        out_shape=[jax.ShapeDtypeStruct((t, d + LANES), F32),
                   jax.ShapeDtypeStruct((8, t), F32),
                   jax.ShapeDtypeStruct((ROUTE_ROWS, LANES), F32)],
        scratch_shapes=[pltpu.VMEM((ROUTE_ROWS, LANES), F32)],
        compiler_params=_params(1),
        name="post",
    )(yp, ys, h, wo, nf, wr, br)


def _permute_kernel(*refs, scatter, chunk):
    if scatter:
        pos_ref, src_ref, _, dst_ref, sem = refs
    else:
        pos_ref, src_ref, dst_ref, sem = refs
    base = pl.program_id(0) * chunk

    def row_copy(r):
        p = pos_ref[0, 0, r]
        if scatter:
            return pltpu.make_async_copy(src_ref.at[pl.ds(base + r, 1)], dst_ref.at[pl.ds(p, 1)], sem)
        return pltpu.make_async_copy(src_ref.at[pl.ds(p, 1)], dst_ref.at[pl.ds(base + r, 1)], sem)

    def body(r, carry):
        row_copy(r).start()
        return carry

    lax.fori_loop(0, chunk, body, 0)
    pltpu.make_async_copy(src_ref.at[pl.ds(0, chunk)], dst_ref.at[pl.ds(0, chunk)], sem).wait()


def _permute_rows(pos, src, n_out, scatter):
    n = pos.shape[0]
    w = src.shape[1]
    chunk = TOK_TILE
    pos3 = pos.reshape(n // chunk, 1, chunk)
    in_specs = [pl.BlockSpec((1, 1, chunk), lambda i: (i, 0, 0), memory_space=pltpu.SMEM),
                pl.BlockSpec(memory_space=pl.ANY)]
    args = [pos3, src]
    aliases = {}
    if scatter:
        in_specs.append(pl.BlockSpec(memory_space=pl.ANY))
        args.append(jnp.zeros((n_out, w), src.dtype))
        aliases = {2: 0}
    return pl.pallas_call(
        functools.partial(_permute_kernel, scatter=scatter, chunk=chunk),
        grid=(n // chunk,),
        in_specs=in_specs,
        out_specs=pl.BlockSpec(memory_space=pl.ANY),
        out_shape=jax.ShapeDtypeStruct((n_out, w), src.dtype),
        scratch_shapes=[pltpu.SemaphoreType.DMA(())],
        input_output_aliases=aliases,
        compiler_params=_params(1),
        name="scatter_rows" if scatter else "gather_rows",
    )(*args)


def _moe_kernel(elo_ref, ehi_ref, nused_ref, x_ref, nf_ref, wg_lo, wu_lo, wd_lo, wg_hi, wu_hi, wd_hi, o_ref):
    i = pl.program_id(0)
    d = o_ref.shape[1]

    @pl.when(i < nused_ref[0])
    def _():
        x = x_ref[:, :d]
        c_lo = x_ref[:, d:d + 1]
        c_hi = x_ref[:, d + 1:d + 2]
        t = _rms(x, nf_ref[...]).astype(BF16)

        def expert(wg, wu, wd):
            gate = _dot(t, wg[0])
            hid = (gate * jax.nn.sigmoid(gate)) * _dot(t, wu[0])
            return _dot(hid.astype(BF16), wd[0])

        y = c_lo * expert(wg_lo, wu_lo, wd_lo)
        y = y + c_hi * expert(wg_hi, wu_hi, wd_hi)
        o_ref[...] = x + y

    @pl.when(i >= nused_ref[0])
    def _():
        o_ref[...] = jnp.zeros(o_ref.shape, F32)


def _moe(e_lo, e_hi, n_used, xs, nf, wg, wu, wd):
    p, wa = xs.shape
    d = wa - LANES
    n_tiles = p // MOE_TILE
    de = wg.shape[2]
    lo = lambda i, elo, ehi, nu: (elo[i], 0, 0)
    hi = lambda i, elo, ehi, nu: (ehi[i], 0, 0)
    grid_spec = pltpu.PrefetchScalarGridSpec(
        num_scalar_prefetch=3,
        grid=(n_tiles,),
        in_specs=[pl.BlockSpec((MOE_TILE, wa), lambda i, elo, ehi, nu: (jnp.minimum(i, nu[0] - 1), 0)),
                  pl.BlockSpec(nf.shape, lambda i, elo, ehi, nu: (0, 0)),
                  pl.BlockSpec((1, d, de), lo), pl.BlockSpec((1, d, de), lo), pl.BlockSpec((1, de, d), lo),
                  pl.BlockSpec((1, d, de), hi), pl.BlockSpec((1, d, de), hi), pl.BlockSpec((1, de, d), hi)],
        out_specs=pl.BlockSpec((MOE_TILE, d), lambda i, elo, ehi, nu: (i, 0)),
    )
    return pl.pallas_call(
        _moe_kernel,
        grid_spec=grid_spec,
        out_shape=jax.ShapeDtypeStruct((p, d), F32),
        compiler_params=_params(1),
        name="expert_pairs",
    )(e_lo, e_hi, n_used, xs, nf, wg, wu, wd, wg, wu, wd)


def _ple_kernel(h_ref, p_ref, wple_ref, nple_ref, npg_ref, wpg_ref, nfin_ref, o_ref, *, final):
    h2 = h_ref[...]
    e = _rms(_dot(p_ref[...].astype(BF16), wple_ref[...]), nple_ref[...])
    gate = jax.nn.sigmoid(_dot(_rms(h2, npg_ref[...]).astype(BF16), wpg_ref[...]))
    h3 = h2 + gate * e
    o_ref[...] = _rms(h3, nfin_ref[...]) if final else h3


def _ple(h2, p, wple, nple, npg, wpg, nfin, final):
    t, d = h2.shape
    row = lambda w: pl.BlockSpec((TOK_TILE, w), lambda i: (i, 0))
    full = lambda a: pl.BlockSpec(a.shape, lambda i: (0,) * a.ndim)
    return pl.pallas_call(
        functools.partial(_ple_kernel, final=final),
        grid=(t // TOK_TILE,),
        in_specs=[row(d), row(p.shape[1]), full(wple), full(nple), full(npg), full(wpg), full(nfin)],
        out_specs=row(d),
        out_shape=jax.ShapeDtypeStruct((t, d), F32),
        compiler_params=_params(1),
        name="ple",
    )(h2, p, wple, nple, npg, wpg, nfin)


_PAIR_LO = (0, 0, 0, 1, 1, 2)
_PAIR_HI = (1, 2, 3, 2, 3, 3)


def _routing_tables(route, cnt, n_tiles):
    counts = cnt[:N_BUCKETS, 0].astype(jnp.int32)
    padded = ((counts + MOE_TILE - 1) // MOE_TILE) * MOE_TILE
    ends = jnp.cumsum(padded)
    starts = ends - padded
    bucket = route[0].astype(jnp.int32)
    rank = route[1].astype(jnp.int32)
    pos = starts[bucket] + rank
    n_used = ends[-1] // MOE_TILE
    tile = jnp.minimum(jnp.arange(n_tiles, dtype=jnp.int32), n_used - 1)
    tb = jnp.sum((ends[None, :] <= (tile * MOE_TILE)[:, None]).astype(jnp.int32), axis=1)
    tb = jnp.minimum(tb, N_BUCKETS - 1)
    grp = tb // N_PAIRS
    pr = tb % N_PAIRS
    e_lo = grp * EXPERTS_PER_GROUP + jnp.asarray(_PAIR_LO, jnp.int32)[pr]
    e_hi = grp * EXPERTS_PER_GROUP + jnp.asarray(_PAIR_HI, jnp.int32)[pr]
    return pos, e_lo, e_hi, n_used.reshape(1)


def kernel(x_prompt, x_sample, state_conv, state_ssm, p_prompt, p_sample, norm_mix, w_in, conv_w, conv_b,
           dt_bias, a_log, d_skip, ssd_norm, gmlp_norm, w_spatial, b_spatial, w_out, norm_ffn,
           w_router_group, b_router_group, w_router_expert, b_router_expert, w_gate, w_up, w_down,
           w_ple, norm_ple, norm_pg, w_pg, norm_final):
    n_seq, seq_len, d_model = x_prompt.shape
    n_dec, dec_seq, _ = x_sample.shape
    depth = w_in.shape[0]
    conv_dim = conv_w.shape[2]
    d_ssd = SSD_HEADS * SSD_HEAD_DIM
    d_gmlp = gmlp_norm.shape[1]
    assert dec_seq == DEC_SEQ_ROWS and conv_w.shape[1] == CONV_WIDTH and seq_len % CHUNK == 0
    assert conv_dim == d_ssd + 2 * SSD_GROUPS * D_STATE and w_spatial.shape[1:] == (GMLP_HEADS, CHUNK, CHUNK)
    n_chunk = seq_len // CHUNK
    n_prompt = n_seq * seq_len
    n_srows = n_dec * GROUP
    t_all = n_prompt + n_srows
    assert n_prompt % TOK_TILE == 0 and n_srows % TOK_TILE == 0
    lead = GROUP - dec_seq
    n_moe_tiles = t_all // MOE_TILE + N_BUCKETS

    def sample_rows(a):
        return jnp.pad(a, ((0, 0), (lead, 0), (0, 0))).reshape(n_srows, a.shape[-1])

    h = jnp.concatenate([x_prompt.reshape(n_prompt, d_model), sample_rows(x_sample)], axis=0)

    o_xbc = d_ssd
    o_dt = o_xbc + conv_dim
    o_uv = o_dt + SSD_HEADS
    head_cols = jnp.arange(d_ssd) // SSD_HEAD_DIM
    grow = jnp.arange(CHUNK) % GROUP - lead
    tril = jnp.tril(jnp.ones((CHUNK, CHUNK), bool))

    convs_p, ssms_p, convs_s, ssms_s, vs_s = [], [], [], [], []
    y = None
    for i in range(depth):
        wi = w_in[i]
        wz = wi[:, :o_xbc].astype(BF16)
        wx = wi[:, o_xbc:o_dt].astype(BF16)
        wdt = jnp.pad(wi[:, o_dt:o_uv], ((0, 0), (0, LANES - SSD_HEADS))).astype(BF16)
        wu = wi[:, o_uv:o_uv + d_gmlp].astype(BF16)
        wv = wi[:, o_uv + d_gmlp:].astype(BF16)
        dtb = jnp.pad(dt_bias[i].astype(F32), (0, LANES - SSD_HEADS)).reshape(1, LANES)
        arow = jnp.pad(-jnp.exp(a_log[i].astype(F32)), (0, LANES - SSD_HEADS)).reshape(1, LANES)
        dsk = d_skip[i].astype(F32)[head_cols].reshape(1, d_ssd)
        ws_tril = jnp.where(tril, w_spatial[i], 0.0)
        wsp_p = ws_tril.astype(BF16)
        bsp_p = jnp.repeat(b_spatial[i].T, d_gmlp // GMLP_HEADS, axis=1)
        valid = (grow[:, None] >= 0) & (grow[None, :] >= 0) & ((jnp.arange(CHUNK)[:, None] // GROUP)
                                                                == (jnp.arange(CHUNK)[None, :] // GROUP))
        gi = jnp.maximum(grow, 0)
        wsp_s = jnp.where(valid[None], ws_tril[:, gi[:, None], gi[None, :]], 0.0).astype(BF16)
        bsp_s = jnp.where((grow >= 0)[:, None], bsp_p[gi], 0.0)
        wr = jnp.concatenate([w_router_group[i].T, w_router_expert[i].T,
                              jnp.zeros((ROUTE_ROWS - N_EXPERT_GROUPS * (1 + EXPERTS_PER_GROUP), d_model), F32)], axis=0)
        br = jnp.concatenate([b_router_group[i], b_router_expert[i],
                              jnp.zeros((ROUTE_ROWS - N_EXPERT_GROUPS * (1 + EXPERTS_PER_GROUP),), F32)]).reshape(ROUTE_ROWS, 1)
        row = lambda a: a.reshape(1, -1).astype(F32)

        z, xbc, dt, u, v = _inproj(h, row(norm_mix[i]), wz, wx, wdt, wu, wv, dtb, row(gmlp_norm[i]))

        yp, ssm_p = _mixer_prompt(z, xbc, dt, u, v, conv_w[i], row(conv_b[i]), arow, dsk, row(ssd_norm[i]),
                                  wsp_p, bsp_p, n_seq, n_chunk)
        cs = jnp.pad(state_conv[i], ((0, 0), (lead - (CONV_WIDTH - 1), dec_seq), (0, 0))).reshape(n_srows, conv_dim)
        s0 = state_ssm[i].reshape(n_dec, d_ssd, D_STATE)
        ys, ssm_s = _mixer_sample(z, xbc, dt, u, v, cs, s0, conv_w[i], row(conv_b[i]), arow, dsk, row(ssd_norm[i]),
                                  wsp_s, bsp_s, n_prompt, n_srows)

        haug, route, cnt = _post(yp, ys, h, w_out[i].astype(BF16), row(norm_ffn[i]), wr, br)
        pos, e_lo, e_hi, n_used = _routing_tables(route, cnt, n_moe_tiles)

        xs = _permute_rows(pos, haug, n_moe_tiles * MOE_TILE, scatter=True)
        h2s = _moe(e_lo, e_hi, n_used, xs, row(norm_ffn[i]), w_gate[i].astype(BF16), w_up[i].astype(BF16),
                   w_down[i].astype(BF16))
        h2 = _permute_rows(pos, h2s, t_all, scatter=False)

        p_all = jnp.concatenate([p_prompt[i].reshape(n_prompt, -1), sample_rows(p_sample[i])], axis=0)
        final = i == depth - 1
        out = _ple(h2, p_all, w_ple[i].astype(BF16), row(norm_ple[i]), row(norm_pg[i]), w_pg[i].astype(BF16),
                   row(norm_final), final)
        if final:
            y = out
        else:
            h = out

        xbc_p = xbc[:n_prompt].reshape(n_seq, seq_len, conv_dim)
        xbc_s = xbc[n_prompt:].reshape(n_dec, GROUP, conv_dim)
        convs_p.append(xbc_p[:, seq_len - (CONV_WIDTH - 1):])
        convs_s.append(xbc_s[:, GROUP - (CONV_WIDTH - 1):])
        ssms_p.append(ssm_p.reshape(n_seq, SSD_HEADS, SSD_HEAD_DIM, D_STATE))
        ssms_s.append(ssm_s.reshape(n_dec, SSD_HEADS, SSD_HEAD_DIM, D_STATE))
        vs_s.append(v[n_prompt:].reshape(n_dec, GROUP, d_gmlp)[:, lead:])

    y_prompt = y[:n_prompt].reshape(n_seq, seq_len, d_model)
    y_sample = y[n_prompt:].reshape(n_dec, GROUP, d_model)[:, lead:]
    return (y_prompt, y_sample, jnp.stack(convs_p), jnp.stack(ssms_p), jnp.stack(convs_s), jnp.stack(ssms_s),
            jnp.stack(vs_s))
```

```python
import functools
import math

import jax
import jax.numpy as jnp
from jax import lax
from jax.experimental import pallas as pl
from jax.experimental.pallas import tpu as pltpu

F32 = jnp.float32
BF16 = jnp.bfloat16

LANES = 128
VMEM_LIMIT_BYTES = 56 * 1024 * 1024

CONV_WIDTH = 4
SSD_HEADS = 8
SSD_HEAD_DIM = 64
SSD_GROUPS = 2
D_STATE = 128
CHUNK = 128
GMLP_HEADS = 8
N_EXPERT_GROUPS = 4
EXPERTS_PER_GROUP = 4
N_PAIRS = 6
N_BUCKETS = N_EXPERT_GROUPS * N_PAIRS
EPS = 1e-6

GROUP = 8
DEC_SEQ_ROWS = 4
TOK_TILE = 512
MOE_TILE = 256
ROUTE_ROWS = 32
X_PITCH = 9
Y_PITCH = 8
NEG_BIG = -1e30


def _dot(a, b):
    return jnp.dot(a, b, preferred_element_type=F32)


def _dot_nt(a, b):
    return lax.dot_general(a, b, (((1,), (1,)), ((), ())), preferred_element_type=F32)


def _rms(x, g):
    ms = jnp.mean(x * x, axis=-1, keepdims=True)
    return (x * lax.rsqrt(ms + EPS)) * g


def _gelu(x):
    return 0.5 * x * (1.0 + lax.erf(x * (1.0 / math.sqrt(2.0))))


def _softplus(x):
    return jnp.maximum(x, 0.0) + jnp.log1p(jnp.exp(-jnp.abs(x)))


def _params(n_grid):
    return pltpu.CompilerParams(dimension_semantics=("arbitrary",) * n_grid,
                                vmem_limit_bytes=VMEM_LIMIT_BYTES)


def _inproj_kernel(h_ref, nm_ref, wz_ref, wx_ref, wdt_ref, wu_ref, wv_ref, dtb_ref, gn_ref,
                   z_ref, xbc_ref, dt_ref, u_ref, v_ref):
    a = _rms(h_ref[...], nm_ref[...]).astype(BF16)
    z_ref[...] = _dot(a, wz_ref[...])
    xbc_ref[...] = _dot(a, wx_ref[...])
    dt_ref[...] = _softplus(_dot(a, wdt_ref[...]) + dtb_ref[...])
    u_ref[...] = _gelu(_dot(a, wu_ref[...]))
    v_ref[...] = _rms(_gelu(_dot(a, wv_ref[...])), gn_ref[...])


def _inproj(h, nm, wz, wx, wdt, wu, wv, dtb, gn):
    t, d = h.shape
    widths = (wz.shape[1], wx.shape[1], wdt.shape[1], wu.shape[1], wv.shape[1])
    row = lambda w: pl.BlockSpec((TOK_TILE, w), lambda i: (i, 0))
    full = lambda a: pl.BlockSpec(a.shape, lambda i: (0,) * a.ndim)
    return pl.pallas_call(
        _inproj_kernel,
        grid=(t // TOK_TILE,),
        in_specs=[row(d), full(nm), full(wz), full(wx), full(wdt), full(wu), full(wv), full(dtb), full(gn)],
        out_specs=[row(w) for w in widths],
        out_shape=[jax.ShapeDtypeStruct((t, w), F32) for w in widths],
        compiler_params=_params(1),
        name="inproj",
    )(h, nm, wz, wx, wdt, wu, wv, dtb, gn)


def _seg_cumsum(x, seg, rowmod):
    d = 1
    while d < seg:
        x = x + jnp.where(rowmod >= d, pltpu.roll(x, d, axis=0), 0.0)
        d *= 2
    return x


def _seg_rev_cumsum(x, seg, rowmod):
    n = x.shape[0]
    d = 1
    while d < seg:
        x = x + jnp.where(rowmod + d < seg, pltpu.roll(x, n - d, axis=0), 0.0)
        d *= 2
    return x


def _expand_heads(m, lane_lt_half):
    parts = []
    for j in range(SSD_HEADS // 2):
        parts.append(jnp.where(lane_lt_half, m[:, 2 * j:2 * j + 1], m[:, 2 * j + 1:2 * j + 2]))
    return jnp.concatenate(parts, axis=1)


def _merge_head_pairs(per_head, lane_lt_half):
    parts = [jnp.where(lane_lt_half, per_head[2 * j], per_head[2 * j + 1]) for j in range(len(per_head) // 2)]
    return jnp.concatenate(parts, axis=1)


def _mixer_kernel(*refs, sample):
    if sample:
        (z_ref, xbc_ref, dt_ref, u_ref, v_ref, cs_ref, s0_ref, cw_ref, cb_ref, arow_ref, dsk_ref, sn_ref,
         wsp_ref, bsp_ref, y_ref, sout_ref, ext_ref, yoff_ref) = refs
        seg = GROUP
        first = pl.program_id(0) == 0
    else:
        (z_ref, xbc_ref, dt_ref, u_ref, v_ref, cw_ref, cb_ref, arow_ref, dsk_ref, sn_ref,
         wsp_ref, bsp_ref, y_ref, sout_ref, ext_ref) = refs
        seg = CHUNK
        first = pl.program_id(1) == 0
    d_ssd = SSD_HEADS * SSD_HEAD_DIM
    gw = d_ssd // SSD_GROUPS
    hpg = SSD_HEADS // SSD_GROUPS
    cs_first = GROUP - DEC_SEQ_ROWS - (CONV_WIDTH - 1)

    rows = lax.broadcasted_iota(jnp.int32, (CHUNK, LANES), 0)
    cols = lax.broadcasted_iota(jnp.int32, (CHUNK, LANES), 1)
    rowmod = rows & (seg - 1)
    lane_lt_half = cols < SSD_HEAD_DIM

    xbc = xbc_ref[...]
    if sample:
        rm = lax.broadcasted_iota(jnp.int32, xbc.shape, 0) & (GROUP - 1)
        xbc = jnp.where((rm >= cs_first) & (rm < cs_first + CONV_WIDTH - 1), cs_ref[...], xbc)

    @pl.when(first)
    def _():
        ext_ref[0:8, :] = jnp.zeros((8, xbc.shape[1]), F32)
        if not sample:
            sout_ref[...] = jnp.zeros(sout_ref.shape, F32)

    ext_ref[8:8 + CHUNK, :] = xbc
    acc = cb_ref[...] + cw_ref[CONV_WIDTH - 1:CONV_WIDTH, :] * xbc
    for j in range(1, CONV_WIDTH):
        acc = acc + cw_ref[CONV_WIDTH - 1 - j:CONV_WIDTH - j, :] * ext_ref[8 - j:8 - j + CHUNK, :]
    if not sample:
        ext_ref[0:8, :] = ext_ref[CHUNK:CHUNK + 8, :]
    xc = acc * jax.nn.sigmoid(acc)
    x = xc[:, :d_ssd]
    bb = xc[:, d_ssd:d_ssd + SSD_GROUPS * D_STATE].astype(BF16)
    cm = xc[:, d_ssd + SSD_GROUPS * D_STATE:]
    cbf = cm.astype(BF16)

    dtc = dt_ref[...]
    if sample:
        dtc = jnp.where(rowmod >= GROUP - DEC_SEQ_ROWS, dtc, 0.0)
    da = dtc * arow_ref[...]
    cum = _seg_cumsum(da, seg, rowmod)
    rev = _seg_rev_cumsum(da, seg, rowmod) - da
    cum_t = cum.T
    ecum = jnp.exp(cum)
    dt_e = _expand_heads(dtc, lane_lt_half)
    ecum_e = _expand_heads(ecum, lane_lt_half)
    erev_e = _expand_heads(jnp.exp(rev), lane_lt_half)

    xdt = x * dt_e
    xdt_bf = xdt.astype(BF16)
    causal = rows >= cols
    if sample:
        causal = causal & ((rows >> 3) == (cols >> 3))

    yd = []
    for g in range(SSD_GROUPS):
        cb_g = _dot_nt(cbf[:, g * D_STATE:(g + 1) * D_STATE], bb[:, g * D_STATE:(g + 1) * D_STATE])
        for hh in range(hpg):
            h = g * hpg + hh
            expo = cum[:, h:h + 1] - cum_t[h:h + 1, :]
            w = (cb_g * jnp.exp(jnp.where(causal, expo, NEG_BIG))).astype(BF16)
            j = h // 2
            yd.append(_dot(w, xdt_bf[:, j * LANES:(j + 1) * LANES]))
    y_diag = _merge_head_pairs(yd, lane_lt_half)

    if sample:
        for i in range(CHUNK // GROUP):
            for g in range(SSD_GROUPS):
                s_g = s0_ref[i, g * gw:(g + 1) * gw, :].astype(BF16)
                yoff_ref[i * GROUP:(i + 1) * GROUP, g * gw:(g + 1) * gw] = _dot_nt(
                    cm[i * GROUP:(i + 1) * GROUP, g * D_STATE:(g + 1) * D_STATE].astype(BF16), s_g)
        y_off = yoff_ref[...]
    else:
        y_off = jnp.concatenate(
            [_dot_nt(cbf[:, g * D_STATE:(g + 1) * D_STATE], sout_ref[0, g * gw:(g + 1) * gw, :].astype(BF16))
             for g in range(SSD_GROUPS)], axis=1)
    y = y_diag + y_off * ecum_e + dsk_ref[...] * x

    xd = xdt * erev_e
    for g in range(SSD_GROUPS):
        xd_t = xd[:, g * gw:(g + 1) * gw].T
        b_g = bb[:, g * D_STATE:(g + 1) * D_STATE]
        if sample:
            tcols = lax.broadcasted_iota(jnp.int32, xd_t.shape, 1) >> 3
            for i in range(CHUNK // GROUP):
                upd = _dot(jnp.where(tcols == i, xd_t, 0.0).astype(BF16), b_g)
                last = i * GROUP + GROUP - 1
                for hh in range(hpg):
                    h = g * hpg + hh
                    r0 = h * SSD_HEAD_DIM
                    sout_ref[i, r0:r0 + SSD_HEAD_DIM, :] = (
                        s0_ref[i, r0:r0 + SSD_HEAD_DIM, :] * ecum[last:last + 1, h:h + 1]
                        + upd[hh * SSD_HEAD_DIM:(hh + 1) * SSD_HEAD_DIM, :])
        else:
            upd = _dot(xd_t.astype(BF16), b_g)
            for hh in range(hpg):
                h = g * hpg + hh
                r0 = h * SSD_HEAD_DIM
                sout_ref[0, r0:r0 + SSD_HEAD_DIM, :] = (
                    sout_ref[0, r0:r0 + SSD_HEAD_DIM, :] * ecum[CHUNK - 1:CHUNK, h:h + 1]
                    + upd[hh * SSD_HEAD_DIM:(hh + 1) * SSD_HEAD_DIM, :])

    zf = z_ref[...]
    yf = y * (zf * jax.nn.sigmoid(zf))
    parts = []
    for g in range(SSD_GROUPS):
        part = yf[:, g * gw:(g + 1) * gw]
        ms = jnp.mean(part * part, axis=-1, keepdims=True)
        parts.append(part * lax.rsqrt(ms + EPS))
    y_ssd = jnp.concatenate(parts, axis=1) * sn_ref[...]

    vb = v_ref[...].astype(BF16)
    sg = [_dot(wsp_ref[h], vb[:, (h // 2) * LANES:(h // 2 + 1) * LANES]) for h in range(GMLP_HEADS)]
    s = _merge_head_pairs(sg, lane_lt_half) + bsp_ref[...]
    y_gm = u_ref[...] * s

    y_ref[:, :d_ssd] = y_ssd.astype(BF16)
    y_ref[:, d_ssd:] = y_gm.astype(BF16)


def _mixer_prompt(z, xbc, dt, u, v, cw, cb, arow, dsk, sn, wsp, bsp, n_seq, n_chunk):
    tok = lambda w: pl.BlockSpec((CHUNK, w), lambda b, c: (b * n_chunk + c, 0))
    full = lambda a: pl.BlockSpec(a.shape, lambda b, c: (0,) * a.ndim)
    d_ssd = z.shape[1]
    d_mix = d_ssd + u.shape[1]
    return pl.pallas_call(
        functools.partial(_mixer_kernel, sample=False),
        grid=(n_seq, n_chunk),
        in_specs=[tok(z.shape[1]), tok(xbc.shape[1]), tok(dt.shape[1]), tok(u.shape[1]), tok(v.shape[1]),
                  full(cw), full(cb), full(arow), full(dsk), full(sn), full(wsp), full(bsp)],
        out_specs=[pl.BlockSpec((CHUNK, d_mix), lambda b, c: (b * n_chunk + c, 0)),
                   pl.BlockSpec((1, d_ssd, D_STATE), lambda b, c: (b, 0, 0))],
        out_shape=[jax.ShapeDtypeStruct((n_seq * n_chunk * CHUNK, d_mix), BF16),
                   jax.ShapeDtypeStruct((n_seq, d_ssd, D_STATE), F32)],
        scratch_shapes=[pltpu.VMEM((CHUNK + 8, xbc.shape[1]), F32)],
        compiler_params=_params(2),
        name="mixer_prompt",
    )(z, xbc, dt, u, v, cw, cb, arow, dsk, sn, wsp, bsp)


def _mixer_sample(z, xbc, dt, u, v, cs, s0, cw, cb, arow, dsk, sn, wsp, bsp, row0, n_rows):
    blk0 = row0 // CHUNK
    n_seq_blk = CHUNK // GROUP
    tok = lambda w: pl.BlockSpec((CHUNK, w), lambda i: (blk0 + i, 0))
    full = lambda a: pl.BlockSpec(a.shape, lambda i: (0,) * a.ndim)
    d_ssd = z.shape[1]
    d_mix = d_ssd + u.shape[1]
    st = pl.BlockSpec((n_seq_blk, d_ssd, D_STATE), lambda i: (i, 0, 0))
    return pl.pallas_call(
        functools.partial(_mixer_kernel, sample=True),
        grid=(n_rows // CHUNK,),
        in_specs=[tok(z.shape[1]), tok(xbc.shape[1]), tok(dt.shape[1]), tok(u.shape[1]), tok(v.shape[1]),
                  pl.BlockSpec((CHUNK, cs.shape[1]), lambda i: (i, 0)), st,
                  full(cw), full(cb), full(arow), full(dsk), full(sn), full(wsp), full(bsp)],
        out_specs=[pl.BlockSpec((CHUNK, d_mix), lambda i: (i, 0)), st],
        out_shape=[jax.ShapeDtypeStruct((n_rows, d_mix), BF16),
                   jax.ShapeDtypeStruct(s0.shape, F32)],
        scratch_shapes=[pltpu.VMEM((CHUNK + 8, xbc.shape[1]), F32), pltpu.VMEM((CHUNK, d_ssd), F32)],
        compiler_params=_params(1),
        name="mixer_sample",
    )(z, xbc, dt, u, v, cs, s0, cw, cb, arow, dsk, sn, wsp, bsp)


def _first_argmax(vals):
    m = vals[0]
    for v in vals[1:]:
        m = jnp.maximum(m, v)
    idx = jnp.full(m.shape, len(vals) - 1, jnp.int32)
    for k in range(len(vals) - 2, -1, -1):
        idx = jnp.where(vals[k] >= m, k, idx)
    return m, idx


def _slab_store(slab_ref, x, pitch):
    for k in range(x.shape[1] // LANES):
        slab_ref[pl.ds(k, x.shape[0], stride=pitch), :] = x[:, k * LANES:(k + 1) * LANES]


def _slab_load(slab_ref, rows, n_pieces, pitch):
    return jnp.concatenate([slab_ref[pl.ds(k, rows, stride=pitch), :] for k in range(n_pieces)], axis=1)


def _post_kernel(yp_ref, ys_ref, h_ref, wo_ref, nf_ref, wr_ref, br_ref,
                 slab_ref, route_ref, cnt_ref, carry_ref, *, n_prompt_tiles):
    i = pl.program_id(0)
    d = h_ref.shape[1]
    tm = h_ref.shape[0]

    @pl.when(i == 0)
    def _():
        carry_ref[...] = jnp.zeros(carry_ref.shape, F32)

    ym = jnp.where(i < n_prompt_tiles, yp_ref[...], ys_ref[...])
    h1 = h_ref[...] + _dot(ym, wo_ref[...])
    _slab_store(slab_ref, h1, X_PITCH)

    t = _rms(h1, nf_ref[...])
    logits = lax.dot_general(wr_ref[...], t, (((1,), (1,)), ((), ())),
                             precision=lax.Precision.HIGHEST, preferred_element_type=F32) + br_ref[...]
    lg = [logits[k:k + 1, :] for k in range(N_EXPERT_GROUPS)]
    m, g = _first_argmax(lg)
    ssum = jnp.exp(lg[0] - m)
    for k in range(1, N_EXPERT_GROUPS):
        ssum = ssum + jnp.exp(lg[k] - m)
    p_sel = 1.0 / ssum
    le = [logits[N_EXPERT_GROUPS + e:N_EXPERT_GROUPS + e + 1, :] for e in range(N_EXPERT_GROUPS * EXPERTS_PER_GROUP)]
    a = []
    for k in range(EXPERTS_PER_GROUP):
        sel = le[(N_EXPERT_GROUPS - 1) * EXPERTS_PER_GROUP + k]
        for gi in range(N_EXPERT_GROUPS - 2, -1, -1):
            sel = jnp.where(g == gi, le[gi * EXPERTS_PER_GROUP + k], sel)
        a.append(sel)
    v1, i1 = _first_argmax(a)
    a2 = [jnp.where(i1 == k, -jnp.inf, a[k]) for k in range(EXPERTS_PER_GROUP)]
    v2, i2 = _first_argmax(a2)
    e2 = jnp.exp(v2 - v1)
    den = 1.0 + e2
    g1 = (1.0 / den) * p_sel
    g2 = (e2 / den) * p_sel
    lo = jnp.minimum(i1, i2)
    hi = jnp.maximum(i1, i2)
    c_lo = jnp.where(i1 < i2, g1, g2)
    c_hi = jnp.where(i1 < i2, g2, g1)
    pair = jnp.where(lo == 0, hi - 1, jnp.where(lo == 1, hi + 1, N_PAIRS - 1))
    bucket = g * N_PAIRS + pair

    brow = lax.broadcasted_iota(jnp.int32, (ROUTE_ROWS, tm), 0)
    onehot = jnp.where(brow == bucket, 1.0, 0.0)
    tr = lax.broadcasted_iota(jnp.int32, (tm, tm), 0)
    tc = lax.broadcasted_iota(jnp.int32, (tm, tm), 1)
    earlier = jnp.where(tr < tc, 1.0, 0.0).astype(BF16)
    prefix = _dot(onehot.astype(BF16), earlier)
    carry = carry_ref[:, 0:1]
    rank = jnp.sum(onehot * (prefix + carry), axis=0, keepdims=True)
    carry = carry + jnp.sum(onehot, axis=1, keepdims=True)
    carry_b = jnp.broadcast_to(carry, carry_ref.shape)
    carry_ref[...] = carry_b
    cnt_ref[...] = carry_b

    rr = lax.broadcasted_iota(jnp.int32, (8, tm), 0)
    route_ref[...] = jnp.where(rr == 0, bucket.astype(F32), jnp.where(rr == 1, rank, 0.0))
    ar = lax.broadcasted_iota(jnp.int32, (LANES, tm), 0)
    aux = jnp.where(ar == 0, c_lo, jnp.where(ar == 1, c_hi, 0.0))
    slab_ref[pl.ds(d // LANES, tm, stride=X_PITCH), :] = aux.T


def _post(yp, ys, h, wo, nf, wr, br):
    t, d = h.shape
    n_prompt_tiles = yp.shape[0] // TOK_TILE
    full = lambda a: pl.BlockSpec(a.shape, lambda i: (0,) * a.ndim)
    return pl.pallas_call(
        functools.partial(_post_kernel, n_prompt_tiles=n_prompt_tiles),
        grid=(t // TOK_TILE,),
        in_specs=[pl.BlockSpec((TOK_TILE, d), lambda i: (jnp.minimum(i, n_prompt_tiles - 1), 0)),
                  pl.BlockSpec((TOK_TILE, d), lambda i: (jnp.maximum(i - n_prompt_tiles, 0), 0)),
                  pl.BlockSpec((TOK_TILE, d), lambda i: (i, 0)),
                  full(wo), full(nf), full(wr), full(br)],
        out_specs=[pl.BlockSpec((TOK_TILE * X_PITCH, LANES), lambda i: (i, 0)),
                   pl.BlockSpec((8, TOK_TILE), lambda i: (0, i)),
                   pl.BlockSpec((ROUTE_ROWS, LANES), lambda i: (0, 0))],
        out_shape=[jax.ShapeDtypeStruct((t * X_PITCH, LANES), F32),
                   jax.ShapeDtypeStruct((8, t), F32),
                   jax.ShapeDtypeStruct((ROUTE_ROWS, LANES), F32)],
        scratch_shapes=[pltpu.VMEM((ROUTE_ROWS, LANES), F32)],
        compiler_params=_params(1),
        name="post",
    )(yp, ys, h, wo, nf, wr, br)


def _permute_kernel(*refs, scatter, chunk, pitch):
    if scatter:
        pos_ref, src_ref, _, dst_ref, sem = refs
    else:
        pos_ref, src_ref, dst_ref, sem = refs
    base = pl.program_id(0) * chunk

    def token_copy(r):
        here = pl.ds((base + r) * pitch, pitch)
        there = pl.ds(pos_ref[0, 0, r] * pitch, pitch)
        if scatter:
            return pltpu.make_async_copy(src_ref.at[here], dst_ref.at[there], sem)
        return pltpu.make_async_copy(src_ref.at[there], dst_ref.at[here], sem)

    def body(r, carry):
        token_copy(r).start()
        return carry

    lax.fori_loop(0, chunk, body, 0)
    whole = pl.ds(0, chunk * pitch)
    pltpu.make_async_copy(src_ref.at[whole], dst_ref.at[whole], sem).wait()


def _permute_tokens(pos, src, n_out, pitch, scatter):
    n = pos.shape[0]
    chunk = TOK_TILE
    pos3 = pos.reshape(n // chunk, 1, chunk)
    in_specs = [pl.BlockSpec((1, 1, chunk), lambda i: (i, 0, 0), memory_space=pltpu.SMEM),
                pl.BlockSpec(memory_space=pl.ANY)]
    args = [pos3, src]
    aliases = {}
    if scatter:
        in_specs.append(pl.BlockSpec(memory_space=pl.ANY))
        args.append(jnp.zeros((n_out * pitch, LANES), src.dtype))
        aliases = {2: 0}
    return pl.pallas_call(
        functools.partial(_permute_kernel, scatter=scatter, chunk=chunk, pitch=pitch),
        grid=(n // chunk,),
        in_specs=in_specs,
        out_specs=pl.BlockSpec(memory_space=pl.ANY),
        out_shape=jax.ShapeDtypeStruct((n_out * pitch, LANES), src.dtype),
        scratch_shapes=[pltpu.SemaphoreType.DMA(())],
        input_output_aliases=aliases,
        compiler_params=_params(1),
        name="scatter_tokens" if scatter else "gather_tokens",
    )(*args)


def _moe_kernel(elo_ref, ehi_ref, nused_ref, x_ref, nf_ref, wg_lo, wu_lo, wd_lo, wg_hi, wu_hi, wd_hi, o_ref):
    i = pl.program_id(0)
    n_pieces = nf_ref.shape[1] // LANES

    @pl.when(i < nused_ref[0])
    def _():
        x = _slab_load(x_ref, MOE_TILE, n_pieces, X_PITCH)
        gates = x_ref[pl.ds(n_pieces, MOE_TILE, stride=X_PITCH), :]
        c_lo = gates[:, 0:1]
        c_hi = gates[:, 1:2]
        t = _rms(x, nf_ref[...]).astype(BF16)

        def expert(wg, wu, wd):
            gate = _dot(t, wg[0])
            hid = (gate * jax.nn.sigmoid(gate)) * _dot(t, wu[0])
            return _dot(hid.astype(BF16), wd[0])

        y = c_lo * expert(wg_lo, wu_lo, wd_lo)
        y = y + c_hi * expert(wg_hi, wu_hi, wd_hi)
        _slab_store(o_ref, x + y, Y_PITCH)

    @pl.when(i >= nused_ref[0])
    def _():
        o_ref[...] = jnp.zeros(o_ref.shape, F32)


def _moe(e_lo, e_hi, n_used, xs, nf, wg, wu, wd):
    n_tiles = xs.shape[0] // (MOE_TILE * X_PITCH)
    d, de = wg.shape[1:]
    lo = lambda i, elo, ehi, nu: (elo[i], 0, 0)
    hi = lambda i, elo, ehi, nu: (ehi[i], 0, 0)
    grid_spec = pltpu.PrefetchScalarGridSpec(
        num_scalar_prefetch=3,
        grid=(n_tiles,),
        in_specs=[pl.BlockSpec((MOE_TILE * X_PITCH, LANES),
                               lambda i, elo, ehi, nu: (jnp.maximum(jnp.minimum(i, nu[0] - 1), 0), 0)),
                  pl.BlockSpec(nf.shape, lambda i, elo, ehi, nu: (0, 0)),
                  pl.BlockSpec((1, d, de), lo), pl.BlockSpec((1, d, de), lo), pl.BlockSpec((1, de, d), lo),
                  pl.BlockSpec((1, d, de), hi), pl.BlockSpec((1, d, de), hi), pl.BlockSpec((1, de, d), hi)],
        out_specs=pl.BlockSpec((MOE_TILE * Y_PITCH, LANES), lambda i, elo, ehi, nu: (i, 0)),
    )
    return pl.pallas_call(
        _moe_kernel,
        grid_spec=grid_spec,
        out_shape=jax.ShapeDtypeStruct((n_tiles * MOE_TILE * Y_PITCH, LANES), F32),
        compiler_params=_params(1),
        name="expert_pairs",
    )(e_lo, e_hi, n_used, xs, nf, wg, wu, wd, wg, wu, wd)


def _ple_kernel(h_ref, p_ref, wple_ref, nple_ref, npg_ref, wpg_ref, nfin_ref, o_ref, *, final):
    h2 = _slab_load(h_ref, TOK_TILE, Y_PITCH, Y_PITCH)
    e = _rms(_dot(p_ref[...].astype(BF16), wple_ref[...]), nple_ref[...])
    gate = jax.nn.sigmoid(_dot(_rms(h2, npg_ref[...]).astype(BF16), wpg_ref[...]))
    h3 = h2 + gate * e
    o_ref[...] = _rms(h3, nfin_ref[...]) if final else h3


def _ple(h2, p, wple, nple, npg, wpg, nfin, final):
    t = h2.shape[0] // Y_PITCH
    d = Y_PITCH * LANES
    row = lambda w: pl.BlockSpec((TOK_TILE, w), lambda i: (i, 0))
    full = lambda a: pl.BlockSpec(a.shape, lambda i: (0,) * a.ndim)
    return pl.pallas_call(
        functools.partial(_ple_kernel, final=final),
        grid=(t // TOK_TILE,),
        in_specs=[pl.BlockSpec((TOK_TILE * Y_PITCH, LANES), lambda i: (i, 0)), row(p.shape[1]),
                  full(wple), full(nple), full(npg), full(wpg), full(nfin)],
        out_specs=row(d),
        out_shape=jax.ShapeDtypeStruct((t, d), F32),
        compiler_params=_params(1),
        name="ple",
    )(h2, p, wple, nple, npg, wpg, nfin)


_PAIR_LO = (0, 0, 0, 1, 1, 2)
_PAIR_HI = (1, 2, 3, 2, 3, 3)


def _routing_tables(route, cnt, n_tiles):
    counts = cnt[:N_BUCKETS, 0].astype(jnp.int32)
    padded = ((counts + MOE_TILE - 1) // MOE_TILE) * MOE_TILE
    ends = jnp.cumsum(padded)
    starts = ends - padded
    bucket = route[0].astype(jnp.int32)
    rank = route[1].astype(jnp.int32)
    pos = starts[bucket] + rank
    n_used = ends[-1] // MOE_TILE
    tile = jnp.minimum(jnp.arange(n_tiles, dtype=jnp.int32), n_used - 1)
    tb = jnp.sum((ends[None, :] <= (tile * MOE_TILE)[:, None]).astype(jnp.int32), axis=1)
    tb = jnp.minimum(tb, N_BUCKETS - 1)
    grp = tb // N_PAIRS
    pr = tb % N_PAIRS
    e_lo = grp * EXPERTS_PER_GROUP + jnp.asarray(_PAIR_LO, jnp.int32)[pr]
    e_hi = grp * EXPERTS_PER_GROUP + jnp.asarray(_PAIR_HI, jnp.int32)[pr]
    return pos, e_lo, e_hi, n_used.reshape(1)


def kernel(x_prompt, x_sample, state_conv, state_ssm, p_prompt, p_sample, norm_mix, w_in, conv_w, conv_b,
           dt_bias, a_log, d_skip, ssd_norm, gmlp_norm, w_spatial, b_spatial, w_out, norm_ffn,
           w_router_group, b_router_group, w_router_expert, b_router_expert, w_gate, w_up, w_down,
           w_ple, norm_ple, norm_pg, w_pg, norm_final):
    n_seq, seq_len, d_model = x_prompt.shape
    n_dec, dec_seq, _ = x_sample.shape
    depth = w_in.shape[0]
    conv_dim = conv_w.shape[2]
    d_ssd = SSD_HEADS * SSD_HEAD_DIM
    d_gmlp = gmlp_norm.shape[1]
    assert dec_seq == DEC_SEQ_ROWS and conv_w.shape[1] == CONV_WIDTH and seq_len % CHUNK == 0
    assert conv_dim == d_ssd + 2 * SSD_GROUPS * D_STATE and w_spatial.shape[1:] == (GMLP_HEADS, CHUNK, CHUNK)
    n_chunk = seq_len // CHUNK
    n_prompt = n_seq * seq_len
    n_srows = n_dec * GROUP
    t_all = n_prompt + n_srows
    assert n_prompt % TOK_TILE == 0 and n_srows % TOK_TILE == 0
    assert d_model == Y_PITCH * LANES and X_PITCH == Y_PITCH + 1
    lead = GROUP - dec_seq
    n_moe_tiles = t_all // MOE_TILE + N_BUCKETS

    def sample_rows(a):
        return jnp.pad(a, ((0, 0), (lead, 0), (0, 0))).reshape(n_srows, a.shape[-1])

    h = jnp.concatenate([x_prompt.reshape(n_prompt, d_model), sample_rows(x_sample)], axis=0)

    o_xbc = d_ssd
    o_dt = o_xbc + conv_dim
    o_uv = o_dt + SSD_HEADS
    head_cols = jnp.arange(d_ssd) // SSD_HEAD_DIM
    seq_eye = jnp.eye(CHUNK // GROUP, dtype=F32)
    tril =jnp.tril(jnp.ones((CHUNK, CHUNK), bool))

    convs_p, ssms_p, convs_s, ssms_s, vs_s = [], [], [], [], []
    y = None
    for i in range(depth):
        wi = w_in[i]
        wz = wi[:, :o_xbc].astype(BF16)
        wx = wi[:, o_xbc:o_dt].astype(BF16)
        wdt = jnp.pad(wi[:, o_dt:o_uv], ((0, 0), (0, LANES - SSD_HEADS))).astype(BF16)
        wu = wi[:, o_uv:o_uv + d_gmlp].astype(BF16)
        wv = wi[:, o_uv + d_gmlp:].astype(BF16)
        dtb = jnp.pad(dt_bias[i].astype(F32), (0, LANES - SSD_HEADS)).reshape(1, LANES)
        arow = jnp.pad(-jnp.exp(a_log[i].astype(F32)), (0, LANES - SSD_HEADS)).reshape(1, LANES)
        dsk = d_skip[i].astype(F32)[head_cols].reshape(1, d_ssd)
        ws_tril = jnp.where(tril, w_spatial[i], 0.0)
        wsp_p = ws_tril.astype(BF16)
        bsp_p = jnp.repeat(b_spatial[i].T, d_gmlp // GMLP_HEADS, axis=1)
        w8 = jnp.pad(ws_tril[:, :dec_seq, :dec_seq], ((0, 0), (lead, 0), (lead, 0)))
        wsp_s = (seq_eye[None, :, None, :, None] * w8[:, None, :, None, :]).reshape(GMLP_HEADS, CHUNK, CHUNK).astype(BF16)
        b8 = jnp.pad(b_spatial[i][:, :dec_seq], ((0, 0), (lead, 0)))
        bsp_s = jnp.repeat(jnp.tile(b8, (1, CHUNK // GROUP)).T, d_gmlp // GMLP_HEADS, axis=1)
        wr = jnp.concatenate([w_router_group[i].T, w_router_expert[i].T,
                              jnp.zeros((ROUTE_ROWS - N_EXPERT_GROUPS * (1 + EXPERTS_PER_GROUP), d_model), F32)], axis=0)
        br = jnp.concatenate([b_router_group[i], b_router_expert[i],
                              jnp.zeros((ROUTE_ROWS - N_EXPERT_GROUPS * (1 + EXPERTS_PER_GROUP),), F32)]).reshape(ROUTE_ROWS, 1)
        row = lambda a: a.reshape(1, -1).astype(F32)

        z, xbc, dt, u, v = _inproj(h, row(norm_mix[i]), wz, wx, wdt, wu, wv, dtb, row(gmlp_norm[i]))

        yp, ssm_p = _mixer_prompt(z, xbc, dt, u, v, conv_w[i], row(conv_b[i]), arow, dsk, row(ssd_norm[i]),
                                  wsp_p, bsp_p, n_seq, n_chunk)
        cs = jnp.pad(state_conv[i], ((0, 0), (lead - (CONV_WIDTH - 1), dec_seq), (0, 0))).reshape(n_srows, conv_dim)
        s0 = state_ssm[i].reshape(n_dec, d_ssd, D_STATE)
        ys, ssm_s = _mixer_sample(z, xbc, dt, u, v, cs, s0, conv_w[i], row(conv_b[i]), arow, dsk, row(ssd_norm[i]),
                                  wsp_s, bsp_s, n_prompt, n_srows)

        slab, route, cnt = _post(yp, ys, h, w_out[i].astype(BF16), row(norm_ffn[i]), wr, br)
        pos, e_lo, e_hi, n_used = _routing_tables(route, cnt, n_moe_tiles)

        xs = _permute_tokens(pos, slab, n_moe_tiles * MOE_TILE, X_PITCH, scatter=True)
        h2s = _moe(e_lo, e_hi, n_used, xs, row(norm_ffn[i]), w_gate[i].astype(BF16), w_up[i].astype(BF16),
                   w_down[i].astype(BF16))
        h2 = _permute_tokens(pos, h2s, t_all, Y_PITCH, scatter=False)

        p_all = jnp.concatenate([p_prompt[i].reshape(n_prompt, -1), sample_rows(p_sample[i])], axis=0)
        final = i == depth - 1
        out = _ple(h2, p_all, w_ple[i].astype(BF16), row(norm_ple[i]), row(norm_pg[i]), w_pg[i].astype(BF16),
                   row(norm_final), final)
        if final:
            y = out
        else:
            h = out

        xbc_p = xbc[:n_prompt].reshape(n_seq, seq_len, conv_dim)
        xbc_s = xbc[n_prompt:].reshape(n_dec, GROUP, conv_dim)
        convs_p.append(xbc_p[:, seq_len - (CONV_WIDTH - 1):])
        convs_s.append(xbc_s[:, GROUP - (CONV_WIDTH - 1):])
        ssms_p.append(ssm_p.reshape(n_seq, SSD_HEADS, SSD_HEAD_DIM, D_STATE))
        ssms_s.append(ssm_s.reshape(n_dec, SSD_HEADS, SSD_HEAD_DIM, D_STATE))
        vs_s.append(v[n_prompt:].reshape(n_dec, GROUP, d_gmlp)[:, lead:])

    y_prompt = y[:n_prompt].reshape(n_seq, seq_len, d_model)
    y_sample = y[n_prompt:].reshape(n_dec, GROUP, d_model)[:, lead:]
    return (y_prompt, y_sample, jnp.stack(convs_p), jnp.stack(ssms_p), jnp.stack(convs_s), jnp.stack(ssms_s),
            jnp.stack(vs_s))
```

```python
import functools
import math

import jax
import jax.numpy as jnp
from jax import lax
from jax.experimental import pallas as pl
from jax.experimental.pallas import tpu as pltpu

F32 = jnp.float32
BF16 = jnp.bfloat16

LANES = 128
VMEM_LIMIT_BYTES = 56 * 1024 * 1024

CONV_WIDTH = 4
SSD_HEADS = 8
SSD_HEAD_DIM = 64
SSD_GROUPS = 2
D_STATE = 128
CHUNK = 128
GMLP_HEADS = 8
N_EXPERT_GROUPS = 4
EXPERTS_PER_GROUP = 4
N_PAIRS = 6
N_BUCKETS = N_EXPERT_GROUPS * N_PAIRS
EPS = 1e-6

GROUP = 8
DEC_SEQ_ROWS = 4
TOK_TILE = 512
MOE_TILE = 256
ROUTE_ROWS = 32
X_PITCH = 9
Y_PITCH = 8
NEG_BIG = -1e30


def _dot(a, b):
    return jnp.dot(a, b, preferred_element_type=F32)


def _dot_nt(a, b):
    return lax.dot_general(a, b, (((1,), (1,)), ((), ())), preferred_element_type=F32)


def _rms(x, g):
    ms = jnp.mean(x * x, axis=-1, keepdims=True)
    return (x * lax.rsqrt(ms + EPS)) * g


def _gelu(x):
    return 0.5 * x * (1.0 + lax.erf(x * (1.0 / math.sqrt(2.0))))


def _softplus(x):
    return jnp.maximum(x, 0.0) + jnp.log1p(jnp.exp(-jnp.abs(x)))


def _params(n_grid):
    return pltpu.CompilerParams(dimension_semantics=("arbitrary",) * n_grid,
                                vmem_limit_bytes=VMEM_LIMIT_BYTES)


def _inproj_kernel(h_ref, nm_ref, wz_ref, wx_ref, wdt_ref, wu_ref, wv_ref, dtb_ref, gn_ref,
                   z_ref, xbc_ref, dt_ref, u_ref, v_ref):
    a = _rms(h_ref[...], nm_ref[...]).astype(BF16)
    z_ref[...] = _dot(a, wz_ref[...])
    xbc_ref[...] = _dot(a, wx_ref[...])
    dt_ref[...] = _softplus(_dot(a, wdt_ref[...]) + dtb_ref[...])
    u_ref[...] = _gelu(_dot(a, wu_ref[...]))
    v_ref[...] = _rms(_gelu(_dot(a, wv_ref[...])), gn_ref[...])


def _inproj(h, nm, wz, wx, wdt, wu, wv, dtb, gn):
    t, d = h.shape
    widths = (wz.shape[1], wx.shape[1], wdt.shape[1], wu.shape[1], wv.shape[1])
    row = lambda w: pl.BlockSpec((TOK_TILE, w), lambda i: (i, 0))
    full = lambda a: pl.BlockSpec(a.shape, lambda i: (0,) * a.ndim)
    return pl.pallas_call(
        _inproj_kernel,
        grid=(t // TOK_TILE,),
        in_specs=[row(d), full(nm), full(wz), full(wx), full(wdt), full(wu), full(wv), full(dtb), full(gn)],
        out_specs=[row(w) for w in widths],
        out_shape=[jax.ShapeDtypeStruct((t, w), F32) for w in widths],
        compiler_params=_params(1),
        name="inproj",
    )(h, nm, wz, wx, wdt, wu, wv, dtb, gn)


def _seg_cumsum(x, seg, rowmod):
    d = 1
    while d < seg:
        x = x + jnp.where(rowmod >= d, pltpu.roll(x, d, axis=0), 0.0)
        d *= 2
    return x


def _seg_rev_cumsum(x, seg, rowmod):
    n = x.shape[0]
    d = 1
    while d < seg:
        x = x + jnp.where(rowmod + d < seg, pltpu.roll(x, n - d, axis=0), 0.0)
        d *= 2
    return x


def _expand_heads(m, lane_lt_half):
    parts = []
    for j in range(SSD_HEADS // 2):
        parts.append(jnp.where(lane_lt_half, m[:, 2 * j:2 * j + 1], m[:, 2 * j + 1:2 * j + 2]))
    return jnp.concatenate(parts, axis=1)


def _merge_head_pairs(per_head, lane_lt_half):
    parts = [jnp.where(lane_lt_half, per_head[2 * j], per_head[2 * j + 1]) for j in range(len(per_head) // 2)]
    return jnp.concatenate(parts, axis=1)


def _mixer_kernel(*refs, sample):
    if sample:
        (z_ref, xbc_ref, dt_ref, u_ref, v_ref, cs_ref, s0_ref, cw_ref, cb_ref, arow_ref, dsk_ref, sn_ref,
         wsp_ref, bsp_ref, y_ref, sout_ref, ext_ref, yoff_ref) = refs
        seg = GROUP
        first = pl.program_id(0) == 0
    else:
        (z_ref, xbc_ref, dt_ref, u_ref, v_ref, cw_ref, cb_ref, arow_ref, dsk_ref, sn_ref,
         wsp_ref, bsp_ref, y_ref, sout_ref, ext_ref) = refs
        seg = CHUNK
        first = pl.program_id(1) == 0
    d_ssd = SSD_HEADS * SSD_HEAD_DIM
    gw = d_ssd // SSD_GROUPS
    hpg = SSD_HEADS // SSD_GROUPS
    cs_first = GROUP - DEC_SEQ_ROWS - (CONV_WIDTH - 1)

    rows = lax.broadcasted_iota(jnp.int32, (CHUNK, LANES), 0)
    cols = lax.broadcasted_iota(jnp.int32, (CHUNK, LANES), 1)
    rowmod = rows & (seg - 1)
    lane_lt_half = cols < SSD_HEAD_DIM

    xbc = xbc_ref[...]
    if sample:
        rm = lax.broadcasted_iota(jnp.int32, xbc.shape, 0) & (GROUP - 1)
        xbc = jnp.where((rm >= cs_first) & (rm < cs_first + CONV_WIDTH - 1), cs_ref[...], xbc)

    @pl.when(first)
    def _():
        ext_ref[0:8, :] = jnp.zeros((8, xbc.shape[1]), F32)
        if not sample:
            sout_ref[...] = jnp.zeros(sout_ref.shape, F32)

    ext_ref[8:8 + CHUNK, :] = xbc
    acc = cb_ref[...] + cw_ref[CONV_WIDTH - 1:CONV_WIDTH, :] * xbc
    for j in range(1, CONV_WIDTH):
        acc = acc + cw_ref[CONV_WIDTH - 1 - j:CONV_WIDTH - j, :] * ext_ref[8 - j:8 - j + CHUNK, :]
    if not sample:
        ext_ref[0:8, :] = ext_ref[CHUNK:CHUNK + 8, :]
    xc = acc * jax.nn.sigmoid(acc)
    x = xc[:, :d_ssd]
    bb = xc[:, d_ssd:d_ssd + SSD_GROUPS * D_STATE].astype(BF16)
    cm = xc[:, d_ssd + SSD_GROUPS * D_STATE:]
    cbf = cm.astype(BF16)

    dtc = dt_ref[...]
    if sample:
        dtc = jnp.where(rowmod >= GROUP - DEC_SEQ_ROWS, dtc, 0.0)
    da = dtc * arow_ref[...]
    cum = _seg_cumsum(da, seg, rowmod)
    rev = _seg_rev_cumsum(da, seg, rowmod) - da
    cum_t = cum.T
    ecum = jnp.exp(cum)
    dt_e = _expand_heads(dtc, lane_lt_half)
    ecum_e = _expand_heads(ecum, lane_lt_half)
    erev_e = _expand_heads(jnp.exp(rev), lane_lt_half)

    xdt = x * dt_e
    xdt_bf = xdt.astype(BF16)
    causal = rows >= cols
    if sample:
        causal = causal & ((rows >> 3) == (cols >> 3))

    yd = []
    for g in range(SSD_GROUPS):
        cb_g = _dot_nt(cbf[:, g * D_STATE:(g + 1) * D_STATE], bb[:, g * D_STATE:(g + 1) * D_STATE])
        for hh in range(hpg):
            h = g * hpg + hh
            expo = cum[:, h:h + 1] - cum_t[h:h + 1, :]
            w = (cb_g * jnp.exp(jnp.where(causal, expo, NEG_BIG))).astype(BF16)
            j = h // 2
            yd.append(_dot(w, xdt_bf[:, j * LANES:(j + 1) * LANES]))
    y_diag = _merge_head_pairs(yd, lane_lt_half)

    if sample:
        for i in range(CHUNK // GROUP):
            for g in range(SSD_GROUPS):
                s_g = s0_ref[i, g * gw:(g + 1) * gw, :].astype(BF16)
                yoff_ref[i * GROUP:(i + 1) * GROUP, g * gw:(g + 1) * gw] = _dot_nt(
                    cm[i * GROUP:(i + 1) * GROUP, g * D_STATE:(g + 1) * D_STATE].astype(BF16), s_g)
        y_off = yoff_ref[...]
    else:
        y_off = jnp.concatenate(
            [_dot_nt(cbf[:, g * D_STATE:(g + 1) * D_STATE], sout_ref[0, g * gw:(g + 1) * gw, :].astype(BF16))
             for g in range(SSD_GROUPS)], axis=1)
    y = y_diag + y_off * ecum_e + dsk_ref[...] * x

    xd = xdt * erev_e
    for g in range(SSD_GROUPS):
        xd_t = xd[:, g * gw:(g + 1) * gw].T
        b_g = bb[:, g * D_STATE:(g + 1) * D_STATE]
        if sample:
            tcols = lax.broadcasted_iota(jnp.int32, xd_t.shape, 1) >> 3
            for i in range(CHUNK // GROUP):
                upd = _dot(jnp.where(tcols == i, xd_t, 0.0).astype(BF16), b_g)
                last = i * GROUP + GROUP - 1
                for hh in range(hpg):
                    h = g * hpg + hh
                    r0 = h * SSD_HEAD_DIM
                    sout_ref[i, r0:r0 + SSD_HEAD_DIM, :] = (
                        s0_ref[i, r0:r0 + SSD_HEAD_DIM, :] * ecum[last:last + 1, h:h + 1]
                        + upd[hh * SSD_HEAD_DIM:(hh + 1) * SSD_HEAD_DIM, :])
        else:
            upd = _dot(xd_t.astype(BF16), b_g)
            for hh in range(hpg):
                h = g * hpg + hh
                r0 = h * SSD_HEAD_DIM
                sout_ref[0, r0:r0 + SSD_HEAD_DIM, :] = (
                    sout_ref[0, r0:r0 + SSD_HEAD_DIM, :] * ecum[CHUNK - 1:CHUNK, h:h + 1]
                    + upd[hh * SSD_HEAD_DIM:(hh + 1) * SSD_HEAD_DIM, :])

    zf = z_ref[...]
    yf = y * (zf * jax.nn.sigmoid(zf))
    parts = []
    for g in range(SSD_GROUPS):
        part = yf[:, g * gw:(g + 1) * gw]
        ms = jnp.mean(part * part, axis=-1, keepdims=True)
        parts.append(part * lax.rsqrt(ms + EPS))
    y_ssd = jnp.concatenate(parts, axis=1) * sn_ref[...]

    vb = v_ref[...].astype(BF16)
    sg = [_dot(wsp_ref[h], vb[:, (h // 2) * LANES:(h // 2 + 1) * LANES]) for h in range(GMLP_HEADS)]
    s = _merge_head_pairs(sg, lane_lt_half) + bsp_ref[...]
    y_gm = u_ref[...] * s

    y_ref[:, :d_ssd] = y_ssd.astype(BF16)
    y_ref[:, d_ssd:] = y_gm.astype(BF16)


def _mixer_prompt(z, xbc, dt, u, v, cw, cb, arow, dsk, sn, wsp, bsp, n_seq, n_chunk):
    tok = lambda w: pl.BlockSpec((CHUNK, w), lambda b, c: (b * n_chunk + c, 0))
    full = lambda a: pl.BlockSpec(a.shape, lambda b, c: (0,) * a.ndim)
    d_ssd = z.shape[1]
    d_mix = d_ssd + u.shape[1]
    return pl.pallas_call(
        functools.partial(_mixer_kernel, sample=False),
        grid=(n_seq, n_chunk),
        in_specs=[tok(z.shape[1]), tok(xbc.shape[1]), tok(dt.shape[1]), tok(u.shape[1]), tok(v.shape[1]),
                  full(cw), full(cb), full(arow), full(dsk), full(sn), full(wsp), full(bsp)],
        out_specs=[pl.BlockSpec((CHUNK, d_mix), lambda b, c: (b * n_chunk + c, 0)),
                   pl.BlockSpec((1, d_ssd, D_STATE), lambda b, c: (b, 0, 0))],
        out_shape=[jax.ShapeDtypeStruct((n_seq * n_chunk * CHUNK, d_mix), BF16),
                   jax.ShapeDtypeStruct((n_seq, d_ssd, D_STATE), F32)],
        scratch_shapes=[pltpu.VMEM((CHUNK + 8, xbc.shape[1]), F32)],
        compiler_params=_params(2),
        name="mixer_prompt",
    )(z, xbc, dt, u, v, cw, cb, arow, dsk, sn, wsp, bsp)


def _mixer_sample(z, xbc, dt, u, v, cs, s0, cw, cb, arow, dsk, sn, wsp, bsp, row0, n_rows):
    blk0 = row0 // CHUNK
    n_seq_blk = CHUNK // GROUP
    tok = lambda w: pl.BlockSpec((CHUNK, w), lambda i: (blk0 + i, 0))
    full = lambda a: pl.BlockSpec(a.shape, lambda i: (0,) * a.ndim)
    d_ssd = z.shape[1]
    d_mix = d_ssd + u.shape[1]
    st = pl.BlockSpec((n_seq_blk, d_ssd, D_STATE), lambda i: (i, 0, 0))
    return pl.pallas_call(
        functools.partial(_mixer_kernel, sample=True),
        grid=(n_rows // CHUNK,),
        in_specs=[tok(z.shape[1]), tok(xbc.shape[1]), tok(dt.shape[1]), tok(u.shape[1]), tok(v.shape[1]),
                  pl.BlockSpec((CHUNK, cs.shape[1]), lambda i: (i, 0)), st,
                  full(cw), full(cb), full(arow), full(dsk), full(sn), full(wsp), full(bsp)],
        out_specs=[pl.BlockSpec((CHUNK, d_mix), lambda i: (i, 0)), st],
        out_shape=[jax.ShapeDtypeStruct((n_rows, d_mix), BF16),
                   jax.ShapeDtypeStruct(s0.shape, F32)],
        scratch_shapes=[pltpu.VMEM((CHUNK + 8, xbc.shape[1]), F32), pltpu.VMEM((CHUNK, d_ssd), F32)],
        compiler_params=_params(1),
        name="mixer_sample",
    )(z, xbc, dt, u, v, cs, s0, cw, cb, arow, dsk, sn, wsp, bsp)


def _first_argmax(vals):
    m = vals[0]
    for v in vals[1:]:
        m = jnp.maximum(m, v)
    idx = jnp.full(m.shape, len(vals) - 1, jnp.int32)
    for k in range(len(vals) - 2, -1, -1):
        idx = jnp.where(vals[k] >= m, k, idx)
    return m, idx


def _slab_store(slab_ref, x, pitch):
    for k in range(x.shape[1] // LANES):
        slab_ref[pl.ds(k, x.shape[0], stride=pitch), :] = x[:, k * LANES:(k + 1) * LANES]


def _slab_load(slab_ref, rows, n_pieces, pitch):
    return jnp.concatenate([slab_ref[pl.ds(k, rows, stride=pitch), :] for k in range(n_pieces)], axis=1)


def _post_kernel(yp_ref, ys_ref, h_ref, wo_ref, nf_ref, wr_ref, br_ref,
                 slab_ref, route_ref, cnt_ref, carry_ref, *, n_prompt_tiles):
    i = pl.program_id(0)
    d = h_ref.shape[1]
    tm = h_ref.shape[0]

    @pl.when(i == 0)
    def _():
        carry_ref[...] = jnp.zeros(carry_ref.shape, F32)

    ym = jnp.where(i < n_prompt_tiles, yp_ref[...], ys_ref[...])
    h1 = h_ref[...] + _dot(ym, wo_ref[...])
    _slab_store(slab_ref, h1, X_PITCH)

    t = _rms(h1, nf_ref[...])
    logits = lax.dot_general(wr_ref[...], t, (((1,), (1,)), ((), ())),
                             precision=lax.Precision.HIGHEST, preferred_element_type=F32) + br_ref[...]
    lg = [logits[k:k + 1, :] for k in range(N_EXPERT_GROUPS)]
    m, g = _first_argmax(lg)
    ssum = jnp.exp(lg[0] - m)
    for k in range(1, N_EXPERT_GROUPS):
        ssum = ssum + jnp.exp(lg[k] - m)
    p_sel = 1.0 / ssum
    le = [logits[N_EXPERT_GROUPS + e:N_EXPERT_GROUPS + e + 1, :] for e in range(N_EXPERT_GROUPS * EXPERTS_PER_GROUP)]
    a = []
    for k in range(EXPERTS_PER_GROUP):
        sel = le[(N_EXPERT_GROUPS - 1) * EXPERTS_PER_GROUP + k]
        for gi in range(N_EXPERT_GROUPS - 2, -1, -1):
            sel = jnp.where(g == gi, le[gi * EXPERTS_PER_GROUP + k], sel)
        a.append(sel)
    v1, i1 = _first_argmax(a)
    a2 = [jnp.where(i1 == k, -jnp.inf, a[k]) for k in range(EXPERTS_PER_GROUP)]
    v2, i2 = _first_argmax(a2)
    e2 = jnp.exp(v2 - v1)
    den = 1.0 + e2
    g1 = (1.0 / den) * p_sel
    g2 = (e2 / den) * p_sel
    lo = jnp.minimum(i1, i2)
    hi = jnp.maximum(i1, i2)
    c_lo = jnp.where(i1 < i2, g1, g2)
    c_hi = jnp.where(i1 < i2, g2, g1)
    pair = jnp.where(lo == 0, hi - 1, jnp.where(lo == 1, hi + 1, N_PAIRS - 1))
    bucket = g * N_PAIRS + pair

    brow = lax.broadcasted_iota(jnp.int32, (ROUTE_ROWS, tm), 0)
    onehot = jnp.where(brow == bucket, 1.0, 0.0)
    tr = lax.broadcasted_iota(jnp.int32, (tm, tm), 0)
    tc = lax.broadcasted_iota(jnp.int32, (tm, tm), 1)
    earlier = jnp.where(tr < tc, 1.0, 0.0).astype(BF16)
    prefix = _dot(onehot.astype(BF16), earlier)
    carry = carry_ref[:, 0:1]
    rank = jnp.sum(onehot * (prefix + carry), axis=0, keepdims=True)
    carry = carry + jnp.sum(onehot, axis=1, keepdims=True)
    carry_b = jnp.broadcast_to(carry, carry_ref.shape)
    carry_ref[...] = carry_b
    cnt_ref[...] = carry_b

    rr = lax.broadcasted_iota(jnp.int32, (8, tm), 0)
    route_ref[...] = jnp.where(rr == 0, bucket.astype(F32), jnp.where(rr == 1, rank, 0.0))
    ar = lax.broadcasted_iota(jnp.int32, (LANES, tm), 0)
    aux = jnp.where(ar == 0, c_lo, jnp.where(ar == 1, c_hi, 0.0))
    slab_ref[pl.ds(d // LANES, tm, stride=X_PITCH), :] = aux.T


def _post(yp, ys, h, wo, nf, wr, br):
    t, d = h.shape
    n_prompt_tiles = yp.shape[0] // TOK_TILE
    full = lambda a: pl.BlockSpec(a.shape, lambda i: (0,) * a.ndim)
    return pl.pallas_call(
        functools.partial(_post_kernel, n_prompt_tiles=n_prompt_tiles),
        grid=(t // TOK_TILE,),
        in_specs=[pl.BlockSpec((TOK_TILE, d), lambda i: (jnp.minimum(i, n_prompt_tiles - 1), 0)),
                  pl.BlockSpec((TOK_TILE, d), lambda i: (jnp.maximum(i - n_prompt_tiles, 0), 0)),
                  pl.BlockSpec((TOK_TILE, d), lambda i: (i, 0)),
                  full(wo), full(nf), full(wr), full(br)],
        out_specs=[pl.BlockSpec((TOK_TILE * X_PITCH, LANES), lambda i: (i, 0)),
                   pl.BlockSpec((8, TOK_TILE), lambda i: (0, i)),
                   pl.BlockSpec((ROUTE_ROWS, LANES), lambda i: (0, 0))],
        out_shape=[jax.ShapeDtypeStruct((t * X_PITCH, LANES), F32),
                   jax.ShapeDtypeStruct((8, t), F32),
                   jax.ShapeDtypeStruct((ROUTE_ROWS, LANES), F32)],
        scratch_shapes=[pltpu.VMEM((ROUTE_ROWS, LANES), F32)],
        compiler_params=_params(1),
        name="post",
    )(yp, ys, h, wo, nf, wr, br)


def _permute_kernel(*refs, scatter, chunk, pitch):
    if scatter:
        pos_ref, src_ref, _, dst_ref, sem = refs
    else:
        pos_ref, src_ref, dst_ref, sem = refs
    def token_copy(r):
        here = pl.ds(r * pitch, pitch)
        there = pl.ds(pos_ref[0, 0, r] * pitch, pitch)
        if scatter:
            return pltpu.make_async_copy(src_ref.at[here], dst_ref.at[there], sem)
        return pltpu.make_async_copy(src_ref.at[there], dst_ref.at[here], sem)

    def body(r, carry):
        token_copy(r).start()
        return carry

    lax.fori_loop(0, chunk, body, 0)
    whole = pl.ds(0, chunk * pitch)
    if scatter:
        pltpu.make_async_copy(src_ref, dst_ref.at[whole], sem).wait()
    else:
        pltpu.make_async_copy(src_ref.at[whole], dst_ref, sem).wait()


def _permute_tokens(pos, src, n_out, pitch, scatter):
    n = pos.shape[0]
    chunk = TOK_TILE
    pos3 = pos.reshape(n // chunk, 1, chunk)
    block = pl.BlockSpec((chunk * pitch, LANES), lambda i: (i, 0))
    hbm = pl.BlockSpec(memory_space=pl.ANY)
    in_specs = [pl.BlockSpec((1, 1, chunk), lambda i: (i, 0, 0), memory_space=pltpu.SMEM),
                block if scatter else hbm]
    args = [pos3, src]
    aliases = {}
    if scatter:
        in_specs.append(hbm)
        args.append(jnp.zeros((n_out * pitch, LANES), src.dtype))
        aliases = {2: 0}
    return pl.pallas_call(
        functools.partial(_permute_kernel, scatter=scatter, chunk=chunk, pitch=pitch),
        grid=(n // chunk,),
        in_specs=in_specs,
        out_specs=hbm if scatter else block,
        out_shape=jax.ShapeDtypeStruct((n_out * pitch, LANES), src.dtype),
        scratch_shapes=[pltpu.SemaphoreType.DMA(())],
        input_output_aliases=aliases,
        compiler_params=_params(1),
        name="scatter_tokens" if scatter else "gather_tokens",
    )(*args)


def _moe_kernel(elo_ref, ehi_ref, nused_ref, x_ref, nf_ref, wg_lo, wu_lo, wd_lo, wg_hi, wu_hi, wd_hi, o_ref):
    i = pl.program_id(0)
    n_pieces = nf_ref.shape[1] // LANES

    @pl.when(i < nused_ref[0])
    def _():
        x = _slab_load(x_ref, MOE_TILE, n_pieces, X_PITCH)
        gates = x_ref[pl.ds(n_pieces, MOE_TILE, stride=X_PITCH), :]
        c_lo = gates[:, 0:1]
        c_hi = gates[:, 1:2]
        t = _rms(x, nf_ref[...]).astype(BF16)

        def expert(wg, wu, wd):
            gate = _dot(t, wg[0])
            hid = (gate * jax.nn.sigmoid(gate)) * _dot(t, wu[0])
            return _dot(hid.astype(BF16), wd[0])

        y = c_lo * expert(wg_lo, wu_lo, wd_lo)
        y = y + c_hi * expert(wg_hi, wu_hi, wd_hi)
        _slab_store(o_ref, x + y, Y_PITCH)

    @pl.when(i >= nused_ref[0])
    def _():
        o_ref[...] = jnp.zeros(o_ref.shape, F32)


def _moe(e_lo, e_hi, n_used, xs, nf, wg, wu, wd):
    n_tiles = xs.shape[0] // (MOE_TILE * X_PITCH)
    d, de = wg.shape[1:]
    lo = lambda i, elo, ehi, nu: (elo[i], 0, 0)
    hi = lambda i, elo, ehi, nu: (ehi[i], 0, 0)
    grid_spec = pltpu.PrefetchScalarGridSpec(
        num_scalar_prefetch=3,
        grid=(n_tiles,),
        in_specs=[pl.BlockSpec((MOE_TILE * X_PITCH, LANES),
                               lambda i, elo, ehi, nu: (jnp.maximum(jnp.minimum(i, nu[0] - 1), 0), 0)),
                  pl.BlockSpec(nf.shape, lambda i, elo, ehi, nu: (0, 0)),
                  pl.BlockSpec((1, d, de), lo), pl.BlockSpec((1, d, de), lo), pl.BlockSpec((1, de, d), lo),
                  pl.BlockSpec((1, d, de), hi), pl.BlockSpec((1, d, de), hi), pl.BlockSpec((1, de, d), hi)],
        out_specs=pl.BlockSpec((MOE_TILE * Y_PITCH, LANES), lambda i, elo, ehi, nu: (i, 0)),
    )
    return pl.pallas_call(
        _moe_kernel,
        grid_spec=grid_spec,
        out_shape=jax.ShapeDtypeStruct((n_tiles * MOE_TILE * Y_PITCH, LANES), F32),
        compiler_params=_params(1),
        name="expert_pairs",
    )(e_lo, e_hi, n_used, xs, nf, wg, wu, wd, wg, wu, wd)


def _ple_kernel(h_ref, p_ref, wple_ref, nple_ref, npg_ref, wpg_ref, nfin_ref, o_ref, *, final):
    h2 = _slab_load(h_ref, TOK_TILE, Y_PITCH, Y_PITCH)
    e = _rms(_dot(p_ref[...].astype(BF16), wple_ref[...]), nple_ref[...])
    gate = jax.nn.sigmoid(_dot(_rms(h2, npg_ref[...]).astype(BF16), wpg_ref[...]))
    h3 = h2 + gate * e
    o_ref[...] = _rms(h3, nfin_ref[...]) if final else h3


def _ple(h2, p, wple, nple, npg, wpg, nfin, final):
    t = h2.shape[0] // Y_PITCH
    d = Y_PITCH * LANES
    row = lambda w: pl.BlockSpec((TOK_TILE, w), lambda i: (i, 0))
    full = lambda a: pl.BlockSpec(a.shape, lambda i: (0,) * a.ndim)
    return pl.pallas_call(
        functools.partial(_ple_kernel, final=final),
        grid=(t // TOK_TILE,),
        in_specs=[pl.BlockSpec((TOK_TILE * Y_PITCH, LANES), lambda i: (i, 0)), row(p.shape[1]),
                  full(wple), full(nple), full(npg), full(wpg), full(nfin)],
        out_specs=row(d),
        out_shape=jax.ShapeDtypeStruct((t, d), F32),
        compiler_params=_params(1),
        name="ple",
    )(h2, p, wple, nple, npg, wpg, nfin)


_PAIR_LO = (0, 0, 0, 1, 1, 2)
_PAIR_HI = (1, 2, 3, 2, 3, 3)


def _routing_tables(route, cnt, n_tiles):
    counts = cnt[:N_BUCKETS, 0].astype(jnp.int32)
    padded = ((counts + MOE_TILE - 1) // MOE_TILE) * MOE_TILE
    ends = jnp.cumsum(padded)
    starts = ends - padded
    bucket = route[0].astype(jnp.int32)
    rank = route[1].astype(jnp.int32)
    pos = starts[bucket] + rank
    n_used = ends[-1] // MOE_TILE
    tile = jnp.minimum(jnp.arange(n_tiles, dtype=jnp.int32), n_used - 1)
    tb = jnp.sum((ends[None, :] <= (tile * MOE_TILE)[:, None]).astype(jnp.int32), axis=1)
    tb = jnp.minimum(tb, N_BUCKETS - 1)
    grp = tb // N_PAIRS
    pr = tb % N_PAIRS
    e_lo = grp * EXPERTS_PER_GROUP + jnp.asarray(_PAIR_LO, jnp.int32)[pr]
    e_hi = grp * EXPERTS_PER_GROUP + jnp.asarray(_PAIR_HI, jnp.int32)[pr]
    return pos, e_lo, e_hi, n_used.reshape(1)


def kernel(x_prompt, x_sample, state_conv, state_ssm, p_prompt, p_sample, norm_mix, w_in, conv_w, conv_b,
           dt_bias, a_log, d_skip, ssd_norm, gmlp_norm, w_spatial, b_spatial, w_out, norm_ffn,
           w_router_group, b_router_group, w_router_expert, b_router_expert, w_gate, w_up, w_down,
           w_ple, norm_ple, norm_pg, w_pg, norm_final):
    n_seq, seq_len, d_model = x_prompt.shape
    n_dec, dec_seq, _ = x_sample.shape
    depth = w_in.shape[0]
    conv_dim = conv_w.shape[2]
    d_ssd = SSD_HEADS * SSD_HEAD_DIM
    d_gmlp = gmlp_norm.shape[1]
    assert dec_seq == DEC_SEQ_ROWS and conv_w.shape[1] == CONV_WIDTH and seq_len % CHUNK == 0
    assert conv_dim == d_ssd + 2 * SSD_GROUPS * D_STATE and w_spatial.shape[1:] == (GMLP_HEADS, CHUNK, CHUNK)
    n_chunk = seq_len // CHUNK
    n_prompt = n_seq * seq_len
    n_srows = n_dec * GROUP
    t_all = n_prompt + n_srows
    assert n_prompt % TOK_TILE == 0 and n_srows % TOK_TILE == 0
    assert d_model == Y_PITCH * LANES and X_PITCH == Y_PITCH + 1
    lead = GROUP - dec_seq
    n_moe_tiles = t_all // MOE_TILE + N_BUCKETS

    def sample_rows(a):
        return jnp.pad(a, ((0, 0), (lead, 0), (0, 0))).reshape(n_srows, a.shape[-1])

    h = jnp.concatenate([x_prompt.reshape(n_prompt, d_model), sample_rows(x_sample)], axis=0)

    o_xbc = d_ssd
    o_dt = o_xbc + conv_dim
    o_uv = o_dt + SSD_HEADS
    head_cols = jnp.arange(d_ssd) // SSD_HEAD_DIM
    seq_eye = jnp.eye(CHUNK // GROUP, dtype=F32)
    tril =jnp.tril(jnp.ones((CHUNK, CHUNK), bool))

    convs_p, ssms_p, convs_s, ssms_s, vs_s = [], [], [], [], []
    y = None
    for i in range(depth):
        wi = w_in[i]
        wz = wi[:, :o_xbc].astype(BF16)
        wx = wi[:, o_xbc:o_dt].astype(BF16)
        wdt = jnp.pad(wi[:, o_dt:o_uv], ((0, 0), (0, LANES - SSD_HEADS))).astype(BF16)
        wu = wi[:, o_uv:o_uv + d_gmlp].astype(BF16)
        wv = wi[:, o_uv + d_gmlp:].astype(BF16)
        dtb = jnp.pad(dt_bias[i].astype(F32), (0, LANES - SSD_HEADS)).reshape(1, LANES)
        arow = jnp.pad(-jnp.exp(a_log[i].astype(F32)), (0, LANES - SSD_HEADS)).reshape(1, LANES)
        dsk = d_skip[i].astype(F32)[head_cols].reshape(1, d_ssd)
        ws_tril = jnp.where(tril, w_spatial[i], 0.0)
        wsp_p = ws_tril.astype(BF16)
        bsp_p = jnp.repeat(b_spatial[i].T, d_gmlp // GMLP_HEADS, axis=1)
        w8 = jnp.pad(ws_tril[:, :dec_seq, :dec_seq], ((0, 0), (lead, 0), (lead, 0)))
        wsp_s = (seq_eye[None, :, None, :, None] * w8[:, None, :, None, :]).reshape(GMLP_HEADS, CHUNK, CHUNK).astype(BF16)
        b8 = jnp.pad(b_spatial[i][:, :dec_seq], ((0, 0), (lead, 0)))
        bsp_s = jnp.repeat(jnp.tile(b8, (1, CHUNK // GROUP)).T, d_gmlp // GMLP_HEADS, axis=1)
        wr = jnp.concatenate([w_router_group[i].T, w_router_expert[i].T,
                              jnp.zeros((ROUTE_ROWS - N_EXPERT_GROUPS * (1 + EXPERTS_PER_GROUP), d_model), F32)], axis=0)
        br = jnp.concatenate([b_router_group[i], b_router_expert[i],
                              jnp.zeros((ROUTE_ROWS - N_EXPERT_GROUPS * (1 + EXPERTS_PER_GROUP),), F32)]).reshape(ROUTE_ROWS, 1)
        row = lambda a: a.reshape(1, -1).astype(F32)

        z, xbc, dt, u, v = _inproj(h, row(norm_mix[i]), wz, wx, wdt, wu, wv, dtb, row(gmlp_norm[i]))

        yp, ssm_p = _mixer_prompt(z, xbc, dt, u, v, conv_w[i], row(conv_b[i]), arow, dsk, row(ssd_norm[i]),
                                  wsp_p, bsp_p, n_seq, n_chunk)
        cs = jnp.pad(state_conv[i], ((0, 0), (lead - (CONV_WIDTH - 1), dec_seq), (0, 0))).reshape(n_srows, conv_dim)
        s0 = state_ssm[i].reshape(n_dec, d_ssd, D_STATE)
        ys, ssm_s = _mixer_sample(z, xbc, dt, u, v, cs, s0, conv_w[i], row(conv_b[i]), arow, dsk, row(ssd_norm[i]),
                                  wsp_s, bsp_s, n_prompt, n_srows)

        slab, route, cnt = _post(yp, ys, h, w_out[i].astype(BF16), row(norm_ffn[i]), wr, br)
        pos, e_lo, e_hi, n_used = _routing_tables(route, cnt, n_moe_tiles)

        xs = _permute_tokens(pos, slab, n_moe_tiles * MOE_TILE, X_PITCH, scatter=True)
        h2s = _moe(e_lo, e_hi, n_used, xs, row(norm_ffn[i]), w_gate[i].astype(BF16), w_up[i].astype(BF16),
                   w_down[i].astype(BF16))
        h2 = _permute_tokens(pos, h2s, t_all, Y_PITCH, scatter=False)

        p_all = jnp.concatenate([p_prompt[i].reshape(n_prompt, -1), sample_rows(p_sample[i])], axis=0)
        final = i == depth - 1
        out = _ple(h2, p_all, w_ple[i].astype(BF16), row(norm_ple[i]), row(norm_pg[i]), w_pg[i].astype(BF16),
                   row(norm_final), final)
        if final:
            y = out
        else:
            h = out

        xbc_p = xbc[:n_prompt].reshape(n_seq, seq_len, conv_dim)
        xbc_s = xbc[n_prompt:].reshape(n_dec, GROUP, conv_dim)
        convs_p.append(xbc_p[:, seq_len - (CONV_WIDTH - 1):])
        convs_s.append(xbc_s[:, GROUP - (CONV_WIDTH - 1):])
        ssms_p.append(ssm_p.reshape(n_seq, SSD_HEADS, SSD_HEAD_DIM, D_STATE))
        ssms_s.append(ssm_s.reshape(n_dec, SSD_HEADS, SSD_HEAD_DIM, D_STATE))
        vs_s.append(v[n_prompt:].reshape(n_dec, GROUP, d_gmlp)[:, lead:])

    y_prompt = y[:n_prompt].reshape(n_seq, seq_len, d_model)
    y_sample = y[n_prompt:].reshape(n_dec, GROUP, d_model)[:, lead:]
    return (y_prompt, y_sample, jnp.stack(convs_p), jnp.stack(ssms_p), jnp.stack(convs_s), jnp.stack(ssms_s),
            jnp.stack(vs_s))
```

```python
import functools
import math

import jax
import jax.numpy as jnp
from jax import lax
from jax.experimental import pallas as pl
from jax.experimental.pallas import tpu as pltpu

F32 = jnp.float32
BF16 = jnp.bfloat16

LANES = 128
VMEM_LIMIT_BYTES = 56 * 1024 * 1024

CONV_WIDTH = 4
SSD_HEADS = 8
SSD_HEAD_DIM = 64
SSD_GROUPS = 2
D_STATE = 128
CHUNK = 128
GMLP_HEADS = 8
N_EXPERT_GROUPS = 4
EXPERTS_PER_GROUP = 4
N_PAIRS = 6
N_BUCKETS = N_EXPERT_GROUPS * N_PAIRS
EPS = 1e-6

GROUP = 8
DEC_SEQ_ROWS = 4
TOK_TILE = 512
MOE_TILE = 256
ROUTE_ROWS = 32
X_PITCH = 9
Y_PITCH = 8
NEG_BIG = -1e30


def _dot(a, b):
    return jnp.dot(a, b, preferred_element_type=F32)


def _dot_nt(a, b):
    return lax.dot_general(a, b, (((1,), (1,)), ((), ())), preferred_element_type=F32)


def _rms(x, g):
    ms = jnp.mean(x * x, axis=-1, keepdims=True)
    return (x * lax.rsqrt(ms + EPS)) * g


def _gelu(x):
    return 0.5 * x * (1.0 + lax.erf(x * (1.0 / math.sqrt(2.0))))


def _softplus(x):
    return jnp.maximum(x, 0.0) + jnp.log1p(jnp.exp(-jnp.abs(x)))


def _params(n_grid):
    return pltpu.CompilerParams(dimension_semantics=("arbitrary",) * n_grid,
                                vmem_limit_bytes=VMEM_LIMIT_BYTES)


def _pair_specs(width, n_prompt_tiles):
    return [pl.BlockSpec((TOK_TILE, width), lambda i: (jnp.minimum(i, n_prompt_tiles - 1), 0)),
            pl.BlockSpec((TOK_TILE, width), lambda i: (jnp.maximum(i - n_prompt_tiles, 0), 0))]


def _pair_load(p_ref, s_ref, n_prompt_tiles):
    return jnp.where(pl.program_id(0) < n_prompt_tiles, p_ref[...], s_ref[...])


def _inproj_kernel(hp_ref, hs_ref, nm_ref, wz_ref, wx_ref, wdt_ref, wu_ref, wv_ref, dtb_ref, gn_ref,
                   z_ref, xbc_ref, dt_ref, u_ref, v_ref, *, n_prompt_tiles):
    a = _rms(_pair_load(hp_ref, hs_ref, n_prompt_tiles), nm_ref[...]).astype(BF16)
    z_ref[...] = _dot(a, wz_ref[...])
    xbc_ref[...] = _dot(a, wx_ref[...])
    dt_ref[...] = _softplus(_dot(a, wdt_ref[...]) + dtb_ref[...])
    u_ref[...] = _gelu(_dot(a, wu_ref[...]))
    v_ref[...] = _rms(_gelu(_dot(a, wv_ref[...])), gn_ref[...])


def _inproj(hp, hs, nm, wz, wx, wdt, wu, wv, dtb, gn):
    d = hp.shape[1]
    t = hp.shape[0] + hs.shape[0]
    n_prompt_tiles = hp.shape[0] // TOK_TILE
    widths = (wz.shape[1], wx.shape[1], wdt.shape[1], wu.shape[1], wv.shape[1])
    row = lambda w: pl.BlockSpec((TOK_TILE, w), lambda i: (i, 0))
    full = lambda a: pl.BlockSpec(a.shape, lambda i: (0,) * a.ndim)
    return pl.pallas_call(
        functools.partial(_inproj_kernel, n_prompt_tiles=n_prompt_tiles),
        grid=(t // TOK_TILE,),
        in_specs=_pair_specs(d, n_prompt_tiles) + [full(nm), full(wz), full(wx), full(wdt), full(wu), full(wv),
                                                   full(dtb), full(gn)],
        out_specs=[row(w) for w in widths],
        out_shape=[jax.ShapeDtypeStruct((t, w), F32) for w in widths],
        compiler_params=_params(1),
        name="inproj",
    )(hp, hs, nm, wz, wx, wdt, wu, wv, dtb, gn)


def _seg_cumsum(x, seg, rowmod):
    d = 1
    while d < seg:
        x = x + jnp.where(rowmod >= d, pltpu.roll(x, d, axis=0), 0.0)
        d *= 2
    return x


def _seg_rev_cumsum(x, seg, rowmod):
    n = x.shape[0]
    d = 1
    while d < seg:
        x = x + jnp.where(rowmod + d < seg, pltpu.roll(x, n - d, axis=0), 0.0)
        d *= 2
    return x


def _expand_heads(m, lane_lt_half):
    parts = []
    for j in range(SSD_HEADS // 2):
        parts.append(jnp.where(lane_lt_half, m[:, 2 * j:2 * j + 1], m[:, 2 * j + 1:2 * j + 2]))
    return jnp.concatenate(parts, axis=1)


def _merge_head_pairs(per_head, lane_lt_half):
    parts = [jnp.where(lane_lt_half, per_head[2 * j], per_head[2 * j + 1]) for j in range(len(per_head) // 2)]
    return jnp.concatenate(parts, axis=1)


def _mixer_kernel(*refs, sample):
    if sample:
        (z_ref, xbc_ref, dt_ref, u_ref, v_ref, cs_ref, s0_ref, cw_ref, cb_ref, arow_ref, dsk_ref, sn_ref,
         wsp_ref, bsp_ref, y_ref, sout_ref, ext_ref, yoff_ref) = refs
        seg = GROUP
        first = pl.program_id(0) == 0
    else:
        (z_ref, xbc_ref, dt_ref, u_ref, v_ref, cw_ref, cb_ref, arow_ref, dsk_ref, sn_ref,
         wsp_ref, bsp_ref, y_ref, sout_ref, ext_ref) = refs
        seg = CHUNK
        first = pl.program_id(1) == 0
    d_ssd = SSD_HEADS * SSD_HEAD_DIM
    gw = d_ssd // SSD_GROUPS
    hpg = SSD_HEADS // SSD_GROUPS
    cs_first = GROUP - DEC_SEQ_ROWS - (CONV_WIDTH - 1)

    rows = lax.broadcasted_iota(jnp.int32, (CHUNK, LANES), 0)
    cols = lax.broadcasted_iota(jnp.int32, (CHUNK, LANES), 1)
    rowmod = rows & (seg - 1)
    lane_lt_half = cols < SSD_HEAD_DIM

    xbc = xbc_ref[...]
    if sample:
        rm = lax.broadcasted_iota(jnp.int32, xbc.shape, 0) & (GROUP - 1)
        xbc = jnp.where((rm >= cs_first) & (rm < cs_first + CONV_WIDTH - 1), cs_ref[...], xbc)

    @pl.when(first)
    def _():
        ext_ref[0:8, :] = jnp.zeros((8, xbc.shape[1]), F32)
        if not sample:
            sout_ref[...] = jnp.zeros(sout_ref.shape, F32)

    ext_ref[8:8 + CHUNK, :] = xbc
    acc = cb_ref[...] + cw_ref[CONV_WIDTH - 1:CONV_WIDTH, :] * xbc
    for j in range(1, CONV_WIDTH):
        acc = acc + cw_ref[CONV_WIDTH - 1 - j:CONV_WIDTH - j, :] * ext_ref[8 - j:8 - j + CHUNK, :]
    if not sample:
        ext_ref[0:8, :] = ext_ref[CHUNK:CHUNK + 8, :]
    xc = acc * jax.nn.sigmoid(acc)
    x = xc[:, :d_ssd]
    bb = xc[:, d_ssd:d_ssd + SSD_GROUPS * D_STATE].astype(BF16)
    cm = xc[:, d_ssd + SSD_GROUPS * D_STATE:]
    cbf = cm.astype(BF16)

    dtc = dt_ref[...]
    if sample:
        dtc = jnp.where(rowmod >= GROUP - DEC_SEQ_ROWS, dtc, 0.0)
    da = dtc * arow_ref[...]
    cum = _seg_cumsum(da, seg, rowmod)
    rev = _seg_rev_cumsum(da, seg, rowmod) - da
    cum_t = cum.T
    ecum = jnp.exp(cum)
    dt_e = _expand_heads(dtc, lane_lt_half)
    ecum_e = _expand_heads(ecum, lane_lt_half)
    erev_e = _expand_heads(jnp.exp(rev), lane_lt_half)

    xdt = x * dt_e
    xdt_bf = xdt.astype(BF16)
    causal = rows >= cols
    if sample:
        causal = causal & ((rows >> 3) == (cols >> 3))

    yd = []
    for g in range(SSD_GROUPS):
        cb_g = _dot_nt(cbf[:, g * D_STATE:(g + 1) * D_STATE], bb[:, g * D_STATE:(g + 1) * D_STATE])
        for hh in range(hpg):
            h = g * hpg + hh
            expo = cum[:, h:h + 1] - cum_t[h:h + 1, :]
            w = (cb_g * jnp.exp(jnp.where(causal, expo, NEG_BIG))).astype(BF16)
            j = h // 2
            yd.append(_dot(w, xdt_bf[:, j * LANES:(j + 1) * LANES]))
    y_diag = _merge_head_pairs(yd, lane_lt_half)

    if sample:
        for i in range(CHUNK // GROUP):
            for g in range(SSD_GROUPS):
                s_g = s0_ref[i, g * gw:(g + 1) * gw, :].astype(BF16)
                yoff_ref[i * GROUP:(i + 1) * GROUP, g * gw:(g + 1) * gw] = _dot_nt(
                    cm[i * GROUP:(i + 1) * GROUP, g * D_STATE:(g + 1) * D_STATE].astype(BF16), s_g)
        y_off = yoff_ref[...]
    else:
        y_off = jnp.concatenate(
            [_dot_nt(cbf[:, g * D_STATE:(g + 1) * D_STATE], sout_ref[0, g * gw:(g + 1) * gw, :].astype(BF16))
             for g in range(SSD_GROUPS)], axis=1)
    y = y_diag + y_off * ecum_e + dsk_ref[...] * x

    xd = xdt * erev_e
    for g in range(SSD_GROUPS):
        xd_t = xd[:, g * gw:(g + 1) * gw].T
        b_g = bb[:, g * D_STATE:(g + 1) * D_STATE]
        if sample:
            tcols = lax.broadcasted_iota(jnp.int32, xd_t.shape, 1) >> 3
            for i in range(CHUNK // GROUP):
                upd = _dot(jnp.where(tcols == i, xd_t, 0.0).astype(BF16), b_g)
                last = i * GROUP + GROUP - 1
                for hh in range(hpg):
                    h = g * hpg + hh
                    r0 = h * SSD_HEAD_DIM
                    sout_ref[i, r0:r0 + SSD_HEAD_DIM, :] = (
                        s0_ref[i, r0:r0 + SSD_HEAD_DIM, :] * ecum[last:last + 1, h:h + 1]
                        + upd[hh * SSD_HEAD_DIM:(hh + 1) * SSD_HEAD_DIM, :])
        else:
            upd = _dot(xd_t.astype(BF16), b_g)
            for hh in range(hpg):
                h = g * hpg + hh
                r0 = h * SSD_HEAD_DIM
                sout_ref[0, r0:r0 + SSD_HEAD_DIM, :] = (
                    sout_ref[0, r0:r0 + SSD_HEAD_DIM, :] * ecum[CHUNK - 1:CHUNK, h:h + 1]
                    + upd[hh * SSD_HEAD_DIM:(hh + 1) * SSD_HEAD_DIM, :])

    zf = z_ref[...]
    yf = y * (zf * jax.nn.sigmoid(zf))
    parts = []
    for g in range(SSD_GROUPS):
        part = yf[:, g * gw:(g + 1) * gw]
        ms = jnp.mean(part * part, axis=-1, keepdims=True)
        parts.append(part * lax.rsqrt(ms + EPS))
    y_ssd = jnp.concatenate(parts, axis=1) * sn_ref[...]

    vb = v_ref[...].astype(BF16)
    sg = [_dot(wsp_ref[h], vb[:, (h // 2) * LANES:(h // 2 + 1) * LANES]) for h in range(GMLP_HEADS)]
    s = _merge_head_pairs(sg, lane_lt_half) + bsp_ref[...]
    y_gm = u_ref[...] * s

    y_ref[:, :d_ssd] = y_ssd.astype(BF16)
    y_ref[:, d_ssd:] = y_gm.astype(BF16)


def _mixer_prompt(z, xbc, dt, u, v, cw, cb, arow, dsk, sn, wsp, bsp, n_seq, n_chunk):
    tok = lambda w: pl.BlockSpec((CHUNK, w), lambda b, c: (b * n_chunk + c, 0))
    full = lambda a: pl.BlockSpec(a.shape, lambda b, c: (0,) * a.ndim)
    d_ssd = z.shape[1]
    d_mix = d_ssd + u.shape[1]
    return pl.pallas_call(
        functools.partial(_mixer_kernel, sample=False),
        grid=(n_seq, n_chunk),
        in_specs=[tok(z.shape[1]), tok(xbc.shape[1]), tok(dt.shape[1]), tok(u.shape[1]), tok(v.shape[1]),
                  full(cw), full(cb), full(arow), full(dsk), full(sn), full(wsp), full(bsp)],
        out_specs=[pl.BlockSpec((CHUNK, d_mix), lambda b, c: (b * n_chunk + c, 0)),
                   pl.BlockSpec((1, d_ssd, D_STATE), lambda b, c: (b, 0, 0))],
        out_shape=[jax.ShapeDtypeStruct((n_seq * n_chunk * CHUNK, d_mix), BF16),
                   jax.ShapeDtypeStruct((n_seq, d_ssd, D_STATE), F32)],
        scratch_shapes=[pltpu.VMEM((CHUNK + 8, xbc.shape[1]), F32)],
        compiler_params=_params(2),
        name="mixer_prompt",
    )(z, xbc, dt, u, v, cw, cb, arow, dsk, sn, wsp, bsp)


def _mixer_sample(z, xbc, dt, u, v, cs, s0, cw, cb, arow, dsk, sn, wsp, bsp, row0, n_rows):
    blk0 = row0 // CHUNK
    n_seq_blk = CHUNK // GROUP
    tok = lambda w: pl.BlockSpec((CHUNK, w), lambda i: (blk0 + i, 0))
    full = lambda a: pl.BlockSpec(a.shape, lambda i: (0,) * a.ndim)
    d_ssd = z.shape[1]
    d_mix = d_ssd + u.shape[1]
    st = pl.BlockSpec((n_seq_blk, d_ssd, D_STATE), lambda i: (i, 0, 0))
    return pl.pallas_call(
        functools.partial(_mixer_kernel, sample=True),
        grid=(n_rows // CHUNK,),
        in_specs=[tok(z.shape[1]), tok(xbc.shape[1]), tok(dt.shape[1]), tok(u.shape[1]), tok(v.shape[1]),
                  pl.BlockSpec((CHUNK, cs.shape[1]), lambda i: (i, 0)), st,
                  full(cw), full(cb), full(arow), full(dsk), full(sn), full(wsp), full(bsp)],
        out_specs=[pl.BlockSpec((CHUNK, d_mix), lambda i: (i, 0)), st],
        out_shape=[jax.ShapeDtypeStruct((n_rows, d_mix), BF16),
                   jax.ShapeDtypeStruct(s0.shape, F32)],
        scratch_shapes=[pltpu.VMEM((CHUNK + 8, xbc.shape[1]), F32), pltpu.VMEM((CHUNK, d_ssd), F32)],
        compiler_params=_params(1),
        name="mixer_sample",
    )(z, xbc, dt, u, v, cs, s0, cw, cb, arow, dsk, sn, wsp, bsp)


def _first_argmax(vals):
    m = vals[0]
    for v in vals[1:]:
        m = jnp.maximum(m, v)
    idx = jnp.full(m.shape, len(vals) - 1, jnp.int32)
    for k in range(len(vals) - 2, -1, -1):
        idx = jnp.where(vals[k] >= m, k, idx)
    return m, idx


def _slab_store(slab_ref, x, pitch):
    for k in range(x.shape[1] // LANES):
        slab_ref[pl.ds(k, x.shape[0], stride=pitch), :] = x[:, k * LANES:(k + 1) * LANES]


def _slab_load(slab_ref, rows, n_pieces, pitch):
    return jnp.concatenate([slab_ref[pl.ds(k, rows, stride=pitch), :] for k in range(n_pieces)], axis=1)


def _post_kernel(yp_ref, ys_ref, hp_ref, hs_ref, wo_ref, nf_ref, wr_ref, br_ref,
                 slab_ref, route_ref, cnt_ref, carry_ref, *, n_prompt_tiles):
    i = pl.program_id(0)
    tm, d = hp_ref.shape

    @pl.when(i == 0)
    def _():
        carry_ref[...] = jnp.zeros(carry_ref.shape, F32)

    ym = _pair_load(yp_ref, ys_ref, n_prompt_tiles)
    h1 = _pair_load(hp_ref, hs_ref, n_prompt_tiles) + _dot(ym, wo_ref[...])
    _slab_store(slab_ref, h1, X_PITCH)

    t = _rms(h1, nf_ref[...])
    logits = lax.dot_general(wr_ref[...], t, (((1,), (1,)), ((), ())),
                             precision=lax.Precision.HIGHEST, preferred_element_type=F32) + br_ref[...]
    lg = [logits[k:k + 1, :] for k in range(N_EXPERT_GROUPS)]
    m, g = _first_argmax(lg)
    ssum = jnp.exp(lg[0] - m)
    for k in range(1, N_EXPERT_GROUPS):
        ssum = ssum + jnp.exp(lg[k] - m)
    p_sel = 1.0 / ssum
    le = [logits[N_EXPERT_GROUPS + e:N_EXPERT_GROUPS + e + 1, :] for e in range(N_EXPERT_GROUPS * EXPERTS_PER_GROUP)]
    a = []
    for k in range(EXPERTS_PER_GROUP):
        sel = le[(N_EXPERT_GROUPS - 1) * EXPERTS_PER_GROUP + k]
        for gi in range(N_EXPERT_GROUPS - 2, -1, -1):
            sel = jnp.where(g == gi, le[gi * EXPERTS_PER_GROUP + k], sel)
        a.append(sel)
    v1, i1 = _first_argmax(a)
    a2 = [jnp.where(i1 == k, -jnp.inf, a[k]) for k in range(EXPERTS_PER_GROUP)]
    v2, i2 = _first_argmax(a2)
    e2 = jnp.exp(v2 - v1)
    den = 1.0 + e2
    g1 = (1.0 / den) * p_sel
    g2 = (e2 / den) * p_sel
    lo = jnp.minimum(i1, i2)
    hi = jnp.maximum(i1, i2)
    c_lo = jnp.where(i1 < i2, g1, g2)
    c_hi = jnp.where(i1 < i2, g2, g1)
    pair = jnp.where(lo == 0, hi - 1, jnp.where(lo == 1, hi + 1, N_PAIRS - 1))
    bucket = g * N_PAIRS + pair

    brow = lax.broadcasted_iota(jnp.int32, (ROUTE_ROWS, tm), 0)
    onehot = jnp.where(brow == bucket, 1.0, 0.0)
    tr = lax.broadcasted_iota(jnp.int32, (tm, tm), 0)
    tc = lax.broadcasted_iota(jnp.int32, (tm, tm), 1)
    earlier = jnp.where(tr < tc, 1.0, 0.0).astype(BF16)
    prefix = _dot(onehot.astype(BF16), earlier)
    carry = carry_ref[:, 0:1]
    rank = jnp.sum(onehot * (prefix + carry), axis=0, keepdims=True)
    carry = carry + jnp.sum(onehot, axis=1, keepdims=True)
    carry_b = jnp.broadcast_to(carry, carry_ref.shape)
    carry_ref[...] = carry_b
    cnt_ref[...] = carry_b

    rr = lax.broadcasted_iota(jnp.int32, (8, tm), 0)
    route_ref[...] = jnp.where(rr == 0, bucket.astype(F32), jnp.where(rr == 1, rank, 0.0))
    ar = lax.broadcasted_iota(jnp.int32, (LANES, tm), 0)
    aux = jnp.where(ar == 0, c_lo, jnp.where(ar == 1, c_hi, 0.0))
    slab_ref[pl.ds(d // LANES, tm, stride=X_PITCH), :] = aux.T


def _post(yp, ys, hp, hs, wo, nf, wr, br):
    d = hp.shape[1]
    t = hp.shape[0] + hs.shape[0]
    n_prompt_tiles = hp.shape[0] // TOK_TILE
    full = lambda a: pl.BlockSpec(a.shape, lambda i: (0,) * a.ndim)
    return pl.pallas_call(
        functools.partial(_post_kernel, n_prompt_tiles=n_prompt_tiles),
        grid=(t // TOK_TILE,),
        in_specs=_pair_specs(d, n_prompt_tiles) + _pair_specs(d, n_prompt_tiles)
        + [full(wo), full(nf), full(wr), full(br)],
        out_specs=[pl.BlockSpec((TOK_TILE * X_PITCH, LANES), lambda i: (i, 0)),
                   pl.BlockSpec((8, TOK_TILE), lambda i: (0, i)),
                   pl.BlockSpec((ROUTE_ROWS, LANES), lambda i: (0, 0))],
        out_shape=[jax.ShapeDtypeStruct((t * X_PITCH, LANES), F32),
                   jax.ShapeDtypeStruct((8, t), F32),
                   jax.ShapeDtypeStruct((ROUTE_ROWS, LANES), F32)],
        scratch_shapes=[pltpu.VMEM((ROUTE_ROWS, LANES), F32)],
        compiler_params=_params(1),
        name="post",
    )(yp, ys, hp, hs, wo, nf, wr, br)


def _permute_kernel(*refs, scatter, chunk, pitch):
    if scatter:
        pos_ref, src_ref, _, dst_ref, sem = refs
    else:
        pos_ref, src_ref, dst_ref, sem = refs
    def token_copy(r):
        here = pl.ds(r * pitch, pitch)
        there = pl.ds(pos_ref[0, 0, r] * pitch, pitch)
        if scatter:
            return pltpu.make_async_copy(src_ref.at[here], dst_ref.at[there], sem)
        return pltpu.make_async_copy(src_ref.at[there], dst_ref.at[here], sem)

    def body(r, carry):
        token_copy(r).start()
        return carry

    lax.fori_loop(0, chunk, body, 0, unroll=8)
    whole = pl.ds(0, chunk * pitch)
    if scatter:
        pltpu.make_async_copy(src_ref, dst_ref.at[whole], sem).wait()
    else:
        pltpu.make_async_copy(src_ref.at[whole], dst_ref, sem).wait()


def _permute_tokens(pos, src, n_out, pitch, scatter):
    n = pos.shape[0]
    chunk = TOK_TILE
    pos3 = pos.reshape(n // chunk, 1, chunk)
    block = pl.BlockSpec((chunk * pitch, LANES), lambda i: (i, 0))
    hbm = pl.BlockSpec(memory_space=pl.ANY)
    in_specs = [pl.BlockSpec((1, 1, chunk), lambda i: (i, 0, 0), memory_space=pltpu.SMEM),
                block if scatter else hbm]
    args = [pos3, src]
    aliases = {}
    if scatter:
        in_specs.append(hbm)
        args.append(jnp.zeros((n_out * pitch, LANES), src.dtype))
        aliases = {2: 0}
    return pl.pallas_call(
        functools.partial(_permute_kernel, scatter=scatter, chunk=chunk, pitch=pitch),
        grid=(n // chunk,),
        in_specs=in_specs,
        out_specs=hbm if scatter else block,
        out_shape=jax.ShapeDtypeStruct((n_out * pitch, LANES), src.dtype),
        scratch_shapes=[pltpu.SemaphoreType.DMA(())],
        input_output_aliases=aliases,
        compiler_params=_params(1),
        name="scatter_tokens" if scatter else "gather_tokens",
    )(*args)


def _moe_kernel(elo_ref, ehi_ref, nused_ref, x_ref, nf_ref, wg_lo, wu_lo, wd_lo, wg_hi, wu_hi, wd_hi, o_ref):
    i = pl.program_id(0)
    n_pieces = nf_ref.shape[1] // LANES

    @pl.when(i < nused_ref[0])
    def _():
        x = _slab_load(x_ref, MOE_TILE, n_pieces, X_PITCH)
        gates = x_ref[pl.ds(n_pieces, MOE_TILE, stride=X_PITCH), :]
        c_lo = gates[:, 0:1]
        c_hi = gates[:, 1:2]
        t = _rms(x, nf_ref[...]).astype(BF16)

        def expert(wg, wu, wd):
            gate = _dot(t, wg[0])
            hid = (gate * jax.nn.sigmoid(gate)) * _dot(t, wu[0])
            return _dot(hid.astype(BF16), wd[0])

        y = c_lo * expert(wg_lo, wu_lo, wd_lo)
        y = y + c_hi * expert(wg_hi, wu_hi, wd_hi)
        _slab_store(o_ref, x + y, Y_PITCH)

    @pl.when(i >= nused_ref[0])
    def _():
        o_ref[...] = jnp.zeros(o_ref.shape, F32)


def _moe(e_lo, e_hi, n_used, xs, nf, wg, wu, wd):
    n_tiles = xs.shape[0] // (MOE_TILE * X_PITCH)
    d, de = wg.shape[1:]
    lo = lambda i, elo, ehi, nu: (elo[i], 0, 0)
    hi = lambda i, elo, ehi, nu: (ehi[i], 0, 0)
    grid_spec = pltpu.PrefetchScalarGridSpec(
        num_scalar_prefetch=3,
        grid=(n_tiles,),
        in_specs=[pl.BlockSpec((MOE_TILE * X_PITCH, LANES),
                               lambda i, elo, ehi, nu: (jnp.maximum(jnp.minimum(i, nu[0] - 1), 0), 0)),
                  pl.BlockSpec(nf.shape, lambda i, elo, ehi, nu: (0, 0)),
                  pl.BlockSpec((1, d, de), lo), pl.BlockSpec((1, d, de), lo), pl.BlockSpec((1, de, d), lo),
                  pl.BlockSpec((1, d, de), hi), pl.BlockSpec((1, d, de), hi), pl.BlockSpec((1, de, d), hi)],
        out_specs=pl.BlockSpec((MOE_TILE * Y_PITCH, LANES), lambda i, elo, ehi, nu: (i, 0)),
    )
    return pl.pallas_call(
        _moe_kernel,
        grid_spec=grid_spec,
        out_shape=jax.ShapeDtypeStruct((n_tiles * MOE_TILE * Y_PITCH, LANES), F32),
        compiler_params=_params(1),
        name="expert_pairs",
    )(e_lo, e_hi, n_used, xs, nf, wg, wu, wd, wg, wu, wd)


def _ple_kernel(h_ref, pp_ref, ps_ref, wple_ref, nple_ref, npg_ref, wpg_ref, nfin_ref, op_ref, os_ref,
                *, final, n_prompt_tiles):
    i = pl.program_id(0)
    h2 = _slab_load(h_ref, TOK_TILE, Y_PITCH, Y_PITCH)
    p = _pair_load(pp_ref, ps_ref, n_prompt_tiles)
    e = _rms(_dot(p.astype(BF16), wple_ref[...]), nple_ref[...])
    gate = jax.nn.sigmoid(_dot(_rms(h2, npg_ref[...]).astype(BF16), wpg_ref[...]))
    h3 = h2 + gate * e
    out = _rms(h3, nfin_ref[...]) if final else h3

    @pl.when(i < n_prompt_tiles)
    def _():
        op_ref[...] = out

    @pl.when(i >= n_prompt_tiles)
    def _():
        os_ref[...] = out


def _ple(h2, pp, ps, wple, nple, npg, wpg, nfin, final):
    t = h2.shape[0] // Y_PITCH
    d = Y_PITCH * LANES
    n_prompt_tiles = pp.shape[0] // TOK_TILE
    full = lambda a: pl.BlockSpec(a.shape, lambda i: (0,) * a.ndim)
    return pl.pallas_call(
        functools.partial(_ple_kernel, final=final, n_prompt_tiles=n_prompt_tiles),
        grid=(t // TOK_TILE,),
        in_specs=[pl.BlockSpec((TOK_TILE * Y_PITCH, LANES), lambda i: (i, 0))]
        + _pair_specs(pp.shape[1], n_prompt_tiles)
        + [full(wple), full(nple), full(npg), full(wpg), full(nfin)],
        out_specs=_pair_specs(d, n_prompt_tiles),
        out_shape=[jax.ShapeDtypeStruct((pp.shape[0], d), F32), jax.ShapeDtypeStruct((ps.shape[0], d), F32)],
        compiler_params=_params(1),
        name="ple",
    )(h2, pp, ps, wple, nple, npg, wpg, nfin)


_PAIR_LO = (0, 0, 0, 1, 1, 2)
_PAIR_HI = (1, 2, 3, 2, 3, 3)


def _routing_tables(route, cnt, n_tiles):
    counts = cnt[:N_BUCKETS, 0].astype(jnp.int32)
    padded = ((counts + MOE_TILE - 1) // MOE_TILE) * MOE_TILE
    ends = jnp.cumsum(padded)
    starts = ends - padded
    bucket = route[0].astype(jnp.int32)
    rank = route[1].astype(jnp.int32)
    pos = starts[bucket] + rank
    n_used = ends[-1] // MOE_TILE
    tile = jnp.minimum(jnp.arange(n_tiles, dtype=jnp.int32), n_used - 1)
    tb = jnp.sum((ends[None, :] <= (tile * MOE_TILE)[:, None]).astype(jnp.int32), axis=1)
    tb = jnp.minimum(tb, N_BUCKETS - 1)
    grp = tb // N_PAIRS
    pr = tb % N_PAIRS
    e_lo = grp * EXPERTS_PER_GROUP + jnp.asarray(_PAIR_LO, jnp.int32)[pr]
    e_hi = grp * EXPERTS_PER_GROUP + jnp.asarray(_PAIR_HI, jnp.int32)[pr]
    return pos, e_lo, e_hi, n_used.reshape(1)


def kernel(x_prompt, x_sample, state_conv, state_ssm, p_prompt, p_sample, norm_mix, w_in, conv_w, conv_b,
           dt_bias, a_log, d_skip, ssd_norm, gmlp_norm, w_spatial, b_spatial, w_out, norm_ffn,
           w_router_group, b_router_group, w_router_expert, b_router_expert, w_gate, w_up, w_down,
           w_ple, norm_ple, norm_pg, w_pg, norm_final):
    n_seq, seq_len, d_model = x_prompt.shape
    n_dec, dec_seq, _ = x_sample.shape
    depth = w_in.shape[0]
    conv_dim = conv_w.shape[2]
    d_ssd = SSD_HEADS * SSD_HEAD_DIM
    d_gmlp = gmlp_norm.shape[1]
    assert dec_seq == DEC_SEQ_ROWS and conv_w.shape[1] == CONV_WIDTH and seq_len % CHUNK == 0
    assert conv_dim == d_ssd + 2 * SSD_GROUPS * D_STATE and w_spatial.shape[1:] == (GMLP_HEADS, CHUNK, CHUNK)
    n_chunk = seq_len // CHUNK
    n_prompt = n_seq * seq_len
    n_srows = n_dec * GROUP
    t_all = n_prompt + n_srows
    assert n_prompt % TOK_TILE == 0 and n_srows % TOK_TILE == 0
    assert d_model == Y_PITCH * LANES and X_PITCH == Y_PITCH + 1
    lead = GROUP - dec_seq
    n_moe_tiles = t_all // MOE_TILE + N_BUCKETS

    def sample_rows(a):
        return jnp.pad(a, ((0, 0), (lead, 0), (0, 0))).reshape(n_srows, a.shape[-1])

    hp, hs = x_prompt.reshape(n_prompt, d_model), sample_rows(x_sample)

    o_xbc = d_ssd
    o_dt = o_xbc + conv_dim
    o_uv = o_dt + SSD_HEADS
    head_cols = jnp.arange(d_ssd) // SSD_HEAD_DIM
    seq_eye = jnp.eye(CHUNK // GROUP, dtype=F32)
    tril =jnp.tril(jnp.ones((CHUNK, CHUNK), bool))

    convs_p, ssms_p, convs_s, ssms_s, vs_s = [], [], [], [], []
    for i in range(depth):
        wi = w_in[i]
        wz = wi[:, :o_xbc].astype(BF16)
        wx = wi[:, o_xbc:o_dt].astype(BF16)
        wdt = jnp.pad(wi[:, o_dt:o_uv], ((0, 0), (0, LANES - SSD_HEADS))).astype(BF16)
        wu = wi[:, o_uv:o_uv + d_gmlp].astype(BF16)
        wv = wi[:, o_uv + d_gmlp:].astype(BF16)
        dtb = jnp.pad(dt_bias[i].astype(F32), (0, LANES - SSD_HEADS)).reshape(1, LANES)
        arow = jnp.pad(-jnp.exp(a_log[i].astype(F32)), (0, LANES - SSD_HEADS)).reshape(1, LANES)
        dsk = d_skip[i].astype(F32)[head_cols].reshape(1, d_ssd)
        ws_tril = jnp.where(tril, w_spatial[i], 0.0)
        wsp_p = ws_tril.astype(BF16)
        bsp_p = jnp.repeat(b_spatial[i].T, d_gmlp // GMLP_HEADS, axis=1)
        w8 = jnp.pad(ws_tril[:, :dec_seq, :dec_seq], ((0, 0), (lead, 0), (lead, 0)))
        wsp_s = (seq_eye[None, :, None, :, None] * w8[:, None, :, None, :]).reshape(GMLP_HEADS, CHUNK, CHUNK).astype(BF16)
        b8 = jnp.pad(b_spatial[i][:, :dec_seq], ((0, 0), (lead, 0)))
        bsp_s = jnp.repeat(jnp.tile(b8, (1, CHUNK // GROUP)).T, d_gmlp // GMLP_HEADS, axis=1)
        wr = jnp.concatenate([w_router_group[i].T, w_router_expert[i].T,
                              jnp.zeros((ROUTE_ROWS - N_EXPERT_GROUPS * (1 + EXPERTS_PER_GROUP), d_model), F32)], axis=0)
        br = jnp.concatenate([b_router_group[i], b_router_expert[i],
                              jnp.zeros((ROUTE_ROWS - N_EXPERT_GROUPS * (1 + EXPERTS_PER_GROUP),), F32)]).reshape(ROUTE_ROWS, 1)
        row = lambda a: a.reshape(1, -1).astype(F32)

        z, xbc, dt, u, v = _inproj(hp, hs, row(norm_mix[i]), wz, wx, wdt, wu, wv, dtb, row(gmlp_norm[i]))

        yp, ssm_p = _mixer_prompt(z, xbc, dt, u, v, conv_w[i], row(conv_b[i]), arow, dsk, row(ssd_norm[i]),
                                  wsp_p, bsp_p, n_seq, n_chunk)
        cs = jnp.pad(state_conv[i], ((0, 0), (lead - (CONV_WIDTH - 1), dec_seq), (0, 0))).reshape(n_srows, conv_dim)
        s0 = state_ssm[i].reshape(n_dec, d_ssd, D_STATE)
        ys, ssm_s = _mixer_sample(z, xbc, dt, u, v, cs, s0, conv_w[i], row(conv_b[i]), arow, dsk, row(ssd_norm[i]),
                                  wsp_s, bsp_s, n_prompt, n_srows)

        slab, route, cnt = _post(yp, ys, hp, hs, w_out[i].astype(BF16), row(norm_ffn[i]), wr, br)
        pos, e_lo, e_hi, n_used = _routing_tables(route, cnt, n_moe_tiles)

        xs = _permute_tokens(pos, slab, n_moe_tiles * MOE_TILE, X_PITCH, scatter=True)
        h2s = _moe(e_lo, e_hi, n_used, xs, row(norm_ffn[i]), w_gate[i].astype(BF16), w_up[i].astype(BF16),
                   w_down[i].astype(BF16))
        h2 = _permute_tokens(pos, h2s, t_all, Y_PITCH, scatter=False)

        hp, hs = _ple(h2, p_prompt[i].reshape(n_prompt, -1), sample_rows(p_sample[i]), w_ple[i].astype(BF16),
                      row(norm_ple[i]), row(norm_pg[i]), w_pg[i].astype(BF16), row(norm_final), i == depth - 1)

        tail = CONV_WIDTH - 1
        convs_p.append(jnp.stack([xbc[(b + 1) * seq_len - tail:(b + 1) * seq_len] for b in range(n_seq)]))
        xbc_s = xbc[n_prompt:].reshape(n_dec, GROUP, conv_dim)
        convs_s.append(xbc_s[:, GROUP - tail:])
        ssms_p.append(ssm_p.reshape(n_seq, SSD_HEADS, SSD_HEAD_DIM, D_STATE))
        ssms_s.append(ssm_s.reshape(n_dec, SSD_HEADS, SSD_HEAD_DIM, D_STATE))
        vs_s.append(v[n_prompt:].reshape(n_dec, GROUP, d_gmlp)[:, lead:])

    y_prompt = hp.reshape(n_seq, seq_len, d_model)
    y_sample = hs.reshape(n_dec, GROUP, d_model)[:, lead:]
    return (y_prompt, y_sample, jnp.stack(convs_p), jnp.stack(ssms_p), jnp.stack(convs_s), jnp.stack(ssms_s),
            jnp.stack(vs_s))
```

```python
import functools
import math

import jax
import jax.numpy as jnp
from jax import lax
from jax.experimental import pallas as pl
from jax.experimental.pallas import tpu as pltpu

F32 = jnp.float32
BF16 = jnp.bfloat16

LANES = 128
VMEM_LIMIT_BYTES = 56 * 1024 * 1024

CONV_WIDTH = 4
SSD_HEADS = 8
SSD_HEAD_DIM = 64
SSD_GROUPS = 2
D_STATE = 128
CHUNK = 128
GMLP_HEADS = 8
N_EXPERT_GROUPS = 4
EXPERTS_PER_GROUP = 4
N_PAIRS = 6
N_BUCKETS = N_EXPERT_GROUPS * N_PAIRS
EPS = 1e-6

GROUP = 8
DEC_SEQ_ROWS = 4
TOK_TILE = 512
MOE_TILE = 256
PROMPT_TILES_PER_STEP = 4
ROUTE_ROWS = 32
X_PITCH = 9
Y_PITCH = 8
NEG_BIG = -1e30


def _dot(a, b):
    return jnp.dot(a, b, preferred_element_type=F32)


def _dot_nt(a, b):
    return lax.dot_general(a, b, (((1,), (1,)), ((), ())), preferred_element_type=F32)


def _rms(x, g):
    ms = jnp.mean(x * x, axis=-1, keepdims=True)
    return (x * lax.rsqrt(ms + EPS)) * g


def _gelu(x):
    return 0.5 * x * (1.0 + lax.erf(x * (1.0 / math.sqrt(2.0))))


def _softplus(x):
    return jnp.maximum(x, 0.0) + jnp.log1p(jnp.exp(-jnp.abs(x)))


def _params(n_grid):
    return pltpu.CompilerParams(dimension_semantics=("arbitrary",) * n_grid,
                                vmem_limit_bytes=VMEM_LIMIT_BYTES)


def _pair_specs(width, n_prompt_tiles, prompt_tile0=0):
    return [pl.BlockSpec((TOK_TILE, width), lambda i: (prompt_tile0 + jnp.minimum(i, n_prompt_tiles - 1), 0)),
            pl.BlockSpec((TOK_TILE, width), lambda i: (jnp.maximum(i - n_prompt_tiles, 0), 0))]


def _pair_load(p_ref, s_ref, n_prompt_tiles):
    return jnp.where(pl.program_id(0) < n_prompt_tiles, p_ref[...], s_ref[...])


def _inproj_kernel(hp_ref, hs_ref, nm_ref, wz_ref, wx_ref, wdt_ref, wu_ref, wv_ref, dtb_ref, gn_ref,
                   z_ref, xbc_ref, dt_ref, u_ref, v_ref, *, n_prompt_tiles):
    a = _rms(_pair_load(hp_ref, hs_ref, n_prompt_tiles), nm_ref[...]).astype(BF16)
    z_ref[...] = _dot(a, wz_ref[...])
    xbc_ref[...] = _dot(a, wx_ref[...])
    dt_ref[...] = _softplus(_dot(a, wdt_ref[...]) + dtb_ref[...])
    u_ref[...] = _gelu(_dot(a, wu_ref[...]))
    v_ref[...] = _rms(_gelu(_dot(a, wv_ref[...])), gn_ref[...])


def _inproj(hp, hs, nm, wz, wx, wdt, wu, wv, dtb, gn):
    d = hp.shape[1]
    t = hp.shape[0] + hs.shape[0]
    n_prompt_tiles = hp.shape[0] // TOK_TILE
    widths = (wz.shape[1], wx.shape[1], wdt.shape[1], wu.shape[1], wv.shape[1])
    row = lambda w: pl.BlockSpec((TOK_TILE, w), lambda i: (i, 0))
    full = lambda a: pl.BlockSpec(a.shape, lambda i: (0,) * a.ndim)
    return pl.pallas_call(
        functools.partial(_inproj_kernel, n_prompt_tiles=n_prompt_tiles),
        grid=(t // TOK_TILE,),
        in_specs=_pair_specs(d, n_prompt_tiles) + [full(nm), full(wz), full(wx), full(wdt), full(wu), full(wv),
                                                   full(dtb), full(gn)],
        out_specs=[row(w) for w in widths],
        out_shape=[jax.ShapeDtypeStruct((t, w), F32) for w in widths],
        compiler_params=_params(1),
        name="inproj",
    )(hp, hs, nm, wz, wx, wdt, wu, wv, dtb, gn)


def _seg_cumsum(x, seg, rowmod):
    d = 1
    while d < seg:
        x = x + jnp.where(rowmod >= d, pltpu.roll(x, d, axis=0), 0.0)
        d *= 2
    return x


def _seg_rev_cumsum(x, seg, rowmod):
    n = x.shape[0]
    d = 1
    while d < seg:
        x = x + jnp.where(rowmod + d < seg, pltpu.roll(x, n - d, axis=0), 0.0)
        d *= 2
    return x


def _expand_heads(m, lane_lt_half):
    parts = []
    for j in range(SSD_HEADS // 2):
        parts.append(jnp.where(lane_lt_half, m[:, 2 * j:2 * j + 1], m[:, 2 * j + 1:2 * j + 2]))
    return jnp.concatenate(parts, axis=1)


def _merge_head_pairs(per_head, lane_lt_half):
    parts = [jnp.where(lane_lt_half, per_head[2 * j], per_head[2 * j + 1]) for j in range(len(per_head) // 2)]
    return jnp.concatenate(parts, axis=1)


def _mixer_kernel(*refs, sample, n_inner):
    n_tok = 6 if sample else 5
    tok_refs, rest = refs[:n_tok], refs[n_tok:]
    if sample:
        (s0_ref, cw_ref, cb_ref, arow_ref, dsk_ref, sn_ref, wsp_ref, bsp_ref,
         y_ref, sout_ref, ext_ref, yoff_ref) = rest
        first = pl.program_id(0) == 0
    else:
        (cw_ref, cb_ref, arow_ref, dsk_ref, sn_ref, wsp_ref, bsp_ref, y_ref, sout_ref, ext_ref) = rest
        s0_ref = yoff_ref = None
        first = pl.program_id(1) == 0

    @pl.when(first)
    def _():
        ext_ref[0:8, :] = jnp.zeros((8, ext_ref.shape[1]), F32)
        if not sample:
            sout_ref[...] = jnp.zeros(sout_ref.shape, F32)

    def tile(c, carry):
        r0 = pl.multiple_of(c * CHUNK, CHUNK)
        views = [r.at[pl.ds(r0, CHUNK)] for r in tok_refs + (y_ref,)]
        _mixer_tile(*views[:n_tok], s0_ref, cw_ref, cb_ref, arow_ref, dsk_ref, sn_ref, wsp_ref, bsp_ref,
                    views[n_tok], sout_ref, ext_ref, yoff_ref, sample=sample)
        return carry

    if n_inner == 1:
        tile(0, 0)
    else:
        lax.fori_loop(0, n_inner, tile, 0)


def _mixer_tile(z_ref, xbc_ref, dt_ref, u_ref, v_ref, *rest, sample):
    if sample:
        cs_ref, s0_ref = rest[0], rest[1]
        rest = rest[2:]
    else:
        rest = rest[1:]
    cw_ref, cb_ref, arow_ref, dsk_ref, sn_ref, wsp_ref, bsp_ref, y_ref, sout_ref, ext_ref, yoff_ref = rest
    seg = GROUP if sample else CHUNK
    d_ssd = SSD_HEADS * SSD_HEAD_DIM
    gw = d_ssd // SSD_GROUPS
    hpg = SSD_HEADS // SSD_GROUPS
    cs_first = GROUP - DEC_SEQ_ROWS - (CONV_WIDTH - 1)

    rows = lax.broadcasted_iota(jnp.int32, (CHUNK, LANES), 0)
    cols = lax.broadcasted_iota(jnp.int32, (CHUNK, LANES), 1)
    rowmod = rows & (seg - 1)
    lane_lt_half = cols < SSD_HEAD_DIM

    xbc = xbc_ref[...]
    if sample:
        rm = lax.broadcasted_iota(jnp.int32, xbc.shape, 0) & (GROUP - 1)
        xbc = jnp.where((rm >= cs_first) & (rm < cs_first + CONV_WIDTH - 1), cs_ref[...], xbc)

    ext_ref[8:8 + CHUNK, :] = xbc
    acc = cb_ref[...] + cw_ref[CONV_WIDTH - 1:CONV_WIDTH, :] * xbc
    for j in range(1, CONV_WIDTH):
        acc = acc + cw_ref[CONV_WIDTH - 1 - j:CONV_WIDTH - j, :] * ext_ref[8 - j:8 - j + CHUNK, :]
    if not sample:
        ext_ref[0:8, :] = ext_ref[CHUNK:CHUNK + 8, :]
    xc = acc * jax.nn.sigmoid(acc)
    x = xc[:, :d_ssd]
    bb = xc[:, d_ssd:d_ssd + SSD_GROUPS * D_STATE].astype(BF16)
    cm = xc[:, d_ssd + SSD_GROUPS * D_STATE:]
    cbf = cm.astype(BF16)

    dtc = dt_ref[...]
    if sample:
        dtc = jnp.where(rowmod >= GROUP - DEC_SEQ_ROWS, dtc, 0.0)
    da = dtc * arow_ref[...]
    cum = _seg_cumsum(da, seg, rowmod)
    rev = _seg_rev_cumsum(da, seg, rowmod) - da
    cum_t = cum.T
    ecum = jnp.exp(cum)
    dt_e = _expand_heads(dtc, lane_lt_half)
    ecum_e = _expand_heads(ecum, lane_lt_half)
    erev_e = _expand_heads(jnp.exp(rev), lane_lt_half)

    xdt = x * dt_e
    xdt_bf = xdt.astype(BF16)
    causal = rows >= cols
    if sample:
        causal = causal & ((rows >> 3) == (cols >> 3))

    yd = []
    for g in range(SSD_GROUPS):
        cb_g = _dot_nt(cbf[:, g * D_STATE:(g + 1) * D_STATE], bb[:, g * D_STATE:(g + 1) * D_STATE])
        for hh in range(hpg):
            h = g * hpg + hh
            expo = cum[:, h:h + 1] - cum_t[h:h + 1, :]
            w = (cb_g * jnp.exp(jnp.where(causal, expo, NEG_BIG))).astype(BF16)
            j = h // 2
            yd.append(_dot(w, xdt_bf[:, j * LANES:(j + 1) * LANES]))
    y_diag = _merge_head_pairs(yd, lane_lt_half)

    if sample:
        for i in range(CHUNK // GROUP):
            for g in range(SSD_GROUPS):
                s_g = s0_ref[i, g * gw:(g + 1) * gw, :].astype(BF16)
                yoff_ref[i * GROUP:(i + 1) * GROUP, g * gw:(g + 1) * gw] = _dot_nt(
                    cm[i * GROUP:(i + 1) * GROUP, g * D_STATE:(g + 1) * D_STATE].astype(BF16), s_g)
        y_off = yoff_ref[...]
    else:
        y_off = jnp.concatenate(
            [_dot_nt(cbf[:, g * D_STATE:(g + 1) * D_STATE], sout_ref[0, g * gw:(g + 1) * gw, :].astype(BF16))
             for g in range(SSD_GROUPS)], axis=1)
    y = y_diag + y_off * ecum_e + dsk_ref[...] * x

    xd = xdt * erev_e
    for g in range(SSD_GROUPS):
        xd_t = xd[:, g * gw:(g + 1) * gw].T
        b_g = bb[:, g * D_STATE:(g + 1) * D_STATE]
        if sample:
            tcols = lax.broadcasted_iota(jnp.int32, xd_t.shape, 1) >> 3
            for i in range(CHUNK // GROUP):
                upd = _dot(jnp.where(tcols == i, xd_t, 0.0).astype(BF16), b_g)
                last = i * GROUP + GROUP - 1
                for hh in range(hpg):
                    h = g * hpg + hh
                    r0 = h * SSD_HEAD_DIM
                    sout_ref[i, r0:r0 + SSD_HEAD_DIM, :] = (
                        s0_ref[i, r0:r0 + SSD_HEAD_DIM, :] * ecum[last:last + 1, h:h + 1]
                        + upd[hh * SSD_HEAD_DIM:(hh + 1) * SSD_HEAD_DIM, :])
        else:
            upd = _dot(xd_t.astype(BF16), b_g)
            for hh in range(hpg):
                h = g * hpg + hh
                r0 = h * SSD_HEAD_DIM
                sout_ref[0, r0:r0 + SSD_HEAD_DIM, :] = (
                    sout_ref[0, r0:r0 + SSD_HEAD_DIM, :] * ecum[CHUNK - 1:CHUNK, h:h + 1]
                    + upd[hh * SSD_HEAD_DIM:(hh + 1) * SSD_HEAD_DIM, :])

    zf = z_ref[...]
    yf = y * (zf * jax.nn.sigmoid(zf))
    parts = []
    for g in range(SSD_GROUPS):
        part = yf[:, g * gw:(g + 1) * gw]
        ms = jnp.mean(part * part, axis=-1, keepdims=True)
        parts.append(part * lax.rsqrt(ms + EPS))
    y_ssd = jnp.concatenate(parts, axis=1) * sn_ref[...]

    vb = v_ref[...].astype(BF16)
    sg = [_dot(wsp_ref[h], vb[:, (h // 2) * LANES:(h // 2 + 1) * LANES]) for h in range(GMLP_HEADS)]
    s = _merge_head_pairs(sg, lane_lt_half) + bsp_ref[...]
    y_gm = u_ref[...] * s

    y_ref[:, :d_ssd] = y_ssd.astype(BF16)
    y_ref[:, d_ssd:] = y_gm.astype(BF16)


def _mixer_prompt(z, xbc, dt, u, v, cw, cb, arow, dsk, sn, wsp, bsp, n_seq, n_chunk):
    n_inner = math.gcd(n_chunk, PROMPT_TILES_PER_STEP)
    n_outer = n_chunk // n_inner
    tok = lambda w: pl.BlockSpec((n_inner * CHUNK, w), lambda b, c: (b * n_outer + c, 0))
    full = lambda a: pl.BlockSpec(a.shape, lambda b, c: (0,) * a.ndim)
    d_ssd = z.shape[1]
    d_mix = d_ssd + u.shape[1]
    return pl.pallas_call(
        functools.partial(_mixer_kernel, sample=False, n_inner=n_inner),
        grid=(n_seq, n_outer),
        in_specs=[tok(z.shape[1]), tok(xbc.shape[1]), tok(dt.shape[1]), tok(u.shape[1]), tok(v.shape[1]),
                  full(cw), full(cb), full(arow), full(dsk), full(sn), full(wsp), full(bsp)],
        out_specs=[tok(d_mix),
                   pl.BlockSpec((1, d_ssd, D_STATE), lambda b, c: (b, 0, 0))],
        out_shape=[jax.ShapeDtypeStruct((n_seq * n_chunk * CHUNK, d_mix), BF16),
                   jax.ShapeDtypeStruct((n_seq, d_ssd, D_STATE), F32)],
        scratch_shapes=[pltpu.VMEM((CHUNK + 8, xbc.shape[1]), F32)],
        compiler_params=_params(2),
        name="mixer_prompt",
    )(z, xbc, dt, u, v, cw, cb, arow, dsk, sn, wsp, bsp)


def _mixer_sample(z, xbc, dt, u, v, cs, s0_all, cw, cb, arow, dsk, sn, wsp, bsp, row0, n_rows, layer):
    blk0 = row0 // CHUNK
    n_seq_blk = CHUNK // GROUP
    n_blk = n_rows // CHUNK
    tok = lambda w: pl.BlockSpec((CHUNK, w), lambda i: (blk0 + i, 0))
    full = lambda a: pl.BlockSpec(a.shape, lambda i: (0,) * a.ndim)
    d_ssd = z.shape[1]
    d_mix = d_ssd + u.shape[1]
    st = (n_seq_blk, d_ssd, D_STATE)
    return pl.pallas_call(
        functools.partial(_mixer_kernel, sample=True, n_inner=1),
        grid=(n_blk,),
        in_specs=[tok(z.shape[1]), tok(xbc.shape[1]), tok(dt.shape[1]), tok(u.shape[1]), tok(v.shape[1]),
                  pl.BlockSpec((CHUNK, cs.shape[1]), lambda i: (i, 0)),
                  pl.BlockSpec(st, lambda i: (layer * n_blk + i, 0, 0)),
                  full(cw), full(cb), full(arow), full(dsk), full(sn), full(wsp), full(bsp)],
        out_specs=[pl.BlockSpec((CHUNK, d_mix), lambda i: (i, 0)), pl.BlockSpec(st, lambda i: (i, 0, 0))],
        out_shape=[jax.ShapeDtypeStruct((n_rows, d_mix), BF16),
                   jax.ShapeDtypeStruct((n_blk * n_seq_blk, d_ssd, D_STATE), F32)],
        scratch_shapes=[pltpu.VMEM((CHUNK + 8, xbc.shape[1]), F32), pltpu.VMEM((CHUNK, d_ssd), F32)],
        compiler_params=_params(1),
        name="mixer_sample",
    )(z, xbc, dt, u, v, cs, s0_all, cw, cb, arow, dsk, sn, wsp, bsp)


def _first_argmax(vals):
    m = vals[0]
    for v in vals[1:]:
        m = jnp.maximum(m, v)
    idx = jnp.full(m.shape, len(vals) - 1, jnp.int32)
    for k in range(len(vals) - 2, -1, -1):
        idx = jnp.where(vals[k] >= m, k, idx)
    return m, idx


def _slab_store(slab_ref, x, pitch):
    for k in range(x.shape[1] // LANES):
        slab_ref[pl.ds(k, x.shape[0], stride=pitch), :] = x[:, k * LANES:(k + 1) * LANES]


def _slab_load(slab_ref, rows, n_pieces, pitch):
    return jnp.concatenate([slab_ref[pl.ds(k, rows, stride=pitch), :] for k in range(n_pieces)], axis=1)


def _post_kernel(yp_ref, ys_ref, hp_ref, hs_ref, wo_ref, nf_ref, wr_ref, br_ref,
                 slab_ref, route_ref, cnt_ref, carry_ref, *, n_prompt_tiles):
    i = pl.program_id(0)
    tm, d = hp_ref.shape

    @pl.when(i == 0)
    def _():
        carry_ref[...] = jnp.zeros(carry_ref.shape, F32)

    ym = _pair_load(yp_ref, ys_ref, n_prompt_tiles)
    h1 = _pair_load(hp_ref, hs_ref, n_prompt_tiles) + _dot(ym, wo_ref[...])
    _slab_store(slab_ref, h1, X_PITCH)

    t = _rms(h1, nf_ref[...])
    logits = lax.dot_general(wr_ref[...], t, (((1,), (1,)), ((), ())),
                             precision=lax.Precision.HIGHEST, preferred_element_type=F32) + br_ref[...]
    lg = [logits[k:k + 1, :] for k in range(N_EXPERT_GROUPS)]
    m, g = _first_argmax(lg)
    ssum = jnp.exp(lg[0] - m)
    for k in range(1, N_EXPERT_GROUPS):
        ssum = ssum + jnp.exp(lg[k] - m)
    p_sel = 1.0 / ssum
    le = [logits[N_EXPERT_GROUPS + e:N_EXPERT_GROUPS + e + 1, :] for e in range(N_EXPERT_GROUPS * EXPERTS_PER_GROUP)]
    a = []
    for k in range(EXPERTS_PER_GROUP):
        sel = le[(N_EXPERT_GROUPS - 1) * EXPERTS_PER_GROUP + k]
        for gi in range(N_EXPERT_GROUPS - 2, -1, -1):
            sel = jnp.where(g == gi, le[gi * EXPERTS_PER_GROUP + k], sel)
        a.append(sel)
    v1, i1 = _first_argmax(a)
    a2 = [jnp.where(i1 == k, -jnp.inf, a[k]) for k in range(EXPERTS_PER_GROUP)]
    v2, i2 = _first_argmax(a2)
    e2 = jnp.exp(v2 - v1)
    den = 1.0 + e2
    g1 = (1.0 / den) * p_sel
    g2 = (e2 / den) * p_sel
    lo = jnp.minimum(i1, i2)
    hi = jnp.maximum(i1, i2)
    c_lo = jnp.where(i1 < i2, g1, g2)
    c_hi = jnp.where(i1 < i2, g2, g1)
    pair = jnp.where(lo == 0, hi - 1, jnp.where(lo == 1, hi + 1, N_PAIRS - 1))
    bucket = g * N_PAIRS + pair

    brow = lax.broadcasted_iota(jnp.int32, (ROUTE_ROWS, tm), 0)
    onehot = jnp.where(brow == bucket, 1.0, 0.0)
    tr = lax.broadcasted_iota(jnp.int32, (tm, tm), 0)
    tc = lax.broadcasted_iota(jnp.int32, (tm, tm), 1)
    earlier = jnp.where(tr < tc, 1.0, 0.0).astype(BF16)
    prefix = _dot(onehot.astype(BF16), earlier)
    carry = carry_ref[:, 0:1]
    rank = jnp.sum(onehot * (prefix + carry), axis=0, keepdims=True)
    carry = carry + jnp.sum(onehot, axis=1, keepdims=True)
    carry_b = jnp.broadcast_to(carry, carry_ref.shape)
    carry_ref[...] = carry_b
    cnt_ref[...] = carry_b

    rr = lax.broadcasted_iota(jnp.int32, (8, tm), 0)
    route_ref[...] = jnp.where(rr == 0, bucket.astype(F32), jnp.where(rr == 1, rank, 0.0))
    ar = lax.broadcasted_iota(jnp.int32, (LANES, tm), 0)
    aux = jnp.where(ar == 0, c_lo, jnp.where(ar == 1, c_hi, 0.0))
    slab_ref[pl.ds(d // LANES, tm, stride=X_PITCH), :] = aux.T


def _post(yp, ys, hp, hs, wo, nf, wr, br):
    d = hp.shape[1]
    t = hp.shape[0] + hs.shape[0]
    n_prompt_tiles = hp.shape[0] // TOK_TILE
    full = lambda a: pl.BlockSpec(a.shape, lambda i: (0,) * a.ndim)
    return pl.pallas_call(
        functools.partial(_post_kernel, n_prompt_tiles=n_prompt_tiles),
        grid=(t // TOK_TILE,),
        in_specs=_pair_specs(d, n_prompt_tiles) + _pair_specs(d, n_prompt_tiles)
        + [full(wo), full(nf), full(wr), full(br)],
        out_specs=[pl.BlockSpec((TOK_TILE * X_PITCH, LANES), lambda i: (i, 0)),
                   pl.BlockSpec((8, TOK_TILE), lambda i: (0, i)),
                   pl.BlockSpec((ROUTE_ROWS, LANES), lambda i: (0, 0))],
        out_shape=[jax.ShapeDtypeStruct((t * X_PITCH, LANES), F32),
                   jax.ShapeDtypeStruct((8, t), F32),
                   jax.ShapeDtypeStruct((ROUTE_ROWS, LANES), F32)],
        scratch_shapes=[pltpu.VMEM((ROUTE_ROWS, LANES), F32)],
        compiler_params=_params(1),
        name="post",
    )(yp, ys, hp, hs, wo, nf, wr, br)


def _permute_kernel(*refs, scatter, chunk, pitch):
    if scatter:
        pos_ref, src_ref, _, dst_ref, sem = refs
    else:
        pos_ref, src_ref, dst_ref, sem = refs
    def token_copy(r):
        here = pl.ds(r * pitch, pitch)
        there = pl.ds(pos_ref[0, 0, r] * pitch, pitch)
        if scatter:
            return pltpu.make_async_copy(src_ref.at[here], dst_ref.at[there], sem)
        return pltpu.make_async_copy(src_ref.at[there], dst_ref.at[here], sem)

    def body(r, carry):
        token_copy(r).start()
        return carry

    lax.fori_loop(0, chunk, body, 0, unroll=8)
    whole = pl.ds(0, chunk * pitch)
    if scatter:
        pltpu.make_async_copy(src_ref, dst_ref.at[whole], sem).wait()
    else:
        pltpu.make_async_copy(src_ref.at[whole], dst_ref, sem).wait()


def _permute_tokens(pos, src, n_out, pitch, scatter):
    n = pos.shape[0]
    chunk = TOK_TILE
    pos3 = pos.reshape(n // chunk, 1, chunk)
    block = pl.BlockSpec((chunk * pitch, LANES), lambda i: (i, 0))
    hbm = pl.BlockSpec(memory_space=pl.ANY)
    in_specs = [pl.BlockSpec((1, 1, chunk), lambda i: (i, 0, 0), memory_space=pltpu.SMEM),
                block if scatter else hbm]
    args = [pos3, src]
    aliases = {}
    if scatter:
        in_specs.append(hbm)
        args.append(jnp.zeros((n_out * pitch, LANES), src.dtype))
        aliases = {2: 0}
    return pl.pallas_call(
        functools.partial(_permute_kernel, scatter=scatter, chunk=chunk, pitch=pitch),
        grid=(n // chunk,),
        in_specs=in_specs,
        out_specs=hbm if scatter else block,
        out_shape=jax.ShapeDtypeStruct((n_out * pitch, LANES), src.dtype),
        scratch_shapes=[pltpu.SemaphoreType.DMA(())],
        input_output_aliases=aliases,
        compiler_params=_params(1),
        name="scatter_tokens" if scatter else "gather_tokens",
    )(*args)


def _moe_kernel(elo_ref, ehi_ref, nused_ref, x_ref, nf_ref, wg_lo, wu_lo, wd_lo, wg_hi, wu_hi, wd_hi, o_ref):
    i = pl.program_id(0)
    n_pieces = nf_ref.shape[1] // LANES

    @pl.when(i < nused_ref[0])
    def _():
        x = _slab_load(x_ref, MOE_TILE, n_pieces, X_PITCH)
        gates = x_ref[pl.ds(n_pieces, MOE_TILE, stride=X_PITCH), :]
        c_lo = gates[:, 0:1]
        c_hi = gates[:, 1:2]
        t = _rms(x, nf_ref[...]).astype(BF16)

        def expert(wg, wu, wd):
            gate = _dot(t, wg[0])
            hid = (gate * jax.nn.sigmoid(gate)) * _dot(t, wu[0])
            return _dot(hid.astype(BF16), wd[0])

        y = c_lo * expert(wg_lo, wu_lo, wd_lo)
        y = y + c_hi * expert(wg_hi, wu_hi, wd_hi)
        _slab_store(o_ref, x + y, Y_PITCH)

    @pl.when(i >= nused_ref[0])
    def _():
        o_ref[...] = jnp.zeros(o_ref.shape, F32)


def _moe(e_lo, e_hi, n_used, xs, nf, wg, wu, wd):
    n_tiles = xs.shape[0] // (MOE_TILE * X_PITCH)
    d, de = wg.shape[1:]
    lo = lambda i, elo, ehi, nu: (elo[i], 0, 0)
    hi = lambda i, elo, ehi, nu: (ehi[i], 0, 0)
    grid_spec = pltpu.PrefetchScalarGridSpec(
        num_scalar_prefetch=3,
        grid=(n_tiles,),
        in_specs=[pl.BlockSpec((MOE_TILE * X_PITCH, LANES),
                               lambda i, elo, ehi, nu: (jnp.maximum(jnp.minimum(i, nu[0] - 1), 0), 0)),
                  pl.BlockSpec(nf.shape, lambda i, elo, ehi, nu: (0, 0)),
                  pl.BlockSpec((1, d, de), lo), pl.BlockSpec((1, d, de), lo), pl.BlockSpec((1, de, d), lo),
                  pl.BlockSpec((1, d, de), hi), pl.BlockSpec((1, d, de), hi), pl.BlockSpec((1, de, d), hi)],
        out_specs=pl.BlockSpec((MOE_TILE * Y_PITCH, LANES), lambda i, elo, ehi, nu: (i, 0)),
    )
    return pl.pallas_call(
        _moe_kernel,
        grid_spec=grid_spec,
        out_shape=jax.ShapeDtypeStruct((n_tiles * MOE_TILE * Y_PITCH, LANES), F32),
        compiler_params=_params(1),
        name="expert_pairs",
    )(e_lo, e_hi, n_used, xs, nf, wg, wu, wd, wg, wu, wd)


def _ple_kernel(h_ref, pp_ref, ps_ref, wple_ref, nple_ref, npg_ref, wpg_ref, nfin_ref, op_ref, os_ref,
                *, final, n_prompt_tiles):
    i = pl.program_id(0)
    h2 = _slab_load(h_ref, TOK_TILE, Y_PITCH, Y_PITCH)
    p = _pair_load(pp_ref, ps_ref, n_prompt_tiles)
    e = _rms(_dot(p.astype(BF16), wple_ref[...]), nple_ref[...])
    gate = jax.nn.sigmoid(_dot(_rms(h2, npg_ref[...]).astype(BF16), wpg_ref[...]))
    h3 = h2 + gate * e
    out = _rms(h3, nfin_ref[...]) if final else h3

    @pl.when(i < n_prompt_tiles)
    def _():
        op_ref[...] = out

    @pl.when(i >= n_prompt_tiles)
    def _():
        os_ref[...] = out


def _ple(h2, pp_all, ps, wple, nple, npg, wpg, nfin, final, layer, n_prompt):
    t = h2.shape[0] // Y_PITCH
    d = Y_PITCH * LANES
    n_prompt_tiles = n_prompt // TOK_TILE
    full = lambda a: pl.BlockSpec(a.shape, lambda i: (0,) * a.ndim)
    return pl.pallas_call(
        functools.partial(_ple_kernel, final=final, n_prompt_tiles=n_prompt_tiles),
        grid=(t // TOK_TILE,),
        in_specs=[pl.BlockSpec((TOK_TILE * Y_PITCH, LANES), lambda i: (i, 0))]
        + _pair_specs(pp_all.shape[1], n_prompt_tiles, layer * n_prompt_tiles)
        + [full(wple), full(nple), full(npg), full(wpg), full(nfin)],
        out_specs=_pair_specs(d, n_prompt_tiles),
        out_shape=[jax.ShapeDtypeStruct((n_prompt, d), F32), jax.ShapeDtypeStruct((ps.shape[0], d), F32)],
        compiler_params=_params(1),
        name="ple",
    )(h2, pp_all, ps, wple, nple, npg, wpg, nfin)


_PAIR_LO = (0, 0, 0, 1, 1, 2)
_PAIR_HI = (1, 2, 3, 2, 3, 3)


def _routing_tables(route, cnt, n_tiles):
    counts = cnt[:N_BUCKETS, 0].astype(jnp.int32)
    padded = ((counts + MOE_TILE - 1) // MOE_TILE) * MOE_TILE
    ends = jnp.cumsum(padded)
    starts = ends - padded
    bucket = route[0].astype(jnp.int32)
    rank = route[1].astype(jnp.int32)
    pos = starts[bucket] + rank
    n_used = ends[-1] // MOE_TILE
    tile = jnp.minimum(jnp.arange(n_tiles, dtype=jnp.int32), n_used - 1)
    tb = jnp.sum((ends[None, :] <= (tile * MOE_TILE)[:, None]).astype(jnp.int32), axis=1)
    tb = jnp.minimum(tb, N_BUCKETS - 1)
    grp = tb // N_PAIRS
    pr = tb % N_PAIRS
    e_lo = grp * EXPERTS_PER_GROUP + jnp.asarray(_PAIR_LO, jnp.int32)[pr]
    e_hi = grp * EXPERTS_PER_GROUP + jnp.asarray(_PAIR_HI, jnp.int32)[pr]
    return pos, e_lo, e_hi, n_used.reshape(1)


def kernel(x_prompt, x_sample, state_conv, state_ssm, p_prompt, p_sample, norm_mix, w_in, conv_w, conv_b,
           dt_bias, a_log, d_skip, ssd_norm, gmlp_norm, w_spatial, b_spatial, w_out, norm_ffn,
           w_router_group, b_router_group, w_router_expert, b_router_expert, w_gate, w_up, w_down,
           w_ple, norm_ple, norm_pg, w_pg, norm_final):
    n_seq, seq_len, d_model = x_prompt.shape
    n_dec, dec_seq, _ = x_sample.shape
    depth = w_in.shape[0]
    conv_dim = conv_w.shape[2]
    d_ssd = SSD_HEADS * SSD_HEAD_DIM
    d_gmlp = gmlp_norm.shape[1]
    assert dec_seq == DEC_SEQ_ROWS and conv_w.shape[1] == CONV_WIDTH and seq_len % CHUNK == 0
    assert conv_dim == d_ssd + 2 * SSD_GROUPS * D_STATE and w_spatial.shape[1:] == (GMLP_HEADS, CHUNK, CHUNK)
    n_chunk = seq_len // CHUNK
    n_prompt = n_seq * seq_len
    n_srows = n_dec * GROUP
    t_all = n_prompt + n_srows
    assert n_prompt % TOK_TILE == 0 and n_srows % TOK_TILE == 0
    assert d_model == Y_PITCH * LANES and X_PITCH == Y_PITCH + 1
    lead = GROUP - dec_seq
    n_moe_tiles = t_all // MOE_TILE + N_BUCKETS

    def sample_rows(a):
        return jnp.pad(a, ((0, 0), (lead, 0), (0, 0))).reshape(n_srows, a.shape[-1])

    hp, hs = x_prompt.reshape(n_prompt, d_model), sample_rows(x_sample)

    o_xbc = d_ssd
    o_dt = o_xbc + conv_dim
    o_uv = o_dt + SSD_HEADS
    head_cols = jnp.arange(d_ssd) // SSD_HEAD_DIM
    seq_eye = jnp.eye(CHUNK // GROUP, dtype=F32)
    tril =jnp.tril(jnp.ones((CHUNK, CHUNK), bool))

    n_exp = w_gate.shape[1]
    wg_all = w_gate.astype(BF16).reshape((depth * n_exp,) + w_gate.shape[2:])
    wu_all = w_up.astype(BF16).reshape((depth * n_exp,) + w_up.shape[2:])
    wd_all = w_down.astype(BF16).reshape((depth * n_exp,) + w_down.shape[2:])
    s0_all = state_ssm.reshape(depth * n_dec, d_ssd, D_STATE)
    pp_all = p_prompt.reshape(depth * n_prompt, p_prompt.shape[-1])

    convs_p, ssms_p, convs_s, ssms_s, vs_s = [], [], [], [], []
    for i in range(depth):
        wi = w_in[i]
        wz = wi[:, :o_xbc].astype(BF16)
        wx = wi[:, o_xbc:o_dt].astype(BF16)
        wdt = jnp.pad(wi[:, o_dt:o_uv], ((0, 0), (0, LANES - SSD_HEADS))).astype(BF16)
        wu = wi[:, o_uv:o_uv + d_gmlp].astype(BF16)
        wv = wi[:, o_uv + d_gmlp:].astype(BF16)
        dtb = jnp.pad(dt_bias[i].astype(F32), (0, LANES - SSD_HEADS)).reshape(1, LANES)
        arow = jnp.pad(-jnp.exp(a_log[i].astype(F32)), (0, LANES - SSD_HEADS)).reshape(1, LANES)
        dsk = d_skip[i].astype(F32)[head_cols].reshape(1, d_ssd)
        ws_tril = jnp.where(tril, w_spatial[i], 0.0)
        wsp_p = ws_tril.astype(BF16)
        bsp_p = jnp.repeat(b_spatial[i].T, d_gmlp // GMLP_HEADS, axis=1)
        w8 = jnp.pad(ws_tril[:, :dec_seq, :dec_seq], ((0, 0), (lead, 0), (lead, 0)))
        wsp_s = (seq_eye[None, :, None, :, None] * w8[:, None, :, None, :]).reshape(GMLP_HEADS, CHUNK, CHUNK).astype(BF16)
        b8 = jnp.pad(b_spatial[i][:, :dec_seq], ((0, 0), (lead, 0)))
        bsp_s = jnp.repeat(jnp.tile(b8, (1, CHUNK // GROUP)).T, d_gmlp // GMLP_HEADS, axis=1)
        wr = jnp.concatenate([w_router_group[i].T, w_router_expert[i].T,
                              jnp.zeros((ROUTE_ROWS - N_EXPERT_GROUPS * (1 + EXPERTS_PER_GROUP), d_model), F32)], axis=0)
        br = jnp.concatenate([b_router_group[i], b_router_expert[i],
                              jnp.zeros((ROUTE_ROWS - N_EXPERT_GROUPS * (1 + EXPERTS_PER_GROUP),), F32)]).reshape(ROUTE_ROWS, 1)
        row = lambda a: a.reshape(1, -1).astype(F32)

        z, xbc, dt, u, v = _inproj(hp, hs, row(norm_mix[i]), wz, wx, wdt, wu, wv, dtb, row(gmlp_norm[i]))

        yp, ssm_p = _mixer_prompt(z, xbc, dt, u, v, conv_w[i], row(conv_b[i]), arow, dsk, row(ssd_norm[i]),
                                  wsp_p, bsp_p, n_seq, n_chunk)
        cs = jnp.pad(state_conv[i], ((0, 0), (lead - (CONV_WIDTH - 1), dec_seq), (0, 0))).reshape(n_srows, conv_dim)
        ys, ssm_s = _mixer_sample(z, xbc, dt, u, v, cs, s0_all, conv_w[i], row(conv_b[i]), arow, dsk,
                                  row(ssd_norm[i]), wsp_s, bsp_s, n_prompt, n_srows, i)

        slab, route, cnt = _post(yp, ys, hp, hs, w_out[i].astype(BF16), row(norm_ffn[i]), wr, br)
        pos, e_lo, e_hi, n_used = _routing_tables(route, cnt, n_moe_tiles)

        xs = _permute_tokens(pos, slab, n_moe_tiles * MOE_TILE, X_PITCH, scatter=True)
        h2s = _moe(e_lo + i * n_exp, e_hi + i * n_exp, n_used, xs, row(norm_ffn[i]), wg_all, wu_all, wd_all)
        h2 = _permute_tokens(pos, h2s, t_all, Y_PITCH, scatter=False)

        hp, hs = _ple(h2, pp_all, sample_rows(p_sample[i]), w_ple[i].astype(BF16), row(norm_ple[i]),
                      row(norm_pg[i]), w_pg[i].astype(BF16), row(norm_final), i == depth - 1, i, n_prompt)

        tail = CONV_WIDTH - 1
        convs_p.append(jnp.stack([xbc[(b + 1) * seq_len - tail:(b + 1) * seq_len] for b in range(n_seq)]))
        xbc_s = xbc[n_prompt:].reshape(n_dec, GROUP, conv_dim)
        convs_s.append(xbc_s[:, GROUP - tail:])
        ssms_p.append(ssm_p.reshape(n_seq, SSD_HEADS, SSD_HEAD_DIM, D_STATE))
        ssms_s.append(ssm_s.reshape(n_dec, SSD_HEADS, SSD_HEAD_DIM, D_STATE))
        vs_s.append(v[n_prompt:].reshape(n_dec, GROUP, d_gmlp)[:, lead:])

    y_prompt = hp.reshape(n_seq, seq_len, d_model)
    y_sample = hs.reshape(n_dec, GROUP, d_model)[:, lead:]
    return (y_prompt, y_sample, jnp.stack(convs_p), jnp.stack(ssms_p), jnp.stack(convs_s), jnp.stack(ssms_s),
            jnp.stack(vs_s))
```

```python
import functools
import math

import jax
import jax.numpy as jnp
from jax import lax
from jax.experimental import pallas as pl
from jax.experimental.pallas import tpu as pltpu

F32 = jnp.float32
BF16 = jnp.bfloat16

LANES = 128
VMEM_LIMIT_BYTES = 56 * 1024 * 1024

CONV_WIDTH = 4
SSD_HEADS = 8
SSD_HEAD_DIM = 64
SSD_GROUPS = 2
D_STATE = 128
CHUNK = 128
GMLP_HEADS = 8
N_EXPERT_GROUPS = 4
EXPERTS_PER_GROUP = 4
N_PAIRS = 6
N_BUCKETS = N_EXPERT_GROUPS * N_PAIRS
EPS = 1e-6

GROUP = 8
DEC_SEQ_ROWS = 4
TOK_TILE = 512
MOE_TILE = 256
PROMPT_TILES_PER_STEP = 4
ROUTE_ROWS = 32
X_PITCH = 9
Y_PITCH = 8
NEG_BIG = -1e30


def _dot(a, b):
    return jnp.dot(a, b, preferred_element_type=F32)


def _dot_nt(a, b):
    return lax.dot_general(a, b, (((1,), (1,)), ((), ())), preferred_element_type=F32)


def _rms(x, g):
    ms = jnp.mean(x * x, axis=-1, keepdims=True)
    return (x * lax.rsqrt(ms + EPS)) * g


def _gelu(x):
    return 0.5 * x * (1.0 + lax.erf(x * (1.0 / math.sqrt(2.0))))


def _sigmoid(x):
    return 0.5 * jnp.tanh(0.5 * x) + 0.5


def _softplus(x):
    return jnp.maximum(x, 0.0) + jnp.log1p(jnp.exp(-jnp.abs(x)))


def _params(n_grid):
    return pltpu.CompilerParams(dimension_semantics=("arbitrary",) * n_grid,
                                vmem_limit_bytes=VMEM_LIMIT_BYTES)


def _pair_specs(width, n_prompt_tiles, prompt_tile0=0):
    return [pl.BlockSpec((TOK_TILE, width), lambda i: (prompt_tile0 + jnp.minimum(i, n_prompt_tiles - 1), 0)),
            pl.BlockSpec((TOK_TILE, width), lambda i: (jnp.maximum(i - n_prompt_tiles, 0), 0))]


def _pair_load(p_ref, s_ref, n_prompt_tiles):
    return jnp.where(pl.program_id(0) < n_prompt_tiles, p_ref[...], s_ref[...])


def _inproj_tile(h, nm_ref, wz_ref, wx_ref, wdt_ref, wu_ref, wv_ref, dtb_ref, gn_ref,
                 z_ref, xbc_ref, dt_ref, u_ref, v_ref):
    a = _rms(h, nm_ref[...]).astype(BF16)
    z_ref[...] = _dot(a, wz_ref[...])
    xbc_ref[...] = _dot(a, wx_ref[...])
    dt_ref[...] = _softplus(_dot(a, wdt_ref[...]) + dtb_ref[...])
    u_ref[...] = _gelu(_dot(a, wu_ref[...]))
    v_ref[...] = _rms(_gelu(_dot(a, wv_ref[...])), gn_ref[...])


def _inproj_kernel(hp_ref, hs_ref, *refs, n_prompt_tiles):
    _inproj_tile(_pair_load(hp_ref, hs_ref, n_prompt_tiles), *refs)


def _inproj_specs(weights, t):
    nm, wz, wx, wdt, wu, wv, dtb, gn = weights
    widths = (wz.shape[1], wx.shape[1], wdt.shape[1], wu.shape[1], wv.shape[1])
    row = lambda w: pl.BlockSpec((TOK_TILE, w), lambda i: (i, 0))
    full = lambda a: pl.BlockSpec(a.shape, lambda i: (0,) * a.ndim)
    return ([full(a) for a in weights], [row(w) for w in widths],
            [jax.ShapeDtypeStruct((t, w), F32) for w in widths])


def _inproj(hp, hs, weights):
    t = hp.shape[0] + hs.shape[0]
    n_prompt_tiles = hp.shape[0] // TOK_TILE
    w_specs, out_specs, out_shape = _inproj_specs(weights, t)
    return pl.pallas_call(
        functools.partial(_inproj_kernel, n_prompt_tiles=n_prompt_tiles),
        grid=(t // TOK_TILE,),
        in_specs=_pair_specs(hp.shape[1], n_prompt_tiles) + w_specs,
        out_specs=out_specs,
        out_shape=out_shape,
        compiler_params=_params(1),
        name="inproj",
    )(hp, hs, *weights)


def _seg_cumsum(x, seg, rowmod):
    d = 1
    while d < seg:
        x = x + jnp.where(rowmod >= d, pltpu.roll(x, d, axis=0), 0.0)
        d *= 2
    return x


def _seg_rev_cumsum(x, seg, rowmod):
    n = x.shape[0]
    d = 1
    while d < seg:
        x = x + jnp.where(rowmod + d < seg, pltpu.roll(x, n - d, axis=0), 0.0)
        d *= 2
    return x


def _expand_heads(m, lane_lt_half):
    parts = []
    for j in range(SSD_HEADS // 2):
        parts.append(jnp.where(lane_lt_half, m[:, 2 * j:2 * j + 1], m[:, 2 * j + 1:2 * j + 2]))
    return jnp.concatenate(parts, axis=1)


def _merge_head_pairs(per_head, lane_lt_half):
    parts = [jnp.where(lane_lt_half, per_head[2 * j], per_head[2 * j + 1]) for j in range(len(per_head) // 2)]
    return jnp.concatenate(parts, axis=1)


def _mixer_kernel(*refs, sample, n_inner):
    n_tok = 6 if sample else 5
    tok_refs, rest = refs[:n_tok], refs[n_tok:]
    if sample:
        (s0_ref, cw_ref, cb_ref, arow_ref, dsk_ref, sn_ref, wsp_ref, bsp_ref,
         y_ref, sout_ref, ext_ref, yoff_ref) = rest
        first = pl.program_id(0) == 0
    else:
        (cw_ref, cb_ref, arow_ref, dsk_ref, sn_ref, wsp_ref, bsp_ref, y_ref, sout_ref, ext_ref) = rest
        s0_ref = yoff_ref = None
        first = pl.program_id(1) == 0

    @pl.when(first)
    def _():
        ext_ref[0:8, :] = jnp.zeros((8, ext_ref.shape[1]), F32)
        if not sample:
            sout_ref[...] = jnp.zeros(sout_ref.shape, F32)

    def tile(c, carry):
        r0 = pl.multiple_of(c * CHUNK, CHUNK)
        views = [r.at[pl.ds(r0, CHUNK)] for r in tok_refs + (y_ref,)]
        _mixer_tile(*views[:n_tok], s0_ref, cw_ref, cb_ref, arow_ref, dsk_ref, sn_ref, wsp_ref, bsp_ref,
                    views[n_tok], sout_ref, ext_ref, yoff_ref, sample=sample)
        return carry

    if n_inner == 1:
        tile(0, 0)
    else:
        lax.fori_loop(0, n_inner, tile, 0)


def _mixer_tile(z_ref, xbc_ref, dt_ref, u_ref, v_ref, *rest, sample):
    if sample:
        cs_ref, s0_ref = rest[0], rest[1]
        rest = rest[2:]
    else:
        rest = rest[1:]
    cw_ref, cb_ref, arow_ref, dsk_ref, sn_ref, wsp_ref, bsp_ref, y_ref, sout_ref, ext_ref, yoff_ref = rest
    seg = GROUP if sample else CHUNK
    d_ssd = SSD_HEADS * SSD_HEAD_DIM
    gw = d_ssd // SSD_GROUPS
    hpg = SSD_HEADS // SSD_GROUPS
    cs_first = GROUP - DEC_SEQ_ROWS - (CONV_WIDTH - 1)

    rows = lax.broadcasted_iota(jnp.int32, (CHUNK, LANES), 0)
    cols = lax.broadcasted_iota(jnp.int32, (CHUNK, LANES), 1)
    rowmod = rows & (seg - 1)
    lane_lt_half = cols < SSD_HEAD_DIM

    xbc = xbc_ref[...]
    if sample:
        rm = lax.broadcasted_iota(jnp.int32, xbc.shape, 0) & (GROUP - 1)
        xbc = jnp.where((rm >= cs_first) & (rm < cs_first + CONV_WIDTH - 1), cs_ref[...], xbc)

    ext_ref[8:8 + CHUNK, :] = xbc
    acc = cb_ref[...] + cw_ref[CONV_WIDTH - 1:CONV_WIDTH, :] * xbc
    for j in range(1, CONV_WIDTH):
        acc = acc + cw_ref[CONV_WIDTH - 1 - j:CONV_WIDTH - j, :] * ext_ref[8 - j:8 - j + CHUNK, :]
    if not sample:
        ext_ref[0:8, :] = ext_ref[CHUNK:CHUNK + 8, :]
    xc = acc * _sigmoid(acc)
    x = xc[:, :d_ssd]
    bb = xc[:, d_ssd:d_ssd + SSD_GROUPS * D_STATE].astype(BF16)
    cm = xc[:, d_ssd + SSD_GROUPS * D_STATE:]
    cbf = cm.astype(BF16)

    dtc = dt_ref[...]
    if sample:
        dtc = jnp.where(rowmod >= GROUP - DEC_SEQ_ROWS, dtc, 0.0)
    da = dtc * arow_ref[...]
    cum = _seg_cumsum(da, seg, rowmod)
    rev = _seg_rev_cumsum(da, seg, rowmod) - da
    cum_t = cum.T
    ecum = jnp.exp(cum)
    dt_e = _expand_heads(dtc, lane_lt_half)
    ecum_e = _expand_heads(ecum, lane_lt_half)
    erev_e = _expand_heads(jnp.exp(rev), lane_lt_half)

    xdt = x * dt_e
    xdt_bf = xdt.astype(BF16)
    causal = rows >= cols
    if sample:
        causal = causal & ((rows >> 3) == (cols >> 3))

    yd = []
    for g in range(SSD_GROUPS):
        cb_g = _dot_nt(cbf[:, g * D_STATE:(g + 1) * D_STATE], bb[:, g * D_STATE:(g + 1) * D_STATE])
        for hh in range(hpg):
            h = g * hpg + hh
            expo = cum[:, h:h + 1] - cum_t[h:h + 1, :]
            w = (cb_g * jnp.exp(jnp.where(causal, expo, NEG_BIG))).astype(BF16)
            j = h // 2
            yd.append(_dot(w, xdt_bf[:, j * LANES:(j + 1) * LANES]))
    y_diag = _merge_head_pairs(yd, lane_lt_half)

    if sample:
        for i in range(CHUNK // GROUP):
            for g in range(SSD_GROUPS):
                s_g = s0_ref[i, g * gw:(g + 1) * gw, :].astype(BF16)
                yoff_ref[i * GROUP:(i + 1) * GROUP, g * gw:(g + 1) * gw] = _dot_nt(
                    cm[i * GROUP:(i + 1) * GROUP, g * D_STATE:(g + 1) * D_STATE].astype(BF16), s_g)
        y_off = yoff_ref[...]
    else:
        y_off = jnp.concatenate(
            [_dot_nt(cbf[:, g * D_STATE:(g + 1) * D_STATE], sout_ref[0, g * gw:(g + 1) * gw, :].astype(BF16))
             for g in range(SSD_GROUPS)], axis=1)
    y = y_diag + y_off * ecum_e + dsk_ref[...] * x

    xd = xdt * erev_e
    for g in range(SSD_GROUPS):
        xd_t = xd[:, g * gw:(g + 1) * gw].T
        b_g = bb[:, g * D_STATE:(g + 1) * D_STATE]
        if sample:
            tcols = lax.broadcasted_iota(jnp.int32, xd_t.shape, 1) >> 3
            for i in range(CHUNK // GROUP):
                upd = _dot(jnp.where(tcols == i, xd_t, 0.0).astype(BF16), b_g)
                last = i * GROUP + GROUP - 1
                for hh in range(hpg):
                    h = g * hpg + hh
                    r0 = h * SSD_HEAD_DIM
                    sout_ref[i, r0:r0 + SSD_HEAD_DIM, :] = (
                        s0_ref[i, r0:r0 + SSD_HEAD_DIM, :] * ecum[last:last + 1, h:h + 1]
                        + upd[hh * SSD_HEAD_DIM:(hh + 1) * SSD_HEAD_DIM, :])
        else:
            upd = _dot(xd_t.astype(BF16), b_g)
            for hh in range(hpg):
                h = g * hpg + hh
                r0 = h * SSD_HEAD_DIM
                sout_ref[0, r0:r0 + SSD_HEAD_DIM, :] = (
                    sout_ref[0, r0:r0 + SSD_HEAD_DIM, :] * ecum[CHUNK - 1:CHUNK, h:h + 1]
                    + upd[hh * SSD_HEAD_DIM:(hh + 1) * SSD_HEAD_DIM, :])

    zf = z_ref[...]
    yf = y * (zf * _sigmoid(zf))
    parts = []
    for g in range(SSD_GROUPS):
        part = yf[:, g * gw:(g + 1) * gw]
        ms = jnp.mean(part * part, axis=-1, keepdims=True)
        parts.append(part * lax.rsqrt(ms + EPS))
    y_ssd = jnp.concatenate(parts, axis=1) * sn_ref[...]

    vb = v_ref[...].astype(BF16)
    sg = [_dot(wsp_ref[h], vb[:, (h // 2) * LANES:(h // 2 + 1) * LANES]) for h in range(GMLP_HEADS)]
    s = _merge_head_pairs(sg, lane_lt_half) + bsp_ref[...]
    y_gm = u_ref[...] * s

    y_ref[:, :d_ssd] = y_ssd.astype(BF16)
    y_ref[:, d_ssd:] = y_gm.astype(BF16)


def _mixer_prompt(z, xbc, dt, u, v, cw, cb, arow, dsk, sn, wsp, bsp, n_seq, n_chunk):
    n_inner = math.gcd(n_chunk, PROMPT_TILES_PER_STEP)
    n_outer = n_chunk // n_inner
    tok = lambda w: pl.BlockSpec((n_inner * CHUNK, w), lambda b, c: (b * n_outer + c, 0))
    full = lambda a: pl.BlockSpec(a.shape, lambda b, c: (0,) * a.ndim)
    d_ssd = z.shape[1]
    d_mix = d_ssd + u.shape[1]
    return pl.pallas_call(
        functools.partial(_mixer_kernel, sample=False, n_inner=n_inner),
        grid=(n_seq, n_outer),
        in_specs=[tok(z.shape[1]), tok(xbc.shape[1]), tok(dt.shape[1]), tok(u.shape[1]), tok(v.shape[1]),
                  full(cw), full(cb), full(arow), full(dsk), full(sn), full(wsp), full(bsp)],
        out_specs=[tok(d_mix),
                   pl.BlockSpec((1, d_ssd, D_STATE), lambda b, c: (b, 0, 0))],
        out_shape=[jax.ShapeDtypeStruct((n_seq * n_chunk * CHUNK, d_mix), BF16),
                   jax.ShapeDtypeStruct((n_seq, d_ssd, D_STATE), F32)],
        scratch_shapes=[pltpu.VMEM((CHUNK + 8, xbc.shape[1]), F32)],
        compiler_params=_params(2),
        name="mixer_prompt",
    )(z, xbc, dt, u, v, cw, cb, arow, dsk, sn, wsp, bsp)


def _mixer_sample(z, xbc, dt, u, v, cs, s0_all, cw, cb, arow, dsk, sn, wsp, bsp, row0, n_rows, layer):
    blk0 = row0 // CHUNK
    n_seq_blk = CHUNK // GROUP
    n_blk = n_rows // CHUNK
    tok = lambda w: pl.BlockSpec((CHUNK, w), lambda i: (blk0 + i, 0))
    full = lambda a: pl.BlockSpec(a.shape, lambda i: (0,) * a.ndim)
    d_ssd = z.shape[1]
    d_mix = d_ssd + u.shape[1]
    st = (n_seq_blk, d_ssd, D_STATE)
    return pl.pallas_call(
        functools.partial(_mixer_kernel, sample=True, n_inner=1),
        grid=(n_blk,),
        in_specs=[tok(z.shape[1]), tok(xbc.shape[1]), tok(dt.shape[1]), tok(u.shape[1]), tok(v.shape[1]),
                  pl.BlockSpec((CHUNK, cs.shape[1]), lambda i: (i, 0)),
                  pl.BlockSpec(st, lambda i: (layer * n_blk + i, 0, 0)),
                  full(cw), full(cb), full(arow), full(dsk), full(sn), full(wsp), full(bsp)],
        out_specs=[pl.BlockSpec((CHUNK, d_mix), lambda i: (i, 0)), pl.BlockSpec(st, lambda i: (i, 0, 0))],
        out_shape=[jax.ShapeDtypeStruct((n_rows, d_mix), BF16),
                   jax.ShapeDtypeStruct((n_blk * n_seq_blk, d_ssd, D_STATE), F32)],
        scratch_shapes=[pltpu.VMEM((CHUNK + 8, xbc.shape[1]), F32), pltpu.VMEM((CHUNK, d_ssd), F32)],
        compiler_params=_params(1),
        name="mixer_sample",
    )(z, xbc, dt, u, v, cs, s0_all, cw, cb, arow, dsk, sn, wsp, bsp)


def _first_argmax(vals):
    m = vals[0]
    for v in vals[1:]:
        m = jnp.maximum(m, v)
    idx = jnp.full(m.shape, len(vals) - 1, jnp.int32)
    for k in range(len(vals) - 2, -1, -1):
        idx = jnp.where(vals[k] >= m, k, idx)
    return m, idx


def _slab_store(slab_ref, x, pitch, tok0=0):
    for k in range(x.shape[1] // LANES):
        slab_ref[pl.ds(tok0 * pitch + k, x.shape[0], stride=pitch), :] = x[:, k * LANES:(k + 1) * LANES]


def _slab_load(slab_ref, rows, n_pieces, pitch, tok0=0):
    return jnp.concatenate([slab_ref[pl.ds(tok0 * pitch + k, rows, stride=pitch), :] for k in range(n_pieces)],
                           axis=1)


def _post_kernel(yp_ref, ys_ref, hp_ref, hs_ref, wo_ref, nf_ref, wr_ref, br_ref,
                 slab_ref, route_ref, cnt_ref, carry_ref, *, n_prompt_tiles):
    i = pl.program_id(0)
    tm, d = hp_ref.shape

    @pl.when(i == 0)
    def _():
        carry_ref[...] = jnp.zeros(carry_ref.shape, F32)

    ym = _pair_load(yp_ref, ys_ref, n_prompt_tiles)
    h1 = _pair_load(hp_ref, hs_ref, n_prompt_tiles) + _dot(ym, wo_ref[...])
    _slab_store(slab_ref, h1, X_PITCH)

    t = _rms(h1, nf_ref[...])
    wr = wr_ref[...]
    wr_hi = wr.astype(BF16)
    wr_lo = (wr - wr_hi.astype(F32)).astype(BF16)
    t_hi = t.astype(BF16)
    t_lo = (t - t_hi.astype(F32)).astype(BF16)
    by_hi = _dot_nt(jnp.concatenate([wr_hi, wr_lo], axis=0), t_hi)
    logits = (by_hi[:ROUTE_ROWS] + (by_hi[ROUTE_ROWS:] + _dot_nt(wr_hi, t_lo))) + br_ref[...]
    lg = [logits[k:k + 1, :] for k in range(N_EXPERT_GROUPS)]
    m, g = _first_argmax(lg)
    ssum = jnp.exp(lg[0] - m)
    for k in range(1, N_EXPERT_GROUPS):
        ssum = ssum + jnp.exp(lg[k] - m)
    p_sel = 1.0 / ssum
    le = [logits[N_EXPERT_GROUPS + e:N_EXPERT_GROUPS + e + 1, :] for e in range(N_EXPERT_GROUPS * EXPERTS_PER_GROUP)]
    a = []
    for k in range(EXPERTS_PER_GROUP):
        sel = le[(N_EXPERT_GROUPS - 1) * EXPERTS_PER_GROUP + k]
        for gi in range(N_EXPERT_GROUPS - 2, -1, -1):
            sel = jnp.where(g == gi, le[gi * EXPERTS_PER_GROUP + k], sel)
        a.append(sel)
    v1, i1 = _first_argmax(a)
    a2 = [jnp.where(i1 == k, -jnp.inf, a[k]) for k in range(EXPERTS_PER_GROUP)]
    v2, i2 = _first_argmax(a2)
    e2 = jnp.exp(v2 - v1)
    den = 1.0 + e2
    g1 = (1.0 / den) * p_sel
    g2 = (e2 / den) * p_sel
    lo = jnp.minimum(i1, i2)
    hi = jnp.maximum(i1, i2)
    c_lo = jnp.where(i1 < i2, g1, g2)
    c_hi = jnp.where(i1 < i2, g2, g1)
    pair = jnp.where(lo == 0, hi - 1, jnp.where(lo == 1, hi + 1, N_PAIRS - 1))
    bucket = g * N_PAIRS + pair

    brow = lax.broadcasted_iota(jnp.int32, (ROUTE_ROWS, tm), 0)
    onehot = jnp.where(brow == bucket, 1.0, 0.0)
    tr = lax.broadcasted_iota(jnp.int32, (tm, tm), 0)
    tc = lax.broadcasted_iota(jnp.int32, (tm, tm), 1)
    earlier = jnp.where(tr < tc, 1.0, 0.0).astype(BF16)
    prefix = _dot(onehot.astype(BF16), earlier)
    carry = carry_ref[:, 0:1]
    rank = jnp.sum(onehot * (prefix + carry), axis=0, keepdims=True)
    carry = carry + jnp.sum(onehot, axis=1, keepdims=True)
    carry_b = jnp.broadcast_to(carry, carry_ref.shape)
    carry_ref[...] = carry_b
    cnt_ref[...] = carry_b

    rr = lax.broadcasted_iota(jnp.int32, (8, tm), 0)
    route_ref[...] = jnp.where(rr == 0, bucket.astype(F32), jnp.where(rr == 1, rank, 0.0))
    ar = lax.broadcasted_iota(jnp.int32, (LANES, tm), 0)
    aux = jnp.where(ar == 0, c_lo, jnp.where(ar == 1, c_hi, 0.0))
    slab_ref[pl.ds(d // LANES, tm, stride=X_PITCH), :] = aux.T


def _post(yp, ys, hp, hs, wo, nf, wr, br):
    d = hp.shape[1]
    t = hp.shape[0] + hs.shape[0]
    n_prompt_tiles = hp.shape[0] // TOK_TILE
    full = lambda a: pl.BlockSpec(a.shape, lambda i: (0,) * a.ndim)
    return pl.pallas_call(
        functools.partial(_post_kernel, n_prompt_tiles=n_prompt_tiles),
        grid=(t // TOK_TILE,),
        in_specs=_pair_specs(d, n_prompt_tiles) + _pair_specs(d, n_prompt_tiles)
        + [full(wo), full(nf), full(wr), full(br)],
        out_specs=[pl.BlockSpec((TOK_TILE * X_PITCH, LANES), lambda i: (i, 0)),
                   pl.BlockSpec((8, TOK_TILE), lambda i: (0, i)),
                   pl.BlockSpec((ROUTE_ROWS, LANES), lambda i: (0, 0))],
        out_shape=[jax.ShapeDtypeStruct((t * X_PITCH, LANES), F32),
                   jax.ShapeDtypeStruct((8, t), F32),
                   jax.ShapeDtypeStruct((ROUTE_ROWS, LANES), F32)],
        scratch_shapes=[pltpu.VMEM((ROUTE_ROWS, LANES), F32)],
        compiler_params=_params(1),
        name="post",
    )(yp, ys, hp, hs, wo, nf, wr, br)


def _permute_kernel(*refs, scatter, chunk, pitch):
    if scatter:
        pos_ref, src_ref, _, dst_ref, sem = refs
    else:
        pos_ref, src_ref, dst_ref, sem = refs
    def token_copy(r):
        here = pl.ds(r * pitch, pitch)
        there = pl.ds(pos_ref[0, 0, r] * pitch, pitch)
        if scatter:
            return pltpu.make_async_copy(src_ref.at[here], dst_ref.at[there], sem)
        return pltpu.make_async_copy(src_ref.at[there], dst_ref.at[here], sem)

    def body(r, carry):
        token_copy(r).start()
        return carry

    lax.fori_loop(0, chunk, body, 0, unroll=8)
    whole = pl.ds(0, chunk * pitch)
    if scatter:
        pltpu.make_async_copy(src_ref, dst_ref.at[whole], sem).wait()
    else:
        pltpu.make_async_copy(src_ref.at[whole], dst_ref, sem).wait()


def _permute_tokens(pos, src, n_out, pitch, scatter, init=None):
    n = pos.shape[0]
    chunk = TOK_TILE
    pos3 = pos.reshape(n // chunk, 1, chunk)
    block = pl.BlockSpec((chunk * pitch, LANES), lambda i: (i, 0))
    hbm = pl.BlockSpec(memory_space=pl.ANY)
    in_specs = [pl.BlockSpec((1, 1, chunk), lambda i: (i, 0, 0), memory_space=pltpu.SMEM),
                block if scatter else hbm]
    args = [pos3, src]
    aliases = {}
    if scatter:
        in_specs.append(hbm)
        args.append(jnp.zeros((n_out * pitch, LANES), src.dtype) if init is None else init)
        aliases = {2: 0}
    return pl.pallas_call(
        functools.partial(_permute_kernel, scatter=scatter, chunk=chunk, pitch=pitch),
        grid=(n // chunk,),
        in_specs=in_specs,
        out_specs=hbm if scatter else block,
        out_shape=jax.ShapeDtypeStruct((n_out * pitch, LANES), src.dtype),
        scratch_shapes=[pltpu.SemaphoreType.DMA(())],
        input_output_aliases=aliases,
        compiler_params=_params(1),
        name="scatter_tokens" if scatter else "gather_tokens",
    )(*args)


def _moe_kernel(elo_ref, ehi_ref, nused_ref, x_ref, nf_ref, wg_lo, wu_lo, wd_lo, wg_hi, wu_hi, wd_hi, o_ref):
    i = pl.program_id(0)
    n_pieces = nf_ref.shape[1] // LANES

    @pl.when(i < nused_ref[0])
    def _():
        x = _slab_load(x_ref, MOE_TILE, n_pieces, X_PITCH)
        gates = x_ref[pl.ds(n_pieces, MOE_TILE, stride=X_PITCH), :]
        c_lo = gates[:, 0:1]
        c_hi = gates[:, 1:2]
        t = _rms(x, nf_ref[...]).astype(BF16)

        def expert(wg, wu, wd):
            gate = _dot(t, wg[...])
            hid = (gate * _sigmoid(gate)) * _dot(t, wu[...])
            return _dot(hid.astype(BF16), wd[...])

        y = c_lo * expert(wg_lo, wu_lo, wd_lo)
        y = y + c_hi * expert(wg_hi, wu_hi, wd_hi)
        _slab_store(o_ref, x + y, Y_PITCH)

    @pl.when(i >= nused_ref[0])
    def _():
        o_ref[...] = jnp.zeros(o_ref.shape, F32)


def _moe(e_lo, e_hi, n_used, xs, nf, wg, wu, wd):
    n_tiles = xs.shape[0] // (MOE_TILE * X_PITCH)
    d, de = wg.shape[1:]
    lo = lambda i, elo, ehi, nu: (elo[i], 0, 0)
    hi = lambda i, elo, ehi, nu: (ehi[i], 0, 0)
    grid_spec = pltpu.PrefetchScalarGridSpec(
        num_scalar_prefetch=3,
        grid=(n_tiles,),
        in_specs=[pl.BlockSpec((MOE_TILE * X_PITCH, LANES),
                               lambda i, elo, ehi, nu: (jnp.maximum(jnp.minimum(i, nu[0] - 1), 0), 0)),
                  pl.BlockSpec(nf.shape, lambda i, elo, ehi, nu: (0, 0)),
                  pl.BlockSpec((None, d, de), lo), pl.BlockSpec((None, d, de), lo), pl.BlockSpec((None, de, d), lo),
                  pl.BlockSpec((None, d, de), hi), pl.BlockSpec((None, d, de), hi), pl.BlockSpec((None, de, d), hi)],
        out_specs=pl.BlockSpec((MOE_TILE * Y_PITCH, LANES), lambda i, elo, ehi, nu: (i, 0)),
    )
    return pl.pallas_call(
        _moe_kernel,
        grid_spec=grid_spec,
        out_shape=jax.ShapeDtypeStruct((n_tiles * MOE_TILE * Y_PITCH, LANES), F32),
        compiler_params=_params(1),
        name="expert_pairs",
    )(e_lo, e_hi, n_used, xs, nf, wg, wu, wd, wg, wu, wd)


def _ple_kernel(h_ref, pp_ref, ps_ref, wple_ref, nple_ref, npg_ref, wpg_ref, *refs, final, n_prompt_tiles):
    i = pl.program_id(0)
    h2 = _slab_load(h_ref, TOK_TILE, Y_PITCH, Y_PITCH)
    p = _pair_load(pp_ref, ps_ref, n_prompt_tiles)
    e = _rms(_dot(p.astype(BF16), wple_ref[...]), nple_ref[...])
    gate = _sigmoid(_dot(_rms(h2, npg_ref[...]).astype(BF16), wpg_ref[...]))
    h3 = h2 + gate * e
    if final:
        nfin_ref, op_ref, os_ref = refs
        out = _rms(h3, nfin_ref[...])
    else:
        op_ref, os_ref = refs[N_INPROJ_PARAMS:N_INPROJ_PARAMS + 2]
        out = h3
        _inproj_tile(h3, *refs[:N_INPROJ_PARAMS], *refs[N_INPROJ_PARAMS + 2:])

    @pl.when(i < n_prompt_tiles)
    def _():
        op_ref[...] = out

    @pl.when(i >= n_prompt_tiles)
    def _():
        os_ref[...] = out


N_INPROJ_PARAMS = 8


def _ple(h2, pp_all, ps, wple, nple, npg, wpg, tail, final, layer, n_prompt):
    t = h2.shape[0] // Y_PITCH
    d = Y_PITCH * LANES
    n_prompt_tiles = n_prompt // TOK_TILE
    full = lambda a: pl.BlockSpec(a.shape, lambda i: (0,) * a.ndim)
    out_specs = _pair_specs(d, n_prompt_tiles)
    out_shape = [jax.ShapeDtypeStruct((n_prompt, d), F32), jax.ShapeDtypeStruct((ps.shape[0], d), F32)]
    if final:
        tail = (tail,)
        tail_specs = [full(tail[0])]
    else:
        assert len(tail) == N_INPROJ_PARAMS
        tail_specs, io, ish = _inproj_specs(tail, t)
        out_specs, out_shape = out_specs + io, out_shape + ish
    return pl.pallas_call(
        functools.partial(_ple_kernel, final=final, n_prompt_tiles=n_prompt_tiles),
        grid=(t // TOK_TILE,),
        in_specs=[pl.BlockSpec((TOK_TILE * Y_PITCH, LANES), lambda i: (i, 0))]
        + _pair_specs(pp_all.shape[1], n_prompt_tiles, layer * n_prompt_tiles)
        + [full(wple), full(nple), full(npg), full(wpg)] + tail_specs,
        out_specs=out_specs,
        out_shape=out_shape,
        compiler_params=_params(1),
        name="ple_final" if final else "ple_inproj",
    )(h2, pp_all, ps, wple, nple, npg, wpg, *tail)


_PAIR_LO = (0, 0, 0, 1, 1, 2)
_PAIR_HI = (1, 2, 3, 2, 3, 3)


def _routing_tables(route, cnt, n_tiles):
    counts = cnt[:N_BUCKETS, 0].astype(jnp.int32)
    padded = ((counts + MOE_TILE - 1) // MOE_TILE) * MOE_TILE
    ends = jnp.cumsum(padded)
    starts = ends - padded
    bucket = route[0].astype(jnp.int32)
    rank = route[1].astype(jnp.int32)
    pos = starts[bucket] + rank
    n_used = ends[-1] // MOE_TILE
    tile = jnp.minimum(jnp.arange(n_tiles, dtype=jnp.int32), n_used - 1)
    tb = jnp.sum((ends[None, :] <= (tile * MOE_TILE)[:, None]).astype(jnp.int32), axis=1)
    tb = jnp.minimum(tb, N_BUCKETS - 1)
    grp = tb // N_PAIRS
    pr = tb % N_PAIRS
    e_lo = grp * EXPERTS_PER_GROUP + jnp.asarray(_PAIR_LO, jnp.int32)[pr]
    e_hi = grp * EXPERTS_PER_GROUP + jnp.asarray(_PAIR_HI, jnp.int32)[pr]
    return pos, e_lo, e_hi, n_used.reshape(1)


def kernel(x_prompt, x_sample, state_conv, state_ssm, p_prompt, p_sample, norm_mix, w_in, conv_w, conv_b,
           dt_bias, a_log, d_skip, ssd_norm, gmlp_norm, w_spatial, b_spatial, w_out, norm_ffn,
           w_router_group, b_router_group, w_router_expert, b_router_expert, w_gate, w_up, w_down,
           w_ple, norm_ple, norm_pg, w_pg, norm_final):
    n_seq, seq_len, d_model = x_prompt.shape
    n_dec, dec_seq, _ = x_sample.shape
    depth = w_in.shape[0]
    conv_dim = conv_w.shape[2]
    d_ssd = SSD_HEADS * SSD_HEAD_DIM
    d_gmlp = gmlp_norm.shape[1]
    assert dec_seq == DEC_SEQ_ROWS and conv_w.shape[1] == CONV_WIDTH and seq_len % CHUNK == 0
    assert conv_dim == d_ssd + 2 * SSD_GROUPS * D_STATE and w_spatial.shape[1:] == (GMLP_HEADS, CHUNK, CHUNK)
    n_chunk = seq_len // CHUNK
    n_prompt = n_seq * seq_len
    n_srows = n_dec * GROUP
    t_all = n_prompt + n_srows
    assert n_prompt % TOK_TILE == 0 and n_srows % TOK_TILE == 0
    assert d_model == Y_PITCH * LANES and X_PITCH == Y_PITCH + 1
    lead = GROUP - dec_seq
    n_moe_tiles = t_all // MOE_TILE + N_BUCKETS

    def sample_rows(a):
        return jnp.pad(a, ((0, 0), (lead, 0), (0, 0))).reshape(n_srows, a.shape[-1])

    hp, hs = x_prompt.reshape(n_prompt, d_model), sample_rows(x_sample)

    o_xbc = d_ssd
    o_dt = o_xbc + conv_dim
    o_uv = o_dt + SSD_HEADS
    head_cols = jnp.arange(d_ssd) // SSD_HEAD_DIM
    seq_eye = jnp.eye(CHUNK // GROUP, dtype=F32)
    tril =jnp.tril(jnp.ones((CHUNK, CHUNK), bool))

    n_exp = w_gate.shape[1]
    wg_all = w_gate.astype(BF16).reshape((depth * n_exp,) + w_gate.shape[2:])
    wu_all = w_up.astype(BF16).reshape((depth * n_exp,) + w_up.shape[2:])
    wd_all = w_down.astype(BF16).reshape((depth * n_exp,) + w_down.shape[2:])
    s0_all = state_ssm.reshape(depth * n_dec, d_ssd, D_STATE)
    pp_all = p_prompt.reshape(depth * n_prompt, p_prompt.shape[-1])
    xs = None

    convs_p, ssms_p, convs_s, ssms_s, vs_s = [], [], [], [], []
    row = lambda a: a.reshape(1, -1).astype(F32)

    def inproj_params(i):
        wi = w_in[i]
        wdt = jnp.pad(wi[:, o_dt:o_uv], ((0, 0), (0, LANES - SSD_HEADS)))
        dtb = jnp.pad(dt_bias[i].astype(F32), (0, LANES - SSD_HEADS)).reshape(1, LANES)
        return (row(norm_mix[i]), wi[:, :o_xbc].astype(BF16), wi[:, o_xbc:o_dt].astype(BF16), wdt.astype(BF16),
                wi[:, o_uv:o_uv + d_gmlp].astype(BF16), wi[:, o_uv + d_gmlp:].astype(BF16), dtb, row(gmlp_norm[i]))

    z, xbc, dt, u, v = _inproj(hp, hs, inproj_params(0))

    for i in range(depth):
        arow =jnp.pad(-jnp.exp(a_log[i].astype(F32)), (0, LANES - SSD_HEADS)).reshape(1, LANES)
        dsk = d_skip[i].astype(F32)[head_cols].reshape(1, d_ssd)
        ws_tril = jnp.where(tril, w_spatial[i], 0.0)
        wsp_p = ws_tril.astype(BF16)
        bsp_p = jnp.repeat(b_spatial[i].T, d_gmlp // GMLP_HEADS, axis=1)
        w8 = jnp.pad(ws_tril[:, :dec_seq, :dec_seq], ((0, 0), (lead, 0), (lead, 0)))
        wsp_s = (seq_eye[None, :, None, :, None] * w8[:, None, :, None, :]).reshape(GMLP_HEADS, CHUNK, CHUNK).astype(BF16)
        b8 = jnp.pad(b_spatial[i][:, :dec_seq], ((0, 0), (lead, 0)))
        bsp_s = jnp.repeat(jnp.tile(b8, (1, CHUNK // GROUP)).T, d_gmlp // GMLP_HEADS, axis=1)
        wr = jnp.concatenate([w_router_group[i].T, w_router_expert[i].T,
                              jnp.zeros((ROUTE_ROWS - N_EXPERT_GROUPS * (1 + EXPERTS_PER_GROUP), d_model), F32)], axis=0)
        br = jnp.concatenate([b_router_group[i], b_router_expert[i],
                              jnp.zeros((ROUTE_ROWS - N_EXPERT_GROUPS * (1 + EXPERTS_PER_GROUP),), F32)]).reshape(ROUTE_ROWS, 1)

        yp, ssm_p = _mixer_prompt(z, xbc, dt, u, v, conv_w[i], row(conv_b[i]), arow, dsk, row(ssd_norm[i]),
                                  wsp_p, bsp_p, n_seq, n_chunk)
        cs = jnp.pad(state_conv[i], ((0, 0), (lead - (CONV_WIDTH - 1), dec_seq), (0, 0))).reshape(n_srows, conv_dim)
        ys, ssm_s = _mixer_sample(z, xbc, dt, u, v, cs, s0_all, conv_w[i], row(conv_b[i]), arow, dsk,
                                  row(ssd_norm[i]), wsp_s, bsp_s, n_prompt, n_srows, i)

        slab, route, cnt = _post(yp, ys, hp, hs, w_out[i].astype(BF16), row(norm_ffn[i]), wr, br)
        pos, e_lo, e_hi, n_used = _routing_tables(route, cnt, n_moe_tiles)

        xs = _permute_tokens(pos, slab, n_moe_tiles * MOE_TILE, X_PITCH, scatter=True, init=xs)
        h2s = _moe(e_lo + i * n_exp, e_hi + i * n_exp, n_used, xs, row(norm_ffn[i]), wg_all, wu_all, wd_all)
        h2 = _permute_tokens(pos, h2s, t_all, Y_PITCH, scatter=False)

        tail = CONV_WIDTH - 1
        convs_p.append(jnp.stack([xbc[(b + 1) * seq_len - tail:(b + 1) * seq_len] for b in range(n_seq)]))
        xbc_s = xbc[n_prompt:].reshape(n_dec, GROUP, conv_dim)
        convs_s.append(xbc_s[:, GROUP - tail:])
        ssms_p.append(ssm_p.reshape(n_seq, SSD_HEADS, SSD_HEAD_DIM, D_STATE))
        ssms_s.append(ssm_s.reshape(n_dec, SSD_HEADS, SSD_HEAD_DIM, D_STATE))
        vs_s.append(v[n_prompt:].reshape(n_dec, GROUP, d_gmlp)[:, lead:])

        final = i == depth - 1
        outs = _ple(h2, pp_all, sample_rows(p_sample[i]), w_ple[i].astype(BF16), row(norm_ple[i]), row(norm_pg[i]),
                    w_pg[i].astype(BF16), row(norm_final) if final else inproj_params(i + 1), final, i, n_prompt)
        hp, hs = outs[:2]
        if not final:
            z, xbc, dt, u, v = outs[2:]

    y_prompt = hp.reshape(n_seq, seq_len, d_model)
    y_sample = hs.reshape(n_dec, GROUP, d_model)[:, lead:]
    return (y_prompt, y_sample, jnp.stack(convs_p), jnp.stack(ssms_p), jnp.stack(convs_s), jnp.stack(ssms_s),
            jnp.stack(vs_s))
```

```python
import functools
import math

import jax
import jax.numpy as jnp
from jax import lax
from jax.experimental import pallas as pl
from jax.experimental.pallas import tpu as pltpu

F32 = jnp.float32
BF16 = jnp.bfloat16

LANES = 128
VMEM_LIMIT_BYTES = 56 * 1024 * 1024

CONV_WIDTH = 4
SSD_HEADS = 8
SSD_HEAD_DIM = 64
SSD_GROUPS = 2
D_STATE = 128
CHUNK = 128
GMLP_HEADS = 8
N_EXPERT_GROUPS = 4
EXPERTS_PER_GROUP = 4
N_PAIRS = 6
N_BUCKETS = N_EXPERT_GROUPS * N_PAIRS
EPS = 1e-6

GROUP = 8
DEC_SEQ_ROWS = 4
TOK_TILE = 512
MOE_TILE = 256
PROMPT_TILES_PER_STEP = 4
ROUTE_ROWS = 32
X_PITCH = 9
Y_PITCH = 8
NEG_BIG = -1e30


def _dot(a, b):
    return jnp.dot(a, b, preferred_element_type=F32)


def _dot_nt(a, b):
    return lax.dot_general(a, b, (((1,), (1,)), ((), ())), preferred_element_type=F32)


def _rms(x, g):
    ms = jnp.mean(x * x, axis=-1, keepdims=True)
    return (x * lax.rsqrt(ms + EPS)) * g


def _gelu(x):
    return 0.5 * x * (1.0 + lax.erf(x * (1.0 / math.sqrt(2.0))))


def _sigmoid(x):
    return 0.5 * jnp.tanh(0.5 * x) + 0.5


def _softplus(x):
    return jnp.maximum(x, 0.0) + jnp.log1p(jnp.exp(-jnp.abs(x)))


def _params(n_grid):
    return pltpu.CompilerParams(dimension_semantics=("arbitrary",) * n_grid,
                                vmem_limit_bytes=VMEM_LIMIT_BYTES)


def _pair_specs(width, n_prompt_tiles, prompt_tile0=0):
    return [pl.BlockSpec((TOK_TILE, width), lambda i: (prompt_tile0 + jnp.minimum(i, n_prompt_tiles - 1), 0)),
            pl.BlockSpec((TOK_TILE, width), lambda i: (jnp.maximum(i - n_prompt_tiles, 0), 0))]


def _pair_load(p_ref, s_ref, n_prompt_tiles):
    return jnp.where(pl.program_id(0) < n_prompt_tiles, p_ref[...], s_ref[...])


def _inproj_tile(h, nm_ref, wz_ref, wx_ref, wdt_ref, wu_ref, wv_ref, dtb_ref, gn_ref,
                 z_ref, xbc_ref, dt_ref, u_ref, v_ref):
    a = _rms(h, nm_ref[...]).astype(BF16)
    z_ref[...] = _dot(a, wz_ref[...])
    xbc_ref[...] = _dot(a, wx_ref[...])
    dt_ref[...] = _softplus(_dot(a, wdt_ref[...]) + dtb_ref[...])
    u_ref[...] = _gelu(_dot(a, wu_ref[...]))
    v_ref[...] = _rms(_gelu(_dot(a, wv_ref[...])), gn_ref[...])


def _inproj_kernel(hp_ref, hs_ref, *refs, n_prompt_tiles):
    _inproj_tile(_pair_load(hp_ref, hs_ref, n_prompt_tiles), *refs)


def _inproj_specs(weights, t):
    nm, wz, wx, wdt, wu, wv, dtb, gn = weights
    widths = (wz.shape[1], wx.shape[1], wdt.shape[1], wu.shape[1], wv.shape[1])
    row = lambda w: pl.BlockSpec((TOK_TILE, w), lambda i: (i, 0))
    full = lambda a: pl.BlockSpec(a.shape, lambda i: (0,) * a.ndim)
    return ([full(a) for a in weights], [row(w) for w in widths],
            [jax.ShapeDtypeStruct((t, w), F32) for w in widths])


def _inproj(hp, hs, weights):
    t = hp.shape[0] + hs.shape[0]
    n_prompt_tiles = hp.shape[0] // TOK_TILE
    w_specs, out_specs, out_shape = _inproj_specs(weights, t)
    return pl.pallas_call(
        functools.partial(_inproj_kernel, n_prompt_tiles=n_prompt_tiles),
        grid=(t // TOK_TILE,),
        in_specs=_pair_specs(hp.shape[1], n_prompt_tiles) + w_specs,
        out_specs=out_specs,
        out_shape=out_shape,
        compiler_params=_params(1),
        name="inproj",
    )(hp, hs, *weights)


def _seg_cumsum(x, seg, rowmod):
    d = 1
    while d < seg:
        x = x + jnp.where(rowmod >= d, pltpu.roll(x, d, axis=0), 0.0)
        d *= 2
    return x


def _seg_rev_cumsum(x, seg, rowmod):
    n = x.shape[0]
    d = 1
    while d < seg:
        x = x + jnp.where(rowmod + d < seg, pltpu.roll(x, n - d, axis=0), 0.0)
        d *= 2
    return x


def _expand_heads(m, lane_lt_half):
    parts = []
    for j in range(SSD_HEADS // 2):
        parts.append(jnp.where(lane_lt_half, m[:, 2 * j:2 * j + 1], m[:, 2 * j + 1:2 * j + 2]))
    return jnp.concatenate(parts, axis=1)


def _merge_head_pairs(per_head, lane_lt_half):
    parts = [jnp.where(lane_lt_half, per_head[2 * j], per_head[2 * j + 1]) for j in range(len(per_head) // 2)]
    return jnp.concatenate(parts, axis=1)


def _mixer_kernel(*refs, sample, n_inner):
    n_tok = 6 if sample else 5
    tok_refs, rest = refs[:n_tok], refs[n_tok:]
    if sample:
        (s0_ref, cw_ref, cb_ref, arow_ref, dsk_ref, sn_ref, wsp_ref, bsp_ref,
         y_ref, sout_ref, ext_ref, yoff_ref) = rest
        first = pl.program_id(0) == 0
    else:
        (cw_ref, cb_ref, arow_ref, dsk_ref, sn_ref, wsp_ref, bsp_ref, y_ref, sout_ref, ext_ref) = rest
        s0_ref = yoff_ref = None
        first = pl.program_id(1) == 0

    @pl.when(first)
    def _():
        ext_ref[0:8, :] = jnp.zeros((8, ext_ref.shape[1]), F32)
        if not sample:
            sout_ref[...] = jnp.zeros(sout_ref.shape, F32)

    def tile(c, carry):
        r0 = pl.multiple_of(c * CHUNK, CHUNK)
        views = [r.at[pl.ds(r0, CHUNK)] for r in tok_refs + (y_ref,)]
        _mixer_tile(*views[:n_tok], s0_ref, cw_ref, cb_ref, arow_ref, dsk_ref, sn_ref, wsp_ref, bsp_ref,
                    views[n_tok], sout_ref, ext_ref, yoff_ref, sample=sample)
        return carry

    if n_inner == 1:
        tile(0, 0)
    else:
        lax.fori_loop(0, n_inner, tile, 0)


def _mixer_tile(z_ref, xbc_ref, dt_ref, u_ref, v_ref, *rest, sample):
    if sample:
        cs_ref, s0_ref = rest[0], rest[1]
        rest = rest[2:]
    else:
        rest = rest[1:]
    cw_ref, cb_ref, arow_ref, dsk_ref, sn_ref, wsp_ref, bsp_ref, y_ref, sout_ref, ext_ref, yoff_ref = rest
    seg = GROUP if sample else CHUNK
    d_ssd = SSD_HEADS * SSD_HEAD_DIM
    gw = d_ssd // SSD_GROUPS
    hpg = SSD_HEADS // SSD_GROUPS
    cs_first = GROUP - DEC_SEQ_ROWS - (CONV_WIDTH - 1)

    rows = lax.broadcasted_iota(jnp.int32, (CHUNK, LANES), 0)
    cols = lax.broadcasted_iota(jnp.int32, (CHUNK, LANES), 1)
    rowmod = rows & (seg - 1)
    lane_lt_half = cols < SSD_HEAD_DIM

    xbc = xbc_ref[...]
    if sample:
        rm = lax.broadcasted_iota(jnp.int32, xbc.shape, 0) & (GROUP - 1)
        xbc = jnp.where((rm >= cs_first) & (rm < cs_first + CONV_WIDTH - 1), cs_ref[...], xbc)

    ext_ref[8:8 + CHUNK, :] = xbc
    acc = cb_ref[...] + cw_ref[CONV_WIDTH - 1:CONV_WIDTH, :] * xbc
    for j in range(1, CONV_WIDTH):
        acc = acc + cw_ref[CONV_WIDTH - 1 - j:CONV_WIDTH - j, :] * ext_ref[8 - j:8 - j + CHUNK, :]
    if not sample:
        ext_ref[0:8, :] = ext_ref[CHUNK:CHUNK + 8, :]
    xc = acc * _sigmoid(acc)
    x = xc[:, :d_ssd]
    bb = xc[:, d_ssd:d_ssd + SSD_GROUPS * D_STATE].astype(BF16)
    cm = xc[:, d_ssd + SSD_GROUPS * D_STATE:]
    cbf = cm.astype(BF16)

    dtc = dt_ref[...]
    if sample:
        dtc = jnp.where(rowmod >= GROUP - DEC_SEQ_ROWS, dtc, 0.0)
    da = dtc * arow_ref[...]
    cum = _seg_cumsum(da, seg, rowmod)
    rev = _seg_rev_cumsum(da, seg, rowmod) - da
    cum_t = cum.T
    ecum = jnp.exp(cum)
    dt_e = _expand_heads(dtc, lane_lt_half)
    ecum_e = _expand_heads(ecum, lane_lt_half)
    erev_e = _expand_heads(jnp.exp(rev), lane_lt_half)

    xdt = x * dt_e
    xdt_bf = xdt.astype(BF16)
    causal = rows >= cols
    if sample:
        causal = causal & ((rows >> 3) == (cols >> 3))

    yd = []
    for g in range(SSD_GROUPS):
        cb_g = _dot_nt(cbf[:, g * D_STATE:(g + 1) * D_STATE], bb[:, g * D_STATE:(g + 1) * D_STATE])
        for hh in range(hpg):
            h = g * hpg + hh
            expo = cum[:, h:h + 1] - cum_t[h:h + 1, :]
            w = (cb_g * jnp.exp(jnp.where(causal, expo, NEG_BIG))).astype(BF16)
            j = h // 2
            yd.append(_dot(w, xdt_bf[:, j * LANES:(j + 1) * LANES]))
    y_diag = _merge_head_pairs(yd, lane_lt_half)

    if sample:
        for i in range(CHUNK // GROUP):
            for g in range(SSD_GROUPS):
                s_g = s0_ref[i, g * gw:(g + 1) * gw, :].astype(BF16)
                yoff_ref[i * GROUP:(i + 1) * GROUP, g * gw:(g + 1) * gw] = _dot_nt(
                    cm[i * GROUP:(i + 1) * GROUP, g * D_STATE:(g + 1) * D_STATE].astype(BF16), s_g)
        y_off = yoff_ref[...]
    else:
        y_off = jnp.concatenate(
            [_dot_nt(cbf[:, g * D_STATE:(g + 1) * D_STATE], sout_ref[0, g * gw:(g + 1) * gw, :].astype(BF16))
             for g in range(SSD_GROUPS)], axis=1)
    y = y_diag + y_off * ecum_e + dsk_ref[...] * x

    xd = xdt * erev_e
    for g in range(SSD_GROUPS):
        xd_t = xd[:, g * gw:(g + 1) * gw].T
        b_g = bb[:, g * D_STATE:(g + 1) * D_STATE]
        if sample:
            tcols = lax.broadcasted_iota(jnp.int32, xd_t.shape, 1) >> 3
            for i in range(CHUNK // GROUP):
                upd = _dot(jnp.where(tcols == i, xd_t, 0.0).astype(BF16), b_g)
                last = i * GROUP + GROUP - 1
                for hh in range(hpg):
                    h = g * hpg + hh
                    r0 = h * SSD_HEAD_DIM
                    sout_ref[i, r0:r0 + SSD_HEAD_DIM, :] = (
                        s0_ref[i, r0:r0 + SSD_HEAD_DIM, :] * ecum[last:last + 1, h:h + 1]
                        + upd[hh * SSD_HEAD_DIM:(hh + 1) * SSD_HEAD_DIM, :])
        else:
            upd = _dot(xd_t.astype(BF16), b_g)
            for hh in range(hpg):
                h = g * hpg + hh
                r0 = h * SSD_HEAD_DIM
                sout_ref[0, r0:r0 + SSD_HEAD_DIM, :] = (
                    sout_ref[0, r0:r0 + SSD_HEAD_DIM, :] * ecum[CHUNK - 1:CHUNK, h:h + 1]
                    + upd[hh * SSD_HEAD_DIM:(hh + 1) * SSD_HEAD_DIM, :])

    zf = z_ref[...]
    yf = y * (zf * _sigmoid(zf))
    parts = []
    for g in range(SSD_GROUPS):
        part = yf[:, g * gw:(g + 1) * gw]
        ms = jnp.mean(part * part, axis=-1, keepdims=True)
        parts.append(part * lax.rsqrt(ms + EPS))
    y_ssd = jnp.concatenate(parts, axis=1) * sn_ref[...]

    vb = v_ref[...].astype(BF16)
    sg = [_dot(wsp_ref[h], vb[:, (h // 2) * LANES:(h // 2 + 1) * LANES]) for h in range(GMLP_HEADS)]
    s = _merge_head_pairs(sg, lane_lt_half) + bsp_ref[...]
    y_gm = u_ref[...] * s

    y_ref[:, :d_ssd] = y_ssd.astype(BF16)
    y_ref[:, d_ssd:] = y_gm.astype(BF16)


def _mixer_prompt(z, xbc, dt, u, v, cw, cb, arow, dsk, sn, wsp, bsp, n_seq, n_chunk):
    n_inner = math.gcd(n_chunk, PROMPT_TILES_PER_STEP)
    n_outer = n_chunk // n_inner
    tok = lambda w: pl.BlockSpec((n_inner * CHUNK, w), lambda b, c: (b * n_outer + c, 0))
    full = lambda a: pl.BlockSpec(a.shape, lambda b, c: (0,) * a.ndim)
    d_ssd = z.shape[1]
    d_mix = d_ssd + u.shape[1]
    return pl.pallas_call(
        functools.partial(_mixer_kernel, sample=False, n_inner=n_inner),
        grid=(n_seq, n_outer),
        in_specs=[tok(z.shape[1]), tok(xbc.shape[1]), tok(dt.shape[1]), tok(u.shape[1]), tok(v.shape[1]),
                  full(cw), full(cb), full(arow), full(dsk), full(sn), full(wsp), full(bsp)],
        out_specs=[tok(d_mix),
                   pl.BlockSpec((1, d_ssd, D_STATE), lambda b, c: (b, 0, 0))],
        out_shape=[jax.ShapeDtypeStruct((n_seq * n_chunk * CHUNK, d_mix), BF16),
                   jax.ShapeDtypeStruct((n_seq, d_ssd, D_STATE), F32)],
        scratch_shapes=[pltpu.VMEM((CHUNK + 8, xbc.shape[1]), F32)],
        compiler_params=_params(2),
        name="mixer_prompt",
    )(z, xbc, dt, u, v, cw, cb, arow, dsk, sn, wsp, bsp)


def _mixer_sample(z, xbc, dt, u, v, cs, s0_all, cw, cb, arow, dsk, sn, wsp, bsp, row0, n_rows, layer):
    blk0 = row0 // CHUNK
    n_seq_blk = CHUNK // GROUP
    n_blk = n_rows // CHUNK
    tok = lambda w: pl.BlockSpec((CHUNK, w), lambda i: (blk0 + i, 0))
    full = lambda a: pl.BlockSpec(a.shape, lambda i: (0,) * a.ndim)
    d_ssd = z.shape[1]
    d_mix = d_ssd + u.shape[1]
    st = (n_seq_blk, d_ssd, D_STATE)
    return pl.pallas_call(
        functools.partial(_mixer_kernel, sample=True, n_inner=1),
        grid=(n_blk,),
        in_specs=[tok(z.shape[1]), tok(xbc.shape[1]), tok(dt.shape[1]), tok(u.shape[1]), tok(v.shape[1]),
                  pl.BlockSpec((CHUNK, cs.shape[1]), lambda i: (i, 0)),
                  pl.BlockSpec(st, lambda i: (layer * n_blk + i, 0, 0)),
                  full(cw), full(cb), full(arow), full(dsk), full(sn), full(wsp), full(bsp)],
        out_specs=[pl.BlockSpec((CHUNK, d_mix), lambda i: (i, 0)), pl.BlockSpec(st, lambda i: (i, 0, 0))],
        out_shape=[jax.ShapeDtypeStruct((n_rows, d_mix), BF16),
                   jax.ShapeDtypeStruct((n_blk * n_seq_blk, d_ssd, D_STATE), F32)],
        scratch_shapes=[pltpu.VMEM((CHUNK + 8, xbc.shape[1]), F32), pltpu.VMEM((CHUNK, d_ssd), F32)],
        compiler_params=_params(1),
        name="mixer_sample",
    )(z, xbc, dt, u, v, cs, s0_all, cw, cb, arow, dsk, sn, wsp, bsp)


def _first_argmax(vals):
    m = vals[0]
    for v in vals[1:]:
        m = jnp.maximum(m, v)
    idx = jnp.full(m.shape, len(vals) - 1, jnp.int32)
    for k in range(len(vals) - 2, -1, -1):
        idx = jnp.where(vals[k] >= m, k, idx)
    return m, idx


def _slab_store(slab_ref, x, pitch, tok0=0):
    for k in range(x.shape[1] // LANES):
        slab_ref[pl.ds(tok0 * pitch + k, x.shape[0], stride=pitch), :] = x[:, k * LANES:(k + 1) * LANES]


def _slab_load(slab_ref, rows, n_pieces, pitch, tok0=0):
    return jnp.concatenate([slab_ref[pl.ds(tok0 * pitch + k, rows, stride=pitch), :] for k in range(n_pieces)],
                           axis=1)


def _post_kernel(yp_ref, ys_ref, hp_ref, hs_ref, wo_ref, nf_ref, wr_ref, br_ref,
                 slab_ref, route_ref, cnt_ref, carry_ref, *, n_prompt_tiles):
    i = pl.program_id(0)
    tm, d = hp_ref.shape

    @pl.when(i == 0)
    def _():
        carry_ref[...] = jnp.zeros(carry_ref.shape, F32)

    ym = _pair_load(yp_ref, ys_ref, n_prompt_tiles)
    h1 = _pair_load(hp_ref, hs_ref, n_prompt_tiles) + _dot(ym, wo_ref[...])
    _slab_store(slab_ref, h1, X_PITCH)

    t = _rms(h1, nf_ref[...])
    wr = wr_ref[...]
    wr_hi = wr.astype(BF16)
    wr_lo = (wr - wr_hi.astype(F32)).astype(BF16)
    t_hi = t.astype(BF16)
    t_lo = (t - t_hi.astype(F32)).astype(BF16)
    by_hi = _dot_nt(jnp.concatenate([wr_hi, wr_lo], axis=0), t_hi)
    logits = (by_hi[:ROUTE_ROWS] + (by_hi[ROUTE_ROWS:] + _dot_nt(wr_hi, t_lo))) + br_ref[...]
    lg = [logits[k:k + 1, :] for k in range(N_EXPERT_GROUPS)]
    m, g = _first_argmax(lg)
    ssum = jnp.exp(lg[0] - m)
    for k in range(1, N_EXPERT_GROUPS):
        ssum = ssum + jnp.exp(lg[k] - m)
    p_sel = 1.0 / ssum
    le = [logits[N_EXPERT_GROUPS + e:N_EXPERT_GROUPS + e + 1, :] for e in range(N_EXPERT_GROUPS * EXPERTS_PER_GROUP)]
    a = []
    for k in range(EXPERTS_PER_GROUP):
        sel = le[(N_EXPERT_GROUPS - 1) * EXPERTS_PER_GROUP + k]
        for gi in range(N_EXPERT_GROUPS - 2, -1, -1):
            sel = jnp.where(g == gi, le[gi * EXPERTS_PER_GROUP + k], sel)
        a.append(sel)
    v1, i1 = _first_argmax(a)
    a2 = [jnp.where(i1 == k, -jnp.inf, a[k]) for k in range(EXPERTS_PER_GROUP)]
    v2, i2 = _first_argmax(a2)
    e2 = jnp.exp(v2 - v1)
    den = 1.0 + e2
    g1 = (1.0 / den) * p_sel
    g2 = (e2 / den) * p_sel
    lo = jnp.minimum(i1, i2)
    hi = jnp.maximum(i1, i2)
    c_lo = jnp.where(i1 < i2, g1, g2)
    c_hi = jnp.where(i1 < i2, g2, g1)
    pair = jnp.where(lo == 0, hi - 1, jnp.where(lo == 1, hi + 1, N_PAIRS - 1))
    bucket = g * N_PAIRS + pair

    brow = lax.broadcasted_iota(jnp.int32, (ROUTE_ROWS, tm), 0)
    onehot = jnp.where(brow == bucket, 1.0, 0.0)
    tr = lax.broadcasted_iota(jnp.int32, (tm, tm), 0)
    tc = lax.broadcasted_iota(jnp.int32, (tm, tm), 1)
    earlier = jnp.where(tr < tc, 1.0, 0.0).astype(BF16)
    prefix = _dot(onehot.astype(BF16), earlier)
    carry = carry_ref[:, 0:1]
    rank = jnp.sum(onehot * (prefix + carry), axis=0, keepdims=True)
    carry = carry + jnp.sum(onehot, axis=1, keepdims=True)
    carry_b = jnp.broadcast_to(carry, carry_ref.shape)
    carry_ref[...] = carry_b
    cnt_ref[...] = carry_b

    rr = lax.broadcasted_iota(jnp.int32, (8, tm), 0)
    route_ref[...] = jnp.where(rr == 0, bucket.astype(F32), jnp.where(rr == 1, rank, 0.0))
    ar = lax.broadcasted_iota(jnp.int32, (LANES, tm), 0)
    aux = jnp.where(ar == 0, c_lo, jnp.where(ar == 1, c_hi, 0.0))
    slab_ref[pl.ds(d // LANES, tm, stride=X_PITCH), :] = aux.T


def _post(yp, ys, hp, hs, wo, nf, wr, br):
    d = hp.shape[1]
    t = hp.shape[0] + hs.shape[0]
    n_prompt_tiles = hp.shape[0] // TOK_TILE
    full = lambda a: pl.BlockSpec(a.shape, lambda i: (0,) * a.ndim)
    return pl.pallas_call(
        functools.partial(_post_kernel, n_prompt_tiles=n_prompt_tiles),
        grid=(t // TOK_TILE,),
        in_specs=_pair_specs(d, n_prompt_tiles) + _pair_specs(d, n_prompt_tiles)
        + [full(wo), full(nf), full(wr), full(br)],
        out_specs=[pl.BlockSpec((TOK_TILE * X_PITCH, LANES), lambda i: (i, 0)),
                   pl.BlockSpec((8, TOK_TILE), lambda i: (0, i)),
                   pl.BlockSpec((ROUTE_ROWS, LANES), lambda i: (0, 0))],
        out_shape=[jax.ShapeDtypeStruct((t * X_PITCH, LANES), F32),
                   jax.ShapeDtypeStruct((8, t), F32),
                   jax.ShapeDtypeStruct((ROUTE_ROWS, LANES), F32)],
        scratch_shapes=[pltpu.VMEM((ROUTE_ROWS, LANES), F32)],
        compiler_params=_params(1),
        name="post",
    )(yp, ys, hp, hs, wo, nf, wr, br)


def _permute_kernel(*refs, scatter, chunk, pitch):
    if scatter:
        pos_ref, src_ref, _, dst_ref, sem = refs
    else:
        pos_ref, src_ref, dst_ref, sem = refs
    def token_copy(r):
        here = pl.ds(r * pitch, pitch)
        there = pl.ds(pos_ref[0, 0, r] * pitch, pitch)
        if scatter:
            return pltpu.make_async_copy(src_ref.at[here], dst_ref.at[there], sem)
        return pltpu.make_async_copy(src_ref.at[there], dst_ref.at[here], sem)

    def body(r, carry):
        token_copy(r).start()
        return carry

    lax.fori_loop(0, chunk, body, 0, unroll=8)
    whole = pl.ds(0, chunk * pitch)
    if scatter:
        pltpu.make_async_copy(src_ref, dst_ref.at[whole], sem).wait()
    else:
        pltpu.make_async_copy(src_ref.at[whole], dst_ref, sem).wait()


def _permute_tokens(pos, src, n_out, pitch, scatter, init=None):
    n = pos.shape[0]
    chunk = TOK_TILE
    pos3 = pos.reshape(n // chunk, 1, chunk)
    block = pl.BlockSpec((chunk * pitch, LANES), lambda i: (i, 0))
    hbm = pl.BlockSpec(memory_space=pl.ANY)
    in_specs = [pl.BlockSpec((1, 1, chunk), lambda i: (i, 0, 0), memory_space=pltpu.SMEM),
                block if scatter else hbm]
    args = [pos3, src]
    aliases = {}
    if scatter:
        in_specs.append(hbm)
        args.append(jnp.zeros((n_out * pitch, LANES), src.dtype) if init is None else init)
        aliases = {2: 0}
    return pl.pallas_call(
        functools.partial(_permute_kernel, scatter=scatter, chunk=chunk, pitch=pitch),
        grid=(n // chunk,),
        in_specs=in_specs,
        out_specs=hbm if scatter else block,
        out_shape=jax.ShapeDtypeStruct((n_out * pitch, LANES), src.dtype),
        scratch_shapes=[pltpu.SemaphoreType.DMA(())],
        input_output_aliases=aliases,
        compiler_params=_params(1),
        name="scatter_tokens" if scatter else "gather_tokens",
    )(*args)


def _moe_kernel(elo_ref, ehi_ref, nused_ref, x_ref, nf_ref, wg_lo, wu_lo, wd_lo, wg_hi, wu_hi, wd_hi, o_ref):
    i = pl.program_id(0)
    n_pieces = nf_ref.shape[1] // LANES

    @pl.when(i < nused_ref[0])
    def _():
        x = _slab_load(x_ref, MOE_TILE, n_pieces, X_PITCH)
        gates = x_ref[pl.ds(n_pieces, MOE_TILE, stride=X_PITCH), :]
        c_lo = gates[:, 0:1]
        c_hi = gates[:, 1:2]
        t = _rms(x, nf_ref[...]).astype(BF16)

        def expert(wg, wu, wd):
            gate = _dot(t, wg[...])
            hid = (gate * _sigmoid(gate)) * _dot(t, wu[...])
            return _dot(hid.astype(BF16), wd[...])

        y = c_lo * expert(wg_lo, wu_lo, wd_lo)
        y = y + c_hi * expert(wg_hi, wu_hi, wd_hi)
        _slab_store(o_ref, x + y, Y_PITCH)

    @pl.when(i >= nused_ref[0])
    def _():
        o_ref[...] = jnp.zeros(o_ref.shape, F32)


def _moe(e_lo, e_hi, n_used, xs, nf, wg, wu, wd):
    n_tiles = xs.shape[0] // (MOE_TILE * X_PITCH)
    d, de = wg.shape[1:]
    lo = lambda i, elo, ehi, nu: (elo[i], 0, 0)
    hi = lambda i, elo, ehi, nu: (ehi[i], 0, 0)
    grid_spec = pltpu.PrefetchScalarGridSpec(
        num_scalar_prefetch=3,
        grid=(n_tiles,),
        in_specs=[pl.BlockSpec((MOE_TILE * X_PITCH, LANES),
                               lambda i, elo, ehi, nu: (jnp.maximum(jnp.minimum(i, nu[0] - 1), 0), 0)),
                  pl.BlockSpec(nf.shape, lambda i, elo, ehi, nu: (0, 0)),
                  pl.BlockSpec((None, d, de), lo), pl.BlockSpec((None, d, de), lo), pl.BlockSpec((None, de, d), lo),
                  pl.BlockSpec((None, d, de), hi), pl.BlockSpec((None, d, de), hi), pl.BlockSpec((None, de, d), hi)],
        out_specs=pl.BlockSpec((MOE_TILE * Y_PITCH, LANES), lambda i, elo, ehi, nu: (i, 0)),
    )
    return pl.pallas_call(
        _moe_kernel,
        grid_spec=grid_spec,
        out_shape=jax.ShapeDtypeStruct((n_tiles * MOE_TILE * Y_PITCH, LANES), F32),
        compiler_params=_params(1),
        name="expert_pairs",
    )(e_lo, e_hi, n_used, xs, nf, wg, wu, wd, wg, wu, wd)


def _ple_kernel(pos_ref, pos_next_ref, h_hbm, pp_ref, ps_ref, wple_ref, nple_ref, npg_ref, wpg_ref, *refs,
                final, n_prompt_tiles):
    refs, (xbuf, sem) = refs[:-2], refs[-2:]
    i = pl.program_id(0)
    last = pl.num_programs(0) - 1
    slot = i % 2

    def start_tile(p_ref, s):
        for r in range(TOK_TILE):
            pltpu.make_async_copy(h_hbm.at[pl.ds(p_ref[0, 0, r] * Y_PITCH, Y_PITCH)],
                                  xbuf.at[s, pl.ds(r * Y_PITCH, Y_PITCH)], sem.at[s]).start()

    def wait_tile(s):
        pltpu.make_async_copy(h_hbm.at[pl.ds(0, TOK_TILE * Y_PITCH)], xbuf.at[s], sem.at[s]).wait()

    @pl.when(i == 0)
    def _():
        start_tile(pos_ref, 0)

    wait_tile(slot)
    start_tile(pos_next_ref, 1 - slot)
    h2 = _slab_load(xbuf.at[slot], TOK_TILE, Y_PITCH, Y_PITCH)
    p = _pair_load(pp_ref, ps_ref, n_prompt_tiles)
    e = _rms(_dot(p.astype(BF16), wple_ref[...]), nple_ref[...])
    gate = _sigmoid(_dot(_rms(h2, npg_ref[...]).astype(BF16), wpg_ref[...]))
    h3 = h2 + gate * e
    if final:
        nfin_ref, op_ref, os_ref = refs
        out = _rms(h3, nfin_ref[...])
    else:
        op_ref, os_ref = refs[N_INPROJ_PARAMS:N_INPROJ_PARAMS + 2]
        out = h3
        _inproj_tile(h3, *refs[:N_INPROJ_PARAMS], *refs[N_INPROJ_PARAMS + 2:])

    @pl.when(i < n_prompt_tiles)
    def _():
        op_ref[...] = out

    @pl.when(i >= n_prompt_tiles)
    def _():
        os_ref[...] = out

    @pl.when(i == last)
    def _():
        wait_tile(1 - slot)


N_INPROJ_PARAMS = 8


def _ple(pos, h2s, pp_all, ps, wple, nple, npg, wpg, tail, final, layer, n_prompt):
    t = pos.shape[0]
    n_steps = t // TOK_TILE
    pos3 = pos.reshape(n_steps, 1, TOK_TILE)
    d = Y_PITCH * LANES
    n_prompt_tiles = n_prompt // TOK_TILE
    full = lambda a: pl.BlockSpec(a.shape, lambda i: (0,) * a.ndim)
    out_specs = _pair_specs(d, n_prompt_tiles)
    out_shape = [jax.ShapeDtypeStruct((n_prompt, d), F32), jax.ShapeDtypeStruct((ps.shape[0], d), F32)]
    if final:
        tail = (tail,)
        tail_specs = [full(tail[0])]
    else:
        assert len(tail) == N_INPROJ_PARAMS
        tail_specs, io, ish = _inproj_specs(tail, t)
        out_specs, out_shape = out_specs + io, out_shape + ish
    return pl.pallas_call(
        functools.partial(_ple_kernel, final=final, n_prompt_tiles=n_prompt_tiles),
        grid=(n_steps,),
        in_specs=[pl.BlockSpec((1, 1, TOK_TILE), lambda i: (i, 0, 0), memory_space=pltpu.SMEM),
                  pl.BlockSpec((1, 1, TOK_TILE), lambda i: (jnp.minimum(i + 1, n_steps - 1), 0, 0),
                               memory_space=pltpu.SMEM),
                  pl.BlockSpec(memory_space=pl.ANY)]
        + _pair_specs(pp_all.shape[1], n_prompt_tiles, layer * n_prompt_tiles)
        + [full(wple), full(nple), full(npg), full(wpg)] + tail_specs,
        out_specs=out_specs,
        out_shape=out_shape,
        scratch_shapes=[pltpu.VMEM((2, TOK_TILE * Y_PITCH, LANES), F32), pltpu.SemaphoreType.DMA((2,))],
        compiler_params=_params(1),
        name="ple_final" if final else "ple_inproj",
    )(pos3, pos3, h2s, pp_all, ps, wple, nple, npg, wpg, *tail)


_PAIR_LO = (0, 0, 0, 1, 1, 2)
_PAIR_HI = (1, 2, 3, 2, 3, 3)


def _routing_tables(route, cnt, n_tiles):
    counts = cnt[:N_BUCKETS, 0].astype(jnp.int32)
    padded = ((counts + MOE_TILE - 1) // MOE_TILE) * MOE_TILE
    ends = jnp.cumsum(padded)
    starts = ends - padded
    bucket = route[0].astype(jnp.int32)
    rank = route[1].astype(jnp.int32)
    pos = starts[bucket] + rank
    n_used = ends[-1] // MOE_TILE
    tile = jnp.minimum(jnp.arange(n_tiles, dtype=jnp.int32), n_used - 1)
    tb = jnp.sum((ends[None, :] <= (tile * MOE_TILE)[:, None]).astype(jnp.int32), axis=1)
    tb = jnp.minimum(tb, N_BUCKETS - 1)
    grp = tb // N_PAIRS
    pr = tb % N_PAIRS
    e_lo = grp * EXPERTS_PER_GROUP + jnp.asarray(_PAIR_LO, jnp.int32)[pr]
    e_hi = grp * EXPERTS_PER_GROUP + jnp.asarray(_PAIR_HI, jnp.int32)[pr]
    return pos, e_lo, e_hi, n_used.reshape(1)


def kernel(x_prompt, x_sample, state_conv, state_ssm, p_prompt, p_sample, norm_mix, w_in, conv_w, conv_b,
           dt_bias, a_log, d_skip, ssd_norm, gmlp_norm, w_spatial, b_spatial, w_out, norm_ffn,
           w_router_group, b_router_group, w_router_expert, b_router_expert, w_gate, w_up, w_down,
           w_ple, norm_ple, norm_pg, w_pg, norm_final):
    n_seq, seq_len, d_model = x_prompt.shape
    n_dec, dec_seq, _ = x_sample.shape
    depth = w_in.shape[0]
    conv_dim = conv_w.shape[2]
    d_ssd = SSD_HEADS * SSD_HEAD_DIM
    d_gmlp = gmlp_norm.shape[1]
    assert dec_seq == DEC_SEQ_ROWS and conv_w.shape[1] == CONV_WIDTH and seq_len % CHUNK == 0
    assert conv_dim == d_ssd + 2 * SSD_GROUPS * D_STATE and w_spatial.shape[1:] == (GMLP_HEADS, CHUNK, CHUNK)
    n_chunk = seq_len // CHUNK
    n_prompt = n_seq * seq_len
    n_srows = n_dec * GROUP
    t_all = n_prompt + n_srows
    assert n_prompt % TOK_TILE == 0 and n_srows % TOK_TILE == 0
    assert d_model == Y_PITCH * LANES and X_PITCH == Y_PITCH + 1
    lead = GROUP - dec_seq
    n_moe_tiles = t_all // MOE_TILE + N_BUCKETS

    def sample_rows(a):
        return jnp.pad(a, ((0, 0), (lead, 0), (0, 0))).reshape(n_srows, a.shape[-1])

    hp, hs = x_prompt.reshape(n_prompt, d_model), sample_rows(x_sample)

    o_xbc = d_ssd
    o_dt = o_xbc + conv_dim
    o_uv = o_dt + SSD_HEADS
    head_cols = jnp.arange(d_ssd) // SSD_HEAD_DIM
    seq_eye = jnp.eye(CHUNK // GROUP, dtype=F32)
    tril =jnp.tril(jnp.ones((CHUNK, CHUNK), bool))

    n_exp = w_gate.shape[1]
    wg_all = w_gate.astype(BF16).reshape((depth * n_exp,) + w_gate.shape[2:])
    wu_all = w_up.astype(BF16).reshape((depth * n_exp,) + w_up.shape[2:])
    wd_all = w_down.astype(BF16).reshape((depth * n_exp,) + w_down.shape[2:])
    s0_all = state_ssm.reshape(depth * n_dec, d_ssd, D_STATE)
    pp_all = p_prompt.reshape(depth * n_prompt, p_prompt.shape[-1])
    xs = None

    convs_p, ssms_p, convs_s, ssms_s, vs_s = [], [], [], [], []
    row = lambda a: a.reshape(1, -1).astype(F32)

    def inproj_params(i):
        wi = w_in[i]
        wdt = jnp.pad(wi[:, o_dt:o_uv], ((0, 0), (0, LANES - SSD_HEADS)))
        dtb = jnp.pad(dt_bias[i].astype(F32), (0, LANES - SSD_HEADS)).reshape(1, LANES)
        return (row(norm_mix[i]), wi[:, :o_xbc].astype(BF16), wi[:, o_xbc:o_dt].astype(BF16), wdt.astype(BF16),
                wi[:, o_uv:o_uv + d_gmlp].astype(BF16), wi[:, o_uv + d_gmlp:].astype(BF16), dtb, row(gmlp_norm[i]))

    z, xbc, dt, u, v = _inproj(hp, hs, inproj_params(0))

    for i in range(depth):
        arow =jnp.pad(-jnp.exp(a_log[i].astype(F32)), (0, LANES - SSD_HEADS)).reshape(1, LANES)
        dsk = d_skip[i].astype(F32)[head_cols].reshape(1, d_ssd)
        ws_tril = jnp.where(tril, w_spatial[i], 0.0)
        wsp_p = ws_tril.astype(BF16)
        bsp_p = jnp.repeat(b_spatial[i].T, d_gmlp // GMLP_HEADS, axis=1)
        w8 = jnp.pad(ws_tril[:, :dec_seq, :dec_seq], ((0, 0), (lead, 0), (lead, 0)))
        wsp_s = (seq_eye[None, :, None, :, None] * w8[:, None, :, None, :]).reshape(GMLP_HEADS, CHUNK, CHUNK).astype(BF16)
        b8 = jnp.pad(b_spatial[i][:, :dec_seq], ((0, 0), (lead, 0)))
        bsp_s = jnp.repeat(jnp.tile(b8, (1, CHUNK // GROUP)).T, d_gmlp // GMLP_HEADS, axis=1)
        wr = jnp.concatenate([w_router_group[i].T, w_router_expert[i].T,
                              jnp.zeros((ROUTE_ROWS - N_EXPERT_GROUPS * (1 + EXPERTS_PER_GROUP), d_model), F32)], axis=0)
        br = jnp.concatenate([b_router_group[i], b_router_expert[i],
                              jnp.zeros((ROUTE_ROWS - N_EXPERT_GROUPS * (1 + EXPERTS_PER_GROUP),), F32)]).reshape(ROUTE_ROWS, 1)

        yp, ssm_p = _mixer_prompt(z, xbc, dt, u, v, conv_w[i], row(conv_b[i]), arow, dsk, row(ssd_norm[i]),
                                  wsp_p, bsp_p, n_seq, n_chunk)
        cs = jnp.pad(state_conv[i], ((0, 0), (lead - (CONV_WIDTH - 1), dec_seq), (0, 0))).reshape(n_srows, conv_dim)
        ys, ssm_s = _mixer_sample(z, xbc, dt, u, v, cs, s0_all, conv_w[i], row(conv_b[i]), arow, dsk,
                                  row(ssd_norm[i]), wsp_s, bsp_s, n_prompt, n_srows, i)

        slab, route, cnt = _post(yp, ys, hp, hs, w_out[i].astype(BF16), row(norm_ffn[i]), wr, br)
        pos, e_lo, e_hi, n_used = _routing_tables(route, cnt, n_moe_tiles)

        xs = _permute_tokens(pos, slab, n_moe_tiles * MOE_TILE, X_PITCH, scatter=True, init=xs)
        h2s = _moe(e_lo + i * n_exp, e_hi + i * n_exp, n_used, xs, row(norm_ffn[i]), wg_all, wu_all, wd_all)

        tail = CONV_WIDTH - 1
        convs_p.append(jnp.stack([xbc[(b + 1) * seq_len - tail:(b + 1) * seq_len] for b in range(n_seq)]))
        xbc_s = xbc[n_prompt:].reshape(n_dec, GROUP, conv_dim)
        convs_s.append(xbc_s[:, GROUP - tail:])
        ssms_p.append(ssm_p.reshape(n_seq, SSD_HEADS, SSD_HEAD_DIM, D_STATE))
        ssms_s.append(ssm_s.reshape(n_dec, SSD_HEADS, SSD_HEAD_DIM, D_STATE))
        vs_s.append(v[n_prompt:].reshape(n_dec, GROUP, d_gmlp)[:, lead:])

        final = i == depth - 1
        outs = _ple(pos, h2s, pp_all, sample_rows(p_sample[i]), w_ple[i].astype(BF16), row(norm_ple[i]), row(norm_pg[i]),
                    w_pg[i].astype(BF16), row(norm_final) if final else inproj_params(i + 1), final, i, n_prompt)
        hp, hs = outs[:2]
        if not final:
            z, xbc, dt, u, v = outs[2:]

    y_prompt = hp.reshape(n_seq, seq_len, d_model)
    y_sample = hs.reshape(n_dec, GROUP, d_model)[:, lead:]
    return (y_prompt, y_sample, jnp.stack(convs_p), jnp.stack(ssms_p), jnp.stack(convs_s), jnp.stack(ssms_s),
            jnp.stack(vs_s))
```

```python
import functools
import math

import jax
import jax.numpy as jnp
from jax import lax
from jax.experimental import pallas as pl
from jax.experimental.pallas import tpu as pltpu

F32 = jnp.float32
BF16 = jnp.bfloat16

LANES = 128
VMEM_LIMIT_BYTES = 56 * 1024 * 1024

CONV_WIDTH = 4
SSD_HEADS = 8
SSD_HEAD_DIM = 64
SSD_GROUPS = 2
D_STATE = 128
CHUNK = 128
GMLP_HEADS = 8
N_EXPERT_GROUPS = 4
EXPERTS_PER_GROUP = 4
N_PAIRS = 6
N_BUCKETS = N_EXPERT_GROUPS * N_PAIRS
EPS = 1e-6

GROUP = 8
DEC_SEQ_ROWS = 4
TOK_TILE = 512
MOE_TILE = 256
PROMPT_TILES_PER_STEP = 4
ROUTE_ROWS = 32
X_PITCH = 9
Y_PITCH = 8
NEG_BIG = -1e30


def _dot(a, b):
    return jnp.dot(a, b, preferred_element_type=F32)


def _dot_nt(a, b):
    return lax.dot_general(a, b, (((1,), (1,)), ((), ())), preferred_element_type=F32)


def _rms(x, g):
    ms = jnp.mean(x * x, axis=-1, keepdims=True)
    return (x * lax.rsqrt(ms + EPS)) * g


def _gelu(x):
    return 0.5 * x * (1.0 + lax.erf(x * (1.0 / math.sqrt(2.0))))


def _sigmoid(x):
    return 0.5 * jnp.tanh(0.5 * x) + 0.5


def _softplus(x):
    return jnp.maximum(x, 0.0) + jnp.log1p(jnp.exp(-jnp.abs(x)))


def _params(n_grid):
    return pltpu.CompilerParams(dimension_semantics=("arbitrary",) * n_grid,
                                vmem_limit_bytes=VMEM_LIMIT_BYTES)


def _pair_specs(width, n_prompt_tiles, prompt_tile0=0):
    return [pl.BlockSpec((TOK_TILE, width), lambda i: (prompt_tile0 + jnp.minimum(i, n_prompt_tiles - 1), 0)),
            pl.BlockSpec((TOK_TILE, width), lambda i: (jnp.maximum(i - n_prompt_tiles, 0), 0))]


def _pair_load(p_ref, s_ref, n_prompt_tiles):
    return jnp.where(pl.program_id(0) < n_prompt_tiles, p_ref[...], s_ref[...])


def _inproj_tile(h, nm_ref, wz_ref, wx_ref, wdt_ref, wu_ref, wv_ref, dtb_ref, gn_ref,
                 z_ref, xbc_ref, dt_ref, u_ref, v_ref):
    a = _rms(h, nm_ref[...]).astype(BF16)
    z_ref[...] = _dot(a, wz_ref[...])
    xbc_ref[...] = _dot(a, wx_ref[...])
    dt_ref[...] = _softplus(_dot(a, wdt_ref[...]) + dtb_ref[...])
    u_ref[...] = _gelu(_dot(a, wu_ref[...]))
    v_ref[...] = _rms(_gelu(_dot(a, wv_ref[...])), gn_ref[...])


def _inproj_kernel(hp_ref, hs_ref, *refs, n_prompt_tiles):
    _inproj_tile(_pair_load(hp_ref, hs_ref, n_prompt_tiles), *refs)


def _inproj_specs(weights, t):
    nm, wz, wx, wdt, wu, wv, dtb, gn = weights
    widths = (wz.shape[1], wx.shape[1], wdt.shape[1], wu.shape[1], wv.shape[1])
    row = lambda w: pl.BlockSpec((TOK_TILE, w), lambda i: (i, 0))
    full = lambda a: pl.BlockSpec(a.shape, lambda i: (0,) * a.ndim)
    return ([full(a) for a in weights], [row(w) for w in widths],
            [jax.ShapeDtypeStruct((t, w), F32) for w in widths])


def _inproj(hp, hs, weights):
    t = hp.shape[0] + hs.shape[0]
    n_prompt_tiles = hp.shape[0] // TOK_TILE
    w_specs, out_specs, out_shape = _inproj_specs(weights, t)
    return pl.pallas_call(
        functools.partial(_inproj_kernel, n_prompt_tiles=n_prompt_tiles),
        grid=(t // TOK_TILE,),
        in_specs=_pair_specs(hp.shape[1], n_prompt_tiles) + w_specs,
        out_specs=out_specs,
        out_shape=out_shape,
        compiler_params=_params(1),
        name="inproj",
    )(hp, hs, *weights)


def _seg_cumsum(x, seg, rowmod):
    d = 1
    while d < seg:
        x = x + jnp.where(rowmod >= d, pltpu.roll(x, d, axis=0), 0.0)
        d *= 2
    return x


def _seg_rev_cumsum(x, seg, rowmod):
    n = x.shape[0]
    d = 1
    while d < seg:
        x = x + jnp.where(rowmod + d < seg, pltpu.roll(x, n - d, axis=0), 0.0)
        d *= 2
    return x


def _expand_heads(m, lane_lt_half):
    parts = []
    for j in range(SSD_HEADS // 2):
        parts.append(jnp.where(lane_lt_half, m[:, 2 * j:2 * j + 1], m[:, 2 * j + 1:2 * j + 2]))
    return jnp.concatenate(parts, axis=1)


def _merge_head_pairs(per_head, lane_lt_half):
    parts = [jnp.where(lane_lt_half, per_head[2 * j], per_head[2 * j + 1]) for j in range(len(per_head) // 2)]
    return jnp.concatenate(parts, axis=1)


def _mixer_kernel(*refs, sample, n_inner, n_prev=0):
    n_tok = 6 if sample else 5
    tok_refs, rest = refs[:n_tok], refs[n_tok:]
    if sample:
        if n_prev:
            prev_ref, rest = rest[0], rest[1:]
        (s0_ref, cw_ref, cb_ref, arow_ref, dsk_ref, sn_ref, wsp_ref, bsp_ref,
         y_ref, stack_ref, ext_ref, yoff_ref) = rest
        if n_prev:
            stack_ref[0:n_prev] = prev_ref[...]
        sout_ref = stack_ref.at[n_prev]
        first = pl.program_id(0) == 0
    else:
        (cw_ref, cb_ref, arow_ref, dsk_ref, sn_ref, wsp_ref, bsp_ref, y_ref, sout_ref, ext_ref) = rest
        s0_ref = yoff_ref = None
        first = pl.program_id(1) == 0

    @pl.when(first)
    def _():
        ext_ref[...] = jnp.zeros(ext_ref.shape, F32)
        if not sample:
            sout_ref[...] = jnp.zeros(sout_ref.shape, F32)

    def tile(c, carry):
        r0 = pl.multiple_of(c * CHUNK, CHUNK)
        views = [r.at[pl.ds(r0, CHUNK)] for r in tok_refs + (y_ref,)]
        _mixer_tile(*views[:n_tok], s0_ref, cw_ref, cb_ref, arow_ref, dsk_ref, sn_ref, wsp_ref, bsp_ref,
                    views[n_tok], sout_ref, ext_ref, yoff_ref, sample=sample)
        return carry

    if n_inner == 1:
        tile(0, 0)
    else:
        lax.fori_loop(0, n_inner, tile, 0)


def _mixer_tile(z_ref, xbc_ref, dt_ref, u_ref, v_ref, *rest, sample):
    if sample:
        cs_ref, s0_ref = rest[0], rest[1]
        rest = rest[2:]
    else:
        rest = rest[1:]
    cw_ref, cb_ref, arow_ref, dsk_ref, sn_ref, wsp_ref, bsp_ref, y_ref, sout_ref, ext_ref, yoff_ref = rest
    seg = GROUP if sample else CHUNK
    d_ssd = SSD_HEADS * SSD_HEAD_DIM
    gw = d_ssd // SSD_GROUPS
    hpg = SSD_HEADS // SSD_GROUPS
    cs_first = GROUP - DEC_SEQ_ROWS - (CONV_WIDTH - 1)

    rows = lax.broadcasted_iota(jnp.int32, (CHUNK, LANES), 0)
    cols = lax.broadcasted_iota(jnp.int32, (CHUNK, LANES), 1)
    rowmod = rows & (seg - 1)
    lane_lt_half = cols < SSD_HEAD_DIM

    xbc = xbc_ref[...]
    if sample:
        rm = lax.broadcasted_iota(jnp.int32, xbc.shape, 0) & (GROUP - 1)
        xbc = jnp.where((rm >= cs_first) & (rm < cs_first + CONV_WIDTH - 1), cs_ref[...], xbc)

    tail = ext_ref[...]
    row8 = lax.broadcasted_iota(jnp.int32, tail.shape, 0)
    acc = cb_ref[...] + cw_ref[CONV_WIDTH - 1:CONV_WIDTH, :] * xbc
    for j in range(1, CONV_WIDTH):
        rolled = pltpu.roll(xbc, j, axis=0)
        head = jnp.where(row8 < j, pltpu.roll(tail, j, axis=0), rolled[0:8, :])
        shifted = jnp.concatenate([head, rolled[8:, :]], axis=0)
        acc = acc + cw_ref[CONV_WIDTH - 1 - j:CONV_WIDTH - j, :] * shifted
    if not sample:
        ext_ref[...] = xbc[CHUNK - 8:, :]
    xc = acc * _sigmoid(acc)
    x = xc[:, :d_ssd]
    bb = xc[:, d_ssd:d_ssd + SSD_GROUPS * D_STATE].astype(BF16)
    cm = xc[:, d_ssd + SSD_GROUPS * D_STATE:]
    cbf = cm.astype(BF16)

    dtc = dt_ref[...]
    if sample:
        dtc = jnp.where(rowmod >= GROUP - DEC_SEQ_ROWS, dtc, 0.0)
    da = dtc * arow_ref[...]
    cum = _seg_cumsum(da, seg, rowmod)
    rev = _seg_rev_cumsum(da, seg, rowmod) - da
    cum_t = cum.T
    ecum = jnp.exp(cum)
    dt_e = _expand_heads(dtc, lane_lt_half)
    ecum_e = _expand_heads(ecum, lane_lt_half)
    erev_e = _expand_heads(jnp.exp(rev), lane_lt_half)

    xdt = x * dt_e
    xdt_bf = xdt.astype(BF16)
    causal = rows >= cols
    if sample:
        causal = causal & ((rows >> 3) == (cols >> 3))

    yd = []
    for g in range(SSD_GROUPS):
        cb_g = _dot_nt(cbf[:, g * D_STATE:(g + 1) * D_STATE], bb[:, g * D_STATE:(g + 1) * D_STATE])
        for hh in range(hpg):
            h = g * hpg + hh
            expo = cum[:, h:h + 1] - cum_t[h:h + 1, :]
            w = (cb_g * jnp.exp(jnp.where(causal, expo, NEG_BIG))).astype(BF16)
            j = h // 2
            yd.append(_dot(w, xdt_bf[:, j * LANES:(j + 1) * LANES]))
    y_diag = _merge_head_pairs(yd, lane_lt_half)

    if sample:
        for i in range(CHUNK // GROUP):
            for g in range(SSD_GROUPS):
                s_g = s0_ref[i, g * gw:(g + 1) * gw, :].astype(BF16)
                yoff_ref[i * GROUP:(i + 1) * GROUP, g * gw:(g + 1) * gw] = _dot_nt(
                    cm[i * GROUP:(i + 1) * GROUP, g * D_STATE:(g + 1) * D_STATE].astype(BF16), s_g)
        y_off = yoff_ref[...]
    else:
        y_off = jnp.concatenate(
            [_dot_nt(cbf[:, g * D_STATE:(g + 1) * D_STATE], sout_ref[0, g * gw:(g + 1) * gw, :].astype(BF16))
             for g in range(SSD_GROUPS)], axis=1)
    y = y_diag + y_off * ecum_e + dsk_ref[...] * x

    xd = xdt * erev_e
    for g in range(SSD_GROUPS):
        xd_t = xd[:, g * gw:(g + 1) * gw].T
        b_g = bb[:, g * D_STATE:(g + 1) * D_STATE]
        if sample:
            tcols = lax.broadcasted_iota(jnp.int32, xd_t.shape, 1) >> 3
            for i in range(CHUNK // GROUP):
                upd = _dot(jnp.where(tcols == i, xd_t, 0.0).astype(BF16), b_g)
                last = i * GROUP + GROUP - 1
                for hh in range(hpg):
                    h = g * hpg + hh
                    r0 = h * SSD_HEAD_DIM
                    sout_ref[i, r0:r0 + SSD_HEAD_DIM, :] = (
                        s0_ref[i, r0:r0 + SSD_HEAD_DIM, :] * ecum[last:last + 1, h:h + 1]
                        + upd[hh * SSD_HEAD_DIM:(hh + 1) * SSD_HEAD_DIM, :])
        else:
            upd = _dot(xd_t.astype(BF16), b_g)
            for hh in range(hpg):
                h = g * hpg + hh
                r0 = h * SSD_HEAD_DIM
                sout_ref[0, r0:r0 + SSD_HEAD_DIM, :] = (
                    sout_ref[0, r0:r0 + SSD_HEAD_DIM, :] * ecum[CHUNK - 1:CHUNK, h:h + 1]
                    + upd[hh * SSD_HEAD_DIM:(hh + 1) * SSD_HEAD_DIM, :])

    zf = z_ref[...]
    yf = y * (zf * _sigmoid(zf))
    parts = []
    for g in range(SSD_GROUPS):
        part = yf[:, g * gw:(g + 1) * gw]
        ms = jnp.mean(part * part, axis=-1, keepdims=True)
        parts.append(part * lax.rsqrt(ms + EPS))
    y_ssd = jnp.concatenate(parts, axis=1) * sn_ref[...]

    vb = v_ref[...].astype(BF16)
    sg = [_dot(wsp_ref[h], vb[:, (h // 2) * LANES:(h // 2 + 1) * LANES]) for h in range(GMLP_HEADS)]
    s = _merge_head_pairs(sg, lane_lt_half) + bsp_ref[...]
    y_gm = u_ref[...] * s

    y_ref[:, :d_ssd] = y_ssd.astype(BF16)
    y_ref[:, d_ssd:] = y_gm.astype(BF16)


def _mixer_prompt(z, xbc, dt, u, v, cw, cb, arow, dsk, sn, wsp, bsp, n_seq, n_chunk):
    n_inner = math.gcd(n_chunk, PROMPT_TILES_PER_STEP)
    n_outer = n_chunk // n_inner
    tok = lambda w: pl.BlockSpec((n_inner * CHUNK, w), lambda b, c: (b * n_outer + c, 0))
    full = lambda a: pl.BlockSpec(a.shape, lambda b, c: (0,) * a.ndim)
    d_ssd = z.shape[1]
    d_mix = d_ssd + u.shape[1]
    return pl.pallas_call(
        functools.partial(_mixer_kernel, sample=False, n_inner=n_inner),
        grid=(n_seq, n_outer),
        in_specs=[tok(z.shape[1]), tok(xbc.shape[1]), tok(dt.shape[1]), tok(u.shape[1]), tok(v.shape[1]),
                  full(cw), full(cb), full(arow), full(dsk), full(sn), full(wsp), full(bsp)],
        out_specs=[tok(d_mix),
                   pl.BlockSpec((1, d_ssd, D_STATE), lambda b, c: (b, 0, 0))],
        out_shape=[jax.ShapeDtypeStruct((n_seq * n_chunk * CHUNK, d_mix), BF16),
                   jax.ShapeDtypeStruct((n_seq, d_ssd, D_STATE), F32)],
        scratch_shapes=[pltpu.VMEM((8, xbc.shape[1]), F32)],
        compiler_params=_params(2),
        name="mixer_prompt",
    )(z, xbc, dt, u, v, cw, cb, arow, dsk, sn, wsp, bsp)


def _mixer_sample(z, xbc, dt, u, v, cs, s0_all, prev, cw, cb, arow, dsk, sn, wsp, bsp, row0, n_rows, layer):
    blk0 = row0 // CHUNK
    n_seq_blk = CHUNK // GROUP
    n_blk = n_rows // CHUNK
    tok = lambda w: pl.BlockSpec((CHUNK, w), lambda i: (blk0 + i, 0))
    full = lambda a: pl.BlockSpec(a.shape, lambda i: (0,) * a.ndim)
    d_ssd = z.shape[1]
    d_mix = d_ssd + u.shape[1]
    st = (n_seq_blk, d_ssd, D_STATE)
    stack = lambda n: pl.BlockSpec((n,) + st, lambda i: (0, i, 0, 0))
    prev_args = [] if layer == 0 else [prev]
    return pl.pallas_call(
        functools.partial(_mixer_kernel, sample=True, n_inner=1, n_prev=layer),
        grid=(n_blk,),
        in_specs=[tok(z.shape[1]), tok(xbc.shape[1]), tok(dt.shape[1]), tok(u.shape[1]), tok(v.shape[1]),
                  pl.BlockSpec((CHUNK, cs.shape[1]), lambda i: (i, 0))]
        + [stack(layer) for _ in prev_args]
        + [pl.BlockSpec(st, lambda i: (layer * n_blk + i, 0, 0)),
           full(cw), full(cb), full(arow), full(dsk), full(sn), full(wsp), full(bsp)],
        out_specs=[pl.BlockSpec((CHUNK, d_mix), lambda i: (i, 0)), stack(layer + 1)],
        out_shape=[jax.ShapeDtypeStruct((n_rows, d_mix), BF16),
                   jax.ShapeDtypeStruct((layer + 1, n_blk * n_seq_blk, d_ssd, D_STATE), F32)],
        scratch_shapes=[pltpu.VMEM((8, xbc.shape[1]), F32), pltpu.VMEM((CHUNK, d_ssd), F32)],
        compiler_params=_params(1),
        name="mixer_sample",
    )(z, xbc, dt, u, v, cs, *prev_args, s0_all, cw, cb, arow, dsk, sn, wsp, bsp)


def _first_argmax(vals):
    m = vals[0]
    for v in vals[1:]:
        m = jnp.maximum(m, v)
    idx = jnp.full(m.shape, len(vals) - 1, jnp.int32)
    for k in range(len(vals) - 2, -1, -1):
        idx = jnp.where(vals[k] >= m, k, idx)
    return m, idx


def _slab_store(slab_ref, x, pitch, tok0=0):
    for k in range(x.shape[1] // LANES):
        slab_ref[pl.ds(tok0 * pitch + k, x.shape[0], stride=pitch), :] = x[:, k * LANES:(k + 1) * LANES]


def _slab_load(slab_ref, rows, n_pieces, pitch, tok0=0):
    return jnp.concatenate([slab_ref[pl.ds(tok0 * pitch + k, rows, stride=pitch), :] for k in range(n_pieces)],
                           axis=1)


def _post_kernel(yp_ref, ys_ref, hp_ref, hs_ref, wo_ref, nf_ref, wr_ref, br_ref,
                 slab_ref, route_ref, cnt_ref, carry_ref, earlier_ref, *, n_prompt_tiles):
    i = pl.program_id(0)
    tm, d = hp_ref.shape

    @pl.when(i == 0)
    def _():
        carry_ref[...] = jnp.zeros(carry_ref.shape, F32)
        tr = lax.broadcasted_iota(jnp.int32, (tm, tm), 0)
        tc = lax.broadcasted_iota(jnp.int32, (tm, tm), 1)
        earlier_ref[...] = jnp.where(tr < tc, 1.0, 0.0).astype(BF16)

    ym = _pair_load(yp_ref, ys_ref, n_prompt_tiles)
    h1 = _pair_load(hp_ref, hs_ref, n_prompt_tiles) + _dot(ym, wo_ref[...])
    _slab_store(slab_ref, h1, X_PITCH)

    t = _rms(h1, nf_ref[...])
    wr = wr_ref[...]
    wr_hi = wr.astype(BF16)
    wr_lo = (wr - wr_hi.astype(F32)).astype(BF16)
    t_hi = t.astype(BF16)
    t_lo = (t - t_hi.astype(F32)).astype(BF16)
    by_hi = _dot_nt(jnp.concatenate([wr_hi, wr_lo], axis=0), t_hi)
    logits = (by_hi[:ROUTE_ROWS] + (by_hi[ROUTE_ROWS:] + _dot_nt(wr_hi, t_lo))) + br_ref[...]
    lg = [logits[k:k + 1, :] for k in range(N_EXPERT_GROUPS)]
    m, g = _first_argmax(lg)
    ssum = jnp.exp(lg[0] - m)
    for k in range(1, N_EXPERT_GROUPS):
        ssum = ssum + jnp.exp(lg[k] - m)
    p_sel = 1.0 / ssum
    le = [logits[N_EXPERT_GROUPS + e:N_EXPERT_GROUPS + e + 1, :] for e in range(N_EXPERT_GROUPS * EXPERTS_PER_GROUP)]
    a = []
    for k in range(EXPERTS_PER_GROUP):
        sel = le[(N_EXPERT_GROUPS - 1) * EXPERTS_PER_GROUP + k]
        for gi in range(N_EXPERT_GROUPS - 2, -1, -1):
            sel = jnp.where(g == gi, le[gi * EXPERTS_PER_GROUP + k], sel)
        a.append(sel)
    v1, i1 = _first_argmax(a)
    a2 = [jnp.where(i1 == k, -jnp.inf, a[k]) for k in range(EXPERTS_PER_GROUP)]
    v2, i2 = _first_argmax(a2)
    e2 = jnp.exp(v2 - v1)
    den = 1.0 + e2
    g1 = (1.0 / den) * p_sel
    g2 = (e2 / den) * p_sel
    lo = jnp.minimum(i1, i2)
    hi = jnp.maximum(i1, i2)
    c_lo = jnp.where(i1 < i2, g1, g2)
    c_hi = jnp.where(i1 < i2, g2, g1)
    pair = jnp.where(lo == 0, hi - 1, jnp.where(lo == 1, hi + 1, N_PAIRS - 1))
    bucket = g * N_PAIRS + pair

    brow = lax.broadcasted_iota(jnp.int32, (ROUTE_ROWS, tm), 0)
    onehot = jnp.where(brow == bucket, 1.0, 0.0)
    prefix = _dot(onehot.astype(BF16), earlier_ref[...])
    carry = carry_ref[:, 0:1]
    rank = jnp.sum(onehot * (prefix + carry), axis=0, keepdims=True)
    carry = carry + jnp.sum(onehot, axis=1, keepdims=True)
    carry_b = jnp.broadcast_to(carry, carry_ref.shape)
    carry_ref[...] = carry_b
    cnt_ref[...] = carry_b

    rr = lax.broadcasted_iota(jnp.int32, (8, tm), 0)
    route_ref[...] = jnp.where(rr == 0, bucket.astype(F32), jnp.where(rr == 1, rank, 0.0))
    ar = lax.broadcasted_iota(jnp.int32, (LANES, tm), 0)
    aux = jnp.where(ar == 0, c_lo, jnp.where(ar == 1, c_hi, 0.0))
    slab_ref[pl.ds(d // LANES, tm, stride=X_PITCH), :] = aux.T


def _post(yp, ys, hp, hs, wo, nf, wr, br):
    d = hp.shape[1]
    t = hp.shape[0] + hs.shape[0]
    n_prompt_tiles = hp.shape[0] // TOK_TILE
    full = lambda a: pl.BlockSpec(a.shape, lambda i: (0,) * a.ndim)
    return pl.pallas_call(
        functools.partial(_post_kernel, n_prompt_tiles=n_prompt_tiles),
        grid=(t // TOK_TILE,),
        in_specs=_pair_specs(d, n_prompt_tiles) + _pair_specs(d, n_prompt_tiles)
        + [full(wo), full(nf), full(wr), full(br)],
        out_specs=[pl.BlockSpec((TOK_TILE * X_PITCH, LANES), lambda i: (i, 0)),
                   pl.BlockSpec((8, TOK_TILE), lambda i: (0, i)),
                   pl.BlockSpec((ROUTE_ROWS, LANES), lambda i: (0, 0))],
        out_shape=[jax.ShapeDtypeStruct((t * X_PITCH, LANES), F32),
                   jax.ShapeDtypeStruct((8, t), F32),
                   jax.ShapeDtypeStruct((ROUTE_ROWS, LANES), F32)],
        scratch_shapes=[pltpu.VMEM((ROUTE_ROWS, LANES), F32), pltpu.VMEM((TOK_TILE, TOK_TILE), BF16)],
        compiler_params=_params(1),
        name="post",
    )(yp, ys, hp, hs, wo, nf, wr, br)


def _scatter_kernel(pos_ref, src_ref, _, dst_ref, sem, *, pitch):
    n_tok = src_ref.shape[0] // pitch
    for r in range(n_tok):
        pltpu.make_async_copy(src_ref.at[pl.ds(r * pitch, pitch)],
                              dst_ref.at[pl.ds(pos_ref[0, 0, r] * pitch, pitch)], sem).start()
    pltpu.make_async_copy(src_ref, dst_ref.at[pl.ds(0, n_tok * pitch)], sem).wait()


def _scatter_tokens(pos, src, n_out, pitch, init=None):
    n = pos.shape[0]
    pos3 = pos.reshape(n // TOK_TILE, 1, TOK_TILE)
    hbm = pl.BlockSpec(memory_space=pl.ANY)
    if init is None:
        init = jnp.zeros((n_out * pitch, LANES), src.dtype)
    return pl.pallas_call(
        functools.partial(_scatter_kernel, pitch=pitch),
        grid=(n // TOK_TILE,),
        in_specs=[pl.BlockSpec((1, 1, TOK_TILE), lambda i: (i, 0, 0), memory_space=pltpu.SMEM),
                  pl.BlockSpec((TOK_TILE * pitch, LANES), lambda i: (i, 0)), hbm],
        out_specs=hbm,
        out_shape=jax.ShapeDtypeStruct((n_out * pitch, LANES), src.dtype),
        scratch_shapes=[pltpu.SemaphoreType.DMA(())],
        input_output_aliases={2: 0},
        compiler_params=_params(1),
        name="scatter_tokens",
    )(pos3, src, init)


def _moe_kernel(elo_ref, ehi_ref, nused_ref, x_ref, nf_ref, wg_lo, wu_lo, wd_lo, wg_hi, wu_hi, wd_hi, o_ref):
    i = pl.program_id(0)
    n_pieces = nf_ref.shape[1] // LANES

    @pl.when(i < nused_ref[0])
    def _():
        x = _slab_load(x_ref, MOE_TILE, n_pieces, X_PITCH)
        gates = x_ref[pl.ds(n_pieces, MOE_TILE, stride=X_PITCH), :]
        c_lo = gates[:, 0:1]
        c_hi = gates[:, 1:2]
        t = _rms(x, nf_ref[...]).astype(BF16)

        def expert(wg, wu, wd):
            gate = _dot(t, wg[...])
            hid = (gate * _sigmoid(gate)) * _dot(t, wu[...])
            return _dot(hid.astype(BF16), wd[...])

        y = c_lo * expert(wg_lo, wu_lo, wd_lo)
        y = y + c_hi * expert(wg_hi, wu_hi, wd_hi)
        _slab_store(o_ref, x + y, Y_PITCH)

    @pl.when(i >= nused_ref[0])
    def _():
        o_ref[...] = jnp.zeros(o_ref.shape, F32)


def _moe(e_lo, e_hi, n_used, xs, nf, wg, wu, wd):
    n_tiles = xs.shape[0] // (MOE_TILE * X_PITCH)
    d, de = wg.shape[1:]
    lo = lambda i, elo, ehi, nu: (elo[i], 0, 0)
    hi = lambda i, elo, ehi, nu: (ehi[i], 0, 0)
    grid_spec = pltpu.PrefetchScalarGridSpec(
        num_scalar_prefetch=3,
        grid=(n_tiles,),
        in_specs=[pl.BlockSpec((MOE_TILE * X_PITCH, LANES),
                               lambda i, elo, ehi, nu: (jnp.maximum(jnp.minimum(i, nu[0] - 1), 0), 0)),
                  pl.BlockSpec(nf.shape, lambda i, elo, ehi, nu: (0, 0)),
                  pl.BlockSpec((None, d, de), lo), pl.BlockSpec((None, d, de), lo), pl.BlockSpec((None, de, d), lo),
                  pl.BlockSpec((None, d, de), hi), pl.BlockSpec((None, d, de), hi), pl.BlockSpec((None, de, d), hi)],
        out_specs=pl.BlockSpec((MOE_TILE * Y_PITCH, LANES), lambda i, elo, ehi, nu: (i, 0)),
    )
    return pl.pallas_call(
        _moe_kernel,
        grid_spec=grid_spec,
        out_shape=jax.ShapeDtypeStruct((n_tiles * MOE_TILE * Y_PITCH, LANES), F32),
        compiler_params=_params(1),
        name="expert_pairs",
    )(e_lo, e_hi, n_used, xs, nf, wg, wu, wd, wg, wu, wd)


def _ple_kernel(pos_ref, pos_next_ref, h_hbm, pp_ref, ps_ref, wple_ref, nple_ref, npg_ref, wpg_ref, *refs,
                final, n_prompt_tiles):
    refs, (xbuf, sem) = refs[:-2], refs[-2:]
    i = pl.program_id(0)
    last = pl.num_programs(0) - 1
    slot = i % 2

    def start_tile(p_ref, s):
        for r in range(TOK_TILE):
            pltpu.make_async_copy(h_hbm.at[pl.ds(p_ref[0, 0, r] * Y_PITCH, Y_PITCH)],
                                  xbuf.at[s, pl.ds(r * Y_PITCH, Y_PITCH)], sem.at[s]).start()

    def wait_tile(s):
        pltpu.make_async_copy(h_hbm.at[pl.ds(0, TOK_TILE * Y_PITCH)], xbuf.at[s], sem.at[s]).wait()

    @pl.when(i == 0)
    def _():
        start_tile(pos_ref, 0)

    wait_tile(slot)
    start_tile(pos_next_ref, 1 - slot)
    h2 = _slab_load(xbuf.at[slot], TOK_TILE, Y_PITCH, Y_PITCH)
    p = _pair_load(pp_ref, ps_ref, n_prompt_tiles)
    e = _rms(_dot(p.astype(BF16), wple_ref[...]), nple_ref[...])
    gate = _sigmoid(_dot(_rms(h2, npg_ref[...]).astype(BF16), wpg_ref[...]))
    h3 = h2 + gate * e
    if final:
        nfin_ref, op_ref, os_ref = refs
        out = _rms(h3, nfin_ref[...])
    else:
        op_ref, os_ref = refs[N_INPROJ_PARAMS:N_INPROJ_PARAMS + 2]
        out = h3
        _inproj_tile(h3, *refs[:N_INPROJ_PARAMS], *refs[N_INPROJ_PARAMS + 2:])

    @pl.when(i < n_prompt_tiles)
    def _():
        op_ref[...] = out

    @pl.when(i >= n_prompt_tiles)
    def _():
        os_ref[...] = out

    @pl.when(i == last)
    def _():
        wait_tile(1 - slot)


N_INPROJ_PARAMS = 8


def _ple(pos, h2s, pp_all, ps, wple, nple, npg, wpg, tail, final, layer, n_prompt):
    t = pos.shape[0]
    n_steps = t // TOK_TILE
    pos3 = pos.reshape(n_steps, 1, TOK_TILE)
    d = Y_PITCH * LANES
    n_prompt_tiles = n_prompt // TOK_TILE
    full = lambda a: pl.BlockSpec(a.shape, lambda i: (0,) * a.ndim)
    out_specs = _pair_specs(d, n_prompt_tiles)
    out_shape = [jax.ShapeDtypeStruct((n_prompt, d), F32), jax.ShapeDtypeStruct((ps.shape[0], d), F32)]
    if final:
        tail = (tail,)
        tail_specs = [full(tail[0])]
    else:
        assert len(tail) == N_INPROJ_PARAMS
        tail_specs, io, ish = _inproj_specs(tail, t)
        out_specs, out_shape = out_specs + io, out_shape + ish
    return pl.pallas_call(
        functools.partial(_ple_kernel, final=final, n_prompt_tiles=n_prompt_tiles),
        grid=(n_steps,),
        in_specs=[pl.BlockSpec((1, 1, TOK_TILE), lambda i: (i, 0, 0), memory_space=pltpu.SMEM),
                  pl.BlockSpec((1, 1, TOK_TILE), lambda i: (jnp.minimum(i + 1, n_steps - 1), 0, 0),
                               memory_space=pltpu.SMEM),
                  pl.BlockSpec(memory_space=pl.ANY)]
        + _pair_specs(pp_all.shape[1], n_prompt_tiles, layer * n_prompt_tiles)
        + [full(wple), full(nple), full(npg), full(wpg)] + tail_specs,
        out_specs=out_specs,
        out_shape=out_shape,
        scratch_shapes=[pltpu.VMEM((2, TOK_TILE * Y_PITCH, LANES), F32), pltpu.SemaphoreType.DMA((2,))],
        compiler_params=_params(1),
        name="ple_final" if final else "ple_inproj",
    )(pos3, pos3, h2s, pp_all, ps, wple, nple, npg, wpg, *tail)


_PAIR_LO = (0, 0, 0, 1, 1, 2)
_PAIR_HI = (1, 2, 3, 2, 3, 3)


def _routing_tables(route, cnt, n_tiles):
    counts = cnt[:N_BUCKETS, 0].astype(jnp.int32)
    padded = ((counts + MOE_TILE - 1) // MOE_TILE) * MOE_TILE
    ends = jnp.cumsum(padded)
    starts = ends - padded
    bucket = route[0].astype(jnp.int32)
    rank = route[1].astype(jnp.int32)
    pos = starts[bucket] + rank
    n_used = ends[-1] // MOE_TILE
    tile = jnp.minimum(jnp.arange(n_tiles, dtype=jnp.int32), n_used - 1)
    tb = jnp.sum((ends[None, :] <= (tile * MOE_TILE)[:, None]).astype(jnp.int32), axis=1)
    tb = jnp.minimum(tb, N_BUCKETS - 1)
    grp = tb // N_PAIRS
    pr = tb % N_PAIRS
    e_lo = grp * EXPERTS_PER_GROUP + jnp.asarray(_PAIR_LO, jnp.int32)[pr]
    e_hi = grp * EXPERTS_PER_GROUP + jnp.asarray(_PAIR_HI, jnp.int32)[pr]
    return pos, e_lo, e_hi, n_used.reshape(1)


def kernel(x_prompt, x_sample, state_conv, state_ssm, p_prompt, p_sample, norm_mix, w_in, conv_w, conv_b,
           dt_bias, a_log, d_skip, ssd_norm, gmlp_norm, w_spatial, b_spatial, w_out, norm_ffn,
           w_router_group, b_router_group, w_router_expert, b_router_expert, w_gate, w_up, w_down,
           w_ple, norm_ple, norm_pg, w_pg, norm_final):
    n_seq, seq_len, d_model = x_prompt.shape
    n_dec, dec_seq, _ = x_sample.shape
    depth = w_in.shape[0]
    conv_dim = conv_w.shape[2]
    d_ssd = SSD_HEADS * SSD_HEAD_DIM
    d_gmlp = gmlp_norm.shape[1]
    assert dec_seq == DEC_SEQ_ROWS and conv_w.shape[1] == CONV_WIDTH and seq_len % CHUNK == 0
    assert conv_dim == d_ssd + 2 * SSD_GROUPS * D_STATE and w_spatial.shape[1:] == (GMLP_HEADS, CHUNK, CHUNK)
    n_chunk = seq_len // CHUNK
    n_prompt = n_seq * seq_len
    n_srows = n_dec * GROUP
    t_all = n_prompt + n_srows
    assert n_prompt % TOK_TILE == 0 and n_srows % TOK_TILE == 0
    assert d_model == Y_PITCH * LANES and X_PITCH == Y_PITCH + 1
    lead = GROUP - dec_seq
    n_moe_tiles = t_all // MOE_TILE + N_BUCKETS

    def sample_rows(a):
        return jnp.pad(a, ((0, 0), (lead, 0), (0, 0))).reshape(n_srows, a.shape[-1])

    hp, hs = x_prompt.reshape(n_prompt, d_model), sample_rows(x_sample)

    o_xbc = d_ssd
    o_dt = o_xbc + conv_dim
    o_uv = o_dt + SSD_HEADS
    head_cols = jnp.arange(d_ssd) // SSD_HEAD_DIM
    seq_eye = jnp.eye(CHUNK // GROUP, dtype=F32)
    tril =jnp.tril(jnp.ones((CHUNK, CHUNK), bool))

    n_exp = w_gate.shape[1]
    wg_all = w_gate.astype(BF16).reshape((depth * n_exp,) + w_gate.shape[2:])
    wu_all = w_up.astype(BF16).reshape((depth * n_exp,) + w_up.shape[2:])
    wd_all = w_down.astype(BF16).reshape((depth * n_exp,) + w_down.shape[2:])
    s0_all = state_ssm.reshape(depth * n_dec, d_ssd, D_STATE)
    pp_all = p_prompt.reshape(depth * n_prompt, p_prompt.shape[-1])
    xs = None

    ssm_s = None
    convs_p, ssms_p, convs_s, vs_s = [], [], [], []
    row = lambda a: a.reshape(1, -1).astype(F32)
    tail = CONV_WIDTH - 1
    tail_rows = (((jnp.arange(n_seq) + 1) * seq_len)[:, None] - tail + jnp.arange(tail)[None, :]).reshape(-1)

    def inproj_params(i):
        wi = w_in[i]
        wdt = jnp.pad(wi[:, o_dt:o_uv], ((0, 0), (0, LANES - SSD_HEADS)))
        dtb = jnp.pad(dt_bias[i].astype(F32), (0, LANES - SSD_HEADS)).reshape(1, LANES)
        return (row(norm_mix[i]), wi[:, :o_xbc].astype(BF16), wi[:, o_xbc:o_dt].astype(BF16), wdt.astype(BF16),
                wi[:, o_uv:o_uv + d_gmlp].astype(BF16), wi[:, o_uv + d_gmlp:].astype(BF16), dtb, row(gmlp_norm[i]))

    z, xbc, dt, u, v = _inproj(hp, hs, inproj_params(0))

    for i in range(depth):
        arow =jnp.pad(-jnp.exp(a_log[i].astype(F32)), (0, LANES - SSD_HEADS)).reshape(1, LANES)
        dsk = d_skip[i].astype(F32)[head_cols].reshape(1, d_ssd)
        ws_tril = jnp.where(tril, w_spatial[i], 0.0)
        wsp_p = ws_tril.astype(BF16)
        bsp_p = jnp.repeat(b_spatial[i].T, d_gmlp // GMLP_HEADS, axis=1)
        w8 = jnp.pad(ws_tril[:, :dec_seq, :dec_seq], ((0, 0), (lead, 0), (lead, 0)))
        wsp_s = (seq_eye[None, :, None, :, None] * w8[:, None, :, None, :]).reshape(GMLP_HEADS, CHUNK, CHUNK).astype(BF16)
        b8 = jnp.pad(b_spatial[i][:, :dec_seq], ((0, 0), (lead, 0)))
        bsp_s = jnp.repeat(jnp.tile(b8, (1, CHUNK // GROUP)).T, d_gmlp // GMLP_HEADS, axis=1)
        wr = jnp.concatenate([w_router_group[i].T, w_router_expert[i].T,
                              jnp.zeros((ROUTE_ROWS - N_EXPERT_GROUPS * (1 + EXPERTS_PER_GROUP), d_model), F32)], axis=0)
        br = jnp.concatenate([b_router_group[i], b_router_expert[i],
                              jnp.zeros((ROUTE_ROWS - N_EXPERT_GROUPS * (1 + EXPERTS_PER_GROUP),), F32)]).reshape(ROUTE_ROWS, 1)

        yp, ssm_p = _mixer_prompt(z, xbc, dt, u, v, conv_w[i], row(conv_b[i]), arow, dsk, row(ssd_norm[i]),
                                  wsp_p, bsp_p, n_seq, n_chunk)
        cs = jnp.pad(state_conv[i], ((0, 0), (lead - (CONV_WIDTH - 1), dec_seq), (0, 0))).reshape(n_srows, conv_dim)
        ys, ssm_s = _mixer_sample(z, xbc, dt, u, v, cs, s0_all, ssm_s, conv_w[i], row(conv_b[i]), arow, dsk,
                                  row(ssd_norm[i]), wsp_s, bsp_s, n_prompt, n_srows, i)

        slab, route, cnt = _post(yp, ys, hp, hs, w_out[i].astype(BF16), row(norm_ffn[i]), wr, br)
        pos, e_lo, e_hi, n_used = _routing_tables(route, cnt, n_moe_tiles)

        xs = _scatter_tokens(pos, slab, n_moe_tiles * MOE_TILE, X_PITCH, init=xs)
        h2s = _moe(e_lo + i * n_exp, e_hi + i * n_exp, n_used, xs, row(norm_ffn[i]), wg_all, wu_all, wd_all)

        convs_p.append(jnp.take(xbc, tail_rows, axis=0).reshape(n_seq, tail, conv_dim))
        xbc_s = xbc[n_prompt:].reshape(n_dec, GROUP, conv_dim)
        convs_s.append(xbc_s[:, GROUP - tail:])
        ssms_p.append(ssm_p.reshape(n_seq, SSD_HEADS, SSD_HEAD_DIM, D_STATE))
        vs_s.append(v[n_prompt:].reshape(n_dec, GROUP, d_gmlp)[:, lead:])

        final = i == depth - 1
        outs = _ple(pos, h2s, pp_all, sample_rows(p_sample[i]), w_ple[i].astype(BF16), row(norm_ple[i]), row(norm_pg[i]),
                    w_pg[i].astype(BF16), row(norm_final) if final else inproj_params(i + 1), final, i, n_prompt)
        hp, hs = outs[:2]
        if not final:
            z, xbc, dt, u, v = outs[2:]

    y_prompt = hp.reshape(n_seq, seq_len, d_model)
    y_sample = hs.reshape(n_dec, GROUP, d_model)[:, lead:]
    new_ssm_sample = ssm_s.reshape(depth, n_dec, SSD_HEADS, SSD_HEAD_DIM, D_STATE)
    return (y_prompt, y_sample, jnp.stack(convs_p), jnp.stack(ssms_p), jnp.stack(convs_s), new_ssm_sample,
            jnp.stack(vs_s))
```

```python
import functools
import math

import jax
import jax.numpy as jnp
from jax import lax
from jax.experimental import pallas as pl
from jax.experimental.pallas import tpu as pltpu

F32 = jnp.float32
BF16 = jnp.bfloat16

LANES = 128
VMEM_LIMIT_BYTES = 56 * 1024 * 1024

CONV_WIDTH = 4
SSD_HEADS = 8
SSD_HEAD_DIM = 64
SSD_GROUPS = 2
D_STATE = 128
CHUNK = 128
GMLP_HEADS = 8
N_EXPERT_GROUPS = 4
EXPERTS_PER_GROUP = 4
N_PAIRS = 6
N_BUCKETS = N_EXPERT_GROUPS * N_PAIRS
EPS = 1e-6

GROUP = 8
DEC_SEQ_ROWS = 4
TOK_TILE = 512
MOE_TILE = 256
PROMPT_TILES_PER_STEP = 4
ROUTE_ROWS = 32
TOKEN_ROWS = 8
X_PITCH = TOKEN_ROWS + 1
Y_PITCH = TOKEN_ROWS + 1
NEG_BIG = -1e30


def _dot(a, b):
    return jnp.dot(a, b, preferred_element_type=F32)


def _dot_nt(a, b):
    return lax.dot_general(a, b, (((1,), (1,)), ((), ())), preferred_element_type=F32)


def _rms(x, g):
    ms = jnp.mean(x * x, axis=-1, keepdims=True)
    return (x * lax.rsqrt(ms + EPS)) * g


def _gelu(x):
    return 0.5 * x * (1.0 + lax.erf(x * (1.0 / math.sqrt(2.0))))


def _sigmoid(x):
    return 0.5 * jnp.tanh(0.5 * x) + 0.5


def _softplus(x):
    return jnp.maximum(x, 0.0) + jnp.log1p(jnp.exp(-jnp.abs(x)))


def _params(n_grid):
    return pltpu.CompilerParams(dimension_semantics=("arbitrary",) * n_grid,
                                vmem_limit_bytes=VMEM_LIMIT_BYTES)


def _pair_specs(width, n_prompt_tiles, prompt_tile0=0):
    return [pl.BlockSpec((TOK_TILE, width), lambda i: (prompt_tile0 + jnp.minimum(i, n_prompt_tiles - 1), 0)),
            pl.BlockSpec((TOK_TILE, width), lambda i: (jnp.maximum(i - n_prompt_tiles, 0), 0))]


def _pair_load(p_ref, s_ref, n_prompt_tiles):
    return jnp.where(pl.program_id(0) < n_prompt_tiles, p_ref[...], s_ref[...])


def _inproj_tile(h, nm_ref, wz_ref, wx_ref, wdt_ref, wu_ref, wv_ref, dtb_ref, gn_ref,
                 z_ref, xbc_ref, dt_ref, u_ref, v_ref):
    a = _rms(h, nm_ref[...]).astype(BF16)
    z_ref[...] = _dot(a, wz_ref[...])
    xbc_ref[...] = _dot(a, wx_ref[...])
    dt_ref[...] = _softplus(_dot(a, wdt_ref[...]) + dtb_ref[...])
    u_ref[...] = _gelu(_dot(a, wu_ref[...]))
    v_ref[...] = _rms(_gelu(_dot(a, wv_ref[...])), gn_ref[...])


def _inproj_kernel(hp_ref, hs_ref, *refs, n_prompt_tiles):
    _inproj_tile(_pair_load(hp_ref, hs_ref, n_prompt_tiles), *refs)


def _inproj_specs(weights, t):
    nm, wz, wx, wdt, wu, wv, dtb, gn = weights
    widths = (wz.shape[1], wx.shape[1], wdt.shape[1], wu.shape[1], wv.shape[1])
    row = lambda w: pl.BlockSpec((TOK_TILE, w), lambda i: (i, 0))
    full = lambda a: pl.BlockSpec(a.shape, lambda i: (0,) * a.ndim)
    return ([full(a) for a in weights], [row(w) for w in widths],
            [jax.ShapeDtypeStruct((t, w), F32) for w in widths])


def _inproj(hp, hs, weights):
    t = hp.shape[0] + hs.shape[0]
    n_prompt_tiles = hp.shape[0] // TOK_TILE
    w_specs, out_specs, out_shape = _inproj_specs(weights, t)
    return pl.pallas_call(
        functools.partial(_inproj_kernel, n_prompt_tiles=n_prompt_tiles),
        grid=(t // TOK_TILE,),
        in_specs=_pair_specs(hp.shape[1], n_prompt_tiles) + w_specs,
        out_specs=out_specs,
        out_shape=out_shape,
        compiler_params=_params(1),
        name="inproj",
    )(hp, hs, *weights)


def _seg_cumsum(x, seg, rowmod):
    d = 1
    while d < seg:
        x = x + jnp.where(rowmod >= d, pltpu.roll(x, d, axis=0), 0.0)
        d *= 2
    return x


def _seg_rev_cumsum(x, seg, rowmod):
    n = x.shape[0]
    d = 1
    while d < seg:
        x = x + jnp.where(rowmod + d < seg, pltpu.roll(x, n - d, axis=0), 0.0)
        d *= 2
    return x


def _expand_heads(m, lane_lt_half):
    parts = []
    for j in range(SSD_HEADS // 2):
        parts.append(jnp.where(lane_lt_half, m[:, 2 * j:2 * j + 1], m[:, 2 * j + 1:2 * j + 2]))
    return jnp.concatenate(parts, axis=1)


def _merge_head_pairs(per_head, lane_lt_half):
    parts = [jnp.where(lane_lt_half, per_head[2 * j], per_head[2 * j + 1]) for j in range(len(per_head) // 2)]
    return jnp.concatenate(parts, axis=1)


def _mixer_kernel(*refs, sample, n_inner, n_prev=0):
    n_tok = 6 if sample else 5
    tok_refs, rest = refs[:n_tok], refs[n_tok:]
    if sample:
        if n_prev:
            prev_ref, rest = rest[0], rest[1:]
        (s0_ref, cw_ref, cb_ref, arow_ref, dsk_ref, sn_ref, wsp_ref, bsp_ref,
         y_ref, stack_ref, ext_ref, yoff_ref) = rest
        if n_prev:
            stack_ref[0:n_prev] = prev_ref[...]
        sout_ref = stack_ref.at[n_prev]
        first = pl.program_id(0) == 0
    else:
        (cw_ref, cb_ref, arow_ref, dsk_ref, sn_ref, wsp_ref, bsp_ref, y_ref, sout_ref, ext_ref) = rest
        s0_ref = yoff_ref = None
        first = pl.program_id(1) == 0

    @pl.when(first)
    def _():
        ext_ref[...] = jnp.zeros(ext_ref.shape, F32)
        if not sample:
            sout_ref[...] = jnp.zeros(sout_ref.shape, F32)

    def tile(c, carry):
        r0 = pl.multiple_of(c * CHUNK, CHUNK)
        views = [r.at[pl.ds(r0, CHUNK)] for r in tok_refs + (y_ref,)]
        _mixer_tile(*views[:n_tok], s0_ref, cw_ref, cb_ref, arow_ref, dsk_ref, sn_ref, wsp_ref, bsp_ref,
                    views[n_tok], sout_ref, ext_ref, yoff_ref, sample=sample)
        return carry

    if n_inner == 1:
        tile(0, 0)
    else:
        lax.fori_loop(0, n_inner, tile, 0)


def _mixer_tile(z_ref, xbc_ref, dt_ref, u_ref, v_ref, *rest, sample):
    if sample:
        cs_ref, s0_ref = rest[0], rest[1]
        rest = rest[2:]
    else:
        rest = rest[1:]
    cw_ref, cb_ref, arow_ref, dsk_ref, sn_ref, wsp_ref, bsp_ref, y_ref, sout_ref, ext_ref, yoff_ref = rest
    seg = GROUP if sample else CHUNK
    d_ssd = SSD_HEADS * SSD_HEAD_DIM
    gw = d_ssd // SSD_GROUPS
    hpg = SSD_HEADS // SSD_GROUPS
    cs_first = GROUP - DEC_SEQ_ROWS - (CONV_WIDTH - 1)

    rows = lax.broadcasted_iota(jnp.int32, (CHUNK, LANES), 0)
    cols = lax.broadcasted_iota(jnp.int32, (CHUNK, LANES), 1)
    rowmod = rows & (seg - 1)
    lane_lt_half = cols < SSD_HEAD_DIM

    xbc = xbc_ref[...]
    if sample:
        rm = lax.broadcasted_iota(jnp.int32, xbc.shape, 0) & (GROUP - 1)
        xbc = jnp.where((rm >= cs_first) & (rm < cs_first + CONV_WIDTH - 1), cs_ref[...], xbc)

    tail = ext_ref[...]
    row8 = lax.broadcasted_iota(jnp.int32, tail.shape, 0)
    acc = cb_ref[...] + cw_ref[CONV_WIDTH - 1:CONV_WIDTH, :] * xbc
    for j in range(1, CONV_WIDTH):
        rolled = pltpu.roll(xbc, j, axis=0)
        head = jnp.where(row8 < j, pltpu.roll(tail, j, axis=0), rolled[0:8, :])
        shifted = jnp.concatenate([head, rolled[8:, :]], axis=0)
        acc = acc + cw_ref[CONV_WIDTH - 1 - j:CONV_WIDTH - j, :] * shifted
    if not sample:
        ext_ref[...] = xbc[CHUNK - 8:, :]
    xc = acc * _sigmoid(acc)
    x = xc[:, :d_ssd]
    bb = xc[:, d_ssd:d_ssd + SSD_GROUPS * D_STATE].astype(BF16)
    cm = xc[:, d_ssd + SSD_GROUPS * D_STATE:]
    cbf = cm.astype(BF16)

    dtc = dt_ref[...]
    if sample:
        dtc = jnp.where(rowmod >= GROUP - DEC_SEQ_ROWS, dtc, 0.0)
    da = dtc * arow_ref[...]
    cum = _seg_cumsum(da, seg, rowmod)
    rev = _seg_rev_cumsum(da, seg, rowmod) - da
    cum_t = cum.T
    ecum = jnp.exp(cum)
    dt_e = _expand_heads(dtc, lane_lt_half)
    ecum_e = _expand_heads(ecum, lane_lt_half)
    erev_e = _expand_heads(jnp.exp(rev), lane_lt_half)

    xdt = x * dt_e
    xdt_bf = xdt.astype(BF16)
    causal = rows >= cols
    if sample:
        causal = causal & ((rows >> 3) == (cols >> 3))

    yd = []
    for g in range(SSD_GROUPS):
        cb_g = _dot_nt(cbf[:, g * D_STATE:(g + 1) * D_STATE], bb[:, g * D_STATE:(g + 1) * D_STATE])
        for hh in range(hpg):
            h = g * hpg + hh
            expo = cum[:, h:h + 1] - cum_t[h:h + 1, :]
            w = (cb_g * jnp.exp(jnp.where(causal, expo, NEG_BIG))).astype(BF16)
            j = h // 2
            yd.append(_dot(w, xdt_bf[:, j * LANES:(j + 1) * LANES]))
    y_diag = _merge_head_pairs(yd, lane_lt_half)

    if sample:
        for i in range(CHUNK // GROUP):
            for g in range(SSD_GROUPS):
                s_g = s0_ref[i, g * gw:(g + 1) * gw, :].astype(BF16)
                yoff_ref[i * GROUP:(i + 1) * GROUP, g * gw:(g + 1) * gw] = _dot_nt(
                    cm[i * GROUP:(i + 1) * GROUP, g * D_STATE:(g + 1) * D_STATE].astype(BF16), s_g)
        y_off = yoff_ref[...]
    else:
        y_off = jnp.concatenate(
            [_dot_nt(cbf[:, g * D_STATE:(g + 1) * D_STATE], sout_ref[0, g * gw:(g + 1) * gw, :].astype(BF16))
             for g in range(SSD_GROUPS)], axis=1)
    y = y_diag + y_off * ecum_e + dsk_ref[...] * x

    xd = xdt * erev_e
    for g in range(SSD_GROUPS):
        xd_t = xd[:, g * gw:(g + 1) * gw].T
        b_g = bb[:, g * D_STATE:(g + 1) * D_STATE]
        if sample:
            tcols = lax.broadcasted_iota(jnp.int32, xd_t.shape, 1) >> 3
            for i in range(CHUNK // GROUP):
                upd = _dot(jnp.where(tcols == i, xd_t, 0.0).astype(BF16), b_g)
                last = i * GROUP + GROUP - 1
                for hh in range(hpg):
                    h = g * hpg + hh
                    r0 = h * SSD_HEAD_DIM
                    sout_ref[i, r0:r0 + SSD_HEAD_DIM, :] = (
                        s0_ref[i, r0:r0 + SSD_HEAD_DIM, :] * ecum[last:last + 1, h:h + 1]
                        + upd[hh * SSD_HEAD_DIM:(hh + 1) * SSD_HEAD_DIM, :])
        else:
            upd = _dot(xd_t.astype(BF16), b_g)
            for hh in range(hpg):
                h = g * hpg + hh
                r0 = h * SSD_HEAD_DIM
                sout_ref[0, r0:r0 + SSD_HEAD_DIM, :] = (
                    sout_ref[0, r0:r0 + SSD_HEAD_DIM, :] * ecum[CHUNK - 1:CHUNK, h:h + 1]
                    + upd[hh * SSD_HEAD_DIM:(hh + 1) * SSD_HEAD_DIM, :])

    zf = z_ref[...]
    yf = y * (zf * _sigmoid(zf))
    parts = []
    for g in range(SSD_GROUPS):
        part = yf[:, g * gw:(g + 1) * gw]
        ms = jnp.mean(part * part, axis=-1, keepdims=True)
        parts.append(part * lax.rsqrt(ms + EPS))
    y_ssd = jnp.concatenate(parts, axis=1) * sn_ref[...]

    vb = v_ref[...].astype(BF16)
    sg = [_dot(wsp_ref[h], vb[:, (h // 2) * LANES:(h // 2 + 1) * LANES]) for h in range(GMLP_HEADS)]
    s = _merge_head_pairs(sg, lane_lt_half) + bsp_ref[...]
    y_gm = u_ref[...] * s

    y_ref[:, :d_ssd] = y_ssd.astype(BF16)
    y_ref[:, d_ssd:] = y_gm.astype(BF16)


def _mixer_prompt(z, xbc, dt, u, v, cw, cb, arow, dsk, sn, wsp, bsp, n_seq, n_chunk):
    n_inner = math.gcd(n_chunk, PROMPT_TILES_PER_STEP)
    n_outer = n_chunk // n_inner
    tok = lambda w: pl.BlockSpec((n_inner * CHUNK, w), lambda b, c: (b * n_outer + c, 0))
    full = lambda a: pl.BlockSpec(a.shape, lambda b, c: (0,) * a.ndim)
    d_ssd = z.shape[1]
    d_mix = d_ssd + u.shape[1]
    return pl.pallas_call(
        functools.partial(_mixer_kernel, sample=False, n_inner=n_inner),
        grid=(n_seq, n_outer),
        in_specs=[tok(z.shape[1]), tok(xbc.shape[1]), tok(dt.shape[1]), tok(u.shape[1]), tok(v.shape[1]),
                  full(cw), full(cb), full(arow), full(dsk), full(sn), full(wsp), full(bsp)],
        out_specs=[tok(d_mix),
                   pl.BlockSpec((1, d_ssd, D_STATE), lambda b, c: (b, 0, 0))],
        out_shape=[jax.ShapeDtypeStruct((n_seq * n_chunk * CHUNK, d_mix), BF16),
                   jax.ShapeDtypeStruct((n_seq, d_ssd, D_STATE), F32)],
        scratch_shapes=[pltpu.VMEM((8, xbc.shape[1]), F32)],
        compiler_params=_params(2),
        name="mixer_prompt",
    )(z, xbc, dt, u, v, cw, cb, arow, dsk, sn, wsp, bsp)


def _mixer_sample(z, xbc, dt, u, v, cs, s0_all, prev, cw, cb, arow, dsk, sn, wsp, bsp, row0, n_rows, layer):
    blk0 = row0 // CHUNK
    n_seq_blk = CHUNK // GROUP
    n_blk = n_rows // CHUNK
    tok = lambda w: pl.BlockSpec((CHUNK, w), lambda i: (blk0 + i, 0))
    full = lambda a: pl.BlockSpec(a.shape, lambda i: (0,) * a.ndim)
    d_ssd = z.shape[1]
    d_mix = d_ssd + u.shape[1]
    st = (n_seq_blk, d_ssd, D_STATE)
    stack = lambda n: pl.BlockSpec((n,) + st, lambda i: (0, i, 0, 0))
    prev_args = [] if layer == 0 else [prev]
    return pl.pallas_call(
        functools.partial(_mixer_kernel, sample=True, n_inner=1, n_prev=layer),
        grid=(n_blk,),
        in_specs=[tok(z.shape[1]), tok(xbc.shape[1]), tok(dt.shape[1]), tok(u.shape[1]), tok(v.shape[1]),
                  pl.BlockSpec((CHUNK, cs.shape[1]), lambda i: (i, 0))]
        + [stack(layer) for _ in prev_args]
        + [pl.BlockSpec(st, lambda i: (layer * n_blk + i, 0, 0)),
           full(cw), full(cb), full(arow), full(dsk), full(sn), full(wsp), full(bsp)],
        out_specs=[pl.BlockSpec((CHUNK, d_mix), lambda i: (i, 0)), stack(layer + 1)],
        out_shape=[jax.ShapeDtypeStruct((n_rows, d_mix), BF16),
                   jax.ShapeDtypeStruct((layer + 1, n_blk * n_seq_blk, d_ssd, D_STATE), F32)],
        scratch_shapes=[pltpu.VMEM((8, xbc.shape[1]), F32), pltpu.VMEM((CHUNK, d_ssd), F32)],
        compiler_params=_params(1),
        name="mixer_sample",
    )(z, xbc, dt, u, v, cs, *prev_args, s0_all, cw, cb, arow, dsk, sn, wsp, bsp)


def _first_argmax(vals):
    m = vals[0]
    for v in vals[1:]:
        m = jnp.maximum(m, v)
    idx = jnp.full(m.shape, len(vals) - 1, jnp.int32)
    for k in range(len(vals) - 2, -1, -1):
        idx = jnp.where(vals[k] >= m, k, idx)
    return m, idx


def _slab_store(slab_ref, x, pitch, tok0=0):
    for k in range(x.shape[1] // LANES):
        slab_ref[pl.ds(tok0 * pitch + k, x.shape[0], stride=pitch), :] = x[:, k * LANES:(k + 1) * LANES]


def _slab_load(slab_ref, rows, n_pieces, pitch, tok0=0):
    return jnp.concatenate([slab_ref[pl.ds(tok0 * pitch + k, rows, stride=pitch), :] for k in range(n_pieces)],
                           axis=1)


def _post_kernel(yp_ref, ys_ref, hp_ref, hs_ref, wo_ref, nf_ref, wr_ref, br_ref,
                 slab_ref, route_ref, cnt_ref, carry_ref, earlier_ref, *, n_prompt_tiles):
    i = pl.program_id(0)
    tm, d = hp_ref.shape

    @pl.when(i == 0)
    def _():
        carry_ref[...] = jnp.zeros(carry_ref.shape, F32)
        tr = lax.broadcasted_iota(jnp.int32, (tm, tm), 0)
        tc = lax.broadcasted_iota(jnp.int32, (tm, tm), 1)
        earlier_ref[...] = jnp.where(tr < tc, 1.0, 0.0).astype(BF16)

    ym = _pair_load(yp_ref, ys_ref, n_prompt_tiles)
    h1 = _pair_load(hp_ref, hs_ref, n_prompt_tiles) + _dot(ym, wo_ref[...])
    _slab_store(slab_ref, h1, X_PITCH)

    t = _rms(h1, nf_ref[...])
    wr = wr_ref[...]
    wr_hi = wr.astype(BF16)
    wr_lo = (wr - wr_hi.astype(F32)).astype(BF16)
    t_hi = t.astype(BF16)
    t_lo = (t - t_hi.astype(F32)).astype(BF16)
    by_hi = _dot_nt(jnp.concatenate([wr_hi, wr_lo], axis=0), t_hi)
    logits = (by_hi[:ROUTE_ROWS] + (by_hi[ROUTE_ROWS:] + _dot_nt(wr_hi, t_lo))) + br_ref[...]
    lg = [logits[k:k + 1, :] for k in range(N_EXPERT_GROUPS)]
    m, g = _first_argmax(lg)
    ssum = jnp.exp(lg[0] - m)
    for k in range(1, N_EXPERT_GROUPS):
        ssum = ssum + jnp.exp(lg[k] - m)
    p_sel = 1.0 / ssum
    le = [logits[N_EXPERT_GROUPS + e:N_EXPERT_GROUPS + e + 1, :] for e in range(N_EXPERT_GROUPS * EXPERTS_PER_GROUP)]
    a = []
    for k in range(EXPERTS_PER_GROUP):
        sel = le[(N_EXPERT_GROUPS - 1) * EXPERTS_PER_GROUP + k]
        for gi in range(N_EXPERT_GROUPS - 2, -1, -1):
            sel = jnp.where(g == gi, le[gi * EXPERTS_PER_GROUP + k], sel)
        a.append(sel)
    v1, i1 = _first_argmax(a)
    a2 = [jnp.where(i1 == k, -jnp.inf, a[k]) for k in range(EXPERTS_PER_GROUP)]
    v2, i2 = _first_argmax(a2)
    e2 = jnp.exp(v2 - v1)
    den = 1.0 + e2
    g1 = (1.0 / den) * p_sel
    g2 = (e2 / den) * p_sel
    lo = jnp.minimum(i1, i2)
    hi = jnp.maximum(i1, i2)
    c_lo = jnp.where(i1 < i2, g1, g2)
    c_hi = jnp.where(i1 < i2, g2, g1)
    pair = jnp.where(lo == 0, hi - 1, jnp.where(lo == 1, hi + 1, N_PAIRS - 1))
    bucket = g * N_PAIRS + pair

    brow = lax.broadcasted_iota(jnp.int32, (ROUTE_ROWS, tm), 0)
    onehot = jnp.where(brow == bucket, 1.0, 0.0)
    prefix = _dot(onehot.astype(BF16), earlier_ref[...])
    carry = carry_ref[:, 0:1]
    rank = jnp.sum(onehot * (prefix + carry), axis=0, keepdims=True)
    carry = carry + jnp.sum(onehot, axis=1, keepdims=True)
    carry_b = jnp.broadcast_to(carry, carry_ref.shape)
    carry_ref[...] = carry_b
    cnt_ref[...] = carry_b

    rr = lax.broadcasted_iota(jnp.int32, (8, tm), 0)
    route_ref[...] = jnp.where(rr == 0, bucket.astype(F32), jnp.where(rr == 1, rank, 0.0))
    ar = lax.broadcasted_iota(jnp.int32, (LANES, tm), 0)
    aux = jnp.where(ar == 0, c_lo, jnp.where(ar == 1, c_hi, 0.0))
    slab_ref[pl.ds(d // LANES, tm, stride=X_PITCH), :] = aux.T


def _post(yp, ys, hp, hs, wo, nf, wr, br):
    d = hp.shape[1]
    t = hp.shape[0] + hs.shape[0]
    n_prompt_tiles = hp.shape[0] // TOK_TILE
    full = lambda a: pl.BlockSpec(a.shape, lambda i: (0,) * a.ndim)
    return pl.pallas_call(
        functools.partial(_post_kernel, n_prompt_tiles=n_prompt_tiles),
        grid=(t // TOK_TILE,),
        in_specs=_pair_specs(d, n_prompt_tiles) + _pair_specs(d, n_prompt_tiles)
        + [full(wo), full(nf), full(wr), full(br)],
        out_specs=[pl.BlockSpec((TOK_TILE * X_PITCH, LANES), lambda i: (i, 0)),
                   pl.BlockSpec((8, TOK_TILE), lambda i: (0, i)),
                   pl.BlockSpec((ROUTE_ROWS, LANES), lambda i: (0, 0))],
        out_shape=[jax.ShapeDtypeStruct((t * X_PITCH, LANES), F32),
                   jax.ShapeDtypeStruct((8, t), F32),
                   jax.ShapeDtypeStruct((ROUTE_ROWS, LANES), F32)],
        scratch_shapes=[pltpu.VMEM((ROUTE_ROWS, LANES), F32), pltpu.VMEM((TOK_TILE, TOK_TILE), BF16)],
        compiler_params=_params(1),
        name="post",
    )(yp, ys, hp, hs, wo, nf, wr, br)


def _scatter_kernel(pos_ref, src_ref, _, dst_ref, sem, *, pitch):
    n_tok = src_ref.shape[0] // pitch
    for r in range(n_tok):
        pltpu.make_async_copy(src_ref.at[pl.ds(r * pitch, pitch)],
                              dst_ref.at[pl.ds(pos_ref[0, 0, r] * pitch, pitch)], sem).start(priority=r % 2)
    pltpu.make_async_copy(src_ref, dst_ref.at[pl.ds(0, n_tok * pitch)], sem).wait()


def _scatter_tokens(pos, src, n_out, pitch, init=None):
    n = pos.shape[0]
    pos3 = pos.reshape(n // TOK_TILE, 1, TOK_TILE)
    hbm = pl.BlockSpec(memory_space=pl.ANY)
    if init is None:
        init = jnp.zeros((n_out * pitch, LANES), src.dtype)
    return pl.pallas_call(
        functools.partial(_scatter_kernel, pitch=pitch),
        grid=(n // TOK_TILE,),
        in_specs=[pl.BlockSpec((1, 1, TOK_TILE), lambda i: (i, 0, 0), memory_space=pltpu.SMEM),
                  pl.BlockSpec((TOK_TILE * pitch, LANES), lambda i: (i, 0)), hbm],
        out_specs=hbm,
        out_shape=jax.ShapeDtypeStruct((n_out * pitch, LANES), src.dtype),
        scratch_shapes=[pltpu.SemaphoreType.DMA(())],
        input_output_aliases={2: 0},
        compiler_params=_params(1),
        name="scatter_tokens",
    )(pos3, src, init)


def _moe_kernel(elo_ref, ehi_ref, nused_ref, x_ref, nf_ref, wg_lo, wu_lo, wd_lo, wg_hi, wu_hi, wd_hi, o_ref):
    i = pl.program_id(0)
    n_pieces = nf_ref.shape[1] // LANES

    @pl.when(i < nused_ref[0])
    def _():
        x = _slab_load(x_ref, MOE_TILE, n_pieces, X_PITCH)
        gates = x_ref[pl.ds(n_pieces, MOE_TILE, stride=X_PITCH), :]
        c_lo = gates[:, 0:1]
        c_hi = gates[:, 1:2]
        t = _rms(x, nf_ref[...]).astype(BF16)

        def expert(wg, wu, wd):
            gate = _dot(t, wg[...])
            hid = (gate * _sigmoid(gate)) * _dot(t, wu[...])
            return _dot(hid.astype(BF16), wd[...])

        y = c_lo * expert(wg_lo, wu_lo, wd_lo)
        y = y + c_hi * expert(wg_hi, wu_hi, wd_hi)
        _slab_store(o_ref, x + y, Y_PITCH)
        o_ref[pl.ds(TOKEN_ROWS, MOE_TILE, stride=Y_PITCH), :] = jnp.zeros((MOE_TILE, LANES), F32)

    @pl.when(i >= nused_ref[0])
    def _():
        o_ref[...] = jnp.zeros(o_ref.shape, F32)


def _moe(e_lo, e_hi, n_used, xs, nf, wg, wu, wd):
    n_tiles = xs.shape[0] // (MOE_TILE * X_PITCH)
    d, de = wg.shape[1:]
    lo = lambda i, elo, ehi, nu: (elo[i], 0, 0)
    hi = lambda i, elo, ehi, nu: (ehi[i], 0, 0)
    grid_spec = pltpu.PrefetchScalarGridSpec(
        num_scalar_prefetch=3,
        grid=(n_tiles,),
        in_specs=[pl.BlockSpec((MOE_TILE * X_PITCH, LANES),
                               lambda i, elo, ehi, nu: (jnp.maximum(jnp.minimum(i, nu[0] - 1), 0), 0)),
                  pl.BlockSpec(nf.shape, lambda i, elo, ehi, nu: (0, 0)),
                  pl.BlockSpec((None, d, de), lo), pl.BlockSpec((None, d, de), lo), pl.BlockSpec((None, de, d), lo),
                  pl.BlockSpec((None, d, de), hi), pl.BlockSpec((None, d, de), hi), pl.BlockSpec((None, de, d), hi)],
        out_specs=pl.BlockSpec((MOE_TILE * Y_PITCH, LANES), lambda i, elo, ehi, nu: (i, 0)),
    )
    return pl.pallas_call(
        _moe_kernel,
        grid_spec=grid_spec,
        out_shape=jax.ShapeDtypeStruct((n_tiles * MOE_TILE * Y_PITCH, LANES), F32),
        compiler_params=_params(1),
        name="expert_pairs",
    )(e_lo, e_hi, n_used, xs, nf, wg, wu, wd, wg, wu, wd)


def _ple_kernel(pos_ref, pos_next_ref, h_hbm, pp_ref, ps_ref, wple_ref, nple_ref, npg_ref, wpg_ref, *refs,
                final, n_prompt_tiles):
    refs, (xbuf, sem) = refs[:-2], refs[-2:]
    i = pl.program_id(0)
    last = pl.num_programs(0) - 1
    slot = i % 2

    def start_tile(p_ref, s):
        for r in range(TOK_TILE):
            pltpu.make_async_copy(h_hbm.at[pl.ds(p_ref[0, 0, r] * Y_PITCH, Y_PITCH)],
                                  xbuf.at[s, pl.ds(r * Y_PITCH, Y_PITCH)], sem.at[s]).start()

    def wait_tile(s):
        pltpu.make_async_copy(h_hbm.at[pl.ds(0, TOK_TILE * Y_PITCH)], xbuf.at[s], sem.at[s]).wait()

    @pl.when(i == 0)
    def _():
        start_tile(pos_ref, 0)

    wait_tile(slot)
    start_tile(pos_next_ref, 1 - slot)
    h2 = _slab_load(xbuf.at[slot], TOK_TILE, TOKEN_ROWS, Y_PITCH)
    p = _pair_load(pp_ref, ps_ref, n_prompt_tiles)
    e = _rms(_dot(p.astype(BF16), wple_ref[...]), nple_ref[...])
    gate = _sigmoid(_dot(_rms(h2, npg_ref[...]).astype(BF16), wpg_ref[...]))
    h3 = h2 + gate * e
    if final:
        nfin_ref, op_ref, os_ref = refs
        out = _rms(h3, nfin_ref[...])
    else:
        op_ref, os_ref = refs[N_INPROJ_PARAMS:N_INPROJ_PARAMS + 2]
        out = h3
        _inproj_tile(h3, *refs[:N_INPROJ_PARAMS], *refs[N_INPROJ_PARAMS + 2:])

    @pl.when(i < n_prompt_tiles)
    def _():
        op_ref[...] = out

    @pl.when(i >= n_prompt_tiles)
    def _():
        os_ref[...] = out

    @pl.when(i == last)
    def _():
        wait_tile(1 - slot)


N_INPROJ_PARAMS = 8


def _ple(pos, h2s, pp_all, ps, wple, nple, npg, wpg, tail, final, layer, n_prompt):
    t = pos.shape[0]
    n_steps = t // TOK_TILE
    pos3 = pos.reshape(n_steps, 1, TOK_TILE)
    d = TOKEN_ROWS * LANES
    n_prompt_tiles = n_prompt // TOK_TILE
    full = lambda a: pl.BlockSpec(a.shape, lambda i: (0,) * a.ndim)
    out_specs = _pair_specs(d, n_prompt_tiles)
    out_shape = [jax.ShapeDtypeStruct((n_prompt, d), F32), jax.ShapeDtypeStruct((ps.shape[0], d), F32)]
    if final:
        tail = (tail,)
        tail_specs = [full(tail[0])]
    else:
        assert len(tail) == N_INPROJ_PARAMS
        tail_specs, io, ish = _inproj_specs(tail, t)
        out_specs, out_shape = out_specs + io, out_shape + ish
    return pl.pallas_call(
        functools.partial(_ple_kernel, final=final, n_prompt_tiles=n_prompt_tiles),
        grid=(n_steps,),
        in_specs=[pl.BlockSpec((1, 1, TOK_TILE), lambda i: (i, 0, 0), memory_space=pltpu.SMEM),
                  pl.BlockSpec((1, 1, TOK_TILE), lambda i: (jnp.minimum(i + 1, n_steps - 1), 0, 0),
                               memory_space=pltpu.SMEM),
                  pl.BlockSpec(memory_space=pl.ANY)]
        + _pair_specs(pp_all.shape[1], n_prompt_tiles, layer * n_prompt_tiles)
        + [full(wple), full(nple), full(npg), full(wpg)] + tail_specs,
        out_specs=out_specs,
        out_shape=out_shape,
        scratch_shapes=[pltpu.VMEM((2, TOK_TILE * Y_PITCH, LANES), F32), pltpu.SemaphoreType.DMA((2,))],
        compiler_params=_params(1),
        name="ple_final" if final else "ple_inproj",
    )(pos3, pos3, h2s, pp_all, ps, wple, nple, npg, wpg, *tail)


_PAIR_LO = (0, 0, 0, 1, 1, 2)
_PAIR_HI = (1, 2, 3, 2, 3, 3)


def _routing_tables(route, cnt, n_tiles):
    counts = cnt[:N_BUCKETS, 0].astype(jnp.int32)
    padded = ((counts + MOE_TILE - 1) // MOE_TILE) * MOE_TILE
    ends = jnp.cumsum(padded)
    starts = ends - padded
    bucket = route[0].astype(jnp.int32)
    rank = route[1].astype(jnp.int32)
    pos = starts[bucket] + rank
    n_used = ends[-1] // MOE_TILE
    tile = jnp.minimum(jnp.arange(n_tiles, dtype=jnp.int32), n_used - 1)
    tb = jnp.sum((ends[None, :] <= (tile * MOE_TILE)[:, None]).astype(jnp.int32), axis=1)
    tb = jnp.minimum(tb, N_BUCKETS - 1)
    grp = tb // N_PAIRS
    pr = tb % N_PAIRS
    e_lo = grp * EXPERTS_PER_GROUP + jnp.asarray(_PAIR_LO, jnp.int32)[pr]
    e_hi = grp * EXPERTS_PER_GROUP + jnp.asarray(_PAIR_HI, jnp.int32)[pr]
    return pos, e_lo, e_hi, n_used.reshape(1)


def kernel(x_prompt, x_sample, state_conv, state_ssm, p_prompt, p_sample, norm_mix, w_in, conv_w, conv_b,
           dt_bias, a_log, d_skip, ssd_norm, gmlp_norm, w_spatial, b_spatial, w_out, norm_ffn,
           w_router_group, b_router_group, w_router_expert, b_router_expert, w_gate, w_up, w_down,
           w_ple, norm_ple, norm_pg, w_pg, norm_final):
    n_seq, seq_len, d_model = x_prompt.shape
    n_dec, dec_seq, _ = x_sample.shape
    depth = w_in.shape[0]
    conv_dim = conv_w.shape[2]
    d_ssd = SSD_HEADS * SSD_HEAD_DIM
    d_gmlp = gmlp_norm.shape[1]
    assert dec_seq == DEC_SEQ_ROWS and conv_w.shape[1] == CONV_WIDTH and seq_len % CHUNK == 0
    assert conv_dim == d_ssd + 2 * SSD_GROUPS * D_STATE and w_spatial.shape[1:] == (GMLP_HEADS, CHUNK, CHUNK)
    n_chunk = seq_len // CHUNK
    n_prompt = n_seq * seq_len
    n_srows = n_dec * GROUP
    t_all = n_prompt + n_srows
    assert n_prompt % TOK_TILE == 0 and n_srows % TOK_TILE == 0
    assert d_model == TOKEN_ROWS * LANES
    lead = GROUP - dec_seq
    n_moe_tiles = t_all // MOE_TILE + N_BUCKETS

    def sample_rows(a):
        return jnp.pad(a, ((0, 0), (lead, 0), (0, 0))).reshape(n_srows, a.shape[-1])

    hp, hs = x_prompt.reshape(n_prompt, d_model), sample_rows(x_sample)

    o_xbc = d_ssd
    o_dt = o_xbc + conv_dim
    o_uv = o_dt + SSD_HEADS
    head_cols = jnp.arange(d_ssd) // SSD_HEAD_DIM
    seq_eye = jnp.eye(CHUNK // GROUP, dtype=F32)
    tril =jnp.tril(jnp.ones((CHUNK, CHUNK), bool))

    n_exp = w_gate.shape[1]
    wg_all = w_gate.astype(BF16).reshape((depth * n_exp,) + w_gate.shape[2:])
    wu_all = w_up.astype(BF16).reshape((depth * n_exp,) + w_up.shape[2:])
    wd_all = w_down.astype(BF16).reshape((depth * n_exp,) + w_down.shape[2:])
    s0_all = state_ssm.reshape(depth * n_dec, d_ssd, D_STATE)
    pp_all = p_prompt.reshape(depth * n_prompt, p_prompt.shape[-1])
    xs = None

    ssm_s = None
    convs_p, ssms_p, convs_s, vs_s = [], [], [], []
    row = lambda a: a.reshape(1, -1).astype(F32)
    tail = CONV_WIDTH - 1
    tail_rows = (((jnp.arange(n_seq) + 1) * seq_len)[:, None] - tail + jnp.arange(tail)[None, :]).reshape(-1)

    def inproj_params(i):
        wi = w_in[i]
        wdt = jnp.pad(wi[:, o_dt:o_uv], ((0, 0), (0, LANES - SSD_HEADS)))
        dtb = jnp.pad(dt_bias[i].astype(F32), (0, LANES - SSD_HEADS)).reshape(1, LANES)
        return (row(norm_mix[i]), wi[:, :o_xbc].astype(BF16), wi[:, o_xbc:o_dt].astype(BF16), wdt.astype(BF16),
                wi[:, o_uv:o_uv + d_gmlp].astype(BF16), wi[:, o_uv + d_gmlp:].astype(BF16), dtb, row(gmlp_norm[i]))

    z, xbc, dt, u, v = _inproj(hp, hs, inproj_params(0))

    for i in range(depth):
        arow =jnp.pad(-jnp.exp(a_log[i].astype(F32)), (0, LANES - SSD_HEADS)).reshape(1, LANES)
        dsk = d_skip[i].astype(F32)[head_cols].reshape(1, d_ssd)
        ws_tril = jnp.where(tril, w_spatial[i], 0.0)
        wsp_p = ws_tril.astype(BF16)
        bsp_p = jnp.repeat(b_spatial[i].T, d_gmlp // GMLP_HEADS, axis=1)
        w8 = jnp.pad(ws_tril[:, :dec_seq, :dec_seq], ((0, 0), (lead, 0), (lead, 0)))
        wsp_s = (seq_eye[None, :, None, :, None] * w8[:, None, :, None, :]).reshape(GMLP_HEADS, CHUNK, CHUNK).astype(BF16)
        b8 = jnp.pad(b_spatial[i][:, :dec_seq], ((0, 0), (lead, 0)))
        bsp_s = jnp.repeat(jnp.tile(b8, (1, CHUNK // GROUP)).T, d_gmlp // GMLP_HEADS, axis=1)
        wr = jnp.concatenate([w_router_group[i].T, w_router_expert[i].T,
                              jnp.zeros((ROUTE_ROWS - N_EXPERT_GROUPS * (1 + EXPERTS_PER_GROUP), d_model), F32)], axis=0)
        br = jnp.concatenate([b_router_group[i], b_router_expert[i],
                              jnp.zeros((ROUTE_ROWS - N_EXPERT_GROUPS * (1 + EXPERTS_PER_GROUP),), F32)]).reshape(ROUTE_ROWS, 1)

        yp, ssm_p = _mixer_prompt(z, xbc, dt, u, v, conv_w[i], row(conv_b[i]), arow, dsk, row(ssd_norm[i]),
                                  wsp_p, bsp_p, n_seq, n_chunk)
        cs = jnp.pad(state_conv[i], ((0, 0), (lead - (CONV_WIDTH - 1), dec_seq), (0, 0))).reshape(n_srows, conv_dim)
        ys, ssm_s = _mixer_sample(z, xbc, dt, u, v, cs, s0_all, ssm_s, conv_w[i], row(conv_b[i]), arow, dsk,
                                  row(ssd_norm[i]), wsp_s, bsp_s, n_prompt, n_srows, i)

        slab, route, cnt = _post(yp, ys, hp, hs, w_out[i].astype(BF16), row(norm_ffn[i]), wr, br)
        pos, e_lo, e_hi, n_used = _routing_tables(route, cnt, n_moe_tiles)

        xs = _scatter_tokens(pos, slab, n_moe_tiles * MOE_TILE, X_PITCH, init=xs)
        h2s = _moe(e_lo + i * n_exp, e_hi + i * n_exp, n_used, xs, row(norm_ffn[i]), wg_all, wu_all, wd_all)

        convs_p.append(jnp.take(xbc, tail_rows, axis=0).reshape(n_seq, tail, conv_dim))
        xbc_s = xbc[n_prompt:].reshape(n_dec, GROUP, conv_dim)
        convs_s.append(xbc_s[:, GROUP - tail:])
        ssms_p.append(ssm_p.reshape(n_seq, SSD_HEADS, SSD_HEAD_DIM, D_STATE))
        vs_s.append(v[n_prompt:].reshape(n_dec, GROUP, d_gmlp)[:, lead:])

        final = i == depth - 1
        outs = _ple(pos, h2s, pp_all, sample_rows(p_sample[i]), w_ple[i].astype(BF16), row(norm_ple[i]), row(norm_pg[i]),
                    w_pg[i].astype(BF16), row(norm_final) if final else inproj_params(i + 1), final, i, n_prompt)
        hp, hs = outs[:2]
        if not final:
            z, xbc, dt, u, v = outs[2:]

    y_prompt = hp.reshape(n_seq, seq_len, d_model)
    y_sample = hs.reshape(n_dec, GROUP, d_model)[:, lead:]
    new_ssm_sample = ssm_s.reshape(depth, n_dec, SSD_HEADS, SSD_HEAD_DIM, D_STATE)
    return (y_prompt, y_sample, jnp.stack(convs_p), jnp.stack(ssms_p), jnp.stack(convs_s), new_ssm_sample,
            jnp.stack(vs_s))
```

```python
import functools
import math

import jax
import jax.numpy as jnp
from jax import lax
from jax.experimental import pallas as pl
from jax.experimental.pallas import tpu as pltpu

F32 = jnp.float32
BF16 = jnp.bfloat16

LANES = 128
VMEM_LIMIT_BYTES = 56 * 1024 * 1024

CONV_WIDTH = 4
SSD_HEADS = 8
SSD_HEAD_DIM = 64
SSD_GROUPS = 2
D_STATE = 128
CHUNK = 128
GMLP_HEADS = 8
N_EXPERT_GROUPS = 4
EXPERTS_PER_GROUP = 4
N_PAIRS = 6
N_BUCKETS = N_EXPERT_GROUPS * N_PAIRS
EPS = 1e-6

GROUP = 8
DEC_SEQ_ROWS = 4
TOK_TILE = 512
MOE_TILE = 256
PROMPT_TILES_PER_STEP = 4
ROUTE_ROWS = 32
TOKEN_ROWS = 8
X_PITCH = TOKEN_ROWS + 1
Y_PITCH = TOKEN_ROWS + 1
NEG_BIG = -1e30


def _dot(a, b):
    return jnp.dot(a, b, preferred_element_type=F32)


def _dot_nt(a, b):
    return lax.dot_general(a, b, (((1,), (1,)), ((), ())), preferred_element_type=F32)


def _rms(x, g):
    ms = jnp.mean(x * x, axis=-1, keepdims=True)
    return (x * lax.rsqrt(ms + EPS)) * g


def _gelu(x):
    return 0.5 * x * (1.0 + lax.erf(x * (1.0 / math.sqrt(2.0))))


def _sigmoid(x):
    return 0.5 * jnp.tanh(0.5 * x) + 0.5


def _softplus(x):
    return jnp.maximum(x, 0.0) + jnp.log1p(jnp.exp(-jnp.abs(x)))


def _params(n_grid):
    return pltpu.CompilerParams(dimension_semantics=("arbitrary",) * n_grid,
                                vmem_limit_bytes=VMEM_LIMIT_BYTES)


def _pair_specs(width, n_prompt_tiles, prompt_tile0=0, sample_tile0=0):
    return [pl.BlockSpec((TOK_TILE, width), lambda i: (prompt_tile0 + jnp.minimum(i, n_prompt_tiles - 1), 0)),
            pl.BlockSpec((TOK_TILE, width), lambda i: (sample_tile0 + jnp.maximum(i - n_prompt_tiles, 0), 0))]


def _pair_load(p_ref, s_ref, n_prompt_tiles):
    return jnp.where(pl.program_id(0) < n_prompt_tiles, p_ref[...], s_ref[...])


def _layer_spec(a, layer):
    zeros = (0,) * (a.ndim - 1)
    return pl.BlockSpec((None,) + a.shape[1:], lambda *_: (layer,) + zeros)


def _inproj_tile(h, nm_ref, w_ref, dtb_ref, gn_ref, z_ref, xbc_ref, dt_ref, u_ref, v_ref):
    a = _rms(h, nm_ref[...]).astype(BF16)
    c = [0]
    for r in (z_ref, xbc_ref, dt_ref, u_ref, v_ref):
        c.append(c[-1] + r.shape[1])
    z_ref[...] = _dot(a, w_ref[:, c[0]:c[1]])
    xbc_ref[...] = _dot(a, w_ref[:, c[1]:c[2]])
    dt_ref[...] = _softplus(_dot(a, w_ref[:, c[2]:c[3]]) + dtb_ref[...])
    u_ref[...] = _gelu(_dot(a, w_ref[:, c[3]:c[4]]))
    v_ref[...] = _rms(_gelu(_dot(a, w_ref[:, c[4]:c[5]])), gn_ref[...])


def _inproj_kernel(hp_ref, hs_ref, *refs, n_prompt_tiles):
    _inproj_tile(_pair_load(hp_ref, hs_ref, n_prompt_tiles), *refs)


N_INPROJ_PARAMS = 4


def _inproj_specs(weights, widths, t, layer):
    assert len(weights) == N_INPROJ_PARAMS
    row = lambda w: pl.BlockSpec((TOK_TILE, w), lambda i: (i, 0))
    return ([_layer_spec(a, layer) for a in weights], [row(w) for w in widths],
            [jax.ShapeDtypeStruct((t, w), F32) for w in widths])


def _inproj(hp, hs, weights, widths, layer):
    t = hp.shape[0] + hs.shape[0]
    n_prompt_tiles = hp.shape[0] // TOK_TILE
    w_specs, out_specs, out_shape = _inproj_specs(weights, widths, t, layer)
    return pl.pallas_call(
        functools.partial(_inproj_kernel, n_prompt_tiles=n_prompt_tiles),
        grid=(t // TOK_TILE,),
        in_specs=_pair_specs(hp.shape[1], n_prompt_tiles) + w_specs,
        out_specs=out_specs,
        out_shape=out_shape,
        compiler_params=_params(1),
        name="inproj",
    )(hp, hs, *weights)


def _seg_cumsum(x, seg, rowmod):
    d = 1
    while d < seg:
        x = x + jnp.where(rowmod >= d, pltpu.roll(x, d, axis=0), 0.0)
        d *= 2
    return x


def _seg_rev_cumsum(x, seg, rowmod):
    n = x.shape[0]
    d = 1
    while d < seg:
        x = x + jnp.where(rowmod + d < seg, pltpu.roll(x, n - d, axis=0), 0.0)
        d *= 2
    return x


def _expand_heads(m, lane_lt_half):
    parts = []
    for j in range(SSD_HEADS // 2):
        parts.append(jnp.where(lane_lt_half, m[:, 2 * j:2 * j + 1], m[:, 2 * j + 1:2 * j + 2]))
    return jnp.concatenate(parts, axis=1)


def _merge_head_pairs(per_head, lane_lt_half):
    parts = [jnp.where(lane_lt_half, per_head[2 * j], per_head[2 * j + 1]) for j in range(len(per_head) // 2)]
    return jnp.concatenate(parts, axis=1)


def _mixer_kernel(*refs, sample, n_inner, n_prev=0):
    n_tok = 6 if sample else 5
    tok_refs, rest = refs[:n_tok], refs[n_tok:]
    if sample:
        if n_prev:
            prev_ref, rest = rest[0], rest[1:]
        (s0_ref, cw_ref, cb_ref, arow_ref, dsk_ref, sn_ref, wsp_ref, bsp_ref,
         y_ref, stack_ref, ext_ref, yoff_ref) = rest
        if n_prev:
            stack_ref[0:n_prev] = prev_ref[...]
        sout_ref = stack_ref.at[n_prev]
        first = pl.program_id(0) == 0
    else:
        (cw_ref, cb_ref, arow_ref, dsk_ref, sn_ref, wsp_ref, bsp_ref, y_ref, sout_ref, ext_ref) = rest
        s0_ref = yoff_ref = None
        first = pl.program_id(1) == 0

    @pl.when(first)
    def _():
        ext_ref[...] = jnp.zeros(ext_ref.shape, F32)
        if not sample:
            sout_ref[...] = jnp.zeros(sout_ref.shape, F32)

    def tile(c, carry):
        r0 = pl.multiple_of(c * CHUNK, CHUNK)
        views = [r.at[pl.ds(r0, CHUNK)] for r in tok_refs + (y_ref,)]
        _mixer_tile(*views[:n_tok], s0_ref, cw_ref, cb_ref, arow_ref, dsk_ref, sn_ref, wsp_ref, bsp_ref,
                    views[n_tok], sout_ref, ext_ref, yoff_ref, sample=sample)
        return carry

    if n_inner == 1:
        tile(0, 0)
    else:
        lax.fori_loop(0, n_inner, tile, 0)


def _mixer_tile(z_ref, xbc_ref, dt_ref, u_ref, v_ref, *rest, sample):
    if sample:
        cs_ref, s0_ref = rest[0], rest[1]
        rest = rest[2:]
    else:
        rest = rest[1:]
    cw_ref, cb_ref, arow_ref, dsk_ref, sn_ref, wsp_ref, bsp_ref, y_ref, sout_ref, ext_ref, yoff_ref = rest
    seg = GROUP if sample else CHUNK
    d_ssd = SSD_HEADS * SSD_HEAD_DIM
    gw = d_ssd // SSD_GROUPS
    hpg = SSD_HEADS // SSD_GROUPS
    cs_first = GROUP - DEC_SEQ_ROWS - (CONV_WIDTH - 1)

    rows = lax.broadcasted_iota(jnp.int32, (CHUNK, LANES), 0)
    cols = lax.broadcasted_iota(jnp.int32, (CHUNK, LANES), 1)
    rowmod = rows & (seg - 1)
    lane_lt_half = cols < SSD_HEAD_DIM

    xbc = xbc_ref[...]
    if sample:
        rm = lax.broadcasted_iota(jnp.int32, xbc.shape, 0) & (GROUP - 1)
        xbc = jnp.where((rm >= cs_first) & (rm < cs_first + CONV_WIDTH - 1), cs_ref[...], xbc)

    tail = ext_ref[...]
    row8 = lax.broadcasted_iota(jnp.int32, tail.shape, 0)
    acc = cb_ref[...] + cw_ref[CONV_WIDTH - 1:CONV_WIDTH, :] * xbc
    for j in range(1, CONV_WIDTH):
        rolled = pltpu.roll(xbc, j, axis=0)
        head = jnp.where(row8 < j, pltpu.roll(tail, j, axis=0), rolled[0:8, :])
        shifted = jnp.concatenate([head, rolled[8:, :]], axis=0)
        acc = acc + cw_ref[CONV_WIDTH - 1 - j:CONV_WIDTH - j, :] * shifted
    if not sample:
        ext_ref[...] = xbc[CHUNK - 8:, :]
    xc = acc * _sigmoid(acc)
    x = xc[:, :d_ssd]
    bb = xc[:, d_ssd:d_ssd + SSD_GROUPS * D_STATE].astype(BF16)
    cm = xc[:, d_ssd + SSD_GROUPS * D_STATE:]
    cbf = cm.astype(BF16)

    dtc = dt_ref[...]
    if sample:
        dtc = jnp.where(rowmod >= GROUP - DEC_SEQ_ROWS, dtc, 0.0)
    da = dtc * arow_ref[...]
    cum = _seg_cumsum(da, seg, rowmod)
    rev = _seg_rev_cumsum(da, seg, rowmod) - da
    cum_t = cum.T
    ecum = jnp.exp(cum)
    dt_e = _expand_heads(dtc, lane_lt_half)
    ecum_e = _expand_heads(ecum, lane_lt_half)
    erev_e = _expand_heads(jnp.exp(rev), lane_lt_half)

    xdt = x * dt_e
    xdt_bf = xdt.astype(BF16)
    causal = rows >= cols
    if sample:
        causal = causal & ((rows >> 3) == (cols >> 3))

    yd = []
    for g in range(SSD_GROUPS):
        cb_g = _dot_nt(cbf[:, g * D_STATE:(g + 1) * D_STATE], bb[:, g * D_STATE:(g + 1) * D_STATE])
        for hh in range(hpg):
            h = g * hpg + hh
            expo = cum[:, h:h + 1] - cum_t[h:h + 1, :]
            w = (cb_g * jnp.exp(jnp.where(causal, expo, NEG_BIG))).astype(BF16)
            j = h // 2
            yd.append(_dot(w, xdt_bf[:, j * LANES:(j + 1) * LANES]))
    y_diag = _merge_head_pairs(yd, lane_lt_half)

    if sample:
        for i in range(CHUNK // GROUP):
            for g in range(SSD_GROUPS):
                s_g = s0_ref[i, g * gw:(g + 1) * gw, :].astype(BF16)
                yoff_ref[i * GROUP:(i + 1) * GROUP, g * gw:(g + 1) * gw] = _dot_nt(
                    cm[i * GROUP:(i + 1) * GROUP, g * D_STATE:(g + 1) * D_STATE].astype(BF16), s_g)
        y_off = yoff_ref[...]
    else:
        y_off = jnp.concatenate(
            [_dot_nt(cbf[:, g * D_STATE:(g + 1) * D_STATE], sout_ref[0, g * gw:(g + 1) * gw, :].astype(BF16))
             for g in range(SSD_GROUPS)], axis=1)
    y = y_diag + y_off * ecum_e + dsk_ref[...] * x

    xd = xdt * erev_e
    for g in range(SSD_GROUPS):
        xd_t = xd[:, g * gw:(g + 1) * gw].T
        b_g = bb[:, g * D_STATE:(g + 1) * D_STATE]
        if sample:
            tcols = lax.broadcasted_iota(jnp.int32, xd_t.shape, 1) >> 3
            for i in range(CHUNK // GROUP):
                upd = _dot(jnp.where(tcols == i, xd_t, 0.0).astype(BF16), b_g)
                last = i * GROUP + GROUP - 1
                for hh in range(hpg):
                    h = g * hpg + hh
                    r0 = h * SSD_HEAD_DIM
                    sout_ref[i, r0:r0 + SSD_HEAD_DIM, :] = (
                        s0_ref[i, r0:r0 + SSD_HEAD_DIM, :] * ecum[last:last + 1, h:h + 1]
                        + upd[hh * SSD_HEAD_DIM:(hh + 1) * SSD_HEAD_DIM, :])
        else:
            upd = _dot(xd_t.astype(BF16), b_g)
            for hh in range(hpg):
                h = g * hpg + hh
                r0 = h * SSD_HEAD_DIM
                sout_ref[0, r0:r0 + SSD_HEAD_DIM, :] = (
                    sout_ref[0, r0:r0 + SSD_HEAD_DIM, :] * ecum[CHUNK - 1:CHUNK, h:h + 1]
                    + upd[hh * SSD_HEAD_DIM:(hh + 1) * SSD_HEAD_DIM, :])

    zf = z_ref[...]
    yf = y * (zf * _sigmoid(zf))
    parts = []
    for g in range(SSD_GROUPS):
        part = yf[:, g * gw:(g + 1) * gw]
        ms = jnp.mean(part * part, axis=-1, keepdims=True)
        parts.append(part * lax.rsqrt(ms + EPS))
    y_ssd = jnp.concatenate(parts, axis=1) * sn_ref[...]

    vb = v_ref[...].astype(BF16)
    sg = [_dot(wsp_ref[h], vb[:, (h // 2) * LANES:(h // 2 + 1) * LANES]) for h in range(GMLP_HEADS)]
    s = _merge_head_pairs(sg, lane_lt_half) + bsp_ref[...]
    y_gm = u_ref[...] * s

    y_ref[:, :d_ssd] = y_ssd.astype(BF16)
    y_ref[:, d_ssd:] = y_gm.astype(BF16)


def _mixer_prompt(z, xbc, dt, u, v, params, n_seq, n_chunk, layer):
    n_inner = math.gcd(n_chunk, PROMPT_TILES_PER_STEP)
    n_outer = n_chunk // n_inner
    tok = lambda w: pl.BlockSpec((n_inner * CHUNK, w), lambda b, c: (b * n_outer + c, 0))
    d_ssd = z.shape[1]
    d_mix = d_ssd + u.shape[1]
    return pl.pallas_call(
        functools.partial(_mixer_kernel, sample=False, n_inner=n_inner),
        grid=(n_seq, n_outer),
        in_specs=[tok(z.shape[1]), tok(xbc.shape[1]), tok(dt.shape[1]), tok(u.shape[1]), tok(v.shape[1])]
        + [_layer_spec(a, layer) for a in params],
        out_specs=[tok(d_mix),
                   pl.BlockSpec((1, d_ssd, D_STATE), lambda b, c: (b, 0, 0))],
        out_shape=[jax.ShapeDtypeStruct((n_seq * n_chunk * CHUNK, d_mix), BF16),
                   jax.ShapeDtypeStruct((n_seq, d_ssd, D_STATE), F32)],
        scratch_shapes=[pltpu.VMEM((8, xbc.shape[1]), F32)],
        compiler_params=_params(2),
        name="mixer_prompt",
    )(z, xbc, dt, u, v, *params)


def _mixer_sample(z, xbc, dt, u, v, cs_all, s0_all, prev, params, row0, n_rows, layer):
    blk0 = row0 // CHUNK
    n_seq_blk = CHUNK // GROUP
    n_blk = n_rows // CHUNK
    tok = lambda w: pl.BlockSpec((CHUNK, w), lambda i: (blk0 + i, 0))
    d_ssd = z.shape[1]
    d_mix = d_ssd + u.shape[1]
    st = (n_seq_blk, d_ssd, D_STATE)
    stack = lambda n: pl.BlockSpec((n,) + st, lambda i: (0, i, 0, 0))
    prev_args = [] if layer == 0 else [prev]
    return pl.pallas_call(
        functools.partial(_mixer_kernel, sample=True, n_inner=1, n_prev=layer),
        grid=(n_blk,),
        in_specs=[tok(z.shape[1]), tok(xbc.shape[1]), tok(dt.shape[1]), tok(u.shape[1]), tok(v.shape[1]),
                  pl.BlockSpec((CHUNK, cs_all.shape[1]), lambda i: (layer * n_blk + i, 0))]
        + [stack(layer) for _ in prev_args]
        + [pl.BlockSpec(st, lambda i: (layer * n_blk + i, 0, 0))]
        + [_layer_spec(a, layer) for a in params],
        out_specs=[pl.BlockSpec((CHUNK, d_mix), lambda i: (i, 0)), stack(layer + 1)],
        out_shape=[jax.ShapeDtypeStruct((n_rows, d_mix), BF16),
                   jax.ShapeDtypeStruct((layer + 1, n_blk * n_seq_blk, d_ssd, D_STATE), F32)],
        scratch_shapes=[pltpu.VMEM((8, xbc.shape[1]), F32), pltpu.VMEM((CHUNK, d_ssd), F32)],
        compiler_params=_params(1),
        name="mixer_sample",
    )(z, xbc, dt, u, v, cs_all, *prev_args, s0_all, *params)


def _first_argmax(vals):
    m = vals[0]
    for v in vals[1:]:
        m = jnp.maximum(m, v)
    idx = jnp.full(m.shape, len(vals) - 1, jnp.int32)
    for k in range(len(vals) - 2, -1, -1):
        idx = jnp.where(vals[k] >= m, k, idx)
    return m, idx


def _slab_store(slab_ref, x, pitch, tok0=0):
    for k in range(x.shape[1] // LANES):
        slab_ref[pl.ds(tok0 * pitch + k, x.shape[0], stride=pitch), :] = x[:, k * LANES:(k + 1) * LANES]


def _slab_load(slab_ref, rows, n_pieces, pitch, tok0=0):
    return jnp.concatenate([slab_ref[pl.ds(tok0 * pitch + k, rows, stride=pitch), :] for k in range(n_pieces)],
                           axis=1)


def _post_kernel(yp_ref, ys_ref, hp_ref, hs_ref, wo_ref, nf_ref, wr_ref, br_ref,
                 slab_ref, route_ref, cnt_ref, carry_ref, earlier_ref, *, n_prompt_tiles):
    i = pl.program_id(0)
    tm, d = hp_ref.shape

    @pl.when(i == 0)
    def _():
        carry_ref[...] = jnp.zeros(carry_ref.shape, F32)
        tr = lax.broadcasted_iota(jnp.int32, (tm, tm), 0)
        tc = lax.broadcasted_iota(jnp.int32, (tm, tm), 1)
        earlier_ref[...] = jnp.where(tr < tc, 1.0, 0.0).astype(BF16)

    ym = _pair_load(yp_ref, ys_ref, n_prompt_tiles)
    h1 = _pair_load(hp_ref, hs_ref, n_prompt_tiles) + _dot(ym, wo_ref[...])
    _slab_store(slab_ref, h1, X_PITCH)

    t = _rms(h1, nf_ref[...])
    wr = wr_ref[...]
    wr_hi = wr.astype(BF16)
    wr_lo = (wr - wr_hi.astype(F32)).astype(BF16)
    t_hi = t.astype(BF16)
    t_lo = (t - t_hi.astype(F32)).astype(BF16)
    by_hi = _dot_nt(jnp.concatenate([wr_hi, wr_lo], axis=0), t_hi)
    logits = (by_hi[:ROUTE_ROWS] + (by_hi[ROUTE_ROWS:] + _dot_nt(wr_hi, t_lo))) + br_ref[...]
    lg = [logits[k:k + 1, :] for k in range(N_EXPERT_GROUPS)]
    m, g = _first_argmax(lg)
    ssum = jnp.exp(lg[0] - m)
    for k in range(1, N_EXPERT_GROUPS):
        ssum = ssum + jnp.exp(lg[k] - m)
    p_sel = 1.0 / ssum
    le = [logits[N_EXPERT_GROUPS + e:N_EXPERT_GROUPS + e + 1, :] for e in range(N_EXPERT_GROUPS * EXPERTS_PER_GROUP)]
    a = []
    for k in range(EXPERTS_PER_GROUP):
        sel = le[(N_EXPERT_GROUPS - 1) * EXPERTS_PER_GROUP + k]
        for gi in range(N_EXPERT_GROUPS - 2, -1, -1):
            sel = jnp.where(g == gi, le[gi * EXPERTS_PER_GROUP + k], sel)
        a.append(sel)
    v1, i1 = _first_argmax(a)
    a2 = [jnp.where(i1 == k, -jnp.inf, a[k]) for k in range(EXPERTS_PER_GROUP)]
    v2, i2 = _first_argmax(a2)
    e2 = jnp.exp(v2 - v1)
    den = 1.0 + e2
    g1 = (1.0 / den) * p_sel
    g2 = (e2 / den) * p_sel
    lo = jnp.minimum(i1, i2)
    hi = jnp.maximum(i1, i2)
    c_lo = jnp.where(i1 < i2, g1, g2)
    c_hi = jnp.where(i1 < i2, g2, g1)
    pair = jnp.where(lo == 0, hi - 1, jnp.where(lo == 1, hi + 1, N_PAIRS - 1))
    bucket = g * N_PAIRS + pair

    brow = lax.broadcasted_iota(jnp.int32, (ROUTE_ROWS, tm), 0)
    onehot = jnp.where(brow == bucket, 1.0, 0.0)
    prefix = _dot(onehot.astype(BF16), earlier_ref[...])
    carry = carry_ref[:, 0:1]
    rank = jnp.sum(onehot * (prefix + carry), axis=0, keepdims=True)
    carry = carry + jnp.sum(onehot, axis=1, keepdims=True)
    carry_b = jnp.broadcast_to(carry, carry_ref.shape)
    carry_ref[...] = carry_b
    cnt_ref[...] = carry_b

    rr = lax.broadcasted_iota(jnp.int32, (8, tm), 0)
    route_ref[...] = jnp.where(rr == 0, bucket.astype(F32), jnp.where(rr == 1, rank, 0.0))
    ar = lax.broadcasted_iota(jnp.int32, (LANES, tm), 0)
    aux = jnp.where(ar == 0, c_lo, jnp.where(ar == 1, c_hi, 0.0))
    slab_ref[pl.ds(d // LANES, tm, stride=X_PITCH), :] = aux.T


def _post(yp, ys, hp, hs, params, layer):
    d = hp.shape[1]
    t = hp.shape[0] + hs.shape[0]
    n_prompt_tiles = hp.shape[0] // TOK_TILE
    return pl.pallas_call(
        functools.partial(_post_kernel, n_prompt_tiles=n_prompt_tiles),
        grid=(t // TOK_TILE,),
        in_specs=_pair_specs(d, n_prompt_tiles) + _pair_specs(d, n_prompt_tiles)
        + [_layer_spec(a, layer) for a in params],
        out_specs=[pl.BlockSpec((TOK_TILE * X_PITCH, LANES), lambda i: (i, 0)),
                   pl.BlockSpec((8, TOK_TILE), lambda i: (0, i)),
                   pl.BlockSpec((ROUTE_ROWS, LANES), lambda i: (0, 0))],
        out_shape=[jax.ShapeDtypeStruct((t * X_PITCH, LANES), F32),
                   jax.ShapeDtypeStruct((8, t), F32),
                   jax.ShapeDtypeStruct((ROUTE_ROWS, LANES), F32)],
        scratch_shapes=[pltpu.VMEM((ROUTE_ROWS, LANES), F32), pltpu.VMEM((TOK_TILE, TOK_TILE), BF16)],
        compiler_params=_params(1),
        name="post",
    )(yp, ys, hp, hs, *params)


def _scatter_kernel(pos_ref, src_ref, _, dst_ref, sem, *, pitch):
    n_tok = src_ref.shape[0] // pitch
    for r in range(n_tok):
        pltpu.make_async_copy(src_ref.at[pl.ds(r * pitch, pitch)],
                              dst_ref.at[pl.ds(pos_ref[0, 0, r] * pitch, pitch)], sem).start(priority=r % 2)
    pltpu.make_async_copy(src_ref, dst_ref.at[pl.ds(0, n_tok * pitch)], sem).wait()


def _scatter_tokens(pos, src, n_out, pitch, init=None):
    n = pos.shape[0]
    pos3 = pos.reshape(n // TOK_TILE, 1, TOK_TILE)
    hbm = pl.BlockSpec(memory_space=pl.ANY)
    if init is None:
        init = jnp.zeros((n_out * pitch, LANES), src.dtype)
    return pl.pallas_call(
        functools.partial(_scatter_kernel, pitch=pitch),
        grid=(n // TOK_TILE,),
        in_specs=[pl.BlockSpec((1, 1, TOK_TILE), lambda i: (i, 0, 0), memory_space=pltpu.SMEM),
                  pl.BlockSpec((TOK_TILE * pitch, LANES), lambda i: (i, 0)), hbm],
        out_specs=hbm,
        out_shape=jax.ShapeDtypeStruct((n_out * pitch, LANES), src.dtype),
        scratch_shapes=[pltpu.SemaphoreType.DMA(())],
        input_output_aliases={2: 0},
        compiler_params=_params(1),
        name="scatter_tokens",
    )(pos3, src, init)


def _moe_kernel(elo_ref, ehi_ref, nused_ref, x_ref, nf_ref, wg_lo, wu_lo, wd_lo, wg_hi, wu_hi, wd_hi, o_ref):
    i = pl.program_id(0)
    n_pieces = nf_ref.shape[1] // LANES

    @pl.when(i < nused_ref[0])
    def _():
        x = _slab_load(x_ref, MOE_TILE, n_pieces, X_PITCH)
        gates = x_ref[pl.ds(n_pieces, MOE_TILE, stride=X_PITCH), :]
        c_lo = gates[:, 0:1]
        c_hi = gates[:, 1:2]
        t = _rms(x, nf_ref[...]).astype(BF16)

        def expert(wg, wu, wd):
            gate = _dot(t, wg[...])
            hid = (gate * _sigmoid(gate)) * _dot(t, wu[...])
            return _dot(hid.astype(BF16), wd[...])

        y = c_lo * expert(wg_lo, wu_lo, wd_lo)
        y = y + c_hi * expert(wg_hi, wu_hi, wd_hi)
        _slab_store(o_ref, x + y, Y_PITCH)
        o_ref[pl.ds(TOKEN_ROWS, MOE_TILE, stride=Y_PITCH), :] = jnp.zeros((MOE_TILE, LANES), F32)

    @pl.when(i >= nused_ref[0])
    def _():
        o_ref[...] = jnp.zeros(o_ref.shape, F32)


def _moe(e_lo, e_hi, n_used, xs, nf, wg, wu, wd, layer):
    n_tiles = xs.shape[0] // (MOE_TILE * X_PITCH)
    d, de = wg.shape[1:]
    lo = lambda i, elo, ehi, nu: (elo[i], 0, 0)
    hi = lambda i, elo, ehi, nu: (ehi[i], 0, 0)
    grid_spec = pltpu.PrefetchScalarGridSpec(
        num_scalar_prefetch=3,
        grid=(n_tiles,),
        in_specs=[pl.BlockSpec((MOE_TILE * X_PITCH, LANES),
                               lambda i, elo, ehi, nu: (jnp.maximum(jnp.minimum(i, nu[0] - 1), 0), 0)),
                  _layer_spec(nf, layer),
                  pl.BlockSpec((None, d, de), lo), pl.BlockSpec((None, d, de), lo), pl.BlockSpec((None, de, d), lo),
                  pl.BlockSpec((None, d, de), hi), pl.BlockSpec((None, d, de), hi), pl.BlockSpec((None, de, d), hi)],
        out_specs=pl.BlockSpec((MOE_TILE * Y_PITCH, LANES), lambda i, elo, ehi, nu: (i, 0)),
    )
    return pl.pallas_call(
        _moe_kernel,
        grid_spec=grid_spec,
        out_shape=jax.ShapeDtypeStruct((n_tiles * MOE_TILE * Y_PITCH, LANES), F32),
        compiler_params=_params(1),
        name="expert_pairs",
    )(e_lo, e_hi, n_used, xs, nf, wg, wu, wd, wg, wu, wd)


def _ple_kernel(pos_ref, pos_next_ref, h_hbm, pp_ref, ps_ref, wple_ref, nple_ref, npg_ref, wpg_ref, *refs,
                final, n_prompt_tiles):
    refs, (xbuf, sem) = refs[:-2], refs[-2:]
    i = pl.program_id(0)
    last = pl.num_programs(0) - 1
    slot = i % 2

    def start_tile(p_ref, s):
        for r in range(TOK_TILE):
            pltpu.make_async_copy(h_hbm.at[pl.ds(p_ref[0, 0, r] * Y_PITCH, Y_PITCH)],
                                  xbuf.at[s, pl.ds(r * Y_PITCH, Y_PITCH)], sem.at[s]).start()

    def wait_tile(s):
        pltpu.make_async_copy(h_hbm.at[pl.ds(0, TOK_TILE * Y_PITCH)], xbuf.at[s], sem.at[s]).wait()

    @pl.when(i == 0)
    def _():
        start_tile(pos_ref, 0)

    wait_tile(slot)
    start_tile(pos_next_ref, 1 - slot)
    h2 = _slab_load(xbuf.at[slot], TOK_TILE, TOKEN_ROWS, Y_PITCH)
    p = _pair_load(pp_ref, ps_ref, n_prompt_tiles)
    e = _rms(_dot(p.astype(BF16), wple_ref[...]), nple_ref[...])
    gate = _sigmoid(_dot(_rms(h2, npg_ref[...]).astype(BF16), wpg_ref[...]))
    h3 = h2 + gate * e
    if final:
        nfin_ref, op_ref, os_ref = refs
        out = _rms(h3, nfin_ref[...])
    else:
        op_ref, os_ref = refs[N_INPROJ_PARAMS:N_INPROJ_PARAMS + 2]
        out = h3
        _inproj_tile(h3, *refs[:N_INPROJ_PARAMS], *refs[N_INPROJ_PARAMS + 2:])

    @pl.when(i < n_prompt_tiles)
    def _():
        op_ref[...] = out

    @pl.when(i >= n_prompt_tiles)
    def _():
        os_ref[...] = out

    @pl.when(i == last)
    def _():
        wait_tile(1 - slot)


def _ple(pos, h2s, pp_all, ps_all, params, tail, inproj_widths, final, layer, n_prompt):
    t = pos.shape[0]
    n_steps = t // TOK_TILE
    pos3 = pos.reshape(n_steps, 1, TOK_TILE)
    d = TOKEN_ROWS * LANES
    n_prompt_tiles = n_prompt // TOK_TILE
    n_sample_tiles = n_steps - n_prompt_tiles
    out_specs = _pair_specs(d, n_prompt_tiles)
    out_shape = [jax.ShapeDtypeStruct((n_prompt, d), F32), jax.ShapeDtypeStruct((t - n_prompt, d), F32)]
    if final:
        tail = (tail,)
        tail_specs = [pl.BlockSpec(tail[0].shape, lambda i: (0,) * tail[0].ndim)]
    else:
        tail_specs, io, ish = _inproj_specs(tail, inproj_widths, t, layer + 1)
        out_specs, out_shape = out_specs + io, out_shape + ish
    return pl.pallas_call(
        functools.partial(_ple_kernel, final=final, n_prompt_tiles=n_prompt_tiles),
        grid=(n_steps,),
        in_specs=[pl.BlockSpec((1, 1, TOK_TILE), lambda i: (i, 0, 0), memory_space=pltpu.SMEM),
                  pl.BlockSpec((1, 1, TOK_TILE), lambda i: (jnp.minimum(i + 1, n_steps - 1), 0, 0),
                               memory_space=pltpu.SMEM),
                  pl.BlockSpec(memory_space=pl.ANY)]
        + _pair_specs(pp_all.shape[1], n_prompt_tiles, layer * n_prompt_tiles, layer * n_sample_tiles)
        + [_layer_spec(a, layer) for a in params] + tail_specs,
        out_specs=out_specs,
        out_shape=out_shape,
        scratch_shapes=[pltpu.VMEM((2, TOK_TILE * Y_PITCH, LANES), F32), pltpu.SemaphoreType.DMA((2,))],
        compiler_params=_params(1),
        name="ple_final" if final else "ple_inproj",
    )(pos3, pos3, h2s, pp_all, ps_all, *params, *tail)


_PAIR_LO = (0, 0, 0, 1, 1, 2)
_PAIR_HI = (1, 2, 3, 2, 3, 3)


def _routing_tables(route, cnt, n_tiles):
    counts = cnt[:N_BUCKETS, 0].astype(jnp.int32)
    padded = ((counts + MOE_TILE - 1) // MOE_TILE) * MOE_TILE
    ends = jnp.cumsum(padded)
    starts = ends - padded
    bucket = route[0].astype(jnp.int32)
    rank = route[1].astype(jnp.int32)
    pos = starts[bucket] + rank
    n_used = ends[-1] // MOE_TILE
    tile = jnp.minimum(jnp.arange(n_tiles, dtype=jnp.int32), n_used - 1)
    tb = jnp.sum((ends[None, :] <= (tile * MOE_TILE)[:, None]).astype(jnp.int32), axis=1)
    tb = jnp.minimum(tb, N_BUCKETS - 1)
    grp = tb // N_PAIRS
    pr = tb % N_PAIRS
    e_lo = grp * EXPERTS_PER_GROUP + jnp.asarray(_PAIR_LO, jnp.int32)[pr]
    e_hi = grp * EXPERTS_PER_GROUP + jnp.asarray(_PAIR_HI, jnp.int32)[pr]
    return pos, e_lo, e_hi, n_used.reshape(1)


def kernel(x_prompt, x_sample, state_conv, state_ssm, p_prompt, p_sample, norm_mix, w_in, conv_w, conv_b,
           dt_bias, a_log, d_skip, ssd_norm, gmlp_norm, w_spatial, b_spatial, w_out, norm_ffn,
           w_router_group, b_router_group, w_router_expert, b_router_expert, w_gate, w_up, w_down,
           w_ple, norm_ple, norm_pg, w_pg, norm_final):
    n_seq, seq_len, d_model = x_prompt.shape
    n_dec, dec_seq, _ = x_sample.shape
    depth = w_in.shape[0]
    conv_dim = conv_w.shape[2]
    d_ssd = SSD_HEADS * SSD_HEAD_DIM
    d_gmlp = gmlp_norm.shape[1]
    assert dec_seq == DEC_SEQ_ROWS and conv_w.shape[1] == CONV_WIDTH and seq_len % CHUNK == 0
    assert conv_dim == d_ssd + 2 * SSD_GROUPS * D_STATE and w_spatial.shape[1:] == (GMLP_HEADS, CHUNK, CHUNK)
    n_chunk = seq_len // CHUNK
    n_prompt = n_seq * seq_len
    n_srows = n_dec * GROUP
    t_all = n_prompt + n_srows
    assert n_prompt % TOK_TILE == 0 and n_srows % TOK_TILE == 0
    assert d_model == TOKEN_ROWS * LANES
    lead = GROUP - dec_seq
    n_moe_tiles = t_all // MOE_TILE + N_BUCKETS

    def sample_rows(a):
        pad = [(0, 0)] * (a.ndim - 2) + [(lead, 0), (0, 0)]
        return jnp.pad(a, pad).reshape(-1, a.shape[-1])

    hp, hs = x_prompt.reshape(n_prompt, d_model), sample_rows(x_sample)

    vec = lambda a: a.astype(F32).reshape(depth, 1, -1)
    lane_pad = ((0, 0), (0, LANES - SSD_HEADS))
    o_dt = d_ssd + conv_dim
    o_uv = o_dt + SSD_HEADS
    w_in_l = jnp.concatenate([w_in[..., :o_dt], jnp.pad(w_in[..., o_dt:o_uv], ((0, 0),) + lane_pad),
                              w_in[..., o_uv:]], axis=-1).astype(BF16)
    inproj_widths = (d_ssd, conv_dim, LANES, d_gmlp, d_gmlp)
    inproj_params = (vec(norm_mix), w_in_l, jnp.pad(dt_bias.astype(F32), lane_pad).reshape(depth, 1, LANES),
                     vec(gmlp_norm))

    head_cols = jnp.arange(d_ssd) // SSD_HEAD_DIM
    per_head = d_gmlp // GMLP_HEADS
    ws_tril = jnp.where(jnp.tril(jnp.ones((CHUNK, CHUNK), bool)), w_spatial, 0.0)
    bsp_p = jnp.repeat(jnp.swapaxes(b_spatial, 1, 2), per_head, axis=2)
    seq_eye = jnp.eye(CHUNK // GROUP, dtype=F32)
    w8 = jnp.pad(ws_tril[:, :, :dec_seq, :dec_seq], ((0, 0), (0, 0), (lead, 0), (lead, 0)))
    wsp_s = (seq_eye[None, None, :, None, :, None] * w8[:, :, None, :, None, :]).reshape(ws_tril.shape)
    b8 = jnp.pad(b_spatial[:, :, :dec_seq], ((0, 0), (0, 0), (lead, 0)))
    bsp_s = jnp.repeat(jnp.swapaxes(jnp.tile(b8, (1, 1, CHUNK // GROUP)), 1, 2), per_head, axis=2)
    mixer_common = (conv_w.astype(F32), vec(conv_b),
                    jnp.pad(-jnp.exp(a_log.astype(F32)), lane_pad).reshape(depth, 1, LANES),
                    d_skip.astype(F32)[:, head_cols].reshape(depth, 1, d_ssd), vec(ssd_norm))
    mixer_p_params = mixer_common + (ws_tril.astype(BF16), bsp_p)
    mixer_s_params = mixer_common + (wsp_s.astype(BF16), bsp_s)

    n_route_pad = ROUTE_ROWS - N_EXPERT_GROUPS * (1 + EXPERTS_PER_GROUP)
    wr = jnp.concatenate([jnp.swapaxes(w_router_group, 1, 2), jnp.swapaxes(w_router_expert, 1, 2),
                          jnp.zeros((depth, n_route_pad, d_model), F32)], axis=1).astype(F32)
    br = jnp.concatenate([b_router_group, b_router_expert, jnp.zeros((depth, n_route_pad), F32)],
                         axis=1).astype(F32).reshape(depth, ROUTE_ROWS, 1)
    post_params = (w_out.astype(BF16), vec(norm_ffn), wr, br)
    ple_params = (w_ple.astype(BF16), vec(norm_ple), vec(norm_pg), w_pg.astype(BF16))

    n_exp = w_gate.shape[1]
    wg_all = w_gate.astype(BF16).reshape((depth * n_exp,) + w_gate.shape[2:])
    wu_all = w_up.astype(BF16).reshape((depth * n_exp,) + w_up.shape[2:])
    wd_all = w_down.astype(BF16).reshape((depth * n_exp,) + w_down.shape[2:])
    s0_all = state_ssm.reshape(depth * n_dec, d_ssd, D_STATE)
    tail = CONV_WIDTH - 1
    cs_all = jnp.pad(state_conv, ((0, 0), (0, 0), (lead - tail, dec_seq), (0, 0))).reshape(depth * n_srows, conv_dim)
    pp_all = p_prompt.reshape(depth * n_prompt, p_prompt.shape[-1])
    ps_all = sample_rows(p_sample)
    tail_rows = (((jnp.arange(n_seq) + 1) * seq_len)[:, None] - tail + jnp.arange(tail)[None, :]).reshape(-1)

    xs = ssm_s = None
    convs_p, ssms_p, convs_s, vs_s = [], [], [], []
    z, xbc, dt, u, v = _inproj(hp, hs, inproj_params, inproj_widths, 0)

    for i in range(depth):
        yp, ssm_p = _mixer_prompt(z, xbc, dt, u, v, mixer_p_params, n_seq, n_chunk, i)
        ys, ssm_s = _mixer_sample(z, xbc, dt, u, v, cs_all, s0_all, ssm_s, mixer_s_params, n_prompt, n_srows, i)

        slab, route, cnt = _post(yp, ys, hp, hs, post_params, i)
        pos, e_lo, e_hi, n_used = _routing_tables(route, cnt, n_moe_tiles)

        xs = _scatter_tokens(pos, slab, n_moe_tiles * MOE_TILE, X_PITCH, init=xs)
        h2s = _moe(e_lo + i * n_exp, e_hi + i * n_exp, n_used, xs, post_params[1], wg_all, wu_all, wd_all, i)

        convs_p.append(jnp.take(xbc, tail_rows, axis=0).reshape(n_seq, tail, conv_dim))
        xbc_s = xbc[n_prompt:].reshape(n_dec, GROUP, conv_dim)
        convs_s.append(xbc_s[:, GROUP - tail:])
        ssms_p.append(ssm_p.reshape(n_seq, SSD_HEADS, SSD_HEAD_DIM, D_STATE))
        vs_s.append(v[n_prompt:].reshape(n_dec, GROUP, d_gmlp)[:, lead:])

        final = i == depth - 1
        outs = _ple(pos, h2s, pp_all, ps_all, ple_params,
                    norm_final.astype(F32).reshape(1, -1) if final else inproj_params, inproj_widths, final, i, n_prompt)
        hp, hs = outs[:2]
        if not final:
            z, xbc, dt, u, v = outs[2:]

    y_prompt = hp.reshape(n_seq, seq_len, d_model)
    y_sample = hs.reshape(n_dec, GROUP, d_model)[:, lead:]
    new_ssm_sample = ssm_s.reshape(depth, n_dec, SSD_HEADS, SSD_HEAD_DIM, D_STATE)
    return (y_prompt, y_sample, jnp.stack(convs_p), jnp.stack(ssms_p), jnp.stack(convs_s), new_ssm_sample,
            jnp.stack(vs_s))
```

```python
import functools
import math

import jax
import jax.numpy as jnp
from jax import lax
from jax.experimental import pallas as pl
from jax.experimental.pallas import tpu as pltpu

F32 = jnp.float32
BF16 = jnp.bfloat16

LANES = 128
VMEM_LIMIT_BYTES = 56 * 1024 * 1024

CONV_WIDTH = 4
SSD_HEADS = 8
SSD_HEAD_DIM = 64
SSD_GROUPS = 2
D_STATE = 128
CHUNK = 128
GMLP_HEADS = 8
N_EXPERT_GROUPS = 4
EXPERTS_PER_GROUP = 4
N_PAIRS = 6
N_BUCKETS = N_EXPERT_GROUPS * N_PAIRS
EPS = 1e-6

GROUP = 8
DEC_SEQ_ROWS = 4
TOK_TILE = 512
MOE_TILE = 256
PROMPT_TILES_PER_STEP = 4
ROUTE_ROWS = 32
TOKEN_ROWS = 8
X_PITCH = TOKEN_ROWS + 1
Y_PITCH = TOKEN_ROWS + 1
NEG_BIG = -1e30


def _dot(a, b):
    return jnp.dot(a, b, preferred_element_type=F32)


def _dot_nt(a, b):
    return lax.dot_general(a, b, (((1,), (1,)), ((), ())), preferred_element_type=F32)


def _rms(x, g):
    ms = jnp.mean(x * x, axis=-1, keepdims=True)
    return (x * lax.rsqrt(ms + EPS)) * g


def _gelu(x):
    return 0.5 * x * (1.0 + lax.erf(x * (1.0 / math.sqrt(2.0))))


def _sigmoid(x):
    return 0.5 * jnp.tanh(0.5 * x) + 0.5


def _softplus(x):
    return jnp.maximum(x, 0.0) + jnp.log1p(jnp.exp(-jnp.abs(x)))


def _params(n_grid):
    return pltpu.CompilerParams(dimension_semantics=("arbitrary",) * n_grid,
                                vmem_limit_bytes=VMEM_LIMIT_BYTES)


def _pair_specs(width, n_prompt_tiles, prompt_tile0=0, sample_tile0=0):
    return [pl.BlockSpec((TOK_TILE, width), lambda i: (prompt_tile0 + jnp.minimum(i, n_prompt_tiles - 1), 0)),
            pl.BlockSpec((TOK_TILE, width), lambda i: (sample_tile0 + jnp.maximum(i - n_prompt_tiles, 0), 0))]


def _pair_load(p_ref, s_ref, n_prompt_tiles):
    return jnp.where(pl.program_id(0) < n_prompt_tiles, p_ref[...], s_ref[...])


def _layer_spec(a, layer):
    zeros = (0,) * (a.ndim - 1)
    return pl.BlockSpec((None,) + a.shape[1:], lambda *_: (layer,) + zeros)


def _inproj_tile(h, nm_ref, w_ref, dtb_ref, gn_ref, z_ref, xbc_ref, dt_ref, u_ref, v_ref):
    a = _rms(h, nm_ref[...]).astype(BF16)
    c = [0]
    for r in (z_ref, xbc_ref, dt_ref, u_ref, v_ref):
        c.append(c[-1] + r.shape[1])
    z_ref[...] = _dot(a, w_ref[:, c[0]:c[1]])
    xbc_ref[...] = _dot(a, w_ref[:, c[1]:c[2]])
    dt_ref[...] = _softplus(_dot(a, w_ref[:, c[2]:c[3]]) + dtb_ref[...])
    u_ref[...] = _gelu(_dot(a, w_ref[:, c[3]:c[4]]))
    v_ref[...] = _rms(_gelu(_dot(a, w_ref[:, c[4]:c[5]])), gn_ref[...])


def _inproj_kernel(hp_ref, hs_ref, *refs, n_prompt_tiles):
    _inproj_tile(_pair_load(hp_ref, hs_ref, n_prompt_tiles), *refs)


N_INPROJ_PARAMS = 4


def _inproj_specs(weights, widths, t, layer):
    assert len(weights) == N_INPROJ_PARAMS
    row = lambda w: pl.BlockSpec((TOK_TILE, w), lambda i: (i, 0))
    return ([_layer_spec(a, layer) for a in weights], [row(w) for w in widths],
            [jax.ShapeDtypeStruct((t, w), F32) for w in widths])


def _inproj(hp, hs, weights, widths, layer):
    t = hp.shape[0] + hs.shape[0]
    n_prompt_tiles = hp.shape[0] // TOK_TILE
    w_specs, out_specs, out_shape = _inproj_specs(weights, widths, t, layer)
    return pl.pallas_call(
        functools.partial(_inproj_kernel, n_prompt_tiles=n_prompt_tiles),
        grid=(t // TOK_TILE,),
        in_specs=_pair_specs(hp.shape[1], n_prompt_tiles) + w_specs,
        out_specs=out_specs,
        out_shape=out_shape,
        compiler_params=_params(1),
        name="inproj",
    )(hp, hs, *weights)


def _seg_cumsum(x, seg, rowmod):
    d = 1
    while d < seg:
        x = x + jnp.where(rowmod >= d, pltpu.roll(x, d, axis=0), 0.0)
        d *= 2
    return x


def _seg_rev_cumsum(x, seg, rowmod):
    n = x.shape[0]
    d = 1
    while d < seg:
        x = x + jnp.where(rowmod + d < seg, pltpu.roll(x, n - d, axis=0), 0.0)
        d *= 2
    return x


def _expand_heads(m, lane_lt_half):
    parts = []
    for j in range(SSD_HEADS // 2):
        parts.append(jnp.where(lane_lt_half, m[:, 2 * j:2 * j + 1], m[:, 2 * j + 1:2 * j + 2]))
    return jnp.concatenate(parts, axis=1)


def _merge_head_pairs(per_head, lane_lt_half):
    parts = [jnp.where(lane_lt_half, per_head[2 * j], per_head[2 * j + 1]) for j in range(len(per_head) // 2)]
    return jnp.concatenate(parts, axis=1)


def _mixer_kernel(*refs, sample, n_inner, n_prev=0):
    n_tok = 6 if sample else 5
    tok_refs, rest = refs[:n_tok], refs[n_tok:]
    if sample:
        if n_prev:
            prev_ref, rest = rest[0], rest[1:]
        (s0_ref, cw_ref, cb_ref, arow_ref, dsk_ref, sn_ref, wsp_ref, bsp_ref,
         y_ref, stack_ref, ext_ref, yoff_ref) = rest
        if n_prev:
            stack_ref[0:n_prev] = prev_ref[...]
        sout_ref = stack_ref.at[n_prev]
        first = pl.program_id(0) == 0
    else:
        (cw_ref, cb_ref, arow_ref, dsk_ref, sn_ref, wsp_ref, bsp_ref, y_ref, sout_ref, ext_ref) = rest
        s0_ref = yoff_ref = None
        first = pl.program_id(1) == 0

    @pl.when(first)
    def _():
        ext_ref[...] = jnp.zeros(ext_ref.shape, F32)
        if not sample:
            sout_ref[...] = jnp.zeros(sout_ref.shape, F32)

    def tile(c, carry):
        r0 = pl.multiple_of(c * CHUNK, CHUNK)
        views = [r.at[pl.ds(r0, CHUNK)] for r in tok_refs + (y_ref,)]
        _mixer_tile(*views[:n_tok], s0_ref, cw_ref, cb_ref, arow_ref, dsk_ref, sn_ref, wsp_ref, bsp_ref,
                    views[n_tok], sout_ref, ext_ref, yoff_ref, sample=sample)
        return carry

    if n_inner == 1:
        tile(0, 0)
    else:
        lax.fori_loop(0, n_inner, tile, 0)


def _mixer_tile(z_ref, xbc_ref, dt_ref, u_ref, v_ref, *rest, sample):
    if sample:
        cs_ref, s0_ref = rest[0], rest[1]
        rest = rest[2:]
    else:
        rest = rest[1:]
    cw_ref, cb_ref, arow_ref, dsk_ref, sn_ref, wsp_ref, bsp_ref, y_ref, sout_ref, ext_ref, yoff_ref = rest
    seg = GROUP if sample else CHUNK
    d_ssd = SSD_HEADS * SSD_HEAD_DIM
    gw = d_ssd // SSD_GROUPS
    hpg = SSD_HEADS // SSD_GROUPS
    cs_first = GROUP - DEC_SEQ_ROWS - (CONV_WIDTH - 1)

    rows = lax.broadcasted_iota(jnp.int32, (CHUNK, LANES), 0)
    cols = lax.broadcasted_iota(jnp.int32, (CHUNK, LANES), 1)
    rowmod = rows & (seg - 1)
    lane_lt_half = cols < SSD_HEAD_DIM

    xbc = xbc_ref[...]
    if sample:
        rm = lax.broadcasted_iota(jnp.int32, xbc.shape, 0) & (GROUP - 1)
        xbc = jnp.where((rm >= cs_first) & (rm < cs_first + CONV_WIDTH - 1), cs_ref[...], xbc)

    tail = ext_ref[...]
    row8 = lax.broadcasted_iota(jnp.int32, tail.shape, 0)
    acc = cb_ref[...] + cw_ref[CONV_WIDTH - 1:CONV_WIDTH, :] * xbc
    for j in range(1, CONV_WIDTH):
        rolled = pltpu.roll(xbc, j, axis=0)
        head = jnp.where(row8 < j, pltpu.roll(tail, j, axis=0), rolled[0:8, :])
        shifted = jnp.concatenate([head, rolled[8:, :]], axis=0)
        acc = acc + cw_ref[CONV_WIDTH - 1 - j:CONV_WIDTH - j, :] * shifted
    if not sample:
        ext_ref[...] = xbc[CHUNK - 8:, :]
    xc = acc * _sigmoid(acc)
    x = xc[:, :d_ssd]
    bb = xc[:, d_ssd:d_ssd + SSD_GROUPS * D_STATE].astype(BF16)
    cm = xc[:, d_ssd + SSD_GROUPS * D_STATE:]
    cbf = cm.astype(BF16)

    dtc = dt_ref[...]
    if sample:
        dtc = jnp.where(rowmod >= GROUP - DEC_SEQ_ROWS, dtc, 0.0)
    da = dtc * arow_ref[...]
    cum = _seg_cumsum(da, seg, rowmod)
    rev = _seg_rev_cumsum(da, seg, rowmod) - da
    cum_t = cum.T
    ecum = jnp.exp(cum)
    dt_e = _expand_heads(dtc, lane_lt_half)
    ecum_e = _expand_heads(ecum, lane_lt_half)
    erev_e = _expand_heads(jnp.exp(rev), lane_lt_half)

    xdt = x * dt_e
    xdt_bf = xdt.astype(BF16)
    causal = rows >= cols
    if sample:
        same_seq = (rows >> 3) == (cols >> 3)
        causal = causal & same_seq

    yd = []
    for g in range(SSD_GROUPS):
        cb_g = _dot_nt(cbf[:, g * D_STATE:(g + 1) * D_STATE], bb[:, g * D_STATE:(g + 1) * D_STATE])
        for hh in range(hpg):
            h = g * hpg + hh
            expo = cum[:, h:h + 1] - cum_t[h:h + 1, :]
            w = (cb_g * jnp.exp(jnp.where(causal, expo, NEG_BIG))).astype(BF16)
            j = h // 2
            yd.append(_dot(w, xdt_bf[:, j * LANES:(j + 1) * LANES]))
    y_diag = _merge_head_pairs(yd, lane_lt_half)

    if sample:
        for i in range(CHUNK // GROUP):
            for g in range(SSD_GROUPS):
                s_g = s0_ref[i, g * gw:(g + 1) * gw, :].astype(BF16)
                yoff_ref[i * GROUP:(i + 1) * GROUP, g * gw:(g + 1) * gw] = _dot_nt(
                    cm[i * GROUP:(i + 1) * GROUP, g * D_STATE:(g + 1) * D_STATE].astype(BF16), s_g)
        y_off = yoff_ref[...]
    else:
        y_off = jnp.concatenate(
            [_dot_nt(cbf[:, g * D_STATE:(g + 1) * D_STATE], sout_ref[0, g * gw:(g + 1) * gw, :].astype(BF16))
             for g in range(SSD_GROUPS)], axis=1)
    y = y_diag + y_off * ecum_e + dsk_ref[...] * x

    xd = xdt * erev_e
    for g in range(SSD_GROUPS):
        xd_t = xd[:, g * gw:(g + 1) * gw].T
        b_g = bb[:, g * D_STATE:(g + 1) * D_STATE]
        if sample:
            tcols = lax.broadcasted_iota(jnp.int32, xd_t.shape, 1) >> 3
            for i in range(CHUNK // GROUP):
                upd = _dot(jnp.where(tcols == i, xd_t, 0.0).astype(BF16), b_g)
                last = i * GROUP + GROUP - 1
                for hh in range(hpg):
                    h = g * hpg + hh
                    r0 = h * SSD_HEAD_DIM
                    sout_ref[i, r0:r0 + SSD_HEAD_DIM, :] = (
                        s0_ref[i, r0:r0 + SSD_HEAD_DIM, :] * ecum[last:last + 1, h:h + 1]
                        + upd[hh * SSD_HEAD_DIM:(hh + 1) * SSD_HEAD_DIM, :])
        else:
            upd = _dot(xd_t.astype(BF16), b_g)
            for hh in range(hpg):
                h = g * hpg + hh
                r0 = h * SSD_HEAD_DIM
                sout_ref[0, r0:r0 + SSD_HEAD_DIM, :] = (
                    sout_ref[0, r0:r0 + SSD_HEAD_DIM, :] * ecum[CHUNK - 1:CHUNK, h:h + 1]
                    + upd[hh * SSD_HEAD_DIM:(hh + 1) * SSD_HEAD_DIM, :])

    zf = z_ref[...]
    yf = y * (zf * _sigmoid(zf))
    parts = []
    for g in range(SSD_GROUPS):
        part = yf[:, g * gw:(g + 1) * gw]
        ms = jnp.mean(part * part, axis=-1, keepdims=True)
        parts.append(part * lax.rsqrt(ms + EPS))
    y_ssd = jnp.concatenate(parts, axis=1) * sn_ref[...]

    vb = v_ref[...].astype(BF16)
    if sample:
        n_grp = CHUNK // GROUP
        wms = [jnp.where(same_seq, jnp.tile(wsp_ref[h], (n_grp, 1)), 0.0).astype(BF16) for h in range(GMLP_HEADS)]
    else:
        wms = [wsp_ref[h] for h in range(GMLP_HEADS)]
    sg = [_dot(wms[h], vb[:, (h // 2) * LANES:(h // 2 + 1) * LANES]) for h in range(GMLP_HEADS)]
    s = _merge_head_pairs(sg, lane_lt_half) + bsp_ref[...]
    y_gm = u_ref[...] * s

    y_ref[:, :d_ssd] = y_ssd.astype(BF16)
    y_ref[:, d_ssd:] = y_gm.astype(BF16)


def _mixer_prompt(z, xbc, dt, u, v, params, n_seq, n_chunk, layer):
    n_inner = math.gcd(n_chunk, PROMPT_TILES_PER_STEP)
    n_outer = n_chunk // n_inner
    tok = lambda w: pl.BlockSpec((n_inner * CHUNK, w), lambda b, c: (b * n_outer + c, 0))
    d_ssd = z.shape[1]
    d_mix = d_ssd + u.shape[1]
    return pl.pallas_call(
        functools.partial(_mixer_kernel, sample=False, n_inner=n_inner),
        grid=(n_seq, n_outer),
        in_specs=[tok(z.shape[1]), tok(xbc.shape[1]), tok(dt.shape[1]), tok(u.shape[1]), tok(v.shape[1])]
        + [_layer_spec(a, layer) for a in params],
        out_specs=[tok(d_mix),
                   pl.BlockSpec((1, d_ssd, D_STATE), lambda b, c: (b, 0, 0))],
        out_shape=[jax.ShapeDtypeStruct((n_seq * n_chunk * CHUNK, d_mix), BF16),
                   jax.ShapeDtypeStruct((n_seq, d_ssd, D_STATE), F32)],
        scratch_shapes=[pltpu.VMEM((8, xbc.shape[1]), F32)],
        compiler_params=_params(2),
        name="mixer_prompt",
    )(z, xbc, dt, u, v, *params)


def _mixer_sample(z, xbc, dt, u, v, cs_all, s0_all, prev, params, row0, n_rows, layer):
    blk0 = row0 // CHUNK
    n_seq_blk = CHUNK // GROUP
    n_blk = n_rows // CHUNK
    tok = lambda w: pl.BlockSpec((CHUNK, w), lambda i: (blk0 + i, 0))
    d_ssd = z.shape[1]
    d_mix = d_ssd + u.shape[1]
    st = (n_seq_blk, d_ssd, D_STATE)
    stack = lambda n: pl.BlockSpec((n,) + st, lambda i: (0, i, 0, 0))
    prev_args = [] if layer == 0 else [prev]
    return pl.pallas_call(
        functools.partial(_mixer_kernel, sample=True, n_inner=1, n_prev=layer),
        grid=(n_blk,),
        in_specs=[tok(z.shape[1]), tok(xbc.shape[1]), tok(dt.shape[1]), tok(u.shape[1]), tok(v.shape[1]),
                  pl.BlockSpec((CHUNK, cs_all.shape[1]), lambda i: (layer * n_blk + i, 0))]
        + [stack(layer) for _ in prev_args]
        + [pl.BlockSpec(st, lambda i: (layer * n_blk + i, 0, 0))]
        + [_layer_spec(a, layer) for a in params],
        out_specs=[pl.BlockSpec((CHUNK, d_mix), lambda i: (i, 0)), stack(layer + 1)],
        out_shape=[jax.ShapeDtypeStruct((n_rows, d_mix), BF16),
                   jax.ShapeDtypeStruct((layer + 1, n_blk * n_seq_blk, d_ssd, D_STATE), F32)],
        scratch_shapes=[pltpu.VMEM((8, xbc.shape[1]), F32), pltpu.VMEM((CHUNK, d_ssd), F32)],
        compiler_params=_params(1),
        name="mixer_sample",
    )(z, xbc, dt, u, v, cs_all, *prev_args, s0_all, *params)


def _first_argmax(vals):
    m = vals[0]
    for v in vals[1:]:
        m = jnp.maximum(m, v)
    idx = jnp.full(m.shape, len(vals) - 1, jnp.int32)
    for k in range(len(vals) - 2, -1, -1):
        idx = jnp.where(vals[k] >= m, k, idx)
    return m, idx


def _slab_store(slab_ref, x, pitch, tok0=0):
    for k in range(x.shape[1] // LANES):
        slab_ref[pl.ds(tok0 * pitch + k, x.shape[0], stride=pitch), :] = x[:, k * LANES:(k + 1) * LANES]


def _slab_load(slab_ref, rows, n_pieces, pitch, tok0=0):
    return jnp.concatenate([slab_ref[pl.ds(tok0 * pitch + k, rows, stride=pitch), :] for k in range(n_pieces)],
                           axis=1)


def _post_kernel(yp_ref, ys_ref, hp_ref, hs_ref, wo_ref, nf_ref, wr_ref, br_ref,
                 slab_ref, bkt_ref, rnk_ref, cnt_ref, carry_ref, earlier_ref, *, n_prompt_tiles):
    i = pl.program_id(0)
    tm, d = hp_ref.shape

    @pl.when(i == 0)
    def _():
        carry_ref[...] = jnp.zeros(carry_ref.shape, F32)
        tr = lax.broadcasted_iota(jnp.int32, (tm, tm), 0)
        tc = lax.broadcasted_iota(jnp.int32, (tm, tm), 1)
        earlier_ref[...] = jnp.where(tr < tc, 1.0, 0.0).astype(BF16)

    ym = _pair_load(yp_ref, ys_ref, n_prompt_tiles)
    h1 = _pair_load(hp_ref, hs_ref, n_prompt_tiles) + _dot(ym, wo_ref[...])
    _slab_store(slab_ref, h1, X_PITCH)

    t = _rms(h1, nf_ref[...])
    wr = wr_ref[...]
    wr_hi = wr.astype(BF16)
    wr_lo = (wr - wr_hi.astype(F32)).astype(BF16)
    t_hi = t.astype(BF16)
    t_lo = (t - t_hi.astype(F32)).astype(BF16)
    by_hi = _dot_nt(jnp.concatenate([wr_hi, wr_lo], axis=0), t_hi)
    logits = (by_hi[:ROUTE_ROWS] + (by_hi[ROUTE_ROWS:] + _dot_nt(wr_hi, t_lo))) + br_ref[...]
    lg = [logits[k:k + 1, :] for k in range(N_EXPERT_GROUPS)]
    m, g = _first_argmax(lg)
    ssum = jnp.exp(lg[0] - m)
    for k in range(1, N_EXPERT_GROUPS):
        ssum = ssum + jnp.exp(lg[k] - m)
    p_sel = 1.0 / ssum
    le = [logits[N_EXPERT_GROUPS + e:N_EXPERT_GROUPS + e + 1, :] for e in range(N_EXPERT_GROUPS * EXPERTS_PER_GROUP)]
    a = []
    for k in range(EXPERTS_PER_GROUP):
        sel = le[(N_EXPERT_GROUPS - 1) * EXPERTS_PER_GROUP + k]
        for gi in range(N_EXPERT_GROUPS - 2, -1, -1):
            sel = jnp.where(g == gi, le[gi * EXPERTS_PER_GROUP + k], sel)
        a.append(sel)
    v1, i1 = _first_argmax(a)
    a2 = [jnp.where(i1 == k, -jnp.inf, a[k]) for k in range(EXPERTS_PER_GROUP)]
    v2, i2 = _first_argmax(a2)
    e2 = jnp.exp(v2 - v1)
    den = 1.0 + e2
    g1 = (1.0 / den) * p_sel
    g2 = (e2 / den) * p_sel
    lo = jnp.minimum(i1, i2)
    hi = jnp.maximum(i1, i2)
    c_lo = jnp.where(i1 < i2, g1, g2)
    c_hi = jnp.where(i1 < i2, g2, g1)
    pair = jnp.where(lo == 0, hi - 1, jnp.where(lo == 1, hi + 1, N_PAIRS - 1))
    bucket = g * N_PAIRS + pair

    brow = lax.broadcasted_iota(jnp.int32, (ROUTE_ROWS, tm), 0)
    onehot = jnp.where(brow == bucket, 1.0, 0.0)
    prefix = _dot(onehot.astype(BF16), earlier_ref[...])
    carry = carry_ref[:, 0:1]
    rank = jnp.sum(onehot * (prefix + carry), axis=0, keepdims=True)
    carry = carry + jnp.sum(onehot, axis=1, keepdims=True)
    carry_b = jnp.broadcast_to(carry, carry_ref.shape)
    carry_ref[...] = carry_b
    cnt_ref[...] = carry_b

    bkt_ref[0] = bucket
    rnk_ref[0] = rank.astype(jnp.int32)
    ar = lax.broadcasted_iota(jnp.int32, (LANES, tm), 0)
    aux = jnp.where(ar == 0, c_lo, jnp.where(ar == 1, c_hi, 0.0))
    slab_ref[pl.ds(d // LANES, tm, stride=X_PITCH), :] = aux.T


def _post(yp, ys, hp, hs, params, layer):
    d = hp.shape[1]
    t = hp.shape[0] + hs.shape[0]
    n_prompt_tiles = hp.shape[0] // TOK_TILE
    return pl.pallas_call(
        functools.partial(_post_kernel, n_prompt_tiles=n_prompt_tiles),
        grid=(t // TOK_TILE,),
        in_specs=_pair_specs(d, n_prompt_tiles) + _pair_specs(d, n_prompt_tiles)
        + [_layer_spec(a, layer) for a in params],
        out_specs=[pl.BlockSpec((TOK_TILE * X_PITCH, LANES), lambda i: (i, 0)),
                   pl.BlockSpec((1, 1, TOK_TILE), lambda i: (i, 0, 0)),
                   pl.BlockSpec((1, 1, TOK_TILE), lambda i: (i, 0, 0)),
                   pl.BlockSpec((ROUTE_ROWS, LANES), lambda i: (0, 0))],
        out_shape=[jax.ShapeDtypeStruct((t * X_PITCH, LANES), F32),
                   jax.ShapeDtypeStruct((t // TOK_TILE, 1, TOK_TILE), jnp.int32),
                   jax.ShapeDtypeStruct((t // TOK_TILE, 1, TOK_TILE), jnp.int32),
                   jax.ShapeDtypeStruct((ROUTE_ROWS, LANES), F32)],
        scratch_shapes=[pltpu.VMEM((ROUTE_ROWS, LANES), F32), pltpu.VMEM((TOK_TILE, TOK_TILE), BF16)],
        compiler_params=_params(1),
        name="post",
    )(yp, ys, hp, hs, *params)


def _scatter_kernel(starts_ref, bkt_ref, rnk_ref, src_ref, _, dst_ref, pos_ref, sem, *, pitch):
    n_tok = src_ref.shape[0] // pitch
    for r in range(n_tok):
        slot = starts_ref[bkt_ref[0, 0, r]] + rnk_ref[0, 0, r]
        pos_ref[0, 0, r] = slot
        pltpu.make_async_copy(src_ref.at[pl.ds(r * pitch, pitch)],
                              dst_ref.at[pl.ds(slot * pitch, pitch)], sem).start(priority=r % 2)
    pltpu.make_async_copy(src_ref, dst_ref.at[pl.ds(0, n_tok * pitch)], sem).wait()


def _scatter_tokens(starts, bkt, rnk, src, n_out, pitch, init=None):
    hbm = pl.BlockSpec(memory_space=pl.ANY)
    tile_smem = pl.BlockSpec((1, 1, TOK_TILE), lambda i: (i, 0, 0), memory_space=pltpu.SMEM)
    if init is None:
        init = jnp.zeros((n_out * pitch, LANES), src.dtype)
    return pl.pallas_call(
        functools.partial(_scatter_kernel, pitch=pitch),
        grid=(bkt.shape[0],),
        in_specs=[pl.BlockSpec(memory_space=pltpu.SMEM), tile_smem, tile_smem,
                  pl.BlockSpec((TOK_TILE * pitch, LANES), lambda i: (i, 0)), hbm],
        out_specs=[hbm, tile_smem],
        out_shape=[jax.ShapeDtypeStruct((n_out * pitch, LANES), src.dtype),
                   jax.ShapeDtypeStruct(bkt.shape, jnp.int32)],
        scratch_shapes=[pltpu.SemaphoreType.DMA(())],
        input_output_aliases={4: 0},
        compiler_params=_params(1),
        name="scatter_tokens",
    )(starts, bkt, rnk, src, init)


def _moe_kernel(elo_ref, ehi_ref, nused_ref, x_ref, nf_ref, wg_lo, wu_lo, wd_lo, wg_hi, wu_hi, wd_hi, o_ref):
    i = pl.program_id(0)
    n_pieces = nf_ref.shape[1] // LANES

    @pl.when(i < nused_ref[0])
    def _():
        x = _slab_load(x_ref, MOE_TILE, n_pieces, X_PITCH)
        gates = x_ref[pl.ds(n_pieces, MOE_TILE, stride=X_PITCH), :]
        c_lo = gates[:, 0:1]
        c_hi = gates[:, 1:2]
        t = _rms(x, nf_ref[...]).astype(BF16)

        def expert(wg, wu, wd):
            gate = _dot(t, wg[...])
            hid = (gate * _sigmoid(gate)) * _dot(t, wu[...])
            return _dot(hid.astype(BF16), wd[...])

        y = c_lo * expert(wg_lo, wu_lo, wd_lo)
        y = y + c_hi * expert(wg_hi, wu_hi, wd_hi)
        _slab_store(o_ref, x + y, Y_PITCH)
        o_ref[pl.ds(TOKEN_ROWS, MOE_TILE, stride=Y_PITCH), :] = jnp.zeros((MOE_TILE, LANES), F32)

    @pl.when(i >= nused_ref[0])
    def _():
        o_ref[...] = jnp.zeros(o_ref.shape, F32)


def _moe(e_lo, e_hi, n_used, xs, nf, wg, wu, wd, layer):
    n_tiles = xs.shape[0] // (MOE_TILE * X_PITCH)
    d, de = wg.shape[1:]
    lo = lambda i, elo, ehi, nu: (elo[i], 0, 0)
    hi = lambda i, elo, ehi, nu: (ehi[i], 0, 0)
    grid_spec = pltpu.PrefetchScalarGridSpec(
        num_scalar_prefetch=3,
        grid=(n_tiles,),
        in_specs=[pl.BlockSpec((MOE_TILE * X_PITCH, LANES),
                               lambda i, elo, ehi, nu: (jnp.maximum(jnp.minimum(i, nu[0] - 1), 0), 0)),
                  _layer_spec(nf, layer),
                  pl.BlockSpec((None, d, de), lo), pl.BlockSpec((None, d, de), lo), pl.BlockSpec((None, de, d), lo),
                  pl.BlockSpec((None, d, de), hi), pl.BlockSpec((None, d, de), hi), pl.BlockSpec((None, de, d), hi)],
        out_specs=pl.BlockSpec((MOE_TILE * Y_PITCH, LANES), lambda i, elo, ehi, nu: (i, 0)),
    )
    return pl.pallas_call(
        _moe_kernel,
        grid_spec=grid_spec,
        out_shape=jax.ShapeDtypeStruct((n_tiles * MOE_TILE * Y_PITCH, LANES), F32),
        compiler_params=_params(1),
        name="expert_pairs",
    )(e_lo, e_hi, n_used, xs, nf, wg, wu, wd, wg, wu, wd)


def _ple_kernel(pos_ref, pos_next_ref, h_hbm, pp_ref, ps_ref, wple_ref, nple_ref, npg_ref, wpg_ref, *refs,
                final, n_prompt_tiles):
    refs, (xbuf, sem) = refs[:-2], refs[-2:]
    i = pl.program_id(0)
    last = pl.num_programs(0) - 1
    slot = i % 2

    def start_tile(p_ref, s):
        for r in range(TOK_TILE):
            pltpu.make_async_copy(h_hbm.at[pl.ds(p_ref[0, 0, r] * Y_PITCH, Y_PITCH)],
                                  xbuf.at[s, pl.ds(r * Y_PITCH, Y_PITCH)], sem.at[s]).start()

    def wait_tile(s):
        pltpu.make_async_copy(h_hbm.at[pl.ds(0, TOK_TILE * Y_PITCH)], xbuf.at[s], sem.at[s]).wait()

    @pl.when(i == 0)
    def _():
        start_tile(pos_ref, 0)

    wait_tile(slot)
    start_tile(pos_next_ref, 1 - slot)
    h2 = _slab_load(xbuf.at[slot], TOK_TILE, TOKEN_ROWS, Y_PITCH)
    p = _pair_load(pp_ref, ps_ref, n_prompt_tiles)
    e = _rms(_dot(p.astype(BF16), wple_ref[...]), nple_ref[...])
    gate = _sigmoid(_dot(_rms(h2, npg_ref[...]).astype(BF16), wpg_ref[...]))
    h3 = h2 + gate * e
    if final:
        nfin_ref, op_ref, os_ref = refs
        out = _rms(h3, nfin_ref[...])
    else:
        op_ref, os_ref = refs[N_INPROJ_PARAMS:N_INPROJ_PARAMS + 2]
        out = h3
        _inproj_tile(h3, *refs[:N_INPROJ_PARAMS], *refs[N_INPROJ_PARAMS + 2:])

    @pl.when(i < n_prompt_tiles)
    def _():
        op_ref[...] = out

    @pl.when(i >= n_prompt_tiles)
    def _():
        os_ref[...] = out

    @pl.when(i == last)
    def _():
        wait_tile(1 - slot)


def _ple(pos, h2s, pp_all, ps_all, params, tail, inproj_widths, final, layer, n_prompt):
    n_steps = pos.shape[0]
    t = n_steps * TOK_TILE
    d = TOKEN_ROWS * LANES
    tile_smem = pl.BlockSpec((1, 1, TOK_TILE), lambda i: (i, 0, 0), memory_space=pltpu.SMEM)
    next_smem = pl.BlockSpec((1, 1, TOK_TILE), lambda i: (jnp.minimum(i + 1, n_steps - 1), 0, 0),
                             memory_space=pltpu.SMEM)
    n_prompt_tiles = n_prompt // TOK_TILE
    n_sample_tiles = n_steps - n_prompt_tiles
    out_specs = _pair_specs(d, n_prompt_tiles)
    out_shape = [jax.ShapeDtypeStruct((n_prompt, d), F32), jax.ShapeDtypeStruct((t - n_prompt, d), F32)]
    if final:
        tail = (tail,)
        tail_specs = [pl.BlockSpec(tail[0].shape, lambda i: (0,) * tail[0].ndim)]
    else:
        tail_specs, io, ish = _inproj_specs(tail, inproj_widths, t, layer + 1)
        out_specs, out_shape = out_specs + io, out_shape + ish
    return pl.pallas_call(
        functools.partial(_ple_kernel, final=final, n_prompt_tiles=n_prompt_tiles),
        grid=(n_steps,),
        in_specs=[tile_smem, next_smem, pl.BlockSpec(memory_space=pl.ANY)]
        + _pair_specs(pp_all.shape[1], n_prompt_tiles, layer * n_prompt_tiles, layer * n_sample_tiles)
        + [_layer_spec(a, layer) for a in params] + tail_specs,
        out_specs=out_specs,
        out_shape=out_shape,
        scratch_shapes=[pltpu.VMEM((2, TOK_TILE * Y_PITCH, LANES), F32), pltpu.SemaphoreType.DMA((2,))],
        compiler_params=_params(1),
        name="ple_final" if final else "ple_inproj",
    )(pos, pos, h2s, pp_all, ps_all, *params, *tail)


_PAIR_LO = (0, 0, 0, 1, 1, 2)
_PAIR_HI = (1, 2, 3, 2, 3, 3)


def _routing_tables(cnt, n_tiles):
    counts = cnt[:N_BUCKETS, 0].astype(jnp.int32)
    padded = ((counts + MOE_TILE - 1) // MOE_TILE) * MOE_TILE
    ends = jnp.cumsum(padded)
    starts = jnp.pad(ends - padded, (0, ROUTE_ROWS - N_BUCKETS))
    n_used = ends[-1] // MOE_TILE
    tile = jnp.minimum(jnp.arange(n_tiles, dtype=jnp.int32), n_used - 1)
    tb = jnp.sum((ends[None, :] <= (tile * MOE_TILE)[:, None]).astype(jnp.int32), axis=1)
    tb = jnp.minimum(tb, N_BUCKETS - 1)
    grp = tb // N_PAIRS
    pr = tb % N_PAIRS
    e_lo = grp * EXPERTS_PER_GROUP + jnp.asarray(_PAIR_LO, jnp.int32)[pr]
    e_hi = grp * EXPERTS_PER_GROUP + jnp.asarray(_PAIR_HI, jnp.int32)[pr]
    return starts, e_lo, e_hi, n_used.reshape(1)


def kernel(x_prompt, x_sample, state_conv, state_ssm, p_prompt, p_sample, norm_mix, w_in, conv_w, conv_b,
           dt_bias, a_log, d_skip, ssd_norm, gmlp_norm, w_spatial, b_spatial, w_out, norm_ffn,
           w_router_group, b_router_group, w_router_expert, b_router_expert, w_gate, w_up, w_down,
           w_ple, norm_ple, norm_pg, w_pg, norm_final):
    n_seq, seq_len, d_model = x_prompt.shape
    n_dec, dec_seq, _ = x_sample.shape
    depth = w_in.shape[0]
    conv_dim = conv_w.shape[2]
    d_ssd = SSD_HEADS * SSD_HEAD_DIM
    d_gmlp = gmlp_norm.shape[1]
    assert dec_seq == DEC_SEQ_ROWS and conv_w.shape[1] == CONV_WIDTH and seq_len % CHUNK == 0
    assert conv_dim == d_ssd + 2 * SSD_GROUPS * D_STATE and w_spatial.shape[1:] == (GMLP_HEADS, CHUNK, CHUNK)
    n_chunk = seq_len // CHUNK
    n_prompt = n_seq * seq_len
    n_srows = n_dec * GROUP
    t_all = n_prompt + n_srows
    assert n_prompt % TOK_TILE == 0 and n_srows % TOK_TILE == 0
    assert d_model == TOKEN_ROWS * LANES
    lead = GROUP - dec_seq
    n_moe_tiles = t_all // MOE_TILE + N_BUCKETS

    def sample_rows(a):
        pad = [(0, 0)] * (a.ndim - 2) + [(lead, 0), (0, 0)]
        return jnp.pad(a, pad).reshape(-1, a.shape[-1])

    hp, hs = x_prompt.reshape(n_prompt, d_model), sample_rows(x_sample)

    vec = lambda a: a.astype(F32).reshape(depth, 1, -1)
    lane_pad = ((0, 0), (0, LANES - SSD_HEADS))
    o_dt = d_ssd + conv_dim
    o_uv = o_dt + SSD_HEADS
    w_in_l = jnp.concatenate([w_in[..., :o_dt], jnp.pad(w_in[..., o_dt:o_uv], ((0, 0),) + lane_pad),
                              w_in[..., o_uv:]], axis=-1).astype(BF16)
    inproj_widths = (d_ssd, conv_dim, LANES, d_gmlp, d_gmlp)
    inproj_params = (vec(norm_mix), w_in_l, jnp.pad(dt_bias.astype(F32), lane_pad).reshape(depth, 1, LANES),
                     vec(gmlp_norm))

    head_cols = jnp.arange(d_ssd) // SSD_HEAD_DIM
    per_head = d_gmlp // GMLP_HEADS
    ws_tril = jnp.where(jnp.tril(jnp.ones((CHUNK, CHUNK), bool)), w_spatial, 0.0)
    bsp_p = jnp.repeat(jnp.swapaxes(b_spatial, 1, 2), per_head, axis=2)
    w8 = jnp.pad(ws_tril[:, :, :dec_seq, :dec_seq], ((0, 0), (0, 0), (lead, 0), (lead, 0)))
    wsp_s = jnp.tile(w8, (1, 1, 1, CHUNK // GROUP))
    b8 =jnp.pad(b_spatial[:, :, :dec_seq], ((0, 0), (0, 0), (lead, 0)))
    bsp_s = jnp.repeat(jnp.swapaxes(jnp.tile(b8, (1, 1, CHUNK // GROUP)), 1, 2), per_head, axis=2)
    mixer_common = (conv_w.astype(F32), vec(conv_b),
                    jnp.pad(-jnp.exp(a_log.astype(F32)), lane_pad).reshape(depth, 1, LANES),
                    d_skip.astype(F32)[:, head_cols].reshape(depth, 1, d_ssd), vec(ssd_norm))
    mixer_p_params = mixer_common + (ws_tril.astype(BF16), bsp_p)
    mixer_s_params = mixer_common + (wsp_s.astype(F32), bsp_s)

    n_route_pad = ROUTE_ROWS - N_EXPERT_GROUPS * (1 + EXPERTS_PER_GROUP)
    wr = jnp.concatenate([jnp.swapaxes(w_router_group, 1, 2), jnp.swapaxes(w_router_expert, 1, 2),
                          jnp.zeros((depth, n_route_pad, d_model), F32)], axis=1).astype(F32)
    br = jnp.concatenate([b_router_group, b_router_expert, jnp.zeros((depth, n_route_pad), F32)],
                         axis=1).astype(F32).reshape(depth, ROUTE_ROWS, 1)
    post_params = (w_out.astype(BF16), vec(norm_ffn), wr, br)
    ple_params = (w_ple.astype(BF16), vec(norm_ple), vec(norm_pg), w_pg.astype(BF16))

    n_exp = w_gate.shape[1]
    wg_all = w_gate.astype(BF16).reshape((depth * n_exp,) + w_gate.shape[2:])
    wu_all = w_up.astype(BF16).reshape((depth * n_exp,) + w_up.shape[2:])
    wd_all = w_down.astype(BF16).reshape((depth * n_exp,) + w_down.shape[2:])
    s0_all = state_ssm.reshape(depth * n_dec, d_ssd, D_STATE)
    tail = CONV_WIDTH - 1
    cs_all = jnp.pad(state_conv, ((0, 0), (0, 0), (lead - tail, dec_seq), (0, 0))).reshape(depth * n_srows, conv_dim)
    pp_all = p_prompt.reshape(depth * n_prompt, p_prompt.shape[-1])
    ps_all = sample_rows(p_sample)

    xs = ssm_s = None
    convs_p, ssms_p, convs_s, vs_s = [], [], [], []
    z, xbc, dt, u, v = _inproj(hp, hs, inproj_params, inproj_widths, 0)

    for i in range(depth):
        yp, ssm_p = _mixer_prompt(z, xbc, dt, u, v, mixer_p_params, n_seq, n_chunk, i)
        ys, ssm_s = _mixer_sample(z, xbc, dt, u, v, cs_all, s0_all, ssm_s, mixer_s_params, n_prompt, n_srows, i)

        slab, bkt, rnk, cnt = _post(yp, ys, hp, hs, post_params, i)
        starts, e_lo, e_hi, n_used = _routing_tables(cnt, n_moe_tiles)

        xs, pos = _scatter_tokens(starts, bkt, rnk, slab, n_moe_tiles * MOE_TILE, X_PITCH, init=xs)
        h2s = _moe(e_lo + i * n_exp, e_hi + i * n_exp, n_used, xs, post_params[1], wg_all, wu_all, wd_all, i)

        convs_p.append(jnp.concatenate([xbc[(b + 1) * seq_len - tail:(b + 1) * seq_len] for b in range(n_seq)])
                       .reshape(n_seq, tail, conv_dim))
        xbc_s = xbc[n_prompt:].reshape(n_dec, GROUP, conv_dim)
        convs_s.append(xbc_s[:, GROUP - tail:])
        ssms_p.append(ssm_p.reshape(n_seq, SSD_HEADS, SSD_HEAD_DIM, D_STATE))
        vs_s.append(v[n_prompt:].reshape(n_dec, GROUP, d_gmlp)[:, lead:])

        final = i == depth - 1
        outs = _ple(pos, h2s, pp_all, ps_all, ple_params,
                    norm_final.astype(F32).reshape(1, -1) if final else inproj_params, inproj_widths, final, i, n_prompt)
        hp, hs = outs[:2]
        if not final:
            z, xbc, dt, u, v = outs[2:]

    y_prompt = hp.reshape(n_seq, seq_len, d_model)
    y_sample = hs.reshape(n_dec, GROUP, d_model)[:, lead:]
    new_ssm_sample = ssm_s.reshape(depth, n_dec, SSD_HEADS, SSD_HEAD_DIM, D_STATE)
    return (y_prompt, y_sample, jnp.stack(convs_p), jnp.stack(ssms_p), jnp.stack(convs_s), new_ssm_sample,
            jnp.stack(vs_s))
```

```python
import functools
import math

import jax
import jax.numpy as jnp
from jax import lax
from jax.experimental import pallas as pl
from jax.experimental.pallas import tpu as pltpu

F32 = jnp.float32
BF16 = jnp.bfloat16

LANES = 128
VMEM_LIMIT_BYTES = 56 * 1024 * 1024

CONV_WIDTH = 4
SSD_HEADS = 8
SSD_HEAD_DIM = 64
SSD_GROUPS = 2
D_STATE = 128
CHUNK = 128
GMLP_HEADS = 8
N_EXPERT_GROUPS = 4
EXPERTS_PER_GROUP = 4
N_PAIRS = 6
N_BUCKETS = N_EXPERT_GROUPS * N_PAIRS
EPS = 1e-6

GROUP = 8
DEC_SEQ_ROWS = 4
TOK_TILE = 512
MOE_TILE = 256
PROMPT_TILES_PER_STEP = 8
ROUTE_ROWS = 32
TOKEN_ROWS = 8
X_PITCH = TOKEN_ROWS + 1
Y_PITCH = TOKEN_ROWS + 1
NEG_BIG = -1e30


def _dot(a, b):
    return jnp.dot(a, b, preferred_element_type=F32)


def _dot_nt(a, b):
    return lax.dot_general(a, b, (((1,), (1,)), ((), ())), preferred_element_type=F32)


def _rms(x, g):
    ms = jnp.mean(x * x, axis=-1, keepdims=True)
    return (x * lax.rsqrt(ms + EPS)) * g


def _gelu(x):
    return 0.5 * x * (1.0 + lax.erf(x * (1.0 / math.sqrt(2.0))))


def _sigmoid(x):
    return 0.5 * jnp.tanh(0.5 * x) + 0.5


def _softplus(x):
    return jnp.maximum(x, 0.0) + jnp.log1p(jnp.exp(-jnp.abs(x)))


def _params(n_grid):
    return pltpu.CompilerParams(dimension_semantics=("arbitrary",) * n_grid,
                                vmem_limit_bytes=VMEM_LIMIT_BYTES)


def _pair_specs(width, n_prompt_tiles, prompt_tile0=0, sample_tile0=0):
    return [pl.BlockSpec((TOK_TILE, width), lambda i: (prompt_tile0 + jnp.minimum(i, n_prompt_tiles - 1), 0)),
            pl.BlockSpec((TOK_TILE, width), lambda i: (sample_tile0 + jnp.maximum(i - n_prompt_tiles, 0), 0))]


def _pair_load(p_ref, s_ref, n_prompt_tiles):
    return jnp.where(pl.program_id(0) < n_prompt_tiles, p_ref[...], s_ref[...])


def _layer_spec(a, layer):
    zeros = (0,) * (a.ndim - 1)
    return pl.BlockSpec((None,) + a.shape[1:], lambda *_: (layer,) + zeros)


def _inproj_tile(h, nm_ref, wz_ref, wx_ref, wdt_ref, wu_ref, wv_ref, dtb_ref, gn_ref,
                 z_ref, xbc_ref, dt_ref, u_ref, v_ref):
    a = _rms(h, nm_ref[...]).astype(BF16)
    z_ref[...] = _dot(a, wz_ref[...])
    xbc_ref[...] = _dot(a, wx_ref[...])
    dt_ref[...] = _softplus(_dot(a, wdt_ref[...]) + dtb_ref[...])
    u_ref[...] = _gelu(_dot(a, wu_ref[...]))
    v_ref[...] = _rms(_gelu(_dot(a, wv_ref[...])), gn_ref[...])


def _inproj_kernel(hp_ref, hs_ref, *refs, n_prompt_tiles):
    _inproj_tile(_pair_load(hp_ref, hs_ref, n_prompt_tiles), *refs)


N_INPROJ_PARAMS = 8


def _inproj_specs(weights, t, layer):
    assert len(weights) == N_INPROJ_PARAMS
    widths = [w.shape[-1] for w in weights[1:6]]
    row = lambda w: pl.BlockSpec((TOK_TILE, w), lambda i: (i, 0))
    return ([_layer_spec(a, layer) for a in weights], [row(w) for w in widths],
            [jax.ShapeDtypeStruct((t, w), F32) for w in widths])


def _inproj(hp, hs, weights, layer):
    t = hp.shape[0] + hs.shape[0]
    n_prompt_tiles = hp.shape[0] // TOK_TILE
    w_specs, out_specs, out_shape = _inproj_specs(weights, t, layer)
    return pl.pallas_call(
        functools.partial(_inproj_kernel, n_prompt_tiles=n_prompt_tiles),
        grid=(t // TOK_TILE,),
        in_specs=_pair_specs(hp.shape[1], n_prompt_tiles) + w_specs,
        out_specs=out_specs,
        out_shape=out_shape,
        compiler_params=_params(1),
        name="inproj",
    )(hp, hs, *weights)


def _seg_cumsum(x, seg, rowmod):
    d = 1
    while d < seg:
        x = x + jnp.where(rowmod >= d, pltpu.roll(x, d, axis=0), 0.0)
        d *= 2
    return x


def _seg_rev_cumsum(x, seg, rowmod):
    n = x.shape[0]
    d = 1
    while d < seg:
        x = x + jnp.where(rowmod + d < seg, pltpu.roll(x, n - d, axis=0), 0.0)
        d *= 2
    return x


def _expand_heads(m, lane_lt_half):
    parts = []
    for j in range(SSD_HEADS // 2):
        parts.append(jnp.where(lane_lt_half, m[:, 2 * j:2 * j + 1], m[:, 2 * j + 1:2 * j + 2]))
    return jnp.concatenate(parts, axis=1)


def _merge_head_pairs(per_head, lane_lt_half):
    parts = [jnp.where(lane_lt_half, per_head[2 * j], per_head[2 * j + 1]) for j in range(len(per_head) // 2)]
    return jnp.concatenate(parts, axis=1)


def _mixer_kernel(*refs, sample, n_inner, n_prev=0):
    n_tok = 6 if sample else 5
    tok_refs, rest = refs[:n_tok], refs[n_tok:]
    if sample:
        if n_prev:
            prev_ref, rest = rest[0], rest[1:]
        (s0_ref, cw_ref, cb_ref, arow_ref, dsk_ref, sn_ref, wsp_ref, bsp_ref,
         y_ref, stack_ref, ext_ref, yoff_ref) = rest
        if n_prev:
            stack_ref[0:n_prev] = prev_ref[...]
        sout_ref = stack_ref.at[n_prev]
        first = pl.program_id(0) == 0
    else:
        (cw_ref, cb_ref, arow_ref, dsk_ref, sn_ref, wsp_ref, bsp_ref, y_ref, sout_ref, ext_ref) = rest
        s0_ref = yoff_ref = None
        first = pl.program_id(1) == 0

    @pl.when(first)
    def _():
        ext_ref[...] = jnp.zeros(ext_ref.shape, F32)
        if not sample:
            sout_ref[...] = jnp.zeros(sout_ref.shape, F32)

    def tile(c, carry):
        r0 = pl.multiple_of(c * CHUNK, CHUNK)
        views = [r.at[pl.ds(r0, CHUNK)] for r in tok_refs + (y_ref,)]
        _mixer_tile(*views[:n_tok], s0_ref, cw_ref, cb_ref, arow_ref, dsk_ref, sn_ref, wsp_ref, bsp_ref,
                    views[n_tok], sout_ref, ext_ref, yoff_ref, sample=sample)
        return carry

    if n_inner == 1:
        tile(0, 0)
    else:
        lax.fori_loop(0, n_inner, tile, 0)


def _mixer_tile(z_ref, xbc_ref, dt_ref, u_ref, v_ref, *rest, sample):
    if sample:
        cs_ref, s0_ref = rest[0], rest[1]
        rest = rest[2:]
    else:
        rest = rest[1:]
    cw_ref, cb_ref, arow_ref, dsk_ref, sn_ref, wsp_ref, bsp_ref, y_ref, sout_ref, ext_ref, yoff_ref = rest
    seg = GROUP if sample else CHUNK
    d_ssd = SSD_HEADS * SSD_HEAD_DIM
    gw = d_ssd // SSD_GROUPS
    hpg = SSD_HEADS // SSD_GROUPS
    cs_first = GROUP - DEC_SEQ_ROWS - (CONV_WIDTH - 1)

    rows = lax.broadcasted_iota(jnp.int32, (CHUNK, LANES), 0)
    cols = lax.broadcasted_iota(jnp.int32, (CHUNK, LANES), 1)
    rowmod = rows & (seg - 1)
    lane_lt_half = cols < SSD_HEAD_DIM

    xbc = xbc_ref[...]
    if sample:
        rm = lax.broadcasted_iota(jnp.int32, xbc.shape, 0) & (GROUP - 1)
        xbc = jnp.where((rm >= cs_first) & (rm < cs_first + CONV_WIDTH - 1), cs_ref[...], xbc)

    tail = ext_ref[...]
    row8 = lax.broadcasted_iota(jnp.int32, tail.shape, 0)
    acc = cb_ref[...] + cw_ref[CONV_WIDTH - 1:CONV_WIDTH, :] * xbc
    for j in range(1, CONV_WIDTH):
        rolled = pltpu.roll(xbc, j, axis=0)
        head = jnp.where(row8 < j, pltpu.roll(tail, j, axis=0), rolled[0:8, :])
        shifted = jnp.concatenate([head, rolled[8:, :]], axis=0)
        acc = acc + cw_ref[CONV_WIDTH - 1 - j:CONV_WIDTH - j, :] * shifted
    if not sample:
        ext_ref[...] = xbc[CHUNK - 8:, :]
    xc = acc * _sigmoid(acc)
    x = xc[:, :d_ssd]
    bb = xc[:, d_ssd:d_ssd + SSD_GROUPS * D_STATE].astype(BF16)
    cm = xc[:, d_ssd + SSD_GROUPS * D_STATE:]
    cbf = cm.astype(BF16)

    dtc = dt_ref[...]
    if sample:
        dtc = jnp.where(rowmod >= GROUP - DEC_SEQ_ROWS, dtc, 0.0)
    da = dtc * arow_ref[...]
    cum = _seg_cumsum(da, seg, rowmod)
    rev = _seg_rev_cumsum(da, seg, rowmod) - da
    cum_t = cum.T
    ecum = jnp.exp(cum)
    dt_e = _expand_heads(dtc, lane_lt_half)
    ecum_e = _expand_heads(ecum, lane_lt_half)
    erev_e = _expand_heads(jnp.exp(rev), lane_lt_half)

    xdt = x * dt_e
    xdt_bf = xdt.astype(BF16)
    causal = rows >= cols
    if sample:
        same_seq = (rows >> 3) == (cols >> 3)
        causal = causal & same_seq

    yd = []
    for g in range(SSD_GROUPS):
        cb_g = _dot_nt(cbf[:, g * D_STATE:(g + 1) * D_STATE], bb[:, g * D_STATE:(g + 1) * D_STATE])
        for hh in range(hpg):
            h = g * hpg + hh
            expo = cum[:, h:h + 1] - cum_t[h:h + 1, :]
            w = (cb_g * jnp.exp(jnp.where(causal, expo, NEG_BIG))).astype(BF16)
            j = h // 2
            yd.append(_dot(w, xdt_bf[:, j * LANES:(j + 1) * LANES]))
    y_diag = _merge_head_pairs(yd, lane_lt_half)

    if sample:
        for i in range(CHUNK // GROUP):
            for g in range(SSD_GROUPS):
                s_g = s0_ref[i, g * gw:(g + 1) * gw, :].astype(BF16)
                yoff_ref[i * GROUP:(i + 1) * GROUP, g * gw:(g + 1) * gw] = _dot_nt(
                    cm[i * GROUP:(i + 1) * GROUP, g * D_STATE:(g + 1) * D_STATE].astype(BF16), s_g)
        y_off = yoff_ref[...]
    else:
        y_off = jnp.concatenate(
            [_dot_nt(cbf[:, g * D_STATE:(g + 1) * D_STATE], sout_ref[0, g * gw:(g + 1) * gw, :].astype(BF16))
             for g in range(SSD_GROUPS)], axis=1)
    y = y_diag + y_off * ecum_e + dsk_ref[...] * x

    xd = xdt * erev_e
    for g in range(SSD_GROUPS):
        xd_t = xd[:, g * gw:(g + 1) * gw].T
        b_g = bb[:, g * D_STATE:(g + 1) * D_STATE]
        if sample:
            tcols = lax.broadcasted_iota(jnp.int32, xd_t.shape, 1) >> 3
            for i in range(CHUNK // GROUP):
                upd = _dot(jnp.where(tcols == i, xd_t, 0.0).astype(BF16), b_g)
                last = i * GROUP + GROUP - 1
                for hh in range(hpg):
                    h = g * hpg + hh
                    r0 = h * SSD_HEAD_DIM
                    sout_ref[i, r0:r0 + SSD_HEAD_DIM, :] = (
                        s0_ref[i, r0:r0 + SSD_HEAD_DIM, :] * ecum[last:last + 1, h:h + 1]
                        + upd[hh * SSD_HEAD_DIM:(hh + 1) * SSD_HEAD_DIM, :])
        else:
            upd = _dot(xd_t.astype(BF16), b_g)
            for hh in range(hpg):
                h = g * hpg + hh
                r0 = h * SSD_HEAD_DIM
                sout_ref[0, r0:r0 + SSD_HEAD_DIM, :] = (
                    sout_ref[0, r0:r0 + SSD_HEAD_DIM, :] * ecum[CHUNK - 1:CHUNK, h:h + 1]
                    + upd[hh * SSD_HEAD_DIM:(hh + 1) * SSD_HEAD_DIM, :])

    zf = z_ref[...]
    yf = y * (zf * _sigmoid(zf))
    parts = []
    for g in range(SSD_GROUPS):
        part = yf[:, g * gw:(g + 1) * gw]
        ms = jnp.mean(part * part, axis=-1, keepdims=True)
        parts.append(part * lax.rsqrt(ms + EPS))
    y_ssd = jnp.concatenate(parts, axis=1) * sn_ref[...]

    vb = v_ref[...].astype(BF16)
    if sample:
        n_grp = CHUNK // GROUP
        wms = [jnp.where(same_seq, jnp.tile(wsp_ref[h], (n_grp, 1)), 0.0).astype(BF16) for h in range(GMLP_HEADS)]
    else:
        wms = [wsp_ref[h] for h in range(GMLP_HEADS)]
    sg = [_dot(wms[h], vb[:, (h // 2) * LANES:(h // 2 + 1) * LANES]) for h in range(GMLP_HEADS)]
    s = _merge_head_pairs(sg, lane_lt_half) + bsp_ref[...]
    y_gm = u_ref[...] * s

    y_ref[:, :d_ssd] = y_ssd.astype(BF16)
    y_ref[:, d_ssd:] = y_gm.astype(BF16)


def _mixer_prompt(z, xbc, dt, u, v, params, n_seq, n_chunk, layer):
    n_inner = math.gcd(n_chunk, PROMPT_TILES_PER_STEP)
    n_outer = n_chunk // n_inner
    tok = lambda w: pl.BlockSpec((n_inner * CHUNK, w), lambda b, c: (b * n_outer + c, 0))
    d_ssd = z.shape[1]
    d_mix = d_ssd + u.shape[1]
    return pl.pallas_call(
        functools.partial(_mixer_kernel, sample=False, n_inner=n_inner),
        grid=(n_seq, n_outer),
        in_specs=[tok(z.shape[1]), tok(xbc.shape[1]), tok(dt.shape[1]), tok(u.shape[1]), tok(v.shape[1])]
        + [_layer_spec(a, layer) for a in params],
        out_specs=[tok(d_mix),
                   pl.BlockSpec((1, d_ssd, D_STATE), lambda b, c: (b, 0, 0))],
        out_shape=[jax.ShapeDtypeStruct((n_seq * n_chunk * CHUNK, d_mix), BF16),
                   jax.ShapeDtypeStruct((n_seq, d_ssd, D_STATE), F32)],
        scratch_shapes=[pltpu.VMEM((8, xbc.shape[1]), F32)],
        compiler_params=_params(2),
        name="mixer_prompt",
    )(z, xbc, dt, u, v, *params)


def _mixer_sample(z, xbc, dt, u, v, cs_all, s0_all, prev, params, row0, n_rows, layer):
    blk0 = row0 // CHUNK
    n_seq_blk = CHUNK // GROUP
    n_blk = n_rows // CHUNK
    tok = lambda w: pl.BlockSpec((CHUNK, w), lambda i: (blk0 + i, 0))
    d_ssd = z.shape[1]
    d_mix = d_ssd + u.shape[1]
    st = (n_seq_blk, d_ssd, D_STATE)
    stack = lambda n: pl.BlockSpec((n,) + st, lambda i: (0, i, 0, 0))
    prev_args = [] if layer == 0 else [prev]
    return pl.pallas_call(
        functools.partial(_mixer_kernel, sample=True, n_inner=1, n_prev=layer),
        grid=(n_blk,),
        in_specs=[tok(z.shape[1]), tok(xbc.shape[1]), tok(dt.shape[1]), tok(u.shape[1]), tok(v.shape[1]),
                  pl.BlockSpec((CHUNK, cs_all.shape[1]), lambda i: (layer * n_blk + i, 0))]
        + [stack(layer) for _ in prev_args]
        + [pl.BlockSpec(st, lambda i: (layer * n_blk + i, 0, 0))]
        + [_layer_spec(a, layer) for a in params],
        out_specs=[pl.BlockSpec((CHUNK, d_mix), lambda i: (i, 0)), stack(layer + 1)],
        out_shape=[jax.ShapeDtypeStruct((n_rows, d_mix), BF16),
                   jax.ShapeDtypeStruct((layer + 1, n_blk * n_seq_blk, d_ssd, D_STATE), F32)],
        scratch_shapes=[pltpu.VMEM((8, xbc.shape[1]), F32), pltpu.VMEM((CHUNK, d_ssd), F32)],
        compiler_params=_params(1),
        name="mixer_sample",
    )(z, xbc, dt, u, v, cs_all, *prev_args, s0_all, *params)


def _first_argmax(vals):
    m = vals[0]
    for v in vals[1:]:
        m = jnp.maximum(m, v)
    idx = jnp.full(m.shape, len(vals) - 1, jnp.int32)
    for k in range(len(vals) - 2, -1, -1):
        idx = jnp.where(vals[k] >= m, k, idx)
    return m, idx


def _slab_store(slab_ref, x, pitch, tok0=0):
    for k in range(x.shape[1] // LANES):
        slab_ref[pl.ds(tok0 * pitch + k, x.shape[0], stride=pitch), :] = x[:, k * LANES:(k + 1) * LANES]


def _slab_load(slab_ref, rows, n_pieces, pitch, tok0=0):
    return jnp.concatenate([slab_ref[pl.ds(tok0 * pitch + k, rows, stride=pitch), :] for k in range(n_pieces)],
                           axis=1)


def _post_kernel(yp_ref, ys_ref, hp_ref, hs_ref, wo_ref, nf_ref, wr_ref, br_ref,
                 slab_ref, bkt_ref, rnk_ref, cnt_ref, carry_ref, earlier_ref, *, n_prompt_tiles):
    i = pl.program_id(0)
    tm, d = hp_ref.shape

    @pl.when(i == 0)
    def _():
        carry_ref[...] = jnp.zeros(carry_ref.shape, F32)
        tr = lax.broadcasted_iota(jnp.int32, (tm, tm), 0)
        tc = lax.broadcasted_iota(jnp.int32, (tm, tm), 1)
        earlier_ref[...] = jnp.where(tr < tc, 1.0, 0.0).astype(BF16)

    ym = _pair_load(yp_ref, ys_ref, n_prompt_tiles)
    h1 = _pair_load(hp_ref, hs_ref, n_prompt_tiles) + _dot(ym, wo_ref[...])
    _slab_store(slab_ref, h1, X_PITCH)

    t = _rms(h1, nf_ref[...])
    wr = wr_ref[...]
    wr_hi = wr.astype(BF16)
    wr_lo = (wr - wr_hi.astype(F32)).astype(BF16)
    t_hi = t.astype(BF16)
    t_lo = (t - t_hi.astype(F32)).astype(BF16)
    by_hi = _dot_nt(jnp.concatenate([wr_hi, wr_lo], axis=0), t_hi)
    logits = (by_hi[:ROUTE_ROWS] + (by_hi[ROUTE_ROWS:] + _dot_nt(wr_hi, t_lo))) + br_ref[...]
    lg = [logits[k:k + 1, :] for k in range(N_EXPERT_GROUPS)]
    m, g = _first_argmax(lg)
    ssum = jnp.exp(lg[0] - m)
    for k in range(1, N_EXPERT_GROUPS):
        ssum = ssum + jnp.exp(lg[k] - m)
    p_sel = 1.0 / ssum
    le = [logits[N_EXPERT_GROUPS + e:N_EXPERT_GROUPS + e + 1, :] for e in range(N_EXPERT_GROUPS * EXPERTS_PER_GROUP)]
    a = []
    for k in range(EXPERTS_PER_GROUP):
        sel = le[(N_EXPERT_GROUPS - 1) * EXPERTS_PER_GROUP + k]
        for gi in range(N_EXPERT_GROUPS - 2, -1, -1):
            sel = jnp.where(g == gi, le[gi * EXPERTS_PER_GROUP + k], sel)
        a.append(sel)
    v1, i1 = _first_argmax(a)
    a2 = [jnp.where(i1 == k, -jnp.inf, a[k]) for k in range(EXPERTS_PER_GROUP)]
    v2, i2 = _first_argmax(a2)
    e2 = jnp.exp(v2 - v1)
    den = 1.0 + e2
    g1 = (1.0 / den) * p_sel
    g2 = (e2 / den) * p_sel
    lo = jnp.minimum(i1, i2)
    hi = jnp.maximum(i1, i2)
    pair = jnp.where(lo == 0, jnp.where(hi == 1, 0, hi), jnp.where(lo == 1, jnp.where(hi == 2, 1, 4), 5))
    first = jnp.where(pair == 0, 0, jnp.where(pair <= 2, 2, 3))
    c_lo = jnp.where(first == i1, g1, g2)
    c_hi = jnp.where(first == i1, g2, g1)
    bucket = g * N_PAIRS + pair

    brow = lax.broadcasted_iota(jnp.int32, (ROUTE_ROWS, tm), 0)
    onehot = jnp.where(brow == bucket, 1.0, 0.0)
    prefix = _dot(onehot.astype(BF16), earlier_ref[...])
    carry = carry_ref[:, 0:1]
    rank = jnp.sum(onehot * (prefix + carry), axis=0, keepdims=True)
    carry = carry + jnp.sum(onehot, axis=1, keepdims=True)
    carry_b = jnp.broadcast_to(carry, carry_ref.shape)
    carry_ref[...] = carry_b
    cnt_ref[...] = carry_b

    bkt_ref[0] = bucket
    rnk_ref[0] = rank.astype(jnp.int32)
    ar = lax.broadcasted_iota(jnp.int32, (LANES, tm), 0)
    aux = jnp.where(ar == 0, c_lo, jnp.where(ar == 1, c_hi, 0.0))
    slab_ref[pl.ds(d // LANES, tm, stride=X_PITCH), :] = aux.T


def _post(yp, ys, hp, hs, params, layer):
    d = hp.shape[1]
    t = hp.shape[0] + hs.shape[0]
    n_prompt_tiles = hp.shape[0] // TOK_TILE
    return pl.pallas_call(
        functools.partial(_post_kernel, n_prompt_tiles=n_prompt_tiles),
        grid=(t // TOK_TILE,),
        in_specs=_pair_specs(d, n_prompt_tiles) + _pair_specs(d, n_prompt_tiles)
        + [_layer_spec(a, layer) for a in params],
        out_specs=[pl.BlockSpec((TOK_TILE * X_PITCH, LANES), lambda i: (i, 0)),
                   pl.BlockSpec((1, 1, TOK_TILE), lambda i: (i, 0, 0)),
                   pl.BlockSpec((1, 1, TOK_TILE), lambda i: (i, 0, 0)),
                   pl.BlockSpec((ROUTE_ROWS, LANES), lambda i: (0, 0))],
        out_shape=[jax.ShapeDtypeStruct((t * X_PITCH, LANES), F32),
                   jax.ShapeDtypeStruct((t // TOK_TILE, 1, TOK_TILE), jnp.int32),
                   jax.ShapeDtypeStruct((t // TOK_TILE, 1, TOK_TILE), jnp.int32),
                   jax.ShapeDtypeStruct((ROUTE_ROWS, LANES), F32)],
        scratch_shapes=[pltpu.VMEM((ROUTE_ROWS, LANES), F32), pltpu.VMEM((TOK_TILE, TOK_TILE), BF16)],
        compiler_params=_params(1),
        name="post",
    )(yp, ys, hp, hs, *params)


def _scatter_kernel(starts_ref, bkt_ref, rnk_ref, src_ref, _, dst_ref, pos_ref, sem, *, pitch):
    n_tok = src_ref.shape[0] // pitch
    for r in range(n_tok):
        slot = starts_ref[bkt_ref[0, 0, r]] + rnk_ref[0, 0, r]
        pos_ref[0, 0, r] = slot
        pltpu.make_async_copy(src_ref.at[pl.ds(r * pitch, pitch)],
                              dst_ref.at[pl.ds(slot * pitch, pitch)], sem).start(priority=r % 2)
    pltpu.make_async_copy(src_ref, dst_ref.at[pl.ds(0, n_tok * pitch)], sem).wait()


def _scatter_tokens(starts, bkt, rnk, src, n_out, pitch, init=None):
    hbm = pl.BlockSpec(memory_space=pl.ANY)
    tile_smem = pl.BlockSpec((1, 1, TOK_TILE), lambda i: (i, 0, 0), memory_space=pltpu.SMEM)
    if init is None:
        init = jnp.zeros((n_out * pitch, LANES), src.dtype)
    return pl.pallas_call(
        functools.partial(_scatter_kernel, pitch=pitch),
        grid=(bkt.shape[0],),
        in_specs=[pl.BlockSpec(memory_space=pltpu.SMEM), tile_smem, tile_smem,
                  pl.BlockSpec((TOK_TILE * pitch, LANES), lambda i: (i, 0)), hbm],
        out_specs=[hbm, tile_smem],
        out_shape=[jax.ShapeDtypeStruct((n_out * pitch, LANES), src.dtype),
                   jax.ShapeDtypeStruct(bkt.shape, jnp.int32)],
        scratch_shapes=[pltpu.SemaphoreType.DMA(())],
        input_output_aliases={4: 0},
        compiler_params=_params(1),
        name="scatter_tokens",
    )(starts, bkt, rnk, src, init)


def _moe_kernel(elo_ref, ehi_ref, nused_ref, new_lo_ref, new_hi_ref, x_ref, nf_ref, *refs):
    f32_lo, f32_hi, o_ref, (wg_lo, wu_lo, wd_lo), (wg_hi, wu_hi, wd_hi) = (
        refs[0:3], refs[3:6], refs[6], refs[7:10], refs[10:13])
    i = pl.program_id(0)
    n_pieces = nf_ref.shape[1] // LANES

    @pl.when(i < nused_ref[0])
    def _():
        for new_ref, srcs, dsts in ((new_lo_ref, f32_lo, (wg_lo, wu_lo, wd_lo)),
                                    (new_hi_ref, f32_hi, (wg_hi, wu_hi, wd_hi))):
            @pl.when(new_ref[i] == 1)
            def _():
                for src, dst in zip(srcs, dsts):
                    dst[...] = src[...].astype(BF16)

        x = _slab_load(x_ref, MOE_TILE, n_pieces, X_PITCH)
        gates = x_ref[pl.ds(n_pieces, MOE_TILE, stride=X_PITCH), :]
        c_lo = gates[:, 0:1]
        c_hi = gates[:, 1:2]
        t = _rms(x, nf_ref[...]).astype(BF16)

        def expert(wg, wu, wd):
            gate = _dot(t, wg[...])
            hid = (gate * _sigmoid(gate)) * _dot(t, wu[...])
            return _dot(hid.astype(BF16), wd[...])

        y = c_lo * expert(wg_lo, wu_lo, wd_lo)
        y = y + c_hi * expert(wg_hi, wu_hi, wd_hi)
        _slab_store(o_ref, x + y, Y_PITCH)
        o_ref[pl.ds(TOKEN_ROWS, MOE_TILE, stride=Y_PITCH), :] = jnp.zeros((MOE_TILE, LANES), F32)

    @pl.when(i >= nused_ref[0])
    def _():
        o_ref[...] = jnp.zeros(o_ref.shape, F32)


def _moe(tables, xs, nf, wg, wu, wd, layer):
    n_tiles = xs.shape[0] // (MOE_TILE * X_PITCH)
    d, de = wg.shape[1:]
    lo = lambda i, elo, ehi, nu, nlo, nhi: (elo[i], 0, 0)
    hi = lambda i, elo, ehi, nu, nlo, nhi: (ehi[i], 0, 0)
    wspec = lambda shape, index_map: pl.BlockSpec((None,) + shape, index_map)
    grid_spec = pltpu.PrefetchScalarGridSpec(
        num_scalar_prefetch=len(tables),
        grid=(n_tiles,),
        in_specs=[pl.BlockSpec((MOE_TILE * X_PITCH, LANES),
                               lambda i, elo, ehi, nu, nlo, nhi: (jnp.maximum(jnp.minimum(i, nu[0] - 1), 0), 0)),
                  _layer_spec(nf, layer),
                  wspec((d, de), lo), wspec((d, de), lo), wspec((de, d), lo),
                  wspec((d, de), hi), wspec((d, de), hi), wspec((de, d), hi)],
        out_specs=pl.BlockSpec((MOE_TILE * Y_PITCH, LANES), lambda i, elo, ehi, nu, nlo, nhi: (i, 0)),
        scratch_shapes=[pltpu.VMEM(s, BF16) for s in ((d, de), (d, de), (de, d)) * 2],
    )
    return pl.pallas_call(
        _moe_kernel,
        grid_spec=grid_spec,
        out_shape=jax.ShapeDtypeStruct((n_tiles * MOE_TILE * Y_PITCH, LANES), F32),
        compiler_params=_params(1),
        name="expert_pairs",
    )(*tables, xs, nf, wg, wu, wd, wg, wu, wd)


def _ple_kernel(pos_ref, pos_next_ref, h_hbm, pp_ref, ps_ref, wple_ref, nple_ref, npg_ref, wpg_ref, *refs,
                final, n_prompt_tiles):
    refs, (xbuf, sem) = refs[:-2], refs[-2:]
    i = pl.program_id(0)
    last = pl.num_programs(0) - 1
    slot = i % 2

    def start_tile(p_ref, s):
        for r in range(TOK_TILE):
            pltpu.make_async_copy(h_hbm.at[pl.ds(p_ref[0, 0, r] * Y_PITCH, Y_PITCH)],
                                  xbuf.at[s, pl.ds(r * Y_PITCH, Y_PITCH)], sem.at[s]).start()

    def wait_tile(s):
        pltpu.make_async_copy(h_hbm.at[pl.ds(0, TOK_TILE * Y_PITCH)], xbuf.at[s], sem.at[s]).wait()

    @pl.when(i == 0)
    def _():
        start_tile(pos_ref, 0)

    wait_tile(slot)
    start_tile(pos_next_ref, 1 - slot)
    h2 = _slab_load(xbuf.at[slot], TOK_TILE, TOKEN_ROWS, Y_PITCH)
    p = _pair_load(pp_ref, ps_ref, n_prompt_tiles)
    e = _rms(_dot(p.astype(BF16), wple_ref[...]), nple_ref[...])
    gate = _sigmoid(_dot(_rms(h2, npg_ref[...]).astype(BF16), wpg_ref[...]))
    h3 = h2 + gate * e
    if final:
        nfin_ref, op_ref, os_ref = refs
        out = _rms(h3, nfin_ref[...])
    else:
        op_ref, os_ref = refs[N_INPROJ_PARAMS:N_INPROJ_PARAMS + 2]
        out = h3
        _inproj_tile(h3, *refs[:N_INPROJ_PARAMS], *refs[N_INPROJ_PARAMS + 2:])

    @pl.when(i < n_prompt_tiles)
    def _():
        op_ref[...] = out

    @pl.when(i >= n_prompt_tiles)
    def _():
        os_ref[...] = out

    @pl.when(i == last)
    def _():
        wait_tile(1 - slot)


def _ple(pos, h2s, pp_all, ps_all, params, tail, final, layer, n_prompt):
    n_steps = pos.shape[0]
    t = n_steps * TOK_TILE
    d = TOKEN_ROWS * LANES
    tile_smem = pl.BlockSpec((1, 1, TOK_TILE), lambda i: (i, 0, 0), memory_space=pltpu.SMEM)
    next_smem = pl.BlockSpec((1, 1, TOK_TILE), lambda i: (jnp.minimum(i + 1, n_steps - 1), 0, 0),
                             memory_space=pltpu.SMEM)
    n_prompt_tiles = n_prompt // TOK_TILE
    n_sample_tiles = n_steps - n_prompt_tiles
    out_specs = _pair_specs(d, n_prompt_tiles)
    out_shape = [jax.ShapeDtypeStruct((n_prompt, d), F32), jax.ShapeDtypeStruct((t - n_prompt, d), F32)]
    if final:
        tail = (tail,)
        tail_specs = [pl.BlockSpec(tail[0].shape, lambda i: (0,) * tail[0].ndim)]
    else:
        tail_specs, io, ish = _inproj_specs(tail, t, layer + 1)
        out_specs, out_shape = out_specs + io, out_shape + ish
    return pl.pallas_call(
        functools.partial(_ple_kernel, final=final, n_prompt_tiles=n_prompt_tiles),
        grid=(n_steps,),
        in_specs=[tile_smem, next_smem, pl.BlockSpec(memory_space=pl.ANY)]
        + _pair_specs(pp_all.shape[1], n_prompt_tiles, layer * n_prompt_tiles, layer * n_sample_tiles)
        + [_layer_spec(a, layer) for a in params] + tail_specs,
        out_specs=out_specs,
        out_shape=out_shape,
        scratch_shapes=[pltpu.VMEM((2, TOK_TILE * Y_PITCH, LANES), F32), pltpu.SemaphoreType.DMA((2,))],
        compiler_params=_params(1),
        name="ple_final" if final else "ple_inproj",
    )(pos, pos, h2s, pp_all, ps_all, *params, *tail)


_PAIR_FIRST = (0, 2, 2, 3, 3, 3)
_PAIR_SECOND = (1, 1, 0, 0, 1, 2)


def _routing_tables(cnt, n_tiles, expert0):
    counts = cnt[:N_BUCKETS, 0].astype(jnp.int32)
    padded = ((counts + MOE_TILE - 1) // MOE_TILE) * MOE_TILE
    ends = jnp.cumsum(padded)
    starts = jnp.pad(ends - padded, (0, ROUTE_ROWS - N_BUCKETS))
    n_used = ends[-1] // MOE_TILE
    tile = jnp.minimum(jnp.arange(n_tiles, dtype=jnp.int32), n_used - 1)
    tb = jnp.sum((ends[None, :] <= (tile * MOE_TILE)[:, None]).astype(jnp.int32), axis=1)
    tb = jnp.minimum(tb, N_BUCKETS - 1)
    grp = tb // N_PAIRS
    pr = tb % N_PAIRS
    e_lo = expert0 + grp * EXPERTS_PER_GROUP + jnp.asarray(_PAIR_FIRST, jnp.int32)[pr]
    e_hi = expert0 + grp * EXPERTS_PER_GROUP + jnp.asarray(_PAIR_SECOND, jnp.int32)[pr]
    is_new = lambda e: jnp.concatenate([jnp.ones((1,), jnp.int32), (e[1:] != e[:-1]).astype(jnp.int32)])
    return starts, (e_lo, e_hi, n_used.reshape(1), is_new(e_lo), is_new(e_hi))


def kernel(x_prompt, x_sample, state_conv, state_ssm, p_prompt, p_sample, norm_mix, w_in, conv_w, conv_b,
           dt_bias, a_log, d_skip, ssd_norm, gmlp_norm, w_spatial, b_spatial, w_out, norm_ffn,
           w_router_group, b_router_group, w_router_expert, b_router_expert, w_gate, w_up, w_down,
           w_ple, norm_ple, norm_pg, w_pg, norm_final):
    n_seq, seq_len, d_model = x_prompt.shape
    n_dec, dec_seq, _ = x_sample.shape
    depth = w_in.shape[0]
    conv_dim = conv_w.shape[2]
    d_ssd = SSD_HEADS * SSD_HEAD_DIM
    d_gmlp = gmlp_norm.shape[1]
    assert dec_seq == DEC_SEQ_ROWS and conv_w.shape[1] == CONV_WIDTH and seq_len % CHUNK == 0
    assert conv_dim == d_ssd + 2 * SSD_GROUPS * D_STATE and w_spatial.shape[1:] == (GMLP_HEADS, CHUNK, CHUNK)
    n_chunk = seq_len // CHUNK
    n_prompt = n_seq * seq_len
    n_srows = n_dec * GROUP
    t_all = n_prompt + n_srows
    assert n_prompt % TOK_TILE == 0 and n_srows % TOK_TILE == 0
    assert d_model == TOKEN_ROWS * LANES
    lead = GROUP - dec_seq
    n_moe_tiles = t_all // MOE_TILE + N_BUCKETS

    def sample_rows(a):
        pad = [(0, 0)] * (a.ndim - 2) + [(lead, 0), (0, 0)]
        return jnp.pad(a, pad).reshape(-1, a.shape[-1])

    hp, hs = x_prompt.reshape(n_prompt, d_model), sample_rows(x_sample)

    vec = lambda a: a.astype(F32).reshape(depth, 1, -1)
    lane_pad = ((0, 0), (0, LANES - SSD_HEADS))
    o_dt = d_ssd + conv_dim
    o_uv = o_dt + SSD_HEADS
    w_cols = lambda a, b: w_in[..., a:b].astype(BF16)
    inproj_params = (vec(norm_mix), w_cols(0, d_ssd), w_cols(d_ssd, o_dt),
                     jnp.pad(w_in[..., o_dt:o_uv], ((0, 0),) + lane_pad).astype(BF16),
                     w_cols(o_uv, o_uv + d_gmlp), w_cols(o_uv + d_gmlp, o_uv + 2 * d_gmlp),
                     jnp.pad(dt_bias.astype(F32), lane_pad).reshape(depth, 1, LANES), vec(gmlp_norm))

    head_cols = jnp.arange(d_ssd) // SSD_HEAD_DIM
    per_head = d_gmlp // GMLP_HEADS
    ws_tril = jnp.where(jnp.tril(jnp.ones((CHUNK, CHUNK), bool)), w_spatial, 0.0)
    bsp_p = jnp.repeat(jnp.swapaxes(b_spatial, 1, 2), per_head, axis=2)
    w8 = jnp.pad(ws_tril[:, :, :dec_seq, :dec_seq], ((0, 0), (0, 0), (lead, 0), (lead, 0)))
    wsp_s = jnp.tile(w8, (1, 1, 1, CHUNK // GROUP))
    b8 =jnp.pad(b_spatial[:, :, :dec_seq], ((0, 0), (0, 0), (lead, 0)))
    bsp_s = jnp.repeat(jnp.swapaxes(jnp.tile(b8, (1, 1, CHUNK // GROUP)), 1, 2), per_head, axis=2)
    mixer_common = (conv_w.astype(F32), vec(conv_b),
                    jnp.pad(-jnp.exp(a_log.astype(F32)), lane_pad).reshape(depth, 1, LANES),
                    d_skip.astype(F32)[:, head_cols].reshape(depth, 1, d_ssd), vec(ssd_norm))
    mixer_p_params = mixer_common + (ws_tril.astype(BF16), bsp_p)
    mixer_s_params = mixer_common + (wsp_s.astype(F32), bsp_s)

    n_route_pad = ROUTE_ROWS - N_EXPERT_GROUPS * (1 + EXPERTS_PER_GROUP)
    wr = jnp.concatenate([jnp.swapaxes(w_router_group, 1, 2), jnp.swapaxes(w_router_expert, 1, 2),
                          jnp.zeros((depth, n_route_pad, d_model), F32)], axis=1).astype(F32)
    br = jnp.concatenate([b_router_group, b_router_expert, jnp.zeros((depth, n_route_pad), F32)],
                         axis=1).astype(F32).reshape(depth, ROUTE_ROWS, 1)
    post_params = (w_out.astype(BF16), vec(norm_ffn), wr, br)
    ple_params = (w_ple.astype(BF16), vec(norm_ple), vec(norm_pg), w_pg.astype(BF16))

    n_exp = w_gate.shape[1]
    wg_all = w_gate.astype(F32).reshape((depth * n_exp,) + w_gate.shape[2:])
    wu_all = w_up.astype(F32).reshape((depth * n_exp,) + w_up.shape[2:])
    wd_all = w_down.astype(F32).reshape((depth * n_exp,) + w_down.shape[2:])
    s0_all = state_ssm.reshape(depth * n_dec, d_ssd, D_STATE)
    tail = CONV_WIDTH - 1
    cs_all = jnp.pad(state_conv, ((0, 0), (0, 0), (lead - tail, dec_seq), (0, 0))).reshape(depth * n_srows, conv_dim)
    pp_all = p_prompt.reshape(depth * n_prompt, p_prompt.shape[-1])
    ps_all = sample_rows(p_sample)

    xs = ssm_s = None
    convs_p, ssms_p, convs_s, vs_s = [], [], [], []
    z, xbc, dt, u, v = _inproj(hp, hs, inproj_params, 0)

    for i in range(depth):
        yp, ssm_p = _mixer_prompt(z, xbc, dt, u, v, mixer_p_params, n_seq, n_chunk, i)
        ys, ssm_s = _mixer_sample(z, xbc, dt, u, v, cs_all, s0_all, ssm_s, mixer_s_params, n_prompt, n_srows, i)

        slab, bkt, rnk, cnt = _post(yp, ys, hp, hs, post_params, i)
        starts, expert_tables = _routing_tables(cnt, n_moe_tiles, i * n_exp)

        xs, pos = _scatter_tokens(starts, bkt, rnk, slab, n_moe_tiles * MOE_TILE, X_PITCH, init=xs)
        h2s = _moe(expert_tables, xs, post_params[1], wg_all, wu_all, wd_all, i)

        convs_p.append(jnp.concatenate([xbc[(b + 1) * seq_len - tail:(b + 1) * seq_len] for b in range(n_seq)])
                       .reshape(n_seq, tail, conv_dim))
        xbc_s = xbc[n_prompt:].reshape(n_dec, GROUP, conv_dim)
        convs_s.append(xbc_s[:, GROUP - tail:])
        ssms_p.append(ssm_p.reshape(n_seq, SSD_HEADS, SSD_HEAD_DIM, D_STATE))
        vs_s.append(v[n_prompt:].reshape(n_dec, GROUP, d_gmlp)[:, lead:])

        final = i == depth - 1
        outs = _ple(pos, h2s, pp_all, ps_all, ple_params,
                    norm_final.astype(F32).reshape(1, -1) if final else inproj_params, final, i, n_prompt)
        hp, hs = outs[:2]
        if not final:
            z, xbc, dt, u, v = outs[2:]

    y_prompt = hp.reshape(n_seq, seq_len, d_model)
    y_sample = hs.reshape(n_dec, GROUP, d_model)[:, lead:]
    new_ssm_sample = ssm_s.reshape(depth, n_dec, SSD_HEADS, SSD_HEAD_DIM, D_STATE)
    return (y_prompt, y_sample, jnp.stack(convs_p), jnp.stack(ssms_p), jnp.stack(convs_s), new_ssm_sample,
            jnp.stack(vs_s))
```

```python
import functools
import math

import jax
import jax.numpy as jnp
from jax import lax
from jax.experimental import pallas as pl
from jax.experimental.pallas import tpu as pltpu

F32 = jnp.float32
BF16 = jnp.bfloat16

LANES = 128
VMEM_LIMIT_BYTES = 56 * 1024 * 1024

CONV_WIDTH = 4
SSD_HEADS = 8
SSD_HEAD_DIM = 64
SSD_GROUPS = 2
D_STATE = 128
CHUNK = 128
GMLP_HEADS = 8
N_EXPERT_GROUPS = 4
EXPERTS_PER_GROUP = 4
N_PAIRS = 6
N_BUCKETS = N_EXPERT_GROUPS * N_PAIRS
EPS = 1e-6

GROUP = 8
DEC_SEQ_ROWS = 4
TOK_TILE = 512
MOE_TILE = 256
PROMPT_TILES_PER_STEP = 8
ROUTE_ROWS = 32
TOKEN_ROWS = 8
X_PITCH = TOKEN_ROWS + 1
Y_PITCH = TOKEN_ROWS + 1
NEG_BIG = -1e30


def _dot(a, b):
    return jnp.dot(a, b, preferred_element_type=F32)


def _dot_nt(a, b):
    return lax.dot_general(a, b, (((1,), (1,)), ((), ())), preferred_element_type=F32)


def _rms(x, g):
    ms = jnp.mean(x * x, axis=-1, keepdims=True)
    return (x * lax.rsqrt(ms + EPS)) * g


def _gelu(x):
    return 0.5 * x * (1.0 + lax.erf(x * (1.0 / math.sqrt(2.0))))


def _sigmoid(x):
    return 0.5 * jnp.tanh(0.5 * x) + 0.5


def _softplus(x):
    return jnp.maximum(x, 0.0) + jnp.log1p(jnp.exp(-jnp.abs(x)))


def _params(n_grid):
    return pltpu.CompilerParams(dimension_semantics=("arbitrary",) * n_grid,
                                vmem_limit_bytes=VMEM_LIMIT_BYTES)


def _pair_specs(width, n_prompt_tiles, prompt_tile0=0, sample_tile0=0):
    return [pl.BlockSpec((TOK_TILE, width), lambda i: (prompt_tile0 + jnp.minimum(i, n_prompt_tiles - 1), 0)),
            pl.BlockSpec((TOK_TILE, width), lambda i: (sample_tile0 + jnp.maximum(i - n_prompt_tiles, 0), 0))]


def _pair_load(p_ref, s_ref, n_prompt_tiles):
    return jnp.where(pl.program_id(0) < n_prompt_tiles, p_ref[...], s_ref[...])


def _layer_spec(a, layer):
    zeros = (0,) * (a.ndim - 1)
    return pl.BlockSpec((None,) + a.shape[1:], lambda *_: (layer,) + zeros)


def _inproj_tile(h, nm_ref, wz_ref, wx_ref, wdt_ref, wu_ref, wv_ref, dtb_ref, gn_ref,
                 z_ref, xbc_ref, dt_ref, u_ref, v_ref):
    a = _rms(h, nm_ref[...]).astype(BF16)
    z_ref[...] = _dot(a, wz_ref[...])
    xbc_ref[...] = _dot(a, wx_ref[...])
    dt_ref[...] = _softplus(_dot(a, wdt_ref[...]) + dtb_ref[...])
    u_ref[...] = _gelu(_dot(a, wu_ref[...]))
    v_ref[...] = _rms(_gelu(_dot(a, wv_ref[...])), gn_ref[...])


def _inproj_kernel(hp_ref, hs_ref, *refs, n_prompt_tiles):
    _inproj_tile(_pair_load(hp_ref, hs_ref, n_prompt_tiles), *refs)


N_INPROJ_PARAMS = 8


def _inproj_specs(weights, t, layer):
    assert len(weights) == N_INPROJ_PARAMS
    widths = [w.shape[-1] for w in weights[1:6]]
    row = lambda w: pl.BlockSpec((TOK_TILE, w), lambda i: (i, 0))
    return ([_layer_spec(a, layer) for a in weights], [row(w) for w in widths],
            [jax.ShapeDtypeStruct((t, w), F32) for w in widths])


def _inproj(hp, hs, weights, layer):
    t = hp.shape[0] + hs.shape[0]
    n_prompt_tiles = hp.shape[0] // TOK_TILE
    w_specs, out_specs, out_shape = _inproj_specs(weights, t, layer)
    return pl.pallas_call(
        functools.partial(_inproj_kernel, n_prompt_tiles=n_prompt_tiles),
        grid=(t // TOK_TILE,),
        in_specs=_pair_specs(hp.shape[1], n_prompt_tiles) + w_specs,
        out_specs=out_specs,
        out_shape=out_shape,
        compiler_params=_params(1),
        name="inproj",
    )(hp, hs, *weights)


def _seg_cumsum(x, seg, rowmod):
    d = 1
    while d < seg:
        x = x + jnp.where(rowmod >= d, pltpu.roll(x, d, axis=0), 0.0)
        d *= 2
    return x


def _seg_rev_cumsum(x, seg, rowmod):
    n = x.shape[0]
    d = 1
    while d < seg:
        x = x + jnp.where(rowmod + d < seg, pltpu.roll(x, n - d, axis=0), 0.0)
        d *= 2
    return x


def _expand_heads(m, lane_lt_half):
    parts = []
    for j in range(SSD_HEADS // 2):
        parts.append(jnp.where(lane_lt_half, m[:, 2 * j:2 * j + 1], m[:, 2 * j + 1:2 * j + 2]))
    return jnp.concatenate(parts, axis=1)


def _merge_head_pairs(per_head, lane_lt_half):
    parts = [jnp.where(lane_lt_half, per_head[2 * j], per_head[2 * j + 1]) for j in range(len(per_head) // 2)]
    return jnp.concatenate(parts, axis=1)


def _mixer_kernel(*refs, sample, n_inner, n_prev=0):
    n_tok = 6 if sample else 5
    tok_refs, rest = refs[:n_tok], refs[n_tok:]
    if sample:
        if n_prev:
            prev_ref, rest = rest[0], rest[1:]
        (s0_ref, cw_ref, cb_ref, arow_ref, dsk_ref, sn_ref, wsp_ref, bsp_ref,
         y_ref, stack_ref, ext_ref, yoff_ref) = rest
        if n_prev:
            stack_ref[0:n_prev] = prev_ref[...]
        sout_ref = stack_ref.at[n_prev]
        first = pl.program_id(0) == 0
    else:
        (cw_ref, cb_ref, arow_ref, dsk_ref, sn_ref, wsp_ref, bsp_ref, y_ref, sout_ref, ext_ref) = rest
        s0_ref = yoff_ref = None
        first = pl.program_id(1) == 0

    @pl.when(first)
    def _():
        ext_ref[...] = jnp.zeros(ext_ref.shape, F32)
        if not sample:
            sout_ref[...] = jnp.zeros(sout_ref.shape, F32)

    def tile(c, carry):
        r0 = pl.multiple_of(c * CHUNK, CHUNK)
        views = [r.at[pl.ds(r0, CHUNK)] for r in tok_refs + (y_ref,)]
        _mixer_tile(*views[:n_tok], s0_ref, cw_ref, cb_ref, arow_ref, dsk_ref, sn_ref, wsp_ref, bsp_ref,
                    views[n_tok], sout_ref, ext_ref, yoff_ref, sample=sample)
        return carry

    if n_inner == 1:
        tile(0, 0)
    else:
        lax.fori_loop(0, n_inner, tile, 0)


def _mixer_tile(z_ref, xbc_ref, dt_ref, u_ref, v_ref, *rest, sample):
    if sample:
        cs_ref, s0_ref = rest[0], rest[1]
        rest = rest[2:]
    else:
        rest = rest[1:]
    cw_ref, cb_ref, arow_ref, dsk_ref, sn_ref, wsp_ref, bsp_ref, y_ref, sout_ref, ext_ref, yoff_ref = rest
    seg = GROUP if sample else CHUNK
    d_ssd = SSD_HEADS * SSD_HEAD_DIM
    gw = d_ssd // SSD_GROUPS
    hpg = SSD_HEADS // SSD_GROUPS
    cs_first = GROUP - DEC_SEQ_ROWS - (CONV_WIDTH - 1)

    rows = lax.broadcasted_iota(jnp.int32, (CHUNK, LANES), 0)
    cols = lax.broadcasted_iota(jnp.int32, (CHUNK, LANES), 1)
    rowmod = rows & (seg - 1)
    lane_lt_half = cols < SSD_HEAD_DIM

    xbc = xbc_ref[...]
    if sample:
        rm = lax.broadcasted_iota(jnp.int32, xbc.shape, 0) & (GROUP - 1)
        xbc = jnp.where((rm >= cs_first) & (rm < cs_first + CONV_WIDTH - 1), cs_ref[...], xbc)

    tail = ext_ref[...]
    row8 = lax.broadcasted_iota(jnp.int32, tail.shape, 0)
    acc = cb_ref[...] + cw_ref[CONV_WIDTH - 1:CONV_WIDTH, :] * xbc
    for j in range(1, CONV_WIDTH):
        rolled = pltpu.roll(xbc, j, axis=0)
        head = jnp.where(row8 < j, pltpu.roll(tail, j, axis=0), rolled[0:8, :])
        shifted = jnp.concatenate([head, rolled[8:, :]], axis=0)
        acc = acc + cw_ref[CONV_WIDTH - 1 - j:CONV_WIDTH - j, :] * shifted
    if not sample:
        ext_ref[...] = xbc[CHUNK - 8:, :]
    xc = acc * _sigmoid(acc)
    x = xc[:, :d_ssd]
    bb = xc[:, d_ssd:d_ssd + SSD_GROUPS * D_STATE].astype(BF16)
    cm = xc[:, d_ssd + SSD_GROUPS * D_STATE:]
    cbf = cm.astype(BF16)

    dtc = dt_ref[...]
    if sample:
        dtc = jnp.where(rowmod >= GROUP - DEC_SEQ_ROWS, dtc, 0.0)
    da = dtc * arow_ref[...]
    cum = _seg_cumsum(da, seg, rowmod)
    rev = _seg_rev_cumsum(da, seg, rowmod) - da
    cum_t = cum.T
    ecum = jnp.exp(cum)
    dt_e = _expand_heads(dtc, lane_lt_half)
    ecum_e = _expand_heads(ecum, lane_lt_half)
    erev_e = _expand_heads(jnp.exp(rev), lane_lt_half)

    xdt = x * dt_e
    xdt_bf = xdt.astype(BF16)
    causal = rows >= cols
    if sample:
        same_seq = (rows >> 3) == (cols >> 3)
        causal = causal & same_seq

    yd = []
    for g in range(SSD_GROUPS):
        cb_g = _dot_nt(cbf[:, g * D_STATE:(g + 1) * D_STATE], bb[:, g * D_STATE:(g + 1) * D_STATE])
        for hh in range(hpg):
            h = g * hpg + hh
            expo = cum[:, h:h + 1] - cum_t[h:h + 1, :]
            w = (cb_g * jnp.exp(jnp.where(causal, expo, NEG_BIG))).astype(BF16)
            j = h // 2
            yd.append(_dot(w, xdt_bf[:, j * LANES:(j + 1) * LANES]))
    y_diag = _merge_head_pairs(yd, lane_lt_half)

    if sample:
        for i in range(CHUNK // GROUP):
            for g in range(SSD_GROUPS):
                s_g = s0_ref[i, g * gw:(g + 1) * gw, :].astype(BF16)
                yoff_ref[i * GROUP:(i + 1) * GROUP, g * gw:(g + 1) * gw] = _dot_nt(
                    cm[i * GROUP:(i + 1) * GROUP, g * D_STATE:(g + 1) * D_STATE].astype(BF16), s_g)
        y_off = yoff_ref[...]
    else:
        y_off = jnp.concatenate(
            [_dot_nt(cbf[:, g * D_STATE:(g + 1) * D_STATE], sout_ref[0, g * gw:(g + 1) * gw, :].astype(BF16))
             for g in range(SSD_GROUPS)], axis=1)
    y = y_diag + y_off * ecum_e + dsk_ref[...] * x

    xd = xdt * erev_e
    for g in range(SSD_GROUPS):
        xd_t = xd[:, g * gw:(g + 1) * gw].T
        b_g = bb[:, g * D_STATE:(g + 1) * D_STATE]
        if sample:
            tcols = lax.broadcasted_iota(jnp.int32, xd_t.shape, 1) >> 3
            for i in range(CHUNK // GROUP):
                upd = _dot(jnp.where(tcols == i, xd_t, 0.0).astype(BF16), b_g)
                last = i * GROUP + GROUP - 1
                for hh in range(hpg):
                    h = g * hpg + hh
                    r0 = h * SSD_HEAD_DIM
                    sout_ref[i, r0:r0 + SSD_HEAD_DIM, :] = (
                        s0_ref[i, r0:r0 + SSD_HEAD_DIM, :] * ecum[last:last + 1, h:h + 1]
                        + upd[hh * SSD_HEAD_DIM:(hh + 1) * SSD_HEAD_DIM, :])
        else:
            upd = _dot(xd_t.astype(BF16), b_g)
            for hh in range(hpg):
                h = g * hpg + hh
                r0 = h * SSD_HEAD_DIM
                sout_ref[0, r0:r0 + SSD_HEAD_DIM, :] = (
                    sout_ref[0, r0:r0 + SSD_HEAD_DIM, :] * ecum[CHUNK - 1:CHUNK, h:h + 1]
                    + upd[hh * SSD_HEAD_DIM:(hh + 1) * SSD_HEAD_DIM, :])

    zf = z_ref[...]
    yf = y * (zf * _sigmoid(zf))
    parts = []
    for g in range(SSD_GROUPS):
        part = yf[:, g * gw:(g + 1) * gw]
        ms = jnp.mean(part * part, axis=-1, keepdims=True)
        parts.append(part * lax.rsqrt(ms + EPS))
    y_ssd = jnp.concatenate(parts, axis=1) * sn_ref[...]

    vb = v_ref[...].astype(BF16)
    if sample:
        n_grp = CHUNK // GROUP
        wms = [jnp.where(same_seq, jnp.tile(wsp_ref[h], (n_grp, 1)), 0.0).astype(BF16) for h in range(GMLP_HEADS)]
    else:
        wms = [wsp_ref[h] for h in range(GMLP_HEADS)]
    sg = [_dot(wms[h], vb[:, (h // 2) * LANES:(h // 2 + 1) * LANES]) for h in range(GMLP_HEADS)]
    s = _merge_head_pairs(sg, lane_lt_half) + bsp_ref[...]
    y_gm = u_ref[...] * s

    y_ref[:, :d_ssd] = y_ssd.astype(BF16)
    y_ref[:, d_ssd:] = y_gm.astype(BF16)


def _mixer_prompt(z, xbc, dt, u, v, params, n_seq, n_chunk, layer):
    n_inner = math.gcd(n_chunk, PROMPT_TILES_PER_STEP)
    n_outer = n_chunk // n_inner
    tok = lambda w: pl.BlockSpec((n_inner * CHUNK, w), lambda b, c: (b * n_outer + c, 0))
    d_ssd = z.shape[1]
    d_mix = d_ssd + u.shape[1]
    return pl.pallas_call(
        functools.partial(_mixer_kernel, sample=False, n_inner=n_inner),
        grid=(n_seq, n_outer),
        in_specs=[tok(z.shape[1]), tok(xbc.shape[1]), tok(dt.shape[1]), tok(u.shape[1]), tok(v.shape[1])]
        + [_layer_spec(a, layer) for a in params],
        out_specs=[tok(d_mix),
                   pl.BlockSpec((1, d_ssd, D_STATE), lambda b, c: (b, 0, 0))],
        out_shape=[jax.ShapeDtypeStruct((n_seq * n_chunk * CHUNK, d_mix), BF16),
                   jax.ShapeDtypeStruct((n_seq, d_ssd, D_STATE), F32)],
        scratch_shapes=[pltpu.VMEM((8, xbc.shape[1]), F32)],
        compiler_params=_params(2),
        name="mixer_prompt",
    )(z, xbc, dt, u, v, *params)


def _mixer_sample(z, xbc, dt, u, v, cs_all, s0_all, prev, params, row0, n_rows, layer):
    blk0 = row0 // CHUNK
    n_seq_blk = CHUNK // GROUP
    n_blk = n_rows // CHUNK
    tok = lambda w: pl.BlockSpec((CHUNK, w), lambda i: (blk0 + i, 0))
    d_ssd = z.shape[1]
    d_mix = d_ssd + u.shape[1]
    st = (n_seq_blk, d_ssd, D_STATE)
    stack = lambda n: pl.BlockSpec((n,) + st, lambda i: (0, i, 0, 0))
    prev_args = [] if layer == 0 else [prev]
    return pl.pallas_call(
        functools.partial(_mixer_kernel, sample=True, n_inner=1, n_prev=layer),
        grid=(n_blk,),
        in_specs=[tok(z.shape[1]), tok(xbc.shape[1]), tok(dt.shape[1]), tok(u.shape[1]), tok(v.shape[1]),
                  pl.BlockSpec((CHUNK, cs_all.shape[1]), lambda i: (layer * n_blk + i, 0))]
        + [stack(layer) for _ in prev_args]
        + [pl.BlockSpec(st, lambda i: (layer * n_blk + i, 0, 0))]
        + [_layer_spec(a, layer) for a in params],
        out_specs=[pl.BlockSpec((CHUNK, d_mix), lambda i: (i, 0)), stack(layer + 1)],
        out_shape=[jax.ShapeDtypeStruct((n_rows, d_mix), BF16),
                   jax.ShapeDtypeStruct((layer + 1, n_blk * n_seq_blk, d_ssd, D_STATE), F32)],
        scratch_shapes=[pltpu.VMEM((8, xbc.shape[1]), F32), pltpu.VMEM((CHUNK, d_ssd), F32)],
        compiler_params=_params(1),
        name="mixer_sample",
    )(z, xbc, dt, u, v, cs_all, *prev_args, s0_all, *params)


def _first_argmax(vals):
    m = vals[0]
    for v in vals[1:]:
        m = jnp.maximum(m, v)
    idx = jnp.full(m.shape, len(vals) - 1, jnp.int32)
    for k in range(len(vals) - 2, -1, -1):
        idx = jnp.where(vals[k] >= m, k, idx)
    return m, idx


def _slab_store(slab_ref, x, pitch, tok0=0):
    for k in range(x.shape[1] // LANES):
        slab_ref[pl.ds(tok0 * pitch + k, x.shape[0], stride=pitch), :] = x[:, k * LANES:(k + 1) * LANES]


def _slab_load(slab_ref, rows, n_pieces, pitch, tok0=0):
    return jnp.concatenate([slab_ref[pl.ds(tok0 * pitch + k, rows, stride=pitch), :] for k in range(n_pieces)],
                           axis=1)


def _post_kernel(yp_ref, ys_ref, hp_ref, hs_ref, wo_ref, nf_ref, wr_ref, br_ref,
                 slab_ref, bkt_ref, rnk_ref, cnt_ref, carry_ref, earlier_ref, *, n_prompt_tiles):
    i = pl.program_id(0)
    tm, d = hp_ref.shape

    @pl.when(i == 0)
    def _():
        carry_ref[...] = jnp.zeros(carry_ref.shape, F32)
        tr = lax.broadcasted_iota(jnp.int32, (tm, tm), 0)
        tc = lax.broadcasted_iota(jnp.int32, (tm, tm), 1)
        earlier_ref[...] = jnp.where(tr < tc, 1.0, 0.0).astype(BF16)

    ym = _pair_load(yp_ref, ys_ref, n_prompt_tiles)
    h1 = _pair_load(hp_ref, hs_ref, n_prompt_tiles) + _dot(ym, wo_ref[...])
    _slab_store(slab_ref, h1, X_PITCH)

    t = _rms(h1, nf_ref[...])
    wr = wr_ref[...]
    wr_hi = wr.astype(BF16)
    wr_lo = (wr - wr_hi.astype(F32)).astype(BF16)
    t_hi = t.astype(BF16)
    t_lo = (t - t_hi.astype(F32)).astype(BF16)
    by_hi = _dot_nt(jnp.concatenate([wr_hi, wr_lo], axis=0), t_hi)
    logits = (by_hi[:ROUTE_ROWS] + (by_hi[ROUTE_ROWS:] + _dot_nt(wr_hi, t_lo))) + br_ref[...]
    lg = [logits[k:k + 1, :] for k in range(N_EXPERT_GROUPS)]
    m, g = _first_argmax(lg)
    ssum = jnp.exp(lg[0] - m)
    for k in range(1, N_EXPERT_GROUPS):
        ssum = ssum + jnp.exp(lg[k] - m)
    p_sel = 1.0 / ssum
    le = [logits[N_EXPERT_GROUPS + e:N_EXPERT_GROUPS + e + 1, :] for e in range(N_EXPERT_GROUPS * EXPERTS_PER_GROUP)]
    a = []
    for k in range(EXPERTS_PER_GROUP):
        sel = le[(N_EXPERT_GROUPS - 1) * EXPERTS_PER_GROUP + k]
        for gi in range(N_EXPERT_GROUPS - 2, -1, -1):
            sel = jnp.where(g == gi, le[gi * EXPERTS_PER_GROUP + k], sel)
        a.append(sel)
    v1, i1 = _first_argmax(a)
    a2 = [jnp.where(i1 == k, -jnp.inf, a[k]) for k in range(EXPERTS_PER_GROUP)]
    v2, i2 = _first_argmax(a2)
    e2 = jnp.exp(v2 - v1)
    den = 1.0 + e2
    g1 = (1.0 / den) * p_sel
    g2 = (e2 / den) * p_sel
    lo = jnp.minimum(i1, i2)
    hi = jnp.maximum(i1, i2)
    pair = jnp.where(lo == 0, jnp.where(hi == 1, 0, hi), jnp.where(lo == 1, jnp.where(hi == 2, 1, 4), 5))
    first = jnp.where(pair == 0, 0, jnp.where(pair <= 2, 2, 3))
    c_lo = jnp.where(first == i1, g1, g2)
    c_hi = jnp.where(first == i1, g2, g1)
    bucket = g * N_PAIRS + pair

    brow = lax.broadcasted_iota(jnp.int32, (ROUTE_ROWS, tm), 0)
    onehot = jnp.where(brow == bucket, 1.0, 0.0)
    prefix = _dot(onehot.astype(BF16), earlier_ref[...])
    carry = carry_ref[:, 0:1]
    rank = jnp.sum(onehot * (prefix + carry), axis=0, keepdims=True)
    carry = carry + jnp.sum(onehot, axis=1, keepdims=True)
    carry_b = jnp.broadcast_to(carry, carry_ref.shape)
    carry_ref[...] = carry_b
    cnt_ref[...] = carry_b

    bkt_ref[0] = bucket
    rnk_ref[0] = rank.astype(jnp.int32)
    ar = lax.broadcasted_iota(jnp.int32, (LANES, tm), 0)
    aux = jnp.where(ar == 0, c_lo, jnp.where(ar == 1, c_hi, 0.0))
    slab_ref[pl.ds(d // LANES, tm, stride=X_PITCH), :] = aux.T


def _post(yp, ys, hp, hs, params, layer):
    d = hp.shape[1]
    t = hp.shape[0] + hs.shape[0]
    n_prompt_tiles = hp.shape[0] // TOK_TILE
    return pl.pallas_call(
        functools.partial(_post_kernel, n_prompt_tiles=n_prompt_tiles),
        grid=(t // TOK_TILE,),
        in_specs=_pair_specs(d, n_prompt_tiles) + _pair_specs(d, n_prompt_tiles)
        + [_layer_spec(a, layer) for a in params],
        out_specs=[pl.BlockSpec((TOK_TILE * X_PITCH, LANES), lambda i: (i, 0)),
                   pl.BlockSpec((1, 1, TOK_TILE), lambda i: (i, 0, 0)),
                   pl.BlockSpec((1, 1, TOK_TILE), lambda i: (i, 0, 0)),
                   pl.BlockSpec((ROUTE_ROWS, LANES), lambda i: (0, 0))],
        out_shape=[jax.ShapeDtypeStruct((t * X_PITCH, LANES), F32),
                   jax.ShapeDtypeStruct((t // TOK_TILE, 1, TOK_TILE), jnp.int32),
                   jax.ShapeDtypeStruct((t // TOK_TILE, 1, TOK_TILE), jnp.int32),
                   jax.ShapeDtypeStruct((ROUTE_ROWS, LANES), F32)],
        scratch_shapes=[pltpu.VMEM((ROUTE_ROWS, LANES), F32), pltpu.VMEM((TOK_TILE, TOK_TILE), BF16)],
        compiler_params=_params(1),
        name="post",
    )(yp, ys, hp, hs, *params)


def _scatter_kernel(starts_ref, bkt_ref, rnk_ref, src_ref, _, dst_ref, pos_ref, sem, *, pitch):
    n_tok = src_ref.shape[0] // pitch
    for r in range(n_tok):
        slot = starts_ref[bkt_ref[0, 0, r]] + rnk_ref[0, 0, r]
        pos_ref[0, 0, r] = slot
        pltpu.make_async_copy(src_ref.at[pl.ds(r * pitch, pitch)],
                              dst_ref.at[pl.ds(slot * pitch, pitch)], sem).start(priority=r % 2)
    pltpu.make_async_copy(src_ref, dst_ref.at[pl.ds(0, n_tok * pitch)], sem).wait()


def _scatter_tokens(starts, bkt, rnk, src, n_out, pitch, init=None):
    hbm = pl.BlockSpec(memory_space=pl.ANY)
    tile_smem = pl.BlockSpec((1, 1, TOK_TILE), lambda i: (i, 0, 0), memory_space=pltpu.SMEM)
    if init is None:
        init = jnp.zeros((n_out * pitch, LANES), src.dtype)
    return pl.pallas_call(
        functools.partial(_scatter_kernel, pitch=pitch),
        grid=(bkt.shape[0],),
        in_specs=[pl.BlockSpec(memory_space=pltpu.SMEM), tile_smem, tile_smem,
                  pl.BlockSpec((TOK_TILE * pitch, LANES), lambda i: (i, 0)), hbm],
        out_specs=[hbm, tile_smem],
        out_shape=[jax.ShapeDtypeStruct((n_out * pitch, LANES), src.dtype),
                   jax.ShapeDtypeStruct(bkt.shape, jnp.int32)],
        scratch_shapes=[pltpu.SemaphoreType.DMA(())],
        input_output_aliases={4: 0},
        compiler_params=_params(1),
        name="scatter_tokens",
    )(starts, bkt, rnk, src, init)


def _moe_kernel(elo_ref, ehi_ref, nused_ref, new_lo_ref, new_hi_ref, x_ref, nf_ref, *refs):
    f32_lo, f32_hi, o_ref, (wg_lo, wu_lo, wd_lo), (wg_hi, wu_hi, wd_hi) = (
        refs[0:3], refs[3:6], refs[6], refs[7:10], refs[10:13])
    i = pl.program_id(0)
    n_pieces = nf_ref.shape[1] // LANES

    @pl.when(i < nused_ref[0])
    def _():
        for new_ref, srcs, dsts in ((new_lo_ref, f32_lo, (wg_lo, wu_lo, wd_lo)),
                                    (new_hi_ref, f32_hi, (wg_hi, wu_hi, wd_hi))):
            @pl.when(new_ref[i] == 1)
            def _():
                for src, dst in zip(srcs, dsts):
                    dst[...] = src[...].astype(BF16)

        x = _slab_load(x_ref, MOE_TILE, n_pieces, X_PITCH)
        gates = x_ref[pl.ds(n_pieces, MOE_TILE, stride=X_PITCH), :]
        c_lo = gates[:, 0:1]
        c_hi = gates[:, 1:2]
        t = _rms(x, nf_ref[...]).astype(BF16)

        def expert(wg, wu, wd):
            gate = _dot(t, wg[...])
            hid = (gate * _sigmoid(gate)) * _dot(t, wu[...])
            return _dot(hid.astype(BF16), wd[...])

        y = c_lo * expert(wg_lo, wu_lo, wd_lo)
        y = y + c_hi * expert(wg_hi, wu_hi, wd_hi)
        _slab_store(o_ref, x + y, Y_PITCH)
        o_ref[pl.ds(TOKEN_ROWS, MOE_TILE, stride=Y_PITCH), :] = jnp.zeros((MOE_TILE, LANES), F32)

    @pl.when(i >= nused_ref[0])
    def _():
        o_ref[...] = jnp.zeros(o_ref.shape, F32)


def _moe(tables, xs, nf, wg, wu, wd, layer):
    n_tiles = tables[0].shape[0]
    d, de = wg.shape[1:]
    lo = lambda i, elo, ehi, nu, nlo, nhi: (elo[i], 0, 0)
    hi = lambda i, elo, ehi, nu, nlo, nhi: (ehi[i], 0, 0)
    wspec = lambda shape, index_map: pl.BlockSpec((None,) + shape, index_map)
    grid_spec = pltpu.PrefetchScalarGridSpec(
        num_scalar_prefetch=len(tables),
        grid=(n_tiles,),
        in_specs=[pl.BlockSpec((MOE_TILE * X_PITCH, LANES),
                               lambda i, elo, ehi, nu, nlo, nhi: (jnp.maximum(jnp.minimum(i, nu[0] - 1), 0), 0)),
                  _layer_spec(nf, layer),
                  wspec((d, de), lo), wspec((d, de), lo), wspec((de, d), lo),
                  wspec((d, de), hi), wspec((d, de), hi), wspec((de, d), hi)],
        out_specs=pl.BlockSpec((MOE_TILE * Y_PITCH, LANES), lambda i, elo, ehi, nu, nlo, nhi: (i, 0)),
        scratch_shapes=[pltpu.VMEM(s, BF16) for s in ((d, de), (d, de), (de, d)) * 2],
    )
    return pl.pallas_call(
        _moe_kernel,
        grid_spec=grid_spec,
        out_shape=jax.ShapeDtypeStruct((n_tiles * MOE_TILE * Y_PITCH, LANES), F32),
        compiler_params=_params(1),
        name="expert_pairs",
    )(*tables, xs, nf, wg, wu, wd, wg, wu, wd)


def _ple_kernel(pos_ref, pos_next_ref, h_hbm, pp_ref, ps_ref, wple_ref, nple_ref, npg_ref, wpg_ref, *refs,
                final, n_prompt_tiles):
    refs, (xbuf, sem) = refs[:-2], refs[-2:]
    i = pl.program_id(0)
    last = pl.num_programs(0) - 1
    slot = i % 2

    def start_tile(p_ref, s):
        for r in range(TOK_TILE):
            pltpu.make_async_copy(h_hbm.at[pl.ds(p_ref[0, 0, r] * Y_PITCH, Y_PITCH)],
                                  xbuf.at[s, pl.ds(r * Y_PITCH, Y_PITCH)], sem.at[s]).start()

    def wait_tile(s):
        pltpu.make_async_copy(h_hbm.at[pl.ds(0, TOK_TILE * Y_PITCH)], xbuf.at[s], sem.at[s]).wait()

    @pl.when(i == 0)
    def _():
        start_tile(pos_ref, 0)

    wait_tile(slot)
    start_tile(pos_next_ref, 1 - slot)
    h2 = _slab_load(xbuf.at[slot], TOK_TILE, TOKEN_ROWS, Y_PITCH)
    p = _pair_load(pp_ref, ps_ref, n_prompt_tiles)
    e = _rms(_dot(p.astype(BF16), wple_ref[...]), nple_ref[...])
    gate = _sigmoid(_dot(_rms(h2, npg_ref[...]).astype(BF16), wpg_ref[...]))
    h3 = h2 + gate * e
    if final:
        nfin_ref, op_ref, os_ref = refs
        out = _rms(h3, nfin_ref[...])
    else:
        op_ref, os_ref = refs[N_INPROJ_PARAMS:N_INPROJ_PARAMS + 2]
        out = h3
        _inproj_tile(h3, *refs[:N_INPROJ_PARAMS], *refs[N_INPROJ_PARAMS + 2:])

    @pl.when(i < n_prompt_tiles)
    def _():
        op_ref[...] = out

    @pl.when(i >= n_prompt_tiles)
    def _():
        os_ref[...] = out

    @pl.when(i == last)
    def _():
        wait_tile(1 - slot)


def _ple(pos, h2s, pp_all, ps_all, params, tail, final, layer, n_prompt):
    n_steps = pos.shape[0]
    t = n_steps * TOK_TILE
    d = TOKEN_ROWS * LANES
    tile_smem = pl.BlockSpec((1, 1, TOK_TILE), lambda i: (i, 0, 0), memory_space=pltpu.SMEM)
    next_smem = pl.BlockSpec((1, 1, TOK_TILE), lambda i: (jnp.minimum(i + 1, n_steps - 1), 0, 0),
                             memory_space=pltpu.SMEM)
    n_prompt_tiles = n_prompt // TOK_TILE
    n_sample_tiles = n_steps - n_prompt_tiles
    out_specs = _pair_specs(d, n_prompt_tiles)
    out_shape = [jax.ShapeDtypeStruct((n_prompt, d), F32), jax.ShapeDtypeStruct((t - n_prompt, d), F32)]
    if final:
        tail = (tail,)
        tail_specs = [pl.BlockSpec(tail[0].shape, lambda i: (0,) * tail[0].ndim)]
    else:
        tail_specs, io, ish = _inproj_specs(tail, t, layer + 1)
        out_specs, out_shape = out_specs + io, out_shape + ish
    return pl.pallas_call(
        functools.partial(_ple_kernel, final=final, n_prompt_tiles=n_prompt_tiles),
        grid=(n_steps,),
        in_specs=[tile_smem, next_smem, pl.BlockSpec(memory_space=pl.ANY)]
        + _pair_specs(pp_all.shape[1], n_prompt_tiles, layer * n_prompt_tiles, layer * n_sample_tiles)
        + [_layer_spec(a, layer) for a in params] + tail_specs,
        out_specs=out_specs,
        out_shape=out_shape,
        scratch_shapes=[pltpu.VMEM((2, TOK_TILE * Y_PITCH, LANES), F32), pltpu.SemaphoreType.DMA((2,))],
        compiler_params=_params(1),
        name="ple_final" if final else "ple_inproj",
    )(pos, pos, h2s, pp_all, ps_all, *params, *tail)


_PAIR_FIRST = (0, 2, 2, 3, 3, 3)
_PAIR_SECOND = (1, 1, 0, 0, 1, 2)


def _routing_tables(cnt, n_tiles, expert0):
    counts = cnt[:N_BUCKETS, 0].astype(jnp.int32)
    padded = ((counts + MOE_TILE - 1) // MOE_TILE) * MOE_TILE
    ends = jnp.cumsum(padded)
    starts = jnp.pad(ends - padded, (0, ROUTE_ROWS - N_BUCKETS))
    n_used = ends[-1] // MOE_TILE
    tile = jnp.minimum(jnp.arange(n_tiles, dtype=jnp.int32), n_used - 1)
    tb = jnp.sum((ends[None, :] <= (tile * MOE_TILE)[:, None]).astype(jnp.int32), axis=1)
    tb = jnp.minimum(tb, N_BUCKETS - 1)
    grp = tb // N_PAIRS
    pr = tb % N_PAIRS
    e_lo = expert0 + grp * EXPERTS_PER_GROUP + jnp.asarray(_PAIR_FIRST, jnp.int32)[pr]
    e_hi = expert0 + grp * EXPERTS_PER_GROUP + jnp.asarray(_PAIR_SECOND, jnp.int32)[pr]
    is_new = lambda e: jnp.concatenate([jnp.ones((1,), jnp.int32), (e[1:] != e[:-1]).astype(jnp.int32)])
    return starts, (e_lo, e_hi, n_used.reshape(1), is_new(e_lo), is_new(e_hi))


def kernel(x_prompt, x_sample, state_conv, state_ssm, p_prompt, p_sample, norm_mix, w_in, conv_w, conv_b,
           dt_bias, a_log, d_skip, ssd_norm, gmlp_norm, w_spatial, b_spatial, w_out, norm_ffn,
           w_router_group, b_router_group, w_router_expert, b_router_expert, w_gate, w_up, w_down,
           w_ple, norm_ple, norm_pg, w_pg, norm_final):
    n_seq, seq_len, d_model = x_prompt.shape
    n_dec, dec_seq, _ = x_sample.shape
    depth = w_in.shape[0]
    conv_dim = conv_w.shape[2]
    d_ssd = SSD_HEADS * SSD_HEAD_DIM
    d_gmlp = gmlp_norm.shape[1]
    assert dec_seq == DEC_SEQ_ROWS and conv_w.shape[1] == CONV_WIDTH and seq_len % CHUNK == 0
    assert conv_dim == d_ssd + 2 * SSD_GROUPS * D_STATE and w_spatial.shape[1:] == (GMLP_HEADS, CHUNK, CHUNK)
    n_chunk = seq_len // CHUNK
    n_prompt = n_seq * seq_len
    n_srows = n_dec * GROUP
    t_all = n_prompt + n_srows
    assert n_prompt % TOK_TILE == 0 and n_srows % TOK_TILE == 0
    assert d_model == TOKEN_ROWS * LANES
    lead = GROUP - dec_seq
    n_moe_tiles = t_all // MOE_TILE + N_BUCKETS

    def sample_rows(a):
        pad = [(0, 0)] * (a.ndim - 2) + [(lead, 0), (0, 0)]
        return jnp.pad(a, pad).reshape(-1, a.shape[-1])

    hp, hs = x_prompt.reshape(n_prompt, d_model), sample_rows(x_sample)

    vec = lambda a: a.astype(F32).reshape(depth, 1, -1)
    lane_pad = ((0, 0), (0, LANES - SSD_HEADS))
    o_dt = d_ssd + conv_dim
    o_uv = o_dt + SSD_HEADS
    w_cols = lambda a, b: w_in[..., a:b].astype(BF16)
    inproj_params = (vec(norm_mix), w_cols(0, d_ssd), w_cols(d_ssd, o_dt),
                     jnp.pad(w_in[..., o_dt:o_uv], ((0, 0),) + lane_pad).astype(BF16),
                     w_cols(o_uv, o_uv + d_gmlp), w_cols(o_uv + d_gmlp, o_uv + 2 * d_gmlp),
                     jnp.pad(dt_bias.astype(F32), lane_pad).reshape(depth, 1, LANES), vec(gmlp_norm))

    head_cols = jnp.arange(d_ssd) // SSD_HEAD_DIM
    per_head = d_gmlp // GMLP_HEADS
    ws_tril = jnp.where(jnp.tril(jnp.ones((CHUNK, CHUNK), bool)), w_spatial, 0.0)
    bsp_p = jnp.repeat(jnp.swapaxes(b_spatial, 1, 2), per_head, axis=2)
    w8 = jnp.pad(ws_tril[:, :, :dec_seq, :dec_seq], ((0, 0), (0, 0), (lead, 0), (lead, 0)))
    wsp_s = jnp.tile(w8, (1, 1, 1, CHUNK // GROUP))
    b8 =jnp.pad(b_spatial[:, :, :dec_seq], ((0, 0), (0, 0), (lead, 0)))
    bsp_s = jnp.repeat(jnp.swapaxes(jnp.tile(b8, (1, 1, CHUNK // GROUP)), 1, 2), per_head, axis=2)
    mixer_common = (conv_w.astype(F32), vec(conv_b),
                    jnp.pad(-jnp.exp(a_log.astype(F32)), lane_pad).reshape(depth, 1, LANES),
                    d_skip.astype(F32)[:, head_cols].reshape(depth, 1, d_ssd), vec(ssd_norm))
    mixer_p_params = mixer_common + (ws_tril.astype(BF16), bsp_p)
    mixer_s_params = mixer_common + (wsp_s.astype(F32), bsp_s)

    n_route_pad = ROUTE_ROWS - N_EXPERT_GROUPS * (1 + EXPERTS_PER_GROUP)
    wr = jnp.concatenate([jnp.swapaxes(w_router_group, 1, 2), jnp.swapaxes(w_router_expert, 1, 2),
                          jnp.zeros((depth, n_route_pad, d_model), F32)], axis=1).astype(F32)
    br = jnp.concatenate([b_router_group, b_router_expert, jnp.zeros((depth, n_route_pad), F32)],
                         axis=1).astype(F32).reshape(depth, ROUTE_ROWS, 1)
    post_params = (w_out.astype(BF16), vec(norm_ffn), wr, br)
    ple_params = (w_ple.astype(BF16), vec(norm_ple), vec(norm_pg), w_pg.astype(BF16))

    n_exp = w_gate.shape[1]
    wg_all = w_gate.astype(F32).reshape((depth * n_exp,) + w_gate.shape[2:])
    wu_all = w_up.astype(F32).reshape((depth * n_exp,) + w_up.shape[2:])
    wd_all = w_down.astype(F32).reshape((depth * n_exp,) + w_down.shape[2:])
    s0_all = state_ssm.reshape(depth * n_dec, d_ssd, D_STATE)
    tail = CONV_WIDTH - 1
    cs_all = jnp.pad(state_conv, ((0, 0), (0, 0), (lead - tail, dec_seq), (0, 0))).reshape(depth * n_srows, conv_dim)
    pp_all = p_prompt.reshape(depth * n_prompt, p_prompt.shape[-1])
    ps_all = sample_rows(p_sample)

    xs = ssm_s = None
    convs_p, ssms_p, convs_s, vs_s = [], [], [], []
    z, xbc, dt, u, v = _inproj(hp, hs, inproj_params, 0)

    for i in range(depth):
        yp, ssm_p = _mixer_prompt(z, xbc, dt, u, v, mixer_p_params, n_seq, n_chunk, i)
        ys, ssm_s = _mixer_sample(z, xbc, dt, u, v, cs_all, s0_all, ssm_s, mixer_s_params, n_prompt, n_srows, i)

        slab, bkt, rnk, cnt = _post(yp, ys, hp, hs, post_params, i)
        starts, expert_tables = _routing_tables(cnt, n_moe_tiles, i * n_exp)

        xs, pos = _scatter_tokens(starts, bkt, rnk, slab, n_moe_tiles * MOE_TILE, X_PITCH, init=xs)
        h2s = _moe(expert_tables, xs, post_params[1], wg_all, wu_all, wd_all, i)

        convs_p.append(jnp.concatenate([xbc[(b + 1) * seq_len - tail:(b + 1) * seq_len] for b in range(n_seq)])
                       .reshape(n_seq, tail, conv_dim))
        xbc_s = xbc[n_prompt:].reshape(n_dec, GROUP, conv_dim)
        convs_s.append(xbc_s[:, GROUP - tail:])
        ssms_p.append(ssm_p.reshape(n_seq, SSD_HEADS, SSD_HEAD_DIM, D_STATE))
        vs_s.append(v[n_prompt:].reshape(n_dec, GROUP, d_gmlp)[:, lead:])

        final = i == depth - 1
        outs = _ple(pos, h2s, pp_all, ps_all, ple_params,
                    norm_final.astype(F32).reshape(1, -1) if final else inproj_params, final, i, n_prompt)
        hp, hs = outs[:2]
        if not final:
            z, xbc, dt, u, v = outs[2:]

    y_prompt = hp.reshape(n_seq, seq_len, d_model)
    y_sample = hs.reshape(n_dec, GROUP, d_model)[:, lead:]
    new_ssm_sample = ssm_s.reshape(depth, n_dec, SSD_HEADS, SSD_HEAD_DIM, D_STATE)
    return (y_prompt, y_sample, jnp.stack(convs_p), jnp.stack(ssms_p), jnp.stack(convs_s), new_ssm_sample,
            jnp.stack(vs_s))
```

```python
import functools
import math

import jax
import jax.numpy as jnp
from jax import lax
from jax.experimental import pallas as pl
from jax.experimental.pallas import tpu as pltpu

F32 = jnp.float32
BF16 = jnp.bfloat16

LANES = 128
VMEM_LIMIT_BYTES = 56 * 1024 * 1024

CONV_WIDTH = 4
SSD_HEADS = 8
SSD_HEAD_DIM = 64
SSD_GROUPS = 2
D_STATE = 128
CHUNK = 128
GMLP_HEADS = 8
N_EXPERT_GROUPS = 4
EXPERTS_PER_GROUP = 4
N_PAIRS = 6
N_BUCKETS = N_EXPERT_GROUPS * N_PAIRS
EPS = 1e-6

GROUP = 8
DEC_SEQ_ROWS = 4
TOK_TILE = 512
MOE_TILE = 256
PROMPT_TILES_PER_STEP = 8
ROUTE_ROWS = 32
TOKEN_ROWS = 8
X_PITCH = TOKEN_ROWS + 1
Y_PITCH = TOKEN_ROWS + 1
NEG_BIG = -1e30


def _dot(a, b):
    return jnp.dot(a, b, preferred_element_type=F32)


def _dot_nt(a, b):
    return lax.dot_general(a, b, (((1,), (1,)), ((), ())), preferred_element_type=F32)


def _rms(x, g):
    ms = jnp.mean(x * x, axis=-1, keepdims=True)
    return (x * lax.rsqrt(ms + EPS)) * g


def _gelu(x):
    return 0.5 * x * (1.0 + lax.erf(x * (1.0 / math.sqrt(2.0))))


def _sigmoid(x):
    return 0.5 * jnp.tanh(0.5 * x) + 0.5


def _softplus(x):
    return jnp.maximum(x, 0.0) + jnp.log1p(jnp.exp(-jnp.abs(x)))


def _params(n_grid):
    return pltpu.CompilerParams(dimension_semantics=("arbitrary",) * n_grid,
                                vmem_limit_bytes=VMEM_LIMIT_BYTES)


def _pair_specs(width, n_prompt_tiles, prompt_tile0=0, sample_tile0=0):
    return [pl.BlockSpec((TOK_TILE, width), lambda i: (prompt_tile0 + jnp.minimum(i, n_prompt_tiles - 1), 0)),
            pl.BlockSpec((TOK_TILE, width), lambda i: (sample_tile0 + jnp.maximum(i - n_prompt_tiles, 0), 0))]


def _pair_load(p_ref, s_ref, n_prompt_tiles):
    return jnp.where(pl.program_id(0) < n_prompt_tiles, p_ref[...], s_ref[...])


def _layer_spec(a, layer):
    zeros = (0,) * (a.ndim - 1)
    return pl.BlockSpec((None,) + a.shape[1:], lambda *_: (layer,) + zeros)


def _inproj_tile(h, nm_ref, wz_ref, wx_ref, wdt_ref, wu_ref, wv_ref, dtb_ref, gn_ref,
                 z_ref, xbc_ref, dt_ref, u_ref, v_ref):
    a = _rms(h, nm_ref[...]).astype(BF16)
    z_ref[...] = _dot(a, wz_ref[...])
    xbc_ref[...] = _dot(a, wx_ref[...])
    dt_ref[...] = _softplus(_dot(a, wdt_ref[...]) + dtb_ref[...])
    u_ref[...] = _gelu(_dot(a, wu_ref[...]))
    v_ref[...] = _rms(_gelu(_dot(a, wv_ref[...])), gn_ref[...])


def _inproj_kernel(hp_ref, hs_ref, *refs, n_prompt_tiles):
    @pl.when(pl.program_id(0) < n_prompt_tiles)
    def _():
        _inproj_tile(hp_ref[...], *refs)

    @pl.when(pl.program_id(0) >= n_prompt_tiles)
    def _():
        _inproj_tile(hs_ref[...], *refs)


N_INPROJ_PARAMS = 8


def _inproj_specs(weights, t, layer):
    assert len(weights) == N_INPROJ_PARAMS
    widths = [w.shape[-1] for w in weights[1:6]]
    row = lambda w: pl.BlockSpec((TOK_TILE, w), lambda i: (i, 0))
    return ([_layer_spec(a, layer) for a in weights], [row(w) for w in widths],
            [jax.ShapeDtypeStruct((t, w), F32) for w in widths])


def _inproj(hp, hs, weights, layer):
    t = hp.shape[0] + hs.shape[0]
    n_prompt_tiles = hp.shape[0] // TOK_TILE
    w_specs, out_specs, out_shape = _inproj_specs(weights, t, layer)
    return pl.pallas_call(
        functools.partial(_inproj_kernel, n_prompt_tiles=n_prompt_tiles),
        grid=(t // TOK_TILE,),
        in_specs=_pair_specs(hp.shape[1], n_prompt_tiles) + w_specs,
        out_specs=out_specs,
        out_shape=out_shape,
        compiler_params=_params(1),
        name="inproj",
    )(hp, hs, *weights)


def _seg_cumsum(x, seg, rowmod):
    d = 1
    while d < seg:
        x = x + jnp.where(rowmod >= d, pltpu.roll(x, d, axis=0), 0.0)
        d *= 2
    return x


def _seg_rev_cumsum(x, seg, rowmod):
    n = x.shape[0]
    d = 1
    while d < seg:
        x = x + jnp.where(rowmod + d < seg, pltpu.roll(x, n - d, axis=0), 0.0)
        d *= 2
    return x


def _expand_heads(m, lane_lt_half):
    parts = []
    for j in range(SSD_HEADS // 2):
        parts.append(jnp.where(lane_lt_half, m[:, 2 * j:2 * j + 1], m[:, 2 * j + 1:2 * j + 2]))
    return jnp.concatenate(parts, axis=1)


def _merge_head_pairs(per_head, lane_lt_half):
    parts = [jnp.where(lane_lt_half, per_head[2 * j], per_head[2 * j + 1]) for j in range(len(per_head) // 2)]
    return jnp.concatenate(parts, axis=1)


def _mixer_kernel(*refs, sample, n_inner, n_prev=0):
    n_tok = 6 if sample else 5
    tok_refs, rest = refs[:n_tok], refs[n_tok:]
    if sample:
        if n_prev:
            prev_ref, rest = rest[0], rest[1:]
        (s0_ref, cw_ref, cb_ref, arow_ref, dsk_ref, sn_ref, wsp_ref, bsp_ref,
         y_ref, stack_ref, ext_ref, yoff_ref) = rest
        if n_prev:
            stack_ref[0:n_prev] = prev_ref[...]
        sout_ref = stack_ref.at[n_prev]
        first = pl.program_id(0) == 0
    else:
        (cw_ref, cb_ref, arow_ref, dsk_ref, sn_ref, wsp_ref, bsp_ref, y_ref, sout_ref, ext_ref) = rest
        s0_ref = yoff_ref = None
        first = pl.program_id(1) == 0

    @pl.when(first)
    def _():
        ext_ref[...] = jnp.zeros(ext_ref.shape, F32)
        if not sample:
            sout_ref[...] = jnp.zeros(sout_ref.shape, F32)

    def tile(c, carry):
        r0 = pl.multiple_of(c * CHUNK, CHUNK)
        views = [r.at[pl.ds(r0, CHUNK)] for r in tok_refs + (y_ref,)]
        _mixer_tile(*views[:n_tok], s0_ref, cw_ref, cb_ref, arow_ref, dsk_ref, sn_ref, wsp_ref, bsp_ref,
                    views[n_tok], sout_ref, ext_ref, yoff_ref, sample=sample)
        return carry

    if n_inner == 1:
        tile(0, 0)
    else:
        lax.fori_loop(0, n_inner, tile, 0)


def _mixer_tile(z_ref, xbc_ref, dt_ref, u_ref, v_ref, *rest, sample):
    if sample:
        cs_ref, s0_ref = rest[0], rest[1]
        rest = rest[2:]
    else:
        rest = rest[1:]
    cw_ref, cb_ref, arow_ref, dsk_ref, sn_ref, wsp_ref, bsp_ref, y_ref, sout_ref, ext_ref, yoff_ref = rest
    seg = GROUP if sample else CHUNK
    d_ssd = SSD_HEADS * SSD_HEAD_DIM
    gw = d_ssd // SSD_GROUPS
    hpg = SSD_HEADS // SSD_GROUPS
    cs_first = GROUP - DEC_SEQ_ROWS - (CONV_WIDTH - 1)

    rows = lax.broadcasted_iota(jnp.int32, (CHUNK, LANES), 0)
    cols = lax.broadcasted_iota(jnp.int32, (CHUNK, LANES), 1)
    rowmod = rows & (seg - 1)
    lane_lt_half = cols < SSD_HEAD_DIM

    xbc = xbc_ref[...]
    if sample:
        rm = lax.broadcasted_iota(jnp.int32, xbc.shape, 0) & (GROUP - 1)
        xbc = jnp.where((rm >= cs_first) & (rm < cs_first + CONV_WIDTH - 1), cs_ref[...], xbc)

    tail = ext_ref[...]
    row8 = lax.broadcasted_iota(jnp.int32, tail.shape, 0)
    acc = cb_ref[...] + cw_ref[CONV_WIDTH - 1:CONV_WIDTH, :] * xbc
    for j in range(1, CONV_WIDTH):
        rolled = pltpu.roll(xbc, j, axis=0)
        head = jnp.where(row8 < j, pltpu.roll(tail, j, axis=0), rolled[0:8, :])
        shifted = jnp.concatenate([head, rolled[8:, :]], axis=0)
        acc = acc + cw_ref[CONV_WIDTH - 1 - j:CONV_WIDTH - j, :] * shifted
    if not sample:
        ext_ref[...] = xbc[CHUNK - 8:, :]
    xc = acc * _sigmoid(acc)
    x = xc[:, :d_ssd]
    bb = xc[:, d_ssd:d_ssd + SSD_GROUPS * D_STATE].astype(BF16)
    cm = xc[:, d_ssd + SSD_GROUPS * D_STATE:]
    cbf = cm.astype(BF16)

    dtc = dt_ref[...]
    if sample:
        dtc = jnp.where(rowmod >= GROUP - DEC_SEQ_ROWS, dtc, 0.0)
    da = dtc * arow_ref[...]
    cum = _seg_cumsum(da, seg, rowmod)
    rev = _seg_rev_cumsum(da, seg, rowmod) - da
    cum_t = cum.T
    ecum = jnp.exp(cum)
    dt_e = _expand_heads(dtc, lane_lt_half)
    ecum_e = _expand_heads(ecum, lane_lt_half)
    erev_e = _expand_heads(jnp.exp(rev), lane_lt_half)

    xdt = x * dt_e
    xdt_bf = xdt.astype(BF16)
    causal = rows >= cols
    if sample:
        same_seq = (rows >> 3) == (cols >> 3)
        causal = causal & same_seq

    yd = []
    for g in range(SSD_GROUPS):
        cb_g = _dot_nt(cbf[:, g * D_STATE:(g + 1) * D_STATE], bb[:, g * D_STATE:(g + 1) * D_STATE])
        for hh in range(hpg):
            h = g * hpg + hh
            expo = cum[:, h:h + 1] - cum_t[h:h + 1, :]
            w = (cb_g * jnp.exp(jnp.where(causal, expo, NEG_BIG))).astype(BF16)
            j = h // 2
            yd.append(_dot(w, xdt_bf[:, j * LANES:(j + 1) * LANES]))
    y_diag = _merge_head_pairs(yd, lane_lt_half)

    if sample:
        for i in range(CHUNK // GROUP):
            for g in range(SSD_GROUPS):
                s_g = s0_ref[i, g * gw:(g + 1) * gw, :].astype(BF16)
                yoff_ref[i * GROUP:(i + 1) * GROUP, g * gw:(g + 1) * gw] = _dot_nt(
                    cm[i * GROUP:(i + 1) * GROUP, g * D_STATE:(g + 1) * D_STATE].astype(BF16), s_g)
        y_off = yoff_ref[...]
    else:
        y_off = jnp.concatenate(
            [_dot_nt(cbf[:, g * D_STATE:(g + 1) * D_STATE], sout_ref[0, g * gw:(g + 1) * gw, :].astype(BF16))
             for g in range(SSD_GROUPS)], axis=1)
    y = y_diag + y_off * ecum_e + dsk_ref[...] * x

    xd = xdt * erev_e
    for g in range(SSD_GROUPS):
        xd_t = xd[:, g * gw:(g + 1) * gw].T
        b_g = bb[:, g * D_STATE:(g + 1) * D_STATE]
        if sample:
            tcols = lax.broadcasted_iota(jnp.int32, xd_t.shape, 1) >> 3
            for i in range(CHUNK // GROUP):
                upd = _dot(jnp.where(tcols == i, xd_t, 0.0).astype(BF16), b_g)
                last = i * GROUP + GROUP - 1
                for hh in range(hpg):
                    h = g * hpg + hh
                    r0 = h * SSD_HEAD_DIM
                    sout_ref[i, r0:r0 + SSD_HEAD_DIM, :] = (
                        s0_ref[i, r0:r0 + SSD_HEAD_DIM, :] * ecum[last:last + 1, h:h + 1]
                        + upd[hh * SSD_HEAD_DIM:(hh + 1) * SSD_HEAD_DIM, :])
        else:
            upd = _dot(xd_t.astype(BF16), b_g)
            for hh in range(hpg):
                h = g * hpg + hh
                r0 = h * SSD_HEAD_DIM
                sout_ref[0, r0:r0 + SSD_HEAD_DIM, :] = (
                    sout_ref[0, r0:r0 + SSD_HEAD_DIM, :] * ecum[CHUNK - 1:CHUNK, h:h + 1]
                    + upd[hh * SSD_HEAD_DIM:(hh + 1) * SSD_HEAD_DIM, :])

    zf = z_ref[...]
    yf = y * (zf * _sigmoid(zf))
    parts = []
    for g in range(SSD_GROUPS):
        part = yf[:, g * gw:(g + 1) * gw]
        ms = jnp.mean(part * part, axis=-1, keepdims=True)
        parts.append(part * lax.rsqrt(ms + EPS))
    y_ssd = jnp.concatenate(parts, axis=1) * sn_ref[...]

    vb = v_ref[...].astype(BF16)
    if sample:
        n_grp = CHUNK // GROUP
        wms = [jnp.where(same_seq, jnp.tile(wsp_ref[h], (n_grp, 1)), 0.0).astype(BF16) for h in range(GMLP_HEADS)]
    else:
        wms = [wsp_ref[h] for h in range(GMLP_HEADS)]
    sg = [_dot(wms[h], vb[:, (h // 2) * LANES:(h // 2 + 1) * LANES]) for h in range(GMLP_HEADS)]
    s = _merge_head_pairs(sg, lane_lt_half) + bsp_ref[...]
    y_gm = u_ref[...] * s

    y_ref[:, :d_ssd] = y_ssd.astype(BF16)
    y_ref[:, d_ssd:] = y_gm.astype(BF16)


def _mixer_prompt(z, xbc, dt, u, v, params, n_seq, n_chunk, layer):
    n_inner = math.gcd(n_chunk, PROMPT_TILES_PER_STEP)
    n_outer = n_chunk // n_inner
    tok = lambda w: pl.BlockSpec((n_inner * CHUNK, w), lambda b, c: (b * n_outer + c, 0))
    d_ssd = z.shape[1]
    d_mix = d_ssd + u.shape[1]
    return pl.pallas_call(
        functools.partial(_mixer_kernel, sample=False, n_inner=n_inner),
        grid=(n_seq, n_outer),
        in_specs=[tok(z.shape[1]), tok(xbc.shape[1]), tok(dt.shape[1]), tok(u.shape[1]), tok(v.shape[1])]
        + [_layer_spec(a, layer) for a in params],
        out_specs=[tok(d_mix),
                   pl.BlockSpec((1, d_ssd, D_STATE), lambda b, c: (b, 0, 0))],
        out_shape=[jax.ShapeDtypeStruct((n_seq * n_chunk * CHUNK, d_mix), BF16),
                   jax.ShapeDtypeStruct((n_seq, d_ssd, D_STATE), F32)],
        scratch_shapes=[pltpu.VMEM((8, xbc.shape[1]), F32)],
        compiler_params=_params(2),
        name="mixer_prompt",
    )(z, xbc, dt, u, v, *params)


def _mixer_sample(z, xbc, dt, u, v, cs_all, s0_all, prev, params, row0, n_rows, layer):
    blk0 = row0 // CHUNK
    n_seq_blk = CHUNK // GROUP
    n_blk = n_rows // CHUNK
    tok = lambda w: pl.BlockSpec((CHUNK, w), lambda i: (blk0 + i, 0))
    d_ssd = z.shape[1]
    d_mix = d_ssd + u.shape[1]
    st = (n_seq_blk, d_ssd, D_STATE)
    stack = lambda n: pl.BlockSpec((n,) + st, lambda i: (0, i, 0, 0))
    prev_args = [] if layer == 0 else [prev]
    return pl.pallas_call(
        functools.partial(_mixer_kernel, sample=True, n_inner=1, n_prev=layer),
        grid=(n_blk,),
        in_specs=[tok(z.shape[1]), tok(xbc.shape[1]), tok(dt.shape[1]), tok(u.shape[1]), tok(v.shape[1]),
                  pl.BlockSpec((CHUNK, cs_all.shape[1]), lambda i: (layer * n_blk + i, 0))]
        + [stack(layer) for _ in prev_args]
        + [pl.BlockSpec(st, lambda i: (layer * n_blk + i, 0, 0))]
        + [_layer_spec(a, layer) for a in params],
        out_specs=[pl.BlockSpec((CHUNK, d_mix), lambda i: (i, 0)), stack(layer + 1)],
        out_shape=[jax.ShapeDtypeStruct((n_rows, d_mix), BF16),
                   jax.ShapeDtypeStruct((layer + 1, n_blk * n_seq_blk, d_ssd, D_STATE), F32)],
        scratch_shapes=[pltpu.VMEM((8, xbc.shape[1]), F32), pltpu.VMEM((CHUNK, d_ssd), F32)],
        compiler_params=_params(1),
        name="mixer_sample",
    )(z, xbc, dt, u, v, cs_all, *prev_args, s0_all, *params)


def _first_argmax(vals):
    m = vals[0]
    for v in vals[1:]:
        m = jnp.maximum(m, v)
    idx = jnp.full(m.shape, len(vals) - 1, jnp.int32)
    for k in range(len(vals) - 2, -1, -1):
        idx = jnp.where(vals[k] >= m, k, idx)
    return m, idx


def _slab_store(slab_ref, x, pitch, tok0=0):
    for k in range(x.shape[1] // LANES):
        slab_ref[pl.ds(tok0 * pitch + k, x.shape[0], stride=pitch), :] = x[:, k * LANES:(k + 1) * LANES]


def _slab_load(slab_ref, rows, n_pieces, pitch, tok0=0):
    return jnp.concatenate([slab_ref[pl.ds(tok0 * pitch + k, rows, stride=pitch), :] for k in range(n_pieces)],
                           axis=1)


def _post_kernel(yp_ref, ys_ref, hp_ref, hs_ref, *refs, n_prompt_tiles):
    i = pl.program_id(0)
    tm = hp_ref.shape[0]
    carry_ref, earlier_ref = refs[-2:]

    @pl.when(i == 0)
    def _():
        carry_ref[...] = jnp.zeros(carry_ref.shape, F32)
        tr = lax.broadcasted_iota(jnp.int32, (tm, tm), 0)
        tc = lax.broadcasted_iota(jnp.int32, (tm, tm), 1)
        earlier_ref[...] = jnp.where(tr < tc, 1.0, 0.0).astype(BF16)

    @pl.when(i < n_prompt_tiles)
    def _():
        _post_tile(yp_ref, hp_ref, *refs)

    @pl.when(i >= n_prompt_tiles)
    def _():
        _post_tile(ys_ref, hs_ref, *refs)


def _post_tile(y_ref, h_ref, wo_ref, nf_ref, wr_ref, br_ref, slab_ref, bkt_ref, rnk_ref, cnt_ref,
               carry_ref, earlier_ref):
    tm, d = h_ref.shape
    h1 = h_ref[...] + _dot(y_ref[...], wo_ref[...])
    _slab_store(slab_ref, h1, X_PITCH)

    t = _rms(h1, nf_ref[...])
    wr = wr_ref[...]
    wr_hi = wr.astype(BF16)
    wr_lo = (wr - wr_hi.astype(F32)).astype(BF16)
    t_hi = t.astype(BF16)
    t_lo = (t - t_hi.astype(F32)).astype(BF16)
    by_hi = _dot_nt(jnp.concatenate([wr_hi, wr_lo], axis=0), t_hi)
    logits = (by_hi[:ROUTE_ROWS] + (by_hi[ROUTE_ROWS:] + _dot_nt(wr_hi, t_lo))) + br_ref[...]
    lg = [logits[k:k + 1, :] for k in range(N_EXPERT_GROUPS)]
    m, g = _first_argmax(lg)
    ssum = jnp.exp(lg[0] - m)
    for k in range(1, N_EXPERT_GROUPS):
        ssum = ssum + jnp.exp(lg[k] - m)
    p_sel = 1.0 / ssum
    le = [logits[N_EXPERT_GROUPS + e:N_EXPERT_GROUPS + e + 1, :] for e in range(N_EXPERT_GROUPS * EXPERTS_PER_GROUP)]
    a = []
    for k in range(EXPERTS_PER_GROUP):
        sel = le[(N_EXPERT_GROUPS - 1) * EXPERTS_PER_GROUP + k]
        for gi in range(N_EXPERT_GROUPS - 2, -1, -1):
            sel = jnp.where(g == gi, le[gi * EXPERTS_PER_GROUP + k], sel)
        a.append(sel)
    v1, i1 = _first_argmax(a)
    a2 = [jnp.where(i1 == k, -jnp.inf, a[k]) for k in range(EXPERTS_PER_GROUP)]
    v2, i2 = _first_argmax(a2)
    e2 = jnp.exp(v2 - v1)
    den = 1.0 + e2
    g1 = (1.0 / den) * p_sel
    g2 = (e2 / den) * p_sel
    lo = jnp.minimum(i1, i2)
    hi = jnp.maximum(i1, i2)
    pair = jnp.where(lo == 0, jnp.where(hi == 1, 0, hi), jnp.where(lo == 1, jnp.where(hi == 2, 1, 4), 5))
    first = jnp.where(pair == 0, 0, jnp.where(pair <= 2, 2, 3))
    c_lo = jnp.where(first == i1, g1, g2)
    c_hi = jnp.where(first == i1, g2, g1)
    bucket = g * N_PAIRS + pair

    brow = lax.broadcasted_iota(jnp.int32, (ROUTE_ROWS, tm), 0)
    onehot = jnp.where(brow == bucket, 1.0, 0.0)
    prefix = _dot(onehot.astype(BF16), earlier_ref[...])
    carry = carry_ref[:, 0:1]
    rank = jnp.sum(onehot * (prefix + carry), axis=0, keepdims=True)
    carry = carry + jnp.sum(onehot, axis=1, keepdims=True)
    carry_b = jnp.broadcast_to(carry, carry_ref.shape)
    carry_ref[...] = carry_b
    cnt_ref[...] = carry_b

    bkt_ref[0] = bucket
    rnk_ref[0] = rank.astype(jnp.int32)
    ar = lax.broadcasted_iota(jnp.int32, (LANES, tm), 0)
    aux = jnp.where(ar == 0, c_lo, jnp.where(ar == 1, c_hi, 0.0))
    slab_ref[pl.ds(d // LANES, tm, stride=X_PITCH), :] = aux.T


def _post(yp, ys, hp, hs, params, layer):
    d = hp.shape[1]
    t = hp.shape[0] + hs.shape[0]
    n_prompt_tiles = hp.shape[0] // TOK_TILE
    return pl.pallas_call(
        functools.partial(_post_kernel, n_prompt_tiles=n_prompt_tiles),
        grid=(t // TOK_TILE,),
        in_specs=_pair_specs(d, n_prompt_tiles) + _pair_specs(d, n_prompt_tiles)
        + [_layer_spec(a, layer) for a in params],
        out_specs=[pl.BlockSpec((TOK_TILE * X_PITCH, LANES), lambda i: (i, 0)),
                   pl.BlockSpec((1, 1, TOK_TILE), lambda i: (i, 0, 0)),
                   pl.BlockSpec((1, 1, TOK_TILE), lambda i: (i, 0, 0)),
                   pl.BlockSpec((ROUTE_ROWS, LANES), lambda i: (0, 0))],
        out_shape=[jax.ShapeDtypeStruct((t * X_PITCH, LANES), F32),
                   jax.ShapeDtypeStruct((t // TOK_TILE, 1, TOK_TILE), jnp.int32),
                   jax.ShapeDtypeStruct((t // TOK_TILE, 1, TOK_TILE), jnp.int32),
                   jax.ShapeDtypeStruct((ROUTE_ROWS, LANES), F32)],
        scratch_shapes=[pltpu.VMEM((ROUTE_ROWS, LANES), F32), pltpu.VMEM((TOK_TILE, TOK_TILE), BF16)],
        compiler_params=_params(1),
        name="post",
    )(yp, ys, hp, hs, *params)


def _scatter_kernel(starts_ref, bkt_ref, rnk_ref, src_ref, _, dst_ref, pos_ref, sem, *, pitch):
    n_tok = src_ref.shape[0] // pitch
    for r in range(n_tok):
        slot = starts_ref[bkt_ref[0, 0, r]] + rnk_ref[0, 0, r]
        pos_ref[0, 0, r] = slot
        pltpu.make_async_copy(src_ref.at[pl.ds(r * pitch, pitch)],
                              dst_ref.at[pl.ds(slot * pitch, pitch)], sem).start(priority=r % 2)
    pltpu.make_async_copy(src_ref, dst_ref.at[pl.ds(0, n_tok * pitch)], sem).wait()


def _scatter_tokens(starts, bkt, rnk, src, n_out, pitch, init=None):
    hbm = pl.BlockSpec(memory_space=pl.ANY)
    tile_smem = pl.BlockSpec((1, 1, TOK_TILE), lambda i: (i, 0, 0), memory_space=pltpu.SMEM)
    if init is None:
        init = jnp.zeros((n_out * pitch, LANES), src.dtype)
    return pl.pallas_call(
        functools.partial(_scatter_kernel, pitch=pitch),
        grid=(bkt.shape[0],),
        in_specs=[pl.BlockSpec(memory_space=pltpu.SMEM), tile_smem, tile_smem,
                  pl.BlockSpec((TOK_TILE * pitch, LANES), lambda i: (i, 0)), hbm],
        out_specs=[hbm, tile_smem],
        out_shape=[jax.ShapeDtypeStruct((n_out * pitch, LANES), src.dtype),
                   jax.ShapeDtypeStruct(bkt.shape, jnp.int32)],
        scratch_shapes=[pltpu.SemaphoreType.DMA(())],
        input_output_aliases={4: 0},
        compiler_params=_params(1),
        name="scatter_tokens",
    )(starts, bkt, rnk, src, init)


def _moe_kernel(elo_ref, ehi_ref, nused_ref, new_lo_ref, new_hi_ref, x_ref, nf_ref, *refs):
    f32_lo, f32_hi, o_ref, (wg_lo, wu_lo, wd_lo), (wg_hi, wu_hi, wd_hi) = (
        refs[0:3], refs[3:6], refs[6], refs[7:10], refs[10:13])
    i = pl.program_id(0)
    n_pieces = nf_ref.shape[1] // LANES

    @pl.when(i < nused_ref[0])
    def _():
        for new_ref, srcs, dsts in ((new_lo_ref, f32_lo, (wg_lo, wu_lo, wd_lo)),
                                    (new_hi_ref, f32_hi, (wg_hi, wu_hi, wd_hi))):
            @pl.when(new_ref[i] == 1)
            def _():
                for src, dst in zip(srcs, dsts):
                    dst[...] = src[...].astype(BF16)

        x = _slab_load(x_ref, MOE_TILE, n_pieces, X_PITCH)
        gates = x_ref[pl.ds(n_pieces, MOE_TILE, stride=X_PITCH), :]
        c_lo = gates[:, 0:1]
        c_hi = gates[:, 1:2]
        t = _rms(x, nf_ref[...]).astype(BF16)

        def expert(wg, wu, wd):
            gate = _dot(t, wg[...])
            hid = (gate * _sigmoid(gate)) * _dot(t, wu[...])
            return _dot(hid.astype(BF16), wd[...])

        y = c_lo * expert(wg_lo, wu_lo, wd_lo)
        y = y + c_hi * expert(wg_hi, wu_hi, wd_hi)
        _slab_store(o_ref, x + y, Y_PITCH)
        o_ref[pl.ds(TOKEN_ROWS, MOE_TILE, stride=Y_PITCH), :] = jnp.zeros((MOE_TILE, LANES), F32)

    @pl.when(i >= nused_ref[0])
    def _():
        o_ref[...] = jnp.zeros(o_ref.shape, F32)


def _moe(tables, xs, nf, wg, wu, wd, layer):
    n_tiles = tables[0].shape[0]
    d, de = wg.shape[1:]
    lo = lambda i, elo, ehi, nu, nlo, nhi: (elo[i], 0, 0)
    hi = lambda i, elo, ehi, nu, nlo, nhi: (ehi[i], 0, 0)
    wspec = lambda shape, index_map: pl.BlockSpec((None,) + shape, index_map)
    grid_spec = pltpu.PrefetchScalarGridSpec(
        num_scalar_prefetch=len(tables),
        grid=(n_tiles,),
        in_specs=[pl.BlockSpec((MOE_TILE * X_PITCH, LANES),
                               lambda i, elo, ehi, nu, nlo, nhi: (jnp.maximum(jnp.minimum(i, nu[0] - 1), 0), 0)),
                  _layer_spec(nf, layer),
                  wspec((d, de), lo), wspec((d, de), lo), wspec((de, d), lo),
                  wspec((d, de), hi), wspec((d, de), hi), wspec((de, d), hi)],
        out_specs=pl.BlockSpec((MOE_TILE * Y_PITCH, LANES), lambda i, elo, ehi, nu, nlo, nhi: (i, 0)),
        scratch_shapes=[pltpu.VMEM(s, BF16) for s in ((d, de), (d, de), (de, d)) * 2],
    )
    return pl.pallas_call(
        _moe_kernel,
        grid_spec=grid_spec,
        out_shape=jax.ShapeDtypeStruct((n_tiles * MOE_TILE * Y_PITCH, LANES), F32),
        compiler_params=_params(1),
        name="expert_pairs",
    )(*tables, xs, nf, wg, wu, wd, wg, wu, wd)


def _ple_kernel(pos_ref, pos_next_ref, h_hbm, pp_ref, ps_ref, wple_ref, nple_ref, npg_ref, wpg_ref, *refs,
                final, n_prompt_tiles):
    refs, (xbuf, sem) = refs[:-2], refs[-2:]
    i = pl.program_id(0)
    last = pl.num_programs(0) - 1
    slot = i % 2

    def start_tile(p_ref, s):
        for r in range(TOK_TILE):
            pltpu.make_async_copy(h_hbm.at[pl.ds(p_ref[0, 0, r] * Y_PITCH, Y_PITCH)],
                                  xbuf.at[s, pl.ds(r * Y_PITCH, Y_PITCH)], sem.at[s]).start()

    def wait_tile(s):
        pltpu.make_async_copy(h_hbm.at[pl.ds(0, TOK_TILE * Y_PITCH)], xbuf.at[s], sem.at[s]).wait()

    @pl.when(i == 0)
    def _():
        start_tile(pos_ref, 0)

    wait_tile(slot)
    start_tile(pos_next_ref, 1 - slot)
    h2 = _slab_load(xbuf.at[slot], TOK_TILE, TOKEN_ROWS, Y_PITCH)
    p = _pair_load(pp_ref, ps_ref, n_prompt_tiles)
    e = _rms(_dot(p.astype(BF16), wple_ref[...]), nple_ref[...])
    gate = _sigmoid(_dot(_rms(h2, npg_ref[...]).astype(BF16), wpg_ref[...]))
    h3 = h2 + gate * e
    if final:
        nfin_ref, op_ref, os_ref = refs
        out = _rms(h3, nfin_ref[...])
    else:
        op_ref, os_ref = refs[N_INPROJ_PARAMS:N_INPROJ_PARAMS + 2]
        out = h3
        _inproj_tile(h3, *refs[:N_INPROJ_PARAMS], *refs[N_INPROJ_PARAMS + 2:])

    @pl.when(i < n_prompt_tiles)
    def _():
        op_ref[...] = out

    @pl.when(i >= n_prompt_tiles)
    def _():
        os_ref[...] = out

    @pl.when(i == last)
    def _():
        wait_tile(1 - slot)


def _ple(pos, h2s, pp_all, ps_all, params, tail, final, layer, n_prompt):
    n_steps = pos.shape[0]
    t = n_steps * TOK_TILE
    d = TOKEN_ROWS * LANES
    tile_smem = pl.BlockSpec((1, 1, TOK_TILE), lambda i: (i, 0, 0), memory_space=pltpu.SMEM)
    next_smem = pl.BlockSpec((1, 1, TOK_TILE), lambda i: (jnp.minimum(i + 1, n_steps - 1), 0, 0),
                             memory_space=pltpu.SMEM)
    n_prompt_tiles = n_prompt // TOK_TILE
    n_sample_tiles = n_steps - n_prompt_tiles
    out_specs = _pair_specs(d, n_prompt_tiles)
    out_shape = [jax.ShapeDtypeStruct((n_prompt, d), F32), jax.ShapeDtypeStruct((t - n_prompt, d), F32)]
    if final:
        tail = (tail,)
        tail_specs = [pl.BlockSpec(tail[0].shape, lambda i: (0,) * tail[0].ndim)]
    else:
        tail_specs, io, ish = _inproj_specs(tail, t, layer + 1)
        out_specs, out_shape = out_specs + io, out_shape + ish
    return pl.pallas_call(
        functools.partial(_ple_kernel, final=final, n_prompt_tiles=n_prompt_tiles),
        grid=(n_steps,),
        in_specs=[tile_smem, next_smem, pl.BlockSpec(memory_space=pl.ANY)]
        + _pair_specs(pp_all.shape[1], n_prompt_tiles, layer * n_prompt_tiles, layer * n_sample_tiles)
        + [_layer_spec(a, layer) for a in params] + tail_specs,
        out_specs=out_specs,
        out_shape=out_shape,
        scratch_shapes=[pltpu.VMEM((2, TOK_TILE * Y_PITCH, LANES), F32), pltpu.SemaphoreType.DMA((2,))],
        compiler_params=_params(1),
        name="ple_final" if final else "ple_inproj",
    )(pos, pos, h2s, pp_all, ps_all, *params, *tail)


_PAIR_FIRST = (0, 2, 2, 3, 3, 3)
_PAIR_SECOND = (1, 1, 0, 0, 1, 2)


def _routing_tables(cnt, n_tiles, expert0):
    counts = cnt[:N_BUCKETS, 0].astype(jnp.int32)
    padded = ((counts + MOE_TILE - 1) // MOE_TILE) * MOE_TILE
    ends = jnp.cumsum(padded)
    starts = jnp.pad(ends - padded, (0, ROUTE_ROWS - N_BUCKETS))
    n_used = ends[-1] // MOE_TILE
    tile = jnp.minimum(jnp.arange(n_tiles, dtype=jnp.int32), n_used - 1)
    tb = jnp.sum((ends[None, :] <= (tile * MOE_TILE)[:, None]).astype(jnp.int32), axis=1)
    tb = jnp.minimum(tb, N_BUCKETS - 1)
    grp = tb // N_PAIRS
    pr = tb % N_PAIRS
    e_lo = expert0 + grp * EXPERTS_PER_GROUP + jnp.asarray(_PAIR_FIRST, jnp.int32)[pr]
    e_hi = expert0 + grp * EXPERTS_PER_GROUP + jnp.asarray(_PAIR_SECOND, jnp.int32)[pr]
    is_new = lambda e: jnp.concatenate([jnp.ones((1,), jnp.int32), (e[1:] != e[:-1]).astype(jnp.int32)])
    return starts, (e_lo, e_hi, n_used.reshape(1), is_new(e_lo), is_new(e_hi))


def kernel(x_prompt, x_sample, state_conv, state_ssm, p_prompt, p_sample, norm_mix, w_in, conv_w, conv_b,
           dt_bias, a_log, d_skip, ssd_norm, gmlp_norm, w_spatial, b_spatial, w_out, norm_ffn,
           w_router_group, b_router_group, w_router_expert, b_router_expert, w_gate, w_up, w_down,
           w_ple, norm_ple, norm_pg, w_pg, norm_final):
    n_seq, seq_len, d_model = x_prompt.shape
    n_dec, dec_seq, _ = x_sample.shape
    depth = w_in.shape[0]
    conv_dim = conv_w.shape[2]
    d_ssd = SSD_HEADS * SSD_HEAD_DIM
    d_gmlp = gmlp_norm.shape[1]
    assert dec_seq == DEC_SEQ_ROWS and conv_w.shape[1] == CONV_WIDTH and seq_len % CHUNK == 0
    assert conv_dim == d_ssd + 2 * SSD_GROUPS * D_STATE and w_spatial.shape[1:] == (GMLP_HEADS, CHUNK, CHUNK)
    n_chunk = seq_len // CHUNK
    n_prompt = n_seq * seq_len
    n_srows = n_dec * GROUP
    t_all = n_prompt + n_srows
    assert n_prompt % TOK_TILE == 0 and n_srows % TOK_TILE == 0
    assert d_model == TOKEN_ROWS * LANES
    lead = GROUP - dec_seq
    n_moe_tiles = t_all // MOE_TILE + N_BUCKETS

    def sample_rows(a):
        pad = [(0, 0)] * (a.ndim - 2) + [(lead, 0), (0, 0)]
        return jnp.pad(a, pad).reshape(-1, a.shape[-1])

    hp, hs = x_prompt.reshape(n_prompt, d_model), sample_rows(x_sample)

    vec = lambda a: a.astype(F32).reshape(depth, 1, -1)
    lane_pad = ((0, 0), (0, LANES - SSD_HEADS))
    o_dt = d_ssd + conv_dim
    o_uv = o_dt + SSD_HEADS
    w_cols = lambda a, b: w_in[..., a:b].astype(BF16)
    inproj_params = (vec(norm_mix), w_cols(0, d_ssd), w_cols(d_ssd, o_dt),
                     jnp.pad(w_in[..., o_dt:o_uv], ((0, 0),) + lane_pad).astype(BF16),
                     w_cols(o_uv, o_uv + d_gmlp), w_cols(o_uv + d_gmlp, o_uv + 2 * d_gmlp),
                     jnp.pad(dt_bias.astype(F32), lane_pad).reshape(depth, 1, LANES), vec(gmlp_norm))

    head_cols = jnp.arange(d_ssd) // SSD_HEAD_DIM
    per_head = d_gmlp // GMLP_HEADS
    ws_tril = jnp.where(jnp.tril(jnp.ones((CHUNK, CHUNK), bool)), w_spatial, 0.0)
    bsp_p = jnp.repeat(jnp.swapaxes(b_spatial, 1, 2), per_head, axis=2)
    w8 = jnp.pad(ws_tril[:, :, :dec_seq, :dec_seq], ((0, 0), (0, 0), (lead, 0), (lead, 0)))
    wsp_s = jnp.tile(w8, (1, 1, 1, CHUNK // GROUP))
    b8 =jnp.pad(b_spatial[:, :, :dec_seq], ((0, 0), (0, 0), (lead, 0)))
    bsp_s = jnp.repeat(jnp.swapaxes(jnp.tile(b8, (1, 1, CHUNK // GROUP)), 1, 2), per_head, axis=2)
    mixer_common = (conv_w.astype(F32), vec(conv_b),
                    jnp.pad(-jnp.exp(a_log.astype(F32)), lane_pad).reshape(depth, 1, LANES),
                    d_skip.astype(F32)[:, head_cols].reshape(depth, 1, d_ssd), vec(ssd_norm))
    mixer_p_params = mixer_common + (ws_tril.astype(BF16), bsp_p)
    mixer_s_params = mixer_common + (wsp_s.astype(F32), bsp_s)

    n_route_pad = ROUTE_ROWS - N_EXPERT_GROUPS * (1 + EXPERTS_PER_GROUP)
    wr = jnp.concatenate([jnp.swapaxes(w_router_group, 1, 2), jnp.swapaxes(w_router_expert, 1, 2),
                          jnp.zeros((depth, n_route_pad, d_model), F32)], axis=1).astype(F32)
    br = jnp.concatenate([b_router_group, b_router_expert, jnp.zeros((depth, n_route_pad), F32)],
                         axis=1).astype(F32).reshape(depth, ROUTE_ROWS, 1)
    post_params = (w_out.astype(BF16), vec(norm_ffn), wr, br)
    ple_params = (w_ple.astype(BF16), vec(norm_ple), vec(norm_pg), w_pg.astype(BF16))

    n_exp = w_gate.shape[1]
    wg_all = w_gate.astype(F32).reshape((depth * n_exp,) + w_gate.shape[2:])
    wu_all = w_up.astype(F32).reshape((depth * n_exp,) + w_up.shape[2:])
    wd_all = w_down.astype(F32).reshape((depth * n_exp,) + w_down.shape[2:])
    s0_all = state_ssm.reshape(depth * n_dec, d_ssd, D_STATE)
    tail = CONV_WIDTH - 1
    cs_all = jnp.pad(state_conv, ((0, 0), (0, 0), (lead - tail, dec_seq), (0, 0))).reshape(depth * n_srows, conv_dim)
    pp_all = p_prompt.reshape(depth * n_prompt, p_prompt.shape[-1])
    ps_all = sample_rows(p_sample)

    xs = ssm_s = None
    convs_p, ssms_p, convs_s, vs_s = [], [], [], []
    z, xbc, dt, u, v = _inproj(hp, hs, inproj_params, 0)

    for i in range(depth):
        yp, ssm_p = _mixer_prompt(z, xbc, dt, u, v, mixer_p_params, n_seq, n_chunk, i)
        ys, ssm_s = _mixer_sample(z, xbc, dt, u, v, cs_all, s0_all, ssm_s, mixer_s_params, n_prompt, n_srows, i)

        slab, bkt, rnk, cnt = _post(yp, ys, hp, hs, post_params, i)
        starts, expert_tables = _routing_tables(cnt, n_moe_tiles, i * n_exp)

        xs, pos = _scatter_tokens(starts, bkt, rnk, slab, n_moe_tiles * MOE_TILE, X_PITCH, init=xs)
        h2s = _moe(expert_tables, xs, post_params[1], wg_all, wu_all, wd_all, i)

        convs_p.append(jnp.concatenate([xbc[(b + 1) * seq_len - tail:(b + 1) * seq_len] for b in range(n_seq)])
                       .reshape(n_seq, tail, conv_dim))
        xbc_s = xbc[n_prompt:].reshape(n_dec, GROUP, conv_dim)
        convs_s.append(xbc_s[:, GROUP - tail:])
        ssms_p.append(ssm_p.reshape(n_seq, SSD_HEADS, SSD_HEAD_DIM, D_STATE))
        vs_s.append(v[n_prompt:].reshape(n_dec, GROUP, d_gmlp)[:, lead:])

        final = i == depth - 1
        outs = _ple(pos, h2s, pp_all, ps_all, ple_params,
                    norm_final.astype(F32).reshape(1, -1) if final else inproj_params, final, i, n_prompt)
        hp, hs = outs[:2]
        if not final:
            z, xbc, dt, u, v = outs[2:]

    y_prompt = hp.reshape(n_seq, seq_len, d_model)
    y_sample = hs.reshape(n_dec, GROUP, d_model)[:, lead:]
    new_ssm_sample = ssm_s.reshape(depth, n_dec, SSD_HEADS, SSD_HEAD_DIM, D_STATE)
    return (y_prompt, y_sample, jnp.stack(convs_p), jnp.stack(ssms_p), jnp.stack(convs_s), new_ssm_sample,
            jnp.stack(vs_s))
```

```python
import functools
import math

import jax
import jax.numpy as jnp
from jax import lax
from jax.experimental import pallas as pl
from jax.experimental.pallas import tpu as pltpu

F32 = jnp.float32
BF16 = jnp.bfloat16

LANES = 128
VMEM_LIMIT_BYTES = 56 * 1024 * 1024

CONV_WIDTH = 4
SSD_HEADS = 8
SSD_HEAD_DIM = 64
SSD_GROUPS = 2
D_STATE = 128
CHUNK = 128
GMLP_HEADS = 8
N_EXPERT_GROUPS = 4
EXPERTS_PER_GROUP = 4
N_PAIRS = 6
N_BUCKETS = N_EXPERT_GROUPS * N_PAIRS
EPS = 1e-6

GROUP = 8
DEC_SEQ_ROWS = 4
TOK_TILE = 512
MOE_TILE = 512
PROMPT_TILES_PER_STEP = 8
ROUTE_ROWS = 32
TOKEN_ROWS = 8
X_PITCH = TOKEN_ROWS + 1
Y_PITCH = TOKEN_ROWS + 1
NEG_BIG = -1e30


def _dot(a, b):
    return jnp.dot(a, b, preferred_element_type=F32)


def _dot_nt(a, b):
    return lax.dot_general(a, b, (((1,), (1,)), ((), ())), preferred_element_type=F32)


def _rms(x, g):
    ms = jnp.mean(x * x, axis=-1, keepdims=True)
    return (x * lax.rsqrt(ms + EPS)) * g


def _gelu(x):
    return 0.5 * x * (1.0 + lax.erf(x * (1.0 / math.sqrt(2.0))))


def _sigmoid(x):
    return 0.5 * jnp.tanh(0.5 * x) + 0.5


def _softplus(x):
    return jnp.maximum(x, 0.0) + jnp.log1p(jnp.exp(-jnp.abs(x)))


def _params(n_grid):
    return pltpu.CompilerParams(dimension_semantics=("arbitrary",) * n_grid,
                                vmem_limit_bytes=VMEM_LIMIT_BYTES)


def _pair_specs(width, n_prompt_tiles, prompt_tile0=0, sample_tile0=0):
    return [pl.BlockSpec((TOK_TILE, width), lambda i: (prompt_tile0 + jnp.minimum(i, n_prompt_tiles - 1), 0)),
            pl.BlockSpec((TOK_TILE, width), lambda i: (sample_tile0 + jnp.maximum(i - n_prompt_tiles, 0), 0))]


def _pair_load(p_ref, s_ref, n_prompt_tiles):
    return jnp.where(pl.program_id(0) < n_prompt_tiles, p_ref[...], s_ref[...])


def _layer_spec(a, layer):
    zeros = (0,) * (a.ndim - 1)
    return pl.BlockSpec((None,) + a.shape[1:], lambda *_: (layer,) + zeros)


def _inproj_tile(h, nm_ref, wz_ref, wx_ref, wdt_ref, wu_ref, wv_ref, dtb_ref, gn_ref,
                 z_ref, xbc_ref, dt_ref, u_ref, v_ref):
    a = _rms(h, nm_ref[...]).astype(BF16)
    z_ref[...] = _dot(a, wz_ref[...])
    xbc_ref[...] = _dot(a, wx_ref[...])
    dt_ref[...] = _softplus(_dot(a, wdt_ref[...]) + dtb_ref[...])
    u_ref[...] = _gelu(_dot(a, wu_ref[...]))
    v_ref[...] = _rms(_gelu(_dot(a, wv_ref[...])), gn_ref[...])


def _inproj_kernel(hp_ref, hs_ref, *refs, n_prompt_tiles):
    _inproj_tile(_pair_load(hp_ref, hs_ref, n_prompt_tiles), *refs)


N_INPROJ_PARAMS = 8


def _inproj_specs(weights, t, layer):
    assert len(weights) == N_INPROJ_PARAMS
    widths = [w.shape[-1] for w in weights[1:6]]
    row = lambda w: pl.BlockSpec((TOK_TILE, w), lambda i: (i, 0))
    return ([_layer_spec(a, layer) for a in weights], [row(w) for w in widths],
            [jax.ShapeDtypeStruct((t, w), F32) for w in widths])


def _inproj(hp, hs, weights, layer):
    t = hp.shape[0] + hs.shape[0]
    n_prompt_tiles = hp.shape[0] // TOK_TILE
    w_specs, out_specs, out_shape = _inproj_specs(weights, t, layer)
    return pl.pallas_call(
        functools.partial(_inproj_kernel, n_prompt_tiles=n_prompt_tiles),
        grid=(t // TOK_TILE,),
        in_specs=_pair_specs(hp.shape[1], n_prompt_tiles) + w_specs,
        out_specs=out_specs,
        out_shape=out_shape,
        compiler_params=_params(1),
        name="inproj",
    )(hp, hs, *weights)


def _seg_cumsum(x, seg, rowmod):
    d = 1
    while d < seg:
        x = x + jnp.where(rowmod >= d, pltpu.roll(x, d, axis=0), 0.0)
        d *= 2
    return x


def _seg_rev_cumsum(x, seg, rowmod):
    n = x.shape[0]
    d = 1
    while d < seg:
        x = x + jnp.where(rowmod + d < seg, pltpu.roll(x, n - d, axis=0), 0.0)
        d *= 2
    return x


def _expand_heads(m, lane_lt_half):
    parts = []
    for j in range(SSD_HEADS // 2):
        parts.append(jnp.where(lane_lt_half, m[:, 2 * j:2 * j + 1], m[:, 2 * j + 1:2 * j + 2]))
    return jnp.concatenate(parts, axis=1)


def _merge_head_pairs(per_head, lane_lt_half):
    parts = [jnp.where(lane_lt_half, per_head[2 * j], per_head[2 * j + 1]) for j in range(len(per_head) // 2)]
    return jnp.concatenate(parts, axis=1)


def _mixer_kernel(*refs, sample, n_inner, n_prev=0):
    n_tok = 6 if sample else 5
    tok_refs, rest = refs[:n_tok], refs[n_tok:]
    if sample:
        if n_prev:
            prev_ref, rest = rest[0], rest[1:]
        (s0_ref, cw_ref, cb_ref, arow_ref, dsk_ref, sn_ref, wsp_ref, bsp_ref,
         y_ref, stack_ref, ext_ref, yoff_ref) = rest
        if n_prev:
            stack_ref[0:n_prev] = prev_ref[...]
        sout_ref = stack_ref.at[n_prev]
        first = pl.program_id(0) == 0
    else:
        (cw_ref, cb_ref, arow_ref, dsk_ref, sn_ref, wsp_ref, bsp_ref, y_ref, sout_ref, ext_ref) = rest
        s0_ref = yoff_ref = None
        first = pl.program_id(1) == 0

    @pl.when(first)
    def _():
        ext_ref[...] = jnp.zeros(ext_ref.shape, F32)
        if not sample:
            sout_ref[...] = jnp.zeros(sout_ref.shape, F32)

    def tile(c, carry):
        r0 = pl.multiple_of(c * CHUNK, CHUNK)
        views = [r.at[pl.ds(r0, CHUNK)] for r in tok_refs + (y_ref,)]
        _mixer_tile(*views[:n_tok], s0_ref, cw_ref, cb_ref, arow_ref, dsk_ref, sn_ref, wsp_ref, bsp_ref,
                    views[n_tok], sout_ref, ext_ref, yoff_ref, sample=sample)
        return carry

    if n_inner == 1:
        tile(0, 0)
    else:
        lax.fori_loop(0, n_inner, tile, 0)


def _mixer_tile(z_ref, xbc_ref, dt_ref, u_ref, v_ref, *rest, sample):
    if sample:
        cs_ref, s0_ref = rest[0], rest[1]
        rest = rest[2:]
    else:
        rest = rest[1:]
    cw_ref, cb_ref, arow_ref, dsk_ref, sn_ref, wsp_ref, bsp_ref, y_ref, sout_ref, ext_ref, yoff_ref = rest
    seg = GROUP if sample else CHUNK
    d_ssd = SSD_HEADS * SSD_HEAD_DIM
    gw = d_ssd // SSD_GROUPS
    hpg = SSD_HEADS // SSD_GROUPS
    cs_first = GROUP - DEC_SEQ_ROWS - (CONV_WIDTH - 1)

    rows = lax.broadcasted_iota(jnp.int32, (CHUNK, LANES), 0)
    cols = lax.broadcasted_iota(jnp.int32, (CHUNK, LANES), 1)
    rowmod = rows & (seg - 1)
    lane_lt_half = cols < SSD_HEAD_DIM

    xbc = xbc_ref[...]
    if sample:
        rm = lax.broadcasted_iota(jnp.int32, xbc.shape, 0) & (GROUP - 1)
        xbc = jnp.where((rm >= cs_first) & (rm < cs_first + CONV_WIDTH - 1), cs_ref[...], xbc)

    tail = ext_ref[...]
    row8 = lax.broadcasted_iota(jnp.int32, tail.shape, 0)
    acc = cb_ref[...] + cw_ref[CONV_WIDTH - 1:CONV_WIDTH, :] * xbc
    for j in range(1, CONV_WIDTH):
        rolled = pltpu.roll(xbc, j, axis=0)
        head = jnp.where(row8 < j, pltpu.roll(tail, j, axis=0), rolled[0:8, :])
        shifted = jnp.concatenate([head, rolled[8:, :]], axis=0)
        acc = acc + cw_ref[CONV_WIDTH - 1 - j:CONV_WIDTH - j, :] * shifted
    if not sample:
        ext_ref[...] = xbc[CHUNK - 8:, :]
    xc = acc * _sigmoid(acc)
    x = xc[:, :d_ssd]
    bb = xc[:, d_ssd:d_ssd + SSD_GROUPS * D_STATE].astype(BF16)
    cm = xc[:, d_ssd + SSD_GROUPS * D_STATE:]
    cbf = cm.astype(BF16)

    dtc = dt_ref[...]
    if sample:
        dtc = jnp.where(rowmod >= GROUP - DEC_SEQ_ROWS, dtc, 0.0)
    da = dtc * arow_ref[...]
    cum = _seg_cumsum(da, seg, rowmod)
    rev = _seg_rev_cumsum(da, seg, rowmod) - da
    cum_t = cum.T
    ecum = jnp.exp(cum)
    dt_e = _expand_heads(dtc, lane_lt_half)
    ecum_e = _expand_heads(ecum, lane_lt_half)
    erev_e = _expand_heads(jnp.exp(rev), lane_lt_half)

    xdt = x * dt_e
    xdt_bf = xdt.astype(BF16)
    causal = rows >= cols
    if sample:
        same_seq = (rows >> 3) == (cols >> 3)
        causal = causal & same_seq

    yd = []
    for g in range(SSD_GROUPS):
        cb_g = _dot_nt(cbf[:, g * D_STATE:(g + 1) * D_STATE], bb[:, g * D_STATE:(g + 1) * D_STATE])
        for hh in range(hpg):
            h = g * hpg + hh
            expo = cum[:, h:h + 1] - cum_t[h:h + 1, :]
            w = (cb_g * jnp.exp(jnp.where(causal, expo, NEG_BIG))).astype(BF16)
            j = h // 2
            yd.append(_dot(w, xdt_bf[:, j * LANES:(j + 1) * LANES]))
    y_diag = _merge_head_pairs(yd, lane_lt_half)

    if sample:
        for i in range(CHUNK // GROUP):
            for g in range(SSD_GROUPS):
                s_g = s0_ref[i, g * gw:(g + 1) * gw, :].astype(BF16)
                yoff_ref[i * GROUP:(i + 1) * GROUP, g * gw:(g + 1) * gw] = _dot_nt(
                    cm[i * GROUP:(i + 1) * GROUP, g * D_STATE:(g + 1) * D_STATE].astype(BF16), s_g)
        y_off = yoff_ref[...]
    else:
        y_off = jnp.concatenate(
            [_dot_nt(cbf[:, g * D_STATE:(g + 1) * D_STATE], sout_ref[0, g * gw:(g + 1) * gw, :].astype(BF16))
             for g in range(SSD_GROUPS)], axis=1)
    y = y_diag + y_off * ecum_e + dsk_ref[...] * x

    xd = xdt * erev_e
    for g in range(SSD_GROUPS):
        xd_t = xd[:, g * gw:(g + 1) * gw].T
        b_g = bb[:, g * D_STATE:(g + 1) * D_STATE]
        if sample:
            tcols = lax.broadcasted_iota(jnp.int32, xd_t.shape, 1) >> 3
            for i in range(CHUNK // GROUP):
                upd = _dot(jnp.where(tcols == i, xd_t, 0.0).astype(BF16), b_g)
                last = i * GROUP + GROUP - 1
                for hh in range(hpg):
                    h = g * hpg + hh
                    r0 = h * SSD_HEAD_DIM
                    sout_ref[i, r0:r0 + SSD_HEAD_DIM, :] = (
                        s0_ref[i, r0:r0 + SSD_HEAD_DIM, :] * ecum[last:last + 1, h:h + 1]
                        + upd[hh * SSD_HEAD_DIM:(hh + 1) * SSD_HEAD_DIM, :])
        else:
            upd = _dot(xd_t.astype(BF16), b_g)
            for hh in range(hpg):
                h = g * hpg + hh
                r0 = h * SSD_HEAD_DIM
                sout_ref[0, r0:r0 + SSD_HEAD_DIM, :] = (
                    sout_ref[0, r0:r0 + SSD_HEAD_DIM, :] * ecum[CHUNK - 1:CHUNK, h:h + 1]
                    + upd[hh * SSD_HEAD_DIM:(hh + 1) * SSD_HEAD_DIM, :])

    zf = z_ref[...]
    yf = y * (zf * _sigmoid(zf))
    parts = []
    for g in range(SSD_GROUPS):
        part = yf[:, g * gw:(g + 1) * gw]
        ms = jnp.mean(part * part, axis=-1, keepdims=True)
        parts.append(part * lax.rsqrt(ms + EPS))
    y_ssd = jnp.concatenate(parts, axis=1) * sn_ref[...]

    vb = v_ref[...].astype(BF16)
    if sample:
        n_grp = CHUNK // GROUP
        wms = [jnp.where(same_seq, jnp.tile(wsp_ref[h], (n_grp, 1)), 0.0).astype(BF16) for h in range(GMLP_HEADS)]
    else:
        wms = [wsp_ref[h] for h in range(GMLP_HEADS)]
    sg = [_dot(wms[h], vb[:, (h // 2) * LANES:(h // 2 + 1) * LANES]) for h in range(GMLP_HEADS)]
    s = _merge_head_pairs(sg, lane_lt_half) + bsp_ref[...]
    y_gm = u_ref[...] * s

    y_ref[:, :d_ssd] = y_ssd.astype(BF16)
    y_ref[:, d_ssd:] = y_gm.astype(BF16)


def _mixer_prompt(z, xbc, dt, u, v, params, n_seq, n_chunk, layer):
    n_inner = math.gcd(n_chunk, PROMPT_TILES_PER_STEP)
    n_outer = n_chunk // n_inner
    tok = lambda w: pl.BlockSpec((n_inner * CHUNK, w), lambda b, c: (b * n_outer + c, 0))
    d_ssd = z.shape[1]
    d_mix = d_ssd + u.shape[1]
    return pl.pallas_call(
        functools.partial(_mixer_kernel, sample=False, n_inner=n_inner),
        grid=(n_seq, n_outer),
        in_specs=[tok(z.shape[1]), tok(xbc.shape[1]), tok(dt.shape[1]), tok(u.shape[1]), tok(v.shape[1])]
        + [_layer_spec(a, layer) for a in params],
        out_specs=[tok(d_mix),
                   pl.BlockSpec((1, d_ssd, D_STATE), lambda b, c: (b, 0, 0))],
        out_shape=[jax.ShapeDtypeStruct((n_seq * n_chunk * CHUNK, d_mix), BF16),
                   jax.ShapeDtypeStruct((n_seq, d_ssd, D_STATE), F32)],
        scratch_shapes=[pltpu.VMEM((8, xbc.shape[1]), F32)],
        compiler_params=_params(2),
        name="mixer_prompt",
    )(z, xbc, dt, u, v, *params)


def _mixer_sample(z, xbc, dt, u, v, cs_all, s0_all, prev, params, row0, n_rows, layer):
    blk0 = row0 // CHUNK
    n_seq_blk = CHUNK // GROUP
    n_blk = n_rows // CHUNK
    tok = lambda w: pl.BlockSpec((CHUNK, w), lambda i: (blk0 + i, 0))
    d_ssd = z.shape[1]
    d_mix = d_ssd + u.shape[1]
    st = (n_seq_blk, d_ssd, D_STATE)
    stack = lambda n: pl.BlockSpec((n,) + st, lambda i: (0, i, 0, 0))
    prev_args = [] if layer == 0 else [prev]
    return pl.pallas_call(
        functools.partial(_mixer_kernel, sample=True, n_inner=1, n_prev=layer),
        grid=(n_blk,),
        in_specs=[tok(z.shape[1]), tok(xbc.shape[1]), tok(dt.shape[1]), tok(u.shape[1]), tok(v.shape[1]),
                  pl.BlockSpec((CHUNK, cs_all.shape[1]), lambda i: (layer * n_blk + i, 0))]
        + [stack(layer) for _ in prev_args]
        + [pl.BlockSpec(st, lambda i: (layer * n_blk + i, 0, 0))]
        + [_layer_spec(a, layer) for a in params],
        out_specs=[pl.BlockSpec((CHUNK, d_mix), lambda i: (i, 0)), stack(layer + 1)],
        out_shape=[jax.ShapeDtypeStruct((n_rows, d_mix), BF16),
                   jax.ShapeDtypeStruct((layer + 1, n_blk * n_seq_blk, d_ssd, D_STATE), F32)],
        scratch_shapes=[pltpu.VMEM((8, xbc.shape[1]), F32), pltpu.VMEM((CHUNK, d_ssd), F32)],
        compiler_params=_params(1),
        name="mixer_sample",
    )(z, xbc, dt, u, v, cs_all, *prev_args, s0_all, *params)


def _first_argmax(vals):
    m = vals[0]
    for v in vals[1:]:
        m = jnp.maximum(m, v)
    idx = jnp.full(m.shape, len(vals) - 1, jnp.int32)
    for k in range(len(vals) - 2, -1, -1):
        idx = jnp.where(vals[k] >= m, k, idx)
    return m, idx


def _slab_store(slab_ref, x, pitch, tok0=0):
    for k in range(x.shape[1] // LANES):
        slab_ref[pl.ds(tok0 * pitch + k, x.shape[0], stride=pitch), :] = x[:, k * LANES:(k + 1) * LANES]


def _slab_load(slab_ref, rows, n_pieces, pitch, tok0=0):
    return jnp.concatenate([slab_ref[pl.ds(tok0 * pitch + k, rows, stride=pitch), :] for k in range(n_pieces)],
                           axis=1)


def _post_kernel(yp_ref, ys_ref, hp_ref, hs_ref, wo_ref, nf_ref, wr_ref, br_ref,
                 slab_ref, bkt_ref, rnk_ref, cnt_ref, carry_ref, earlier_ref, *, n_prompt_tiles):
    i = pl.program_id(0)
    tm, d = hp_ref.shape

    @pl.when(i == 0)
    def _():
        carry_ref[...] = jnp.zeros(carry_ref.shape, F32)
        tr = lax.broadcasted_iota(jnp.int32, (tm, tm), 0)
        tc = lax.broadcasted_iota(jnp.int32, (tm, tm), 1)
        earlier_ref[...] = jnp.where(tr < tc, 1.0, 0.0).astype(BF16)

    ym = _pair_load(yp_ref, ys_ref, n_prompt_tiles)
    h1 = _pair_load(hp_ref, hs_ref, n_prompt_tiles) + _dot(ym, wo_ref[...])
    _slab_store(slab_ref, h1, X_PITCH)

    t = _rms(h1, nf_ref[...])
    wr = wr_ref[...]
    wr_hi = wr.astype(BF16)
    wr_lo = (wr - wr_hi.astype(F32)).astype(BF16)
    t_hi = t.astype(BF16)
    t_lo = (t - t_hi.astype(F32)).astype(BF16)
    by_hi = _dot_nt(jnp.concatenate([wr_hi, wr_lo], axis=0), t_hi)
    logits = (by_hi[:ROUTE_ROWS] + (by_hi[ROUTE_ROWS:] + _dot_nt(wr_hi, t_lo))) + br_ref[...]
    lg = [logits[k:k + 1, :] for k in range(N_EXPERT_GROUPS)]
    m, g = _first_argmax(lg)
    ssum = jnp.exp(lg[0] - m)
    for k in range(1, N_EXPERT_GROUPS):
        ssum = ssum + jnp.exp(lg[k] - m)
    p_sel = 1.0 / ssum
    le = [logits[N_EXPERT_GROUPS + e:N_EXPERT_GROUPS + e + 1, :] for e in range(N_EXPERT_GROUPS * EXPERTS_PER_GROUP)]
    a = []
    for k in range(EXPERTS_PER_GROUP):
        sel = le[(N_EXPERT_GROUPS - 1) * EXPERTS_PER_GROUP + k]
        for gi in range(N_EXPERT_GROUPS - 2, -1, -1):
            sel = jnp.where(g == gi, le[gi * EXPERTS_PER_GROUP + k], sel)
        a.append(sel)
    v1, i1 = _first_argmax(a)
    a2 = [jnp.where(i1 == k, -jnp.inf, a[k]) for k in range(EXPERTS_PER_GROUP)]
    v2, i2 = _first_argmax(a2)
    e2 = jnp.exp(v2 - v1)
    den = 1.0 + e2
    g1 = (1.0 / den) * p_sel
    g2 = (e2 / den) * p_sel
    lo = jnp.minimum(i1, i2)
    hi = jnp.maximum(i1, i2)
    pair = jnp.where(lo == 0, jnp.where(hi == 1, 0, hi), jnp.where(lo == 1, jnp.where(hi == 2, 1, 4), 5))
    first = jnp.where(pair == 0, 0, jnp.where(pair <= 2, 2, 3))
    c_lo = jnp.where(first == i1, g1, g2)
    c_hi = jnp.where(first == i1, g2, g1)
    bucket = g * N_PAIRS + pair

    brow = lax.broadcasted_iota(jnp.int32, (ROUTE_ROWS, tm), 0)
    onehot = jnp.where(brow == bucket, 1.0, 0.0)
    prefix = _dot(onehot.astype(BF16), earlier_ref[...])
    carry = carry_ref[:, 0:1]
    rank = jnp.sum(onehot * (prefix + carry), axis=0, keepdims=True)
    carry = carry + jnp.sum(onehot, axis=1, keepdims=True)
    carry_b = jnp.broadcast_to(carry, carry_ref.shape)
    carry_ref[...] = carry_b
    cnt_ref[...] = carry_b

    bkt_ref[0] = bucket
    rnk_ref[0] = rank.astype(jnp.int32)
    ar = lax.broadcasted_iota(jnp.int32, (LANES, tm), 0)
    aux = jnp.where(ar == 0, c_lo, jnp.where(ar == 1, c_hi, 0.0))
    slab_ref[pl.ds(d // LANES, tm, stride=X_PITCH), :] = aux.T


def _post(yp, ys, hp, hs, params, layer):
    d = hp.shape[1]
    t = hp.shape[0] + hs.shape[0]
    n_prompt_tiles = hp.shape[0] // TOK_TILE
    return pl.pallas_call(
        functools.partial(_post_kernel, n_prompt_tiles=n_prompt_tiles),
        grid=(t // TOK_TILE,),
        in_specs=_pair_specs(d, n_prompt_tiles) + _pair_specs(d, n_prompt_tiles)
        + [_layer_spec(a, layer) for a in params],
        out_specs=[pl.BlockSpec((TOK_TILE * X_PITCH, LANES), lambda i: (i, 0)),
                   pl.BlockSpec((1, 1, TOK_TILE), lambda i: (i, 0, 0)),
                   pl.BlockSpec((1, 1, TOK_TILE), lambda i: (i, 0, 0)),
                   pl.BlockSpec((ROUTE_ROWS, LANES), lambda i: (0, 0))],
        out_shape=[jax.ShapeDtypeStruct((t * X_PITCH, LANES), F32),
                   jax.ShapeDtypeStruct((t // TOK_TILE, 1, TOK_TILE), jnp.int32),
                   jax.ShapeDtypeStruct((t // TOK_TILE, 1, TOK_TILE), jnp.int32),
                   jax.ShapeDtypeStruct((ROUTE_ROWS, LANES), F32)],
        scratch_shapes=[pltpu.VMEM((ROUTE_ROWS, LANES), F32), pltpu.VMEM((TOK_TILE, TOK_TILE), BF16)],
        compiler_params=_params(1),
        name="post",
    )(yp, ys, hp, hs, *params)


def _scatter_kernel(starts_ref, bkt_ref, rnk_ref, src_ref, _, dst_ref, pos_ref, sem, *, pitch):
    n_tok = src_ref.shape[0] // pitch
    for r in range(n_tok):
        slot = starts_ref[bkt_ref[0, 0, r]] + rnk_ref[0, 0, r]
        pos_ref[0, 0, r] = slot
        pltpu.make_async_copy(src_ref.at[pl.ds(r * pitch, pitch)],
                              dst_ref.at[pl.ds(slot * pitch, pitch)], sem).start(priority=r % 2)
    pltpu.make_async_copy(src_ref, dst_ref.at[pl.ds(0, n_tok * pitch)], sem).wait()


def _scatter_tokens(starts, bkt, rnk, src, n_out, pitch, init=None):
    hbm = pl.BlockSpec(memory_space=pl.ANY)
    tile_smem = pl.BlockSpec((1, 1, TOK_TILE), lambda i: (i, 0, 0), memory_space=pltpu.SMEM)
    if init is None:
        init = jnp.zeros((n_out * pitch, LANES), src.dtype)
    return pl.pallas_call(
        functools.partial(_scatter_kernel, pitch=pitch),
        grid=(bkt.shape[0],),
        in_specs=[pl.BlockSpec(memory_space=pltpu.SMEM), tile_smem, tile_smem,
                  pl.BlockSpec((TOK_TILE * pitch, LANES), lambda i: (i, 0)), hbm],
        out_specs=[hbm, tile_smem],
        out_shape=[jax.ShapeDtypeStruct((n_out * pitch, LANES), src.dtype),
                   jax.ShapeDtypeStruct(bkt.shape, jnp.int32)],
        scratch_shapes=[pltpu.SemaphoreType.DMA(())],
        input_output_aliases={4: 0},
        compiler_params=_params(1),
        name="scatter_tokens",
    )(starts, bkt, rnk, src, init)


def _moe_kernel(elo_ref, ehi_ref, nused_ref, new_lo_ref, new_hi_ref, x_ref, nf_ref, *refs):
    f32_lo, f32_hi, o_ref, (wg_lo, wu_lo, wd_lo), (wg_hi, wu_hi, wd_hi) = (
        refs[0:3], refs[3:6], refs[6], refs[7:10], refs[10:13])
    i = pl.program_id(0)
    n_pieces = nf_ref.shape[1] // LANES

    @pl.when(i < nused_ref[0])
    def _():
        for new_ref, srcs, dsts in ((new_lo_ref, f32_lo, (wg_lo, wu_lo, wd_lo)),
                                    (new_hi_ref, f32_hi, (wg_hi, wu_hi, wd_hi))):
            @pl.when(new_ref[i] == 1)
            def _():
                for src, dst in zip(srcs, dsts):
                    dst[...] = src[...].astype(BF16)

        x = _slab_load(x_ref, MOE_TILE, n_pieces, X_PITCH)
        gates = x_ref[pl.ds(n_pieces, MOE_TILE, stride=X_PITCH), :]
        c_lo = gates[:, 0:1]
        c_hi = gates[:, 1:2]
        t = _rms(x, nf_ref[...]).astype(BF16)

        def expert(wg, wu, wd):
            gate = _dot(t, wg[...])
            hid = (gate * _sigmoid(gate)) * _dot(t, wu[...])
            return _dot(hid.astype(BF16), wd[...])

        y = c_lo * expert(wg_lo, wu_lo, wd_lo)
        y = y + c_hi * expert(wg_hi, wu_hi, wd_hi)
        _slab_store(o_ref, x + y, Y_PITCH)
        o_ref[pl.ds(TOKEN_ROWS, MOE_TILE, stride=Y_PITCH), :] = jnp.zeros((MOE_TILE, LANES), F32)

    @pl.when(i >= nused_ref[0])
    def _():
        o_ref[...] = jnp.zeros(o_ref.shape, F32)


def _moe(tables, xs, nf, wg, wu, wd, layer):
    n_tiles = tables[0].shape[0]
    d, de = wg.shape[1:]
    lo = lambda i, elo, ehi, nu, nlo, nhi: (elo[i], 0, 0)
    hi = lambda i, elo, ehi, nu, nlo, nhi: (ehi[i], 0, 0)
    wspec = lambda shape, index_map: pl.BlockSpec((None,) + shape, index_map)
    grid_spec = pltpu.PrefetchScalarGridSpec(
        num_scalar_prefetch=len(tables),
        grid=(n_tiles,),
        in_specs=[pl.BlockSpec((MOE_TILE * X_PITCH, LANES),
                               lambda i, elo, ehi, nu, nlo, nhi: (jnp.maximum(jnp.minimum(i, nu[0] - 1), 0), 0)),
                  _layer_spec(nf, layer),
                  wspec((d, de), lo), wspec((d, de), lo), wspec((de, d), lo),
                  wspec((d, de), hi), wspec((d, de), hi), wspec((de, d), hi)],
        out_specs=pl.BlockSpec((MOE_TILE * Y_PITCH, LANES), lambda i, elo, ehi, nu, nlo, nhi: (i, 0)),
        scratch_shapes=[pltpu.VMEM(s, BF16) for s in ((d, de), (d, de), (de, d)) * 2],
    )
    return pl.pallas_call(
        _moe_kernel,
        grid_spec=grid_spec,
        out_shape=jax.ShapeDtypeStruct((n_tiles * MOE_TILE * Y_PITCH, LANES), F32),
        compiler_params=_params(1),
        name="expert_pairs",
    )(*tables, xs, nf, wg, wu, wd, wg, wu, wd)


def _ple_kernel(pos_ref, pos_next_ref, h_hbm, pp_ref, ps_ref, wple_ref, nple_ref, npg_ref, wpg_ref, *refs,
                final, n_prompt_tiles):
    refs, (xbuf, sem) = refs[:-2], refs[-2:]
    i = pl.program_id(0)
    last = pl.num_programs(0) - 1
    slot = i % 2

    def start_tile(p_ref, s):
        for r in range(TOK_TILE):
            pltpu.make_async_copy(h_hbm.at[pl.ds(p_ref[0, 0, r] * Y_PITCH, Y_PITCH)],
                                  xbuf.at[s, pl.ds(r * Y_PITCH, Y_PITCH)], sem.at[s]).start()

    def wait_tile(s):
        pltpu.make_async_copy(h_hbm.at[pl.ds(0, TOK_TILE * Y_PITCH)], xbuf.at[s], sem.at[s]).wait()

    @pl.when(i == 0)
    def _():
        start_tile(pos_ref, 0)

    wait_tile(slot)
    start_tile(pos_next_ref, 1 - slot)
    h2 = _slab_load(xbuf.at[slot], TOK_TILE, TOKEN_ROWS, Y_PITCH)
    p = _pair_load(pp_ref, ps_ref, n_prompt_tiles)
    e = _rms(_dot(p.astype(BF16), wple_ref[...]), nple_ref[...])
    gate = _sigmoid(_dot(_rms(h2, npg_ref[...]).astype(BF16), wpg_ref[...]))
    h3 = h2 + gate * e
    if final:
        nfin_ref, op_ref, os_ref = refs
        out = _rms(h3, nfin_ref[...])
    else:
        op_ref, os_ref = refs[N_INPROJ_PARAMS:N_INPROJ_PARAMS + 2]
        out = h3
        _inproj_tile(h3, *refs[:N_INPROJ_PARAMS], *refs[N_INPROJ_PARAMS + 2:])

    @pl.when(i < n_prompt_tiles)
    def _():
        op_ref[...] = out

    @pl.when(i >= n_prompt_tiles)
    def _():
        os_ref[...] = out

    @pl.when(i == last)
    def _():
        wait_tile(1 - slot)


def _ple(pos, h2s, pp_all, ps_all, params, tail, final, layer, n_prompt):
    n_steps = pos.shape[0]
    t = n_steps * TOK_TILE
    d = TOKEN_ROWS * LANES
    tile_smem = pl.BlockSpec((1, 1, TOK_TILE), lambda i: (i, 0, 0), memory_space=pltpu.SMEM)
    next_smem = pl.BlockSpec((1, 1, TOK_TILE), lambda i: (jnp.minimum(i + 1, n_steps - 1), 0, 0),
                             memory_space=pltpu.SMEM)
    n_prompt_tiles = n_prompt // TOK_TILE
    n_sample_tiles = n_steps - n_prompt_tiles
    out_specs = _pair_specs(d, n_prompt_tiles)
    out_shape = [jax.ShapeDtypeStruct((n_prompt, d), F32), jax.ShapeDtypeStruct((t - n_prompt, d), F32)]
    if final:
        tail = (tail,)
        tail_specs = [pl.BlockSpec(tail[0].shape, lambda i: (0,) * tail[0].ndim)]
    else:
        tail_specs, io, ish = _inproj_specs(tail, t, layer + 1)
        out_specs, out_shape = out_specs + io, out_shape + ish
    return pl.pallas_call(
        functools.partial(_ple_kernel, final=final, n_prompt_tiles=n_prompt_tiles),
        grid=(n_steps,),
        in_specs=[tile_smem, next_smem, pl.BlockSpec(memory_space=pl.ANY)]
        + _pair_specs(pp_all.shape[1], n_prompt_tiles, layer * n_prompt_tiles, layer * n_sample_tiles)
        + [_layer_spec(a, layer) for a in params] + tail_specs,
        out_specs=out_specs,
        out_shape=out_shape,
        scratch_shapes=[pltpu.VMEM((2, TOK_TILE * Y_PITCH, LANES), F32), pltpu.SemaphoreType.DMA((2,))],
        compiler_params=_params(1),
        name="ple_final" if final else "ple_inproj",
    )(pos, pos, h2s, pp_all, ps_all, *params, *tail)


_PAIR_FIRST = (0, 2, 2, 3, 3, 3)
_PAIR_SECOND = (1, 1, 0, 0, 1, 2)


def _routing_tables(cnt, n_tiles, expert0):
    counts = cnt[:N_BUCKETS, 0].astype(jnp.int32)
    padded = ((counts + MOE_TILE - 1) // MOE_TILE) * MOE_TILE
    ends = jnp.cumsum(padded)
    starts = jnp.pad(ends - padded, (0, ROUTE_ROWS - N_BUCKETS))
    n_used = ends[-1] // MOE_TILE
    tile = jnp.minimum(jnp.arange(n_tiles, dtype=jnp.int32), n_used - 1)
    tb = jnp.sum((ends[None, :] <= (tile * MOE_TILE)[:, None]).astype(jnp.int32), axis=1)
    tb = jnp.minimum(tb, N_BUCKETS - 1)
    grp = tb // N_PAIRS
    pr = tb % N_PAIRS
    e_lo = expert0 + grp * EXPERTS_PER_GROUP + jnp.asarray(_PAIR_FIRST, jnp.int32)[pr]
    e_hi = expert0 + grp * EXPERTS_PER_GROUP + jnp.asarray(_PAIR_SECOND, jnp.int32)[pr]
    is_new = lambda e: jnp.concatenate([jnp.ones((1,), jnp.int32), (e[1:] != e[:-1]).astype(jnp.int32)])
    return starts, (e_lo, e_hi, n_used.reshape(1), is_new(e_lo), is_new(e_hi))


def kernel(x_prompt, x_sample, state_conv, state_ssm, p_prompt, p_sample, norm_mix, w_in, conv_w, conv_b,
           dt_bias, a_log, d_skip, ssd_norm, gmlp_norm, w_spatial, b_spatial, w_out, norm_ffn,
           w_router_group, b_router_group, w_router_expert, b_router_expert, w_gate, w_up, w_down,
           w_ple, norm_ple, norm_pg, w_pg, norm_final):
    n_seq, seq_len, d_model = x_prompt.shape
    n_dec, dec_seq, _ = x_sample.shape
    depth = w_in.shape[0]
    conv_dim = conv_w.shape[2]
    d_ssd = SSD_HEADS * SSD_HEAD_DIM
    d_gmlp = gmlp_norm.shape[1]
    assert dec_seq == DEC_SEQ_ROWS and conv_w.shape[1] == CONV_WIDTH and seq_len % CHUNK == 0
    assert conv_dim == d_ssd + 2 * SSD_GROUPS * D_STATE and w_spatial.shape[1:] == (GMLP_HEADS, CHUNK, CHUNK)
    n_chunk = seq_len // CHUNK
    n_prompt = n_seq * seq_len
    n_srows = n_dec * GROUP
    t_all = n_prompt + n_srows
    assert n_prompt % TOK_TILE == 0 and n_srows % TOK_TILE == 0
    assert d_model == TOKEN_ROWS * LANES
    lead = GROUP - dec_seq
    n_moe_tiles = t_all // MOE_TILE + N_BUCKETS

    def sample_rows(a):
        pad = [(0, 0)] * (a.ndim - 2) + [(lead, 0), (0, 0)]
        return jnp.pad(a, pad).reshape(-1, a.shape[-1])

    hp, hs = x_prompt.reshape(n_prompt, d_model), sample_rows(x_sample)

    vec = lambda a: a.astype(F32).reshape(depth, 1, -1)
    lane_pad = ((0, 0), (0, LANES - SSD_HEADS))
    o_dt = d_ssd + conv_dim
    o_uv = o_dt + SSD_HEADS
    w_cols = lambda a, b: w_in[..., a:b].astype(BF16)
    inproj_params = (vec(norm_mix), w_cols(0, d_ssd), w_cols(d_ssd, o_dt),
                     jnp.pad(w_in[..., o_dt:o_uv], ((0, 0),) + lane_pad).astype(BF16),
                     w_cols(o_uv, o_uv + d_gmlp), w_cols(o_uv + d_gmlp, o_uv + 2 * d_gmlp),
                     jnp.pad(dt_bias.astype(F32), lane_pad).reshape(depth, 1, LANES), vec(gmlp_norm))

    head_cols = jnp.arange(d_ssd) // SSD_HEAD_DIM
    per_head = d_gmlp // GMLP_HEADS
    ws_tril = jnp.where(jnp.tril(jnp.ones((CHUNK, CHUNK), bool)), w_spatial, 0.0)
    bsp_p = jnp.repeat(jnp.swapaxes(b_spatial, 1, 2), per_head, axis=2)
    w8 = jnp.pad(ws_tril[:, :, :dec_seq, :dec_seq], ((0, 0), (0, 0), (lead, 0), (lead, 0)))
    wsp_s = jnp.tile(w8, (1, 1, 1, CHUNK // GROUP))
    b8 =jnp.pad(b_spatial[:, :, :dec_seq], ((0, 0), (0, 0), (lead, 0)))
    bsp_s = jnp.repeat(jnp.swapaxes(jnp.tile(b8, (1, 1, CHUNK // GROUP)), 1, 2), per_head, axis=2)
    mixer_common = (conv_w.astype(F32), vec(conv_b),
                    jnp.pad(-jnp.exp(a_log.astype(F32)), lane_pad).reshape(depth, 1, LANES),
                    d_skip.astype(F32)[:, head_cols].reshape(depth, 1, d_ssd), vec(ssd_norm))
    mixer_p_params = mixer_common + (ws_tril.astype(BF16), bsp_p)
    mixer_s_params = mixer_common + (wsp_s.astype(F32), bsp_s)

    n_route_pad = ROUTE_ROWS - N_EXPERT_GROUPS * (1 + EXPERTS_PER_GROUP)
    wr = jnp.concatenate([jnp.swapaxes(w_router_group, 1, 2), jnp.swapaxes(w_router_expert, 1, 2),
                          jnp.zeros((depth, n_route_pad, d_model), F32)], axis=1).astype(F32)
    br = jnp.concatenate([b_router_group, b_router_expert, jnp.zeros((depth, n_route_pad), F32)],
                         axis=1).astype(F32).reshape(depth, ROUTE_ROWS, 1)
    post_params = (w_out.astype(BF16), vec(norm_ffn), wr, br)
    ple_params = (w_ple.astype(BF16), vec(norm_ple), vec(norm_pg), w_pg.astype(BF16))

    n_exp = w_gate.shape[1]
    wg_all = w_gate.astype(F32).reshape((depth * n_exp,) + w_gate.shape[2:])
    wu_all = w_up.astype(F32).reshape((depth * n_exp,) + w_up.shape[2:])
    wd_all = w_down.astype(F32).reshape((depth * n_exp,) + w_down.shape[2:])
    s0_all = state_ssm.reshape(depth * n_dec, d_ssd, D_STATE)
    tail = CONV_WIDTH - 1
    cs_all = jnp.pad(state_conv, ((0, 0), (0, 0), (lead - tail, dec_seq), (0, 0))).reshape(depth * n_srows, conv_dim)
    pp_all = p_prompt.reshape(depth * n_prompt, p_prompt.shape[-1])
    ps_all = sample_rows(p_sample)

    xs = ssm_s = None
    convs_p, ssms_p, convs_s, vs_s = [], [], [], []
    z, xbc, dt, u, v = _inproj(hp, hs, inproj_params, 0)

    for i in range(depth):
        yp, ssm_p = _mixer_prompt(z, xbc, dt, u, v, mixer_p_params, n_seq, n_chunk, i)
        ys, ssm_s = _mixer_sample(z, xbc, dt, u, v, cs_all, s0_all, ssm_s, mixer_s_params, n_prompt, n_srows, i)

        slab, bkt, rnk, cnt = _post(yp, ys, hp, hs, post_params, i)
        starts, expert_tables = _routing_tables(cnt, n_moe_tiles, i * n_exp)

        xs, pos = _scatter_tokens(starts, bkt, rnk, slab, n_moe_tiles * MOE_TILE, X_PITCH, init=xs)
        h2s = _moe(expert_tables, xs, post_params[1], wg_all, wu_all, wd_all, i)

        convs_p.append(jnp.concatenate([xbc[(b + 1) * seq_len - tail:(b + 1) * seq_len] for b in range(n_seq)])
                       .reshape(n_seq, tail, conv_dim))
        xbc_s = xbc[n_prompt:].reshape(n_dec, GROUP, conv_dim)
        convs_s.append(xbc_s[:, GROUP - tail:])
        ssms_p.append(ssm_p.reshape(n_seq, SSD_HEADS, SSD_HEAD_DIM, D_STATE))
        vs_s.append(v[n_prompt:].reshape(n_dec, GROUP, d_gmlp)[:, lead:])

        final = i == depth - 1
        outs = _ple(pos, h2s, pp_all, ps_all, ple_params,
                    norm_final.astype(F32).reshape(1, -1) if final else inproj_params, final, i, n_prompt)
        hp, hs = outs[:2]
        if not final:
            z, xbc, dt, u, v = outs[2:]

    y_prompt = hp.reshape(n_seq, seq_len, d_model)
    y_sample = hs.reshape(n_dec, GROUP, d_model)[:, lead:]
    new_ssm_sample = ssm_s.reshape(depth, n_dec, SSD_HEADS, SSD_HEAD_DIM, D_STATE)
    return (y_prompt, y_sample, jnp.stack(convs_p), jnp.stack(ssms_p), jnp.stack(convs_s), new_ssm_sample,
            jnp.stack(vs_s))
```

```python
import functools
import math

import jax
import jax.numpy as jnp
from jax import lax
from jax.experimental import pallas as pl
from jax.experimental.pallas import tpu as pltpu

F32 = jnp.float32
BF16 = jnp.bfloat16

LANES = 128
VMEM_LIMIT_BYTES = 56 * 1024 * 1024

CONV_WIDTH = 4
SSD_HEADS = 8
SSD_HEAD_DIM = 64
SSD_GROUPS = 2
D_STATE = 128
CHUNK = 128
GMLP_HEADS = 8
N_EXPERT_GROUPS = 4
EXPERTS_PER_GROUP = 4
N_PAIRS = 6
N_BUCKETS = N_EXPERT_GROUPS * N_PAIRS
EPS = 1e-6

GROUP = 8
DEC_SEQ_ROWS = 4
TOK_TILE = 512
MOE_TILE = 512
PROMPT_TILES_PER_STEP = 8
ROUTE_ROWS = 32
TOKEN_ROWS = 8
X_PITCH = TOKEN_ROWS + 1
Y_PITCH = TOKEN_ROWS + 1
NEG_BIG = -1e30


def _dot(a, b):
    return jnp.dot(a, b, preferred_element_type=F32)


def _dot_nt(a, b):
    return lax.dot_general(a, b, (((1,), (1,)), ((), ())), preferred_element_type=F32)


def _rms(x, g):
    ms = jnp.mean(x * x, axis=-1, keepdims=True)
    return (x * lax.rsqrt(ms + EPS)) * g


def _gelu(x):
    return 0.5 * x * (1.0 + lax.erf(x * (1.0 / math.sqrt(2.0))))


def _sigmoid(x):
    return 0.5 * jnp.tanh(0.5 * x) + 0.5


def _softplus(x):
    return jnp.maximum(x, 0.0) + jnp.log1p(jnp.exp(-jnp.abs(x)))


def _params(n_grid):
    return pltpu.CompilerParams(dimension_semantics=("arbitrary",) * n_grid,
                                vmem_limit_bytes=VMEM_LIMIT_BYTES)


def _pair_specs(width, n_prompt_tiles, prompt_tile0=0, sample_tile0=0):
    return [pl.BlockSpec((TOK_TILE, width), lambda i: (prompt_tile0 + jnp.minimum(i, n_prompt_tiles - 1), 0)),
            pl.BlockSpec((TOK_TILE, width), lambda i: (sample_tile0 + jnp.maximum(i - n_prompt_tiles, 0), 0))]


def _pair_load(p_ref, s_ref, n_prompt_tiles):
    return jnp.where(pl.program_id(0) < n_prompt_tiles, p_ref[...], s_ref[...])


def _layer_spec(a, layer):
    zeros = (0,) * (a.ndim - 1)
    return pl.BlockSpec((None,) + a.shape[1:], lambda *_: (layer,) + zeros)


def _inproj_tile(h, nm_ref, wz_ref, wx_ref, wdt_ref, wu_ref, wv_ref, dtb_ref, gn_ref,
                 z_ref, xbc_ref, dt_ref, u_ref, v_ref):
    a = _rms(h, nm_ref[...]).astype(BF16)
    z_ref[...] = _dot(a, wz_ref[...])
    xbc_ref[...] = _dot(a, wx_ref[...])
    dt_ref[...] = _softplus(_dot(a, wdt_ref[...]) + dtb_ref[...])
    u_ref[...] = _gelu(_dot(a, wu_ref[...]))
    v_ref[...] = _rms(_gelu(_dot(a, wv_ref[...])), gn_ref[...])


def _inproj_kernel(hp_ref, hs_ref, *refs, n_prompt_tiles):
    _inproj_tile(_pair_load(hp_ref, hs_ref, n_prompt_tiles), *refs)


N_INPROJ_PARAMS = 8


def _inproj_specs(weights, t, layer):
    assert len(weights) == N_INPROJ_PARAMS
    widths = [w.shape[-1] for w in weights[1:6]]
    row = lambda w: pl.BlockSpec((TOK_TILE, w), lambda i: (i, 0))
    return ([_layer_spec(a, layer) for a in weights], [row(w) for w in widths],
            [jax.ShapeDtypeStruct((t, w), F32) for w in widths])


def _inproj(hp, hs, weights, layer):
    t = hp.shape[0] + hs.shape[0]
    n_prompt_tiles = hp.shape[0] // TOK_TILE
    w_specs, out_specs, out_shape = _inproj_specs(weights, t, layer)
    return pl.pallas_call(
        functools.partial(_inproj_kernel, n_prompt_tiles=n_prompt_tiles),
        grid=(t // TOK_TILE,),
        in_specs=_pair_specs(hp.shape[1], n_prompt_tiles) + w_specs,
        out_specs=out_specs,
        out_shape=out_shape,
        compiler_params=_params(1),
        name="inproj",
    )(hp, hs, *weights)


def _seg_cumsum(x, seg, rowmod):
    d = 1
    while d < seg:
        x = x + jnp.where(rowmod >= d, pltpu.roll(x, d, axis=0), 0.0)
        d *= 2
    return x


def _seg_rev_cumsum(x, seg, rowmod):
    n = x.shape[0]
    d = 1
    while d < seg:
        x = x + jnp.where(rowmod + d < seg, pltpu.roll(x, n - d, axis=0), 0.0)
        d *= 2
    return x


def _expand_heads(m, lane_lt_half):
    parts = []
    for j in range(SSD_HEADS // 2):
        parts.append(jnp.where(lane_lt_half, m[:, 2 * j:2 * j + 1], m[:, 2 * j + 1:2 * j + 2]))
    return jnp.concatenate(parts, axis=1)


def _merge_head_pairs(per_head, lane_lt_half):
    parts = [jnp.where(lane_lt_half, per_head[2 * j], per_head[2 * j + 1]) for j in range(len(per_head) // 2)]
    return jnp.concatenate(parts, axis=1)


def _mixer_kernel(*refs, sample, n_inner, n_prev=0):
    n_tok = 6 if sample else 5
    tok_refs, rest = refs[:n_tok], refs[n_tok:]
    if sample:
        if n_prev:
            prev_ref, rest = rest[0], rest[1:]
        (s0_ref, cw_ref, cb_ref, arow_ref, dsk_ref, sn_ref, wsp_ref, bsp_ref,
         y_ref, stack_ref, ext_ref, yoff_ref) = rest
        if n_prev:
            stack_ref[0:n_prev] = prev_ref[...]
        sout_ref = stack_ref.at[n_prev]
        first = pl.program_id(0) == 0
    else:
        (cw_ref, cb_ref, arow_ref, dsk_ref, sn_ref, wsp_ref, bsp_ref, y_ref, sout_ref, ext_ref) = rest
        s0_ref = yoff_ref = None
        first = pl.program_id(1) == 0

    @pl.when(first)
    def _():
        ext_ref[...] = jnp.zeros(ext_ref.shape, F32)
        if not sample:
            sout_ref[...] = jnp.zeros(sout_ref.shape, F32)

    def tile(c, carry):
        r0 = pl.multiple_of(c * CHUNK, CHUNK)
        views = [r.at[pl.ds(r0, CHUNK)] for r in tok_refs + (y_ref,)]
        _mixer_tile(*views[:n_tok], s0_ref, cw_ref, cb_ref, arow_ref, dsk_ref, sn_ref, wsp_ref, bsp_ref,
                    views[n_tok], sout_ref, ext_ref, yoff_ref, sample=sample)
        return carry

    if n_inner == 1:
        tile(0, 0)
    else:
        lax.fori_loop(0, n_inner, tile, 0)


def _mixer_tile(z_ref, xbc_ref, dt_ref, u_ref, v_ref, *rest, sample):
    if sample:
        cs_ref, s0_ref = rest[0], rest[1]
        rest = rest[2:]
    else:
        rest = rest[1:]
    cw_ref, cb_ref, arow_ref, dsk_ref, sn_ref, wsp_ref, bsp_ref, y_ref, sout_ref, ext_ref, yoff_ref = rest
    seg = GROUP if sample else CHUNK
    d_ssd = SSD_HEADS * SSD_HEAD_DIM
    gw = d_ssd // SSD_GROUPS
    hpg = SSD_HEADS // SSD_GROUPS
    cs_first = GROUP - DEC_SEQ_ROWS - (CONV_WIDTH - 1)

    rows = lax.broadcasted_iota(jnp.int32, (CHUNK, LANES), 0)
    cols = lax.broadcasted_iota(jnp.int32, (CHUNK, LANES), 1)
    rowmod = rows & (seg - 1)
    lane_lt_half = cols < SSD_HEAD_DIM

    xbc = xbc_ref[...]
    if sample:
        rm = lax.broadcasted_iota(jnp.int32, xbc.shape, 0) & (GROUP - 1)
        xbc = jnp.where((rm >= cs_first) & (rm < cs_first + CONV_WIDTH - 1), cs_ref[...], xbc)

    tail = ext_ref[...]
    row8 = lax.broadcasted_iota(jnp.int32, tail.shape, 0)
    acc = cb_ref[...] + cw_ref[CONV_WIDTH - 1:CONV_WIDTH, :] * xbc
    for j in range(1, CONV_WIDTH):
        rolled = pltpu.roll(xbc, j, axis=0)
        head = jnp.where(row8 < j, pltpu.roll(tail, j, axis=0), rolled[0:8, :])
        shifted = jnp.concatenate([head, rolled[8:, :]], axis=0)
        acc = acc + cw_ref[CONV_WIDTH - 1 - j:CONV_WIDTH - j, :] * shifted
    if not sample:
        ext_ref[...] = xbc[CHUNK - 8:, :]
    xc = acc * _sigmoid(acc)
    x = xc[:, :d_ssd]
    bb = xc[:, d_ssd:d_ssd + SSD_GROUPS * D_STATE].astype(BF16)
    cm = xc[:, d_ssd + SSD_GROUPS * D_STATE:]
    cbf = cm.astype(BF16)

    dtc = dt_ref[...]
    if sample:
        dtc = jnp.where(rowmod >= GROUP - DEC_SEQ_ROWS, dtc, 0.0)
    da = dtc * arow_ref[...]
    cum = _seg_cumsum(da, seg, rowmod)
    rev = _seg_rev_cumsum(da, seg, rowmod) - da
    cum_t = cum.T
    ecum = jnp.exp(cum)
    dt_e = _expand_heads(dtc, lane_lt_half)
    ecum_e = _expand_heads(ecum, lane_lt_half)
    erev_e = _expand_heads(jnp.exp(rev), lane_lt_half)

    xdt = x * dt_e
    xdt_bf = xdt.astype(BF16)
    causal = rows >= cols
    if sample:
        same_seq = (rows >> 3) == (cols >> 3)
        causal = causal & same_seq

    yd = []
    for g in range(SSD_GROUPS):
        cb_g = _dot_nt(cbf[:, g * D_STATE:(g + 1) * D_STATE], bb[:, g * D_STATE:(g + 1) * D_STATE])
        for hh in range(hpg):
            h = g * hpg + hh
            expo = cum[:, h:h + 1] - cum_t[h:h + 1, :]
            w = (cb_g * jnp.exp(jnp.where(causal, expo, NEG_BIG))).astype(BF16)
            j = h // 2
            yd.append(_dot(w, xdt_bf[:, j * LANES:(j + 1) * LANES]))
    y_diag = _merge_head_pairs(yd, lane_lt_half)

    if sample:
        for i in range(CHUNK // GROUP):
            for g in range(SSD_GROUPS):
                s_g = s0_ref[i, g * gw:(g + 1) * gw, :].astype(BF16)
                yoff_ref[i * GROUP:(i + 1) * GROUP, g * gw:(g + 1) * gw] = _dot_nt(
                    cm[i * GROUP:(i + 1) * GROUP, g * D_STATE:(g + 1) * D_STATE].astype(BF16), s_g)
        y_off = yoff_ref[...]
    else:
        y_off = jnp.concatenate(
            [_dot_nt(cbf[:, g * D_STATE:(g + 1) * D_STATE], sout_ref[0, g * gw:(g + 1) * gw, :].astype(BF16))
             for g in range(SSD_GROUPS)], axis=1)
    y = y_diag + y_off * ecum_e + dsk_ref[...] * x

    xd = xdt * erev_e
    for g in range(SSD_GROUPS):
        xd_t = xd[:, g * gw:(g + 1) * gw].T
        b_g = bb[:, g * D_STATE:(g + 1) * D_STATE]
        if sample:
            tcols = lax.broadcasted_iota(jnp.int32, xd_t.shape, 1) >> 3
            for i in range(CHUNK // GROUP):
                upd = _dot(jnp.where(tcols == i, xd_t, 0.0).astype(BF16), b_g)
                last = i * GROUP + GROUP - 1
                for hh in range(hpg):
                    h = g * hpg + hh
                    r0 = h * SSD_HEAD_DIM
                    sout_ref[i, r0:r0 + SSD_HEAD_DIM, :] = (
                        s0_ref[i, r0:r0 + SSD_HEAD_DIM, :] * ecum[last:last + 1, h:h + 1]
                        + upd[hh * SSD_HEAD_DIM:(hh + 1) * SSD_HEAD_DIM, :])
        else:
            upd = _dot(xd_t.astype(BF16), b_g)
            for hh in range(hpg):
                h = g * hpg + hh
                r0 = h * SSD_HEAD_DIM
                sout_ref[0, r0:r0 + SSD_HEAD_DIM, :] = (
                    sout_ref[0, r0:r0 + SSD_HEAD_DIM, :] * ecum[CHUNK - 1:CHUNK, h:h + 1]
                    + upd[hh * SSD_HEAD_DIM:(hh + 1) * SSD_HEAD_DIM, :])

    zf = z_ref[...]
    yf = y * (zf * _sigmoid(zf))
    parts = []
    for g in range(SSD_GROUPS):
        part = yf[:, g * gw:(g + 1) * gw]
        ms = jnp.mean(part * part, axis=-1, keepdims=True)
        parts.append(part * lax.rsqrt(ms + EPS))
    y_ssd = jnp.concatenate(parts, axis=1) * sn_ref[...]

    vb = v_ref[...].astype(BF16)
    if sample:
        n_grp = CHUNK // GROUP
        wms = [jnp.where(same_seq, jnp.tile(wsp_ref[h], (n_grp, 1)), 0.0).astype(BF16) for h in range(GMLP_HEADS)]
    else:
        wms = [wsp_ref[h] for h in range(GMLP_HEADS)]
    sg = [_dot(wms[h], vb[:, (h // 2) * LANES:(h // 2 + 1) * LANES]) for h in range(GMLP_HEADS)]
    s = _merge_head_pairs(sg, lane_lt_half) + bsp_ref[...]
    y_gm = u_ref[...] * s

    y_ref[:, :d_ssd] = y_ssd.astype(BF16)
    y_ref[:, d_ssd:] = y_gm.astype(BF16)


def _mixer_prompt(z, xbc, dt, u, v, params, n_seq, n_chunk, layer):
    n_inner = math.gcd(n_chunk, PROMPT_TILES_PER_STEP)
    n_outer = n_chunk // n_inner
    tok = lambda w: pl.BlockSpec((n_inner * CHUNK, w), lambda b, c: (b * n_outer + c, 0))
    d_ssd = z.shape[1]
    d_mix = d_ssd + u.shape[1]
    return pl.pallas_call(
        functools.partial(_mixer_kernel, sample=False, n_inner=n_inner),
        grid=(n_seq, n_outer),
        in_specs=[tok(z.shape[1]), tok(xbc.shape[1]), tok(dt.shape[1]), tok(u.shape[1]), tok(v.shape[1])]
        + [_layer_spec(a, layer) for a in params],
        out_specs=[tok(d_mix),
                   pl.BlockSpec((1, d_ssd, D_STATE), lambda b, c: (b, 0, 0))],
        out_shape=[jax.ShapeDtypeStruct((n_seq * n_chunk * CHUNK, d_mix), BF16),
                   jax.ShapeDtypeStruct((n_seq, d_ssd, D_STATE), F32)],
        scratch_shapes=[pltpu.VMEM((8, xbc.shape[1]), F32)],
        compiler_params=_params(2),
        name="mixer_prompt",
    )(z, xbc, dt, u, v, *params)


def _mixer_sample(z, xbc, dt, u, v, cs_all, s0_all, prev, params, row0, n_rows, layer):
    blk0 = row0 // CHUNK
    n_seq_blk = CHUNK // GROUP
    n_blk = n_rows // CHUNK
    tok = lambda w: pl.BlockSpec((CHUNK, w), lambda i: (blk0 + i, 0))
    d_ssd = z.shape[1]
    d_mix = d_ssd + u.shape[1]
    st = (n_seq_blk, d_ssd, D_STATE)
    stack = lambda n: pl.BlockSpec((n,) + st, lambda i: (0, i, 0, 0))
    prev_args = [] if layer == 0 else [prev]
    return pl.pallas_call(
        functools.partial(_mixer_kernel, sample=True, n_inner=1, n_prev=layer),
        grid=(n_blk,),
        in_specs=[tok(z.shape[1]), tok(xbc.shape[1]), tok(dt.shape[1]), tok(u.shape[1]), tok(v.shape[1]),
                  pl.BlockSpec((CHUNK, cs_all.shape[1]), lambda i: (layer * n_blk + i, 0))]
        + [stack(layer) for _ in prev_args]
        + [pl.BlockSpec(st, lambda i: (layer * n_blk + i, 0, 0))]
        + [_layer_spec(a, layer) for a in params],
        out_specs=[pl.BlockSpec((CHUNK, d_mix), lambda i: (i, 0)), stack(layer + 1)],
        out_shape=[jax.ShapeDtypeStruct((n_rows, d_mix), BF16),
                   jax.ShapeDtypeStruct((layer + 1, n_blk * n_seq_blk, d_ssd, D_STATE), F32)],
        scratch_shapes=[pltpu.VMEM((8, xbc.shape[1]), F32), pltpu.VMEM((CHUNK, d_ssd), F32)],
        compiler_params=_params(1),
        name="mixer_sample",
    )(z, xbc, dt, u, v, cs_all, *prev_args, s0_all, *params)


def _first_argmax(vals):
    m = vals[0]
    for v in vals[1:]:
        m = jnp.maximum(m, v)
    idx = jnp.full(m.shape, len(vals) - 1, jnp.int32)
    for k in range(len(vals) - 2, -1, -1):
        idx = jnp.where(vals[k] >= m, k, idx)
    return m, idx


def _slab_store(slab_ref, x, pitch, tok0=0):
    for k in range(x.shape[1] // LANES):
        slab_ref[pl.ds(tok0 * pitch + k, x.shape[0], stride=pitch), :] = x[:, k * LANES:(k + 1) * LANES]


def _slab_load(slab_ref, rows, n_pieces, pitch, tok0=0):
    return jnp.concatenate([slab_ref[pl.ds(tok0 * pitch + k, rows, stride=pitch), :] for k in range(n_pieces)],
                           axis=1)


def _post_kernel(yp_ref, ys_ref, hp_ref, hs_ref, wo_ref, nf_ref, wr_ref, br_ref,
                 slab_ref, bkt_ref, rnk_ref, cnt_ref, carry_ref, earlier_ref, *, n_prompt_tiles):
    i = pl.program_id(0)
    tm, d = hp_ref.shape

    @pl.when(i == 0)
    def _():
        carry_ref[...] = jnp.zeros(carry_ref.shape, F32)
        tr = lax.broadcasted_iota(jnp.int32, (tm, tm), 0)
        tc = lax.broadcasted_iota(jnp.int32, (tm, tm), 1)
        earlier_ref[...] = jnp.where(tr < tc, 1.0, 0.0).astype(BF16)

    ym = _pair_load(yp_ref, ys_ref, n_prompt_tiles)
    h1 = _pair_load(hp_ref, hs_ref, n_prompt_tiles) + _dot(ym, wo_ref[...])
    _slab_store(slab_ref, h1, X_PITCH)

    t = _rms(h1, nf_ref[...])
    wr = wr_ref[...]
    wr_hi = wr.astype(BF16)
    wr_lo = (wr - wr_hi.astype(F32)).astype(BF16)
    t_hi = t.astype(BF16)
    t_lo = (t - t_hi.astype(F32)).astype(BF16)
    by_hi = _dot_nt(jnp.concatenate([wr_hi, wr_lo], axis=0), t_hi)
    logits = (by_hi[:ROUTE_ROWS] + (by_hi[ROUTE_ROWS:] + _dot_nt(wr_hi, t_lo))) + br_ref[...]
    lg = [logits[k:k + 1, :] for k in range(N_EXPERT_GROUPS)]
    m, g = _first_argmax(lg)
    ssum = jnp.exp(lg[0] - m)
    for k in range(1, N_EXPERT_GROUPS):
        ssum = ssum + jnp.exp(lg[k] - m)
    p_sel = 1.0 / ssum
    le = [logits[N_EXPERT_GROUPS + e:N_EXPERT_GROUPS + e + 1, :] for e in range(N_EXPERT_GROUPS * EXPERTS_PER_GROUP)]
    a = []
    for k in range(EXPERTS_PER_GROUP):
        sel = le[(N_EXPERT_GROUPS - 1) * EXPERTS_PER_GROUP + k]
        for gi in range(N_EXPERT_GROUPS - 2, -1, -1):
            sel = jnp.where(g == gi, le[gi * EXPERTS_PER_GROUP + k], sel)
        a.append(sel)
    v1, i1 = _first_argmax(a)
    a2 = [jnp.where(i1 == k, -jnp.inf, a[k]) for k in range(EXPERTS_PER_GROUP)]
    v2, i2 = _first_argmax(a2)
    e2 = jnp.exp(v2 - v1)
    den = 1.0 + e2
    g1 = (1.0 / den) * p_sel
    g2 = (e2 / den) * p_sel
    lo = jnp.minimum(i1, i2)
    hi = jnp.maximum(i1, i2)
    pair = jnp.where(lo == 0, jnp.where(hi == 1, 0, hi), jnp.where(lo == 1, jnp.where(hi == 2, 1, 4), 5))
    first = jnp.where(pair == 0, 0, jnp.where(pair <= 2, 2, 3))
    c_first = jnp.where(first == i1, g1, g2)
    c_second = jnp.where(first == i1, g2, g1)
    bucket = g * N_PAIRS + pair

    brow = lax.broadcasted_iota(jnp.int32, (ROUTE_ROWS, tm), 0)
    onehot = jnp.where(brow == bucket, 1.0, 0.0)
    prefix = _dot(onehot.astype(BF16), earlier_ref[...])
    carry = carry_ref[:, 0:1]
    rank = jnp.sum(onehot * (prefix + carry), axis=0, keepdims=True)
    carry = carry + jnp.sum(onehot, axis=1, keepdims=True)
    carry_b = jnp.broadcast_to(carry, carry_ref.shape)
    carry_ref[...] = carry_b
    cnt_ref[...] = carry_b

    bkt_ref[0] = bucket
    rnk_ref[0] = rank.astype(jnp.int32)
    ar = lax.broadcasted_iota(jnp.int32, (LANES, tm), 0)
    aux = jnp.where(ar == 0, c_first, jnp.where(ar == 1, c_second, 0.0))
    slab_ref[pl.ds(d // LANES, tm, stride=X_PITCH), :] = aux.T


def _post(yp, ys, hp, hs, params, layer):
    d = hp.shape[1]
    t = hp.shape[0] + hs.shape[0]
    n_prompt_tiles = hp.shape[0] // TOK_TILE
    return pl.pallas_call(
        functools.partial(_post_kernel, n_prompt_tiles=n_prompt_tiles),
        grid=(t // TOK_TILE,),
        in_specs=_pair_specs(d, n_prompt_tiles) + _pair_specs(d, n_prompt_tiles)
        + [_layer_spec(a, layer) for a in params],
        out_specs=[pl.BlockSpec((TOK_TILE * X_PITCH, LANES), lambda i: (i, 0)),
                   pl.BlockSpec((1, 1, TOK_TILE), lambda i: (i, 0, 0)),
                   pl.BlockSpec((1, 1, TOK_TILE), lambda i: (i, 0, 0)),
                   pl.BlockSpec((ROUTE_ROWS, LANES), lambda i: (0, 0))],
        out_shape=[jax.ShapeDtypeStruct((t * X_PITCH, LANES), F32),
                   jax.ShapeDtypeStruct((t // TOK_TILE, 1, TOK_TILE), jnp.int32),
                   jax.ShapeDtypeStruct((t // TOK_TILE, 1, TOK_TILE), jnp.int32),
                   jax.ShapeDtypeStruct((ROUTE_ROWS, LANES), F32)],
        scratch_shapes=[pltpu.VMEM((ROUTE_ROWS, LANES), F32), pltpu.VMEM((TOK_TILE, TOK_TILE), BF16)],
        compiler_params=_params(1),
        name="post",
    )(yp, ys, hp, hs, *params)


def _scatter_kernel(starts_ref, bkt_ref, rnk_ref, src_ref, _, dst_ref, pos_ref, sem, *, pitch):
    n_tok = src_ref.shape[0] // pitch
    for r in range(n_tok):
        slot = starts_ref[bkt_ref[0, 0, r]] + rnk_ref[0, 0, r]
        pos_ref[0, 0, r] = slot
        pltpu.make_async_copy(src_ref.at[pl.ds(r * pitch, pitch)],
                              dst_ref.at[pl.ds(slot * pitch, pitch)], sem).start(priority=r % 2)
    pltpu.make_async_copy(src_ref, dst_ref.at[pl.ds(0, n_tok * pitch)], sem).wait()


def _scatter_tokens(starts, bkt, rnk, src, n_out, pitch, init=None):
    hbm = pl.BlockSpec(memory_space=pl.ANY)
    tile_smem = pl.BlockSpec((1, 1, TOK_TILE), lambda i: (i, 0, 0), memory_space=pltpu.SMEM)
    if init is None:
        init = jnp.zeros((n_out * pitch, LANES), src.dtype)
    return pl.pallas_call(
        functools.partial(_scatter_kernel, pitch=pitch),
        grid=(bkt.shape[0],),
        in_specs=[pl.BlockSpec(memory_space=pltpu.SMEM), tile_smem, tile_smem,
                  pl.BlockSpec((TOK_TILE * pitch, LANES), lambda i: (i, 0)), hbm],
        out_specs=[hbm, tile_smem],
        out_shape=[jax.ShapeDtypeStruct((n_out * pitch, LANES), src.dtype),
                   jax.ShapeDtypeStruct(bkt.shape, jnp.int32)],
        scratch_shapes=[pltpu.SemaphoreType.DMA(())],
        input_output_aliases={4: 0},
        compiler_params=_params(1),
        name="scatter_tokens",
    )(starts, bkt, rnk, src, init)


def _moe_kernel(first_ref, second_ref, nused_ref, new_first_ref, new_second_ref, x_ref, nf_ref, *refs):
    f32_first, f32_second, o_ref, bf_first, bf_second = refs[0:3], refs[3:6], refs[6], refs[7:10], refs[10:13]
    i = pl.program_id(0)
    n_pieces = nf_ref.shape[1] // LANES

    @pl.when(i < nused_ref[0])
    def _():
        for new_ref, srcs, dsts in ((new_first_ref, f32_first, bf_first), (new_second_ref, f32_second, bf_second)):
            @pl.when(new_ref[i] == 1)
            def _():
                for src, dst in zip(srcs, dsts):
                    dst[...] = src[...].astype(BF16)

        x = _slab_load(x_ref, MOE_TILE, n_pieces, X_PITCH)
        gates = x_ref[pl.ds(n_pieces, MOE_TILE, stride=X_PITCH), :]
        c_first = gates[:, 0:1]
        c_second = gates[:, 1:2]
        t = _rms(x, nf_ref[...]).astype(BF16)

        def expert(wg, wu, wd):
            gate = _dot(t, wg[...])
            hid = (gate * _sigmoid(gate)) * _dot(t, wu[...])
            return _dot(hid.astype(BF16), wd[...])

        y = c_first * expert(*bf_first)
        y = y + c_second * expert(*bf_second)
        _slab_store(o_ref, x + y, Y_PITCH)
        o_ref[pl.ds(TOKEN_ROWS, MOE_TILE, stride=Y_PITCH), :] = jnp.zeros((MOE_TILE, LANES), F32)

    @pl.when(i >= nused_ref[0])
    def _():
        o_ref[...] = jnp.zeros(o_ref.shape, F32)


def _moe(tables, xs, nf, wg, wu, wd, layer):
    n_tiles = tables[0].shape[0]
    d, de = wg.shape[1:]
    first = lambda i, first, second, *_: (first[i], 0, 0)
    second = lambda i, first, second, *_: (second[i], 0, 0)
    wspec = lambda shape, index_map: pl.BlockSpec((None,) + shape, index_map)
    grid_spec = pltpu.PrefetchScalarGridSpec(
        num_scalar_prefetch=len(tables),
        grid=(n_tiles,),
        in_specs=[pl.BlockSpec((MOE_TILE * X_PITCH, LANES),
                               lambda i, first, second, n_used, *_: (jnp.maximum(jnp.minimum(i, n_used[0] - 1), 0), 0)),
                  _layer_spec(nf, layer),
                  wspec((d, de), first), wspec((d, de), first), wspec((de, d), first),
                  wspec((d, de), second), wspec((d, de), second), wspec((de, d), second)],
        out_specs=pl.BlockSpec((MOE_TILE * Y_PITCH, LANES), lambda i, *_: (i, 0)),
        scratch_shapes=[pltpu.VMEM(s, BF16) for s in ((d, de), (d, de), (de, d)) * 2],
    )
    return pl.pallas_call(
        _moe_kernel,
        grid_spec=grid_spec,
        out_shape=jax.ShapeDtypeStruct((n_tiles * MOE_TILE * Y_PITCH, LANES), F32),
        compiler_params=_params(1),
        name="expert_pairs",
    )(*tables, xs, nf, wg, wu, wd, wg, wu, wd)


def _ple_kernel(pos_ref, pos_next_ref, h_hbm, pp_ref, ps_ref, wple_ref, nple_ref, npg_ref, wpg_ref, *refs,
                final, n_prompt_tiles):
    refs, (xbuf, sem) = refs[:-2], refs[-2:]
    i = pl.program_id(0)
    last = pl.num_programs(0) - 1
    slot = i % 2

    def start_tile(p_ref, s):
        for r in range(TOK_TILE):
            pltpu.make_async_copy(h_hbm.at[pl.ds(p_ref[0, 0, r] * Y_PITCH, Y_PITCH)],
                                  xbuf.at[s, pl.ds(r * Y_PITCH, Y_PITCH)], sem.at[s]).start()

    def wait_tile(s):
        pltpu.make_async_copy(h_hbm.at[pl.ds(0, TOK_TILE * Y_PITCH)], xbuf.at[s], sem.at[s]).wait()

    @pl.when(i == 0)
    def _():
        start_tile(pos_ref, 0)

    wait_tile(slot)
    start_tile(pos_next_ref, 1 - slot)
    h2 = _slab_load(xbuf.at[slot], TOK_TILE, TOKEN_ROWS, Y_PITCH)
    p = _pair_load(pp_ref, ps_ref, n_prompt_tiles)
    e = _rms(_dot(p.astype(BF16), wple_ref[...]), nple_ref[...])
    gate = _sigmoid(_dot(_rms(h2, npg_ref[...]).astype(BF16), wpg_ref[...]))
    h3 = h2 + gate * e
    if final:
        nfin_ref, op_ref, os_ref = refs
        out = _rms(h3, nfin_ref[...])
    else:
        op_ref, os_ref = refs[N_INPROJ_PARAMS:N_INPROJ_PARAMS + 2]
        out = h3
        _inproj_tile(h3, *refs[:N_INPROJ_PARAMS], *refs[N_INPROJ_PARAMS + 2:])

    @pl.when(i < n_prompt_tiles)
    def _():
        op_ref[...] = out

    @pl.when(i >= n_prompt_tiles)
    def _():
        os_ref[...] = out

    @pl.when(i == last)
    def _():
        wait_tile(1 - slot)


def _ple(pos, h2s, pp_all, ps_all, params, tail, final, layer, n_prompt):
    n_steps = pos.shape[0]
    t = n_steps * TOK_TILE
    d = TOKEN_ROWS * LANES
    tile_smem = pl.BlockSpec((1, 1, TOK_TILE), lambda i: (i, 0, 0), memory_space=pltpu.SMEM)
    next_smem = pl.BlockSpec((1, 1, TOK_TILE), lambda i: (jnp.minimum(i + 1, n_steps - 1), 0, 0),
                             memory_space=pltpu.SMEM)
    n_prompt_tiles = n_prompt // TOK_TILE
    n_sample_tiles = n_steps - n_prompt_tiles
    out_specs = _pair_specs(d, n_prompt_tiles)
    out_shape = [jax.ShapeDtypeStruct((n_prompt, d), F32), jax.ShapeDtypeStruct((t - n_prompt, d), F32)]
    if final:
        tail = (tail,)
        tail_specs = [pl.BlockSpec(tail[0].shape, lambda i: (0,) * tail[0].ndim)]
    else:
        tail_specs, io, ish = _inproj_specs(tail, t, layer + 1)
        out_specs, out_shape = out_specs + io, out_shape + ish
    return pl.pallas_call(
        functools.partial(_ple_kernel, final=final, n_prompt_tiles=n_prompt_tiles),
        grid=(n_steps,),
        in_specs=[tile_smem, next_smem, pl.BlockSpec(memory_space=pl.ANY)]
        + _pair_specs(pp_all.shape[1], n_prompt_tiles, layer * n_prompt_tiles, layer * n_sample_tiles)
        + [_layer_spec(a, layer) for a in params] + tail_specs,
        out_specs=out_specs,
        out_shape=out_shape,
        scratch_shapes=[pltpu.VMEM((2, TOK_TILE * Y_PITCH, LANES), F32), pltpu.SemaphoreType.DMA((2,))],
        compiler_params=_params(1),
        name="ple_final" if final else "ple_inproj",
    )(pos, pos, h2s, pp_all, ps_all, *params, *tail)


_PAIR_FIRST = (0, 2, 2, 3, 3, 3)
_PAIR_SECOND = (1, 1, 0, 0, 1, 2)


def _routing_tables(cnt, n_tiles, expert0):
    counts = cnt[:N_BUCKETS, 0].astype(jnp.int32)
    padded = ((counts + MOE_TILE - 1) // MOE_TILE) * MOE_TILE
    ends = jnp.cumsum(padded)
    starts = jnp.pad(ends - padded, (0, ROUTE_ROWS - N_BUCKETS))
    n_used = ends[-1] // MOE_TILE
    tile = jnp.minimum(jnp.arange(n_tiles, dtype=jnp.int32), n_used - 1)
    tb = jnp.sum((ends[None, :] <= (tile * MOE_TILE)[:, None]).astype(jnp.int32), axis=1)
    tb = jnp.minimum(tb, N_BUCKETS - 1)
    grp = tb // N_PAIRS
    pr = tb % N_PAIRS
    e_first = expert0 + grp * EXPERTS_PER_GROUP + jnp.asarray(_PAIR_FIRST, jnp.int32)[pr]
    e_second = expert0 + grp * EXPERTS_PER_GROUP + jnp.asarray(_PAIR_SECOND, jnp.int32)[pr]
    is_new = lambda e: jnp.concatenate([jnp.ones((1,), jnp.int32), (e[1:] != e[:-1]).astype(jnp.int32)])
    return starts, (e_first, e_second, n_used.reshape(1), is_new(e_first), is_new(e_second))


def kernel(x_prompt, x_sample, state_conv, state_ssm, p_prompt, p_sample, norm_mix, w_in, conv_w, conv_b,
           dt_bias, a_log, d_skip, ssd_norm, gmlp_norm, w_spatial, b_spatial, w_out, norm_ffn,
           w_router_group, b_router_group, w_router_expert, b_router_expert, w_gate, w_up, w_down,
           w_ple, norm_ple, norm_pg, w_pg, norm_final):
    n_seq, seq_len, d_model = x_prompt.shape
    n_dec, dec_seq, _ = x_sample.shape
    depth = w_in.shape[0]
    conv_dim = conv_w.shape[2]
    d_ssd = SSD_HEADS * SSD_HEAD_DIM
    d_gmlp = gmlp_norm.shape[1]
    assert dec_seq == DEC_SEQ_ROWS and conv_w.shape[1] == CONV_WIDTH and seq_len % CHUNK == 0
    assert conv_dim == d_ssd + 2 * SSD_GROUPS * D_STATE and w_spatial.shape[1:] == (GMLP_HEADS, CHUNK, CHUNK)
    n_chunk = seq_len // CHUNK
    n_prompt = n_seq * seq_len
    n_srows = n_dec * GROUP
    t_all = n_prompt + n_srows
    assert n_prompt % TOK_TILE == 0 and n_srows % TOK_TILE == 0
    assert d_model == TOKEN_ROWS * LANES
    lead = GROUP - dec_seq
    n_moe_tiles = t_all // MOE_TILE + N_BUCKETS

    def sample_rows(a):
        pad = [(0, 0)] * (a.ndim - 2) + [(lead, 0), (0, 0)]
        return jnp.pad(a, pad).reshape(-1, a.shape[-1])

    hp, hs = x_prompt.reshape(n_prompt, d_model), sample_rows(x_sample)

    vec = lambda a: a.astype(F32).reshape(depth, 1, -1)
    lane_pad = ((0, 0), (0, LANES - SSD_HEADS))
    o_dt = d_ssd + conv_dim
    o_uv = o_dt + SSD_HEADS
    w_cols = lambda a, b: w_in[..., a:b].astype(BF16)
    inproj_params = (vec(norm_mix), w_cols(0, d_ssd), w_cols(d_ssd, o_dt),
                     jnp.pad(w_in[..., o_dt:o_uv], ((0, 0),) + lane_pad).astype(BF16),
                     w_cols(o_uv, o_uv + d_gmlp), w_cols(o_uv + d_gmlp, o_uv + 2 * d_gmlp),
                     jnp.pad(dt_bias.astype(F32), lane_pad).reshape(depth, 1, LANES), vec(gmlp_norm))

    head_cols = jnp.arange(d_ssd) // SSD_HEAD_DIM
    per_head = d_gmlp // GMLP_HEADS
    ws_tril = jnp.where(jnp.tril(jnp.ones((CHUNK, CHUNK), bool)), w_spatial, 0.0)
    bsp_p = jnp.repeat(jnp.swapaxes(b_spatial, 1, 2), per_head, axis=2)
    w8 = jnp.pad(ws_tril[:, :, :dec_seq, :dec_seq], ((0, 0), (0, 0), (lead, 0), (lead, 0)))
    wsp_s = jnp.tile(w8, (1, 1, 1, CHUNK // GROUP))
    b8 =jnp.pad(b_spatial[:, :, :dec_seq], ((0, 0), (0, 0), (lead, 0)))
    bsp_s = jnp.repeat(jnp.swapaxes(jnp.tile(b8, (1, 1, CHUNK // GROUP)), 1, 2), per_head, axis=2)
    mixer_common = (conv_w.astype(F32), vec(conv_b),
                    jnp.pad(-jnp.exp(a_log.astype(F32)), lane_pad).reshape(depth, 1, LANES),
                    d_skip.astype(F32)[:, head_cols].reshape(depth, 1, d_ssd), vec(ssd_norm))
    mixer_p_params = mixer_common + (ws_tril.astype(BF16), bsp_p)
    mixer_s_params = mixer_common + (wsp_s.astype(F32), bsp_s)

    n_route_pad = ROUTE_ROWS - N_EXPERT_GROUPS * (1 + EXPERTS_PER_GROUP)
    wr = jnp.concatenate([jnp.swapaxes(w_router_group, 1, 2), jnp.swapaxes(w_router_expert, 1, 2),
                          jnp.zeros((depth, n_route_pad, d_model), F32)], axis=1).astype(F32)
    br = jnp.concatenate([b_router_group, b_router_expert, jnp.zeros((depth, n_route_pad), F32)],
                         axis=1).astype(F32).reshape(depth, ROUTE_ROWS, 1)
    post_params = (w_out.astype(BF16), vec(norm_ffn), wr, br)
    ple_params = (w_ple.astype(BF16), vec(norm_ple), vec(norm_pg), w_pg.astype(BF16))

    n_exp = w_gate.shape[1]
    wg_all = w_gate.astype(F32).reshape((depth * n_exp,) + w_gate.shape[2:])
    wu_all = w_up.astype(F32).reshape((depth * n_exp,) + w_up.shape[2:])
    wd_all = w_down.astype(F32).reshape((depth * n_exp,) + w_down.shape[2:])
    s0_all = state_ssm.reshape(depth * n_dec, d_ssd, D_STATE)
    tail = CONV_WIDTH - 1
    cs_all = jnp.pad(state_conv, ((0, 0), (0, 0), (lead - tail, dec_seq), (0, 0))).reshape(depth * n_srows, conv_dim)
    pp_all = p_prompt.reshape(depth * n_prompt, p_prompt.shape[-1])
    ps_all = sample_rows(p_sample)

    xs = ssm_s = None
    convs_p, ssms_p, convs_s, vs_s = [], [], [], []
    z, xbc, dt, u, v = _inproj(hp, hs, inproj_params, 0)

    for i in range(depth):
        yp, ssm_p = _mixer_prompt(z, xbc, dt, u, v, mixer_p_params, n_seq, n_chunk, i)
        ys, ssm_s = _mixer_sample(z, xbc, dt, u, v, cs_all, s0_all, ssm_s, mixer_s_params, n_prompt, n_srows, i)

        slab, bkt, rnk, cnt = _post(yp, ys, hp, hs, post_params, i)
        starts, expert_tables = _routing_tables(cnt, n_moe_tiles, i * n_exp)

        xs, pos = _scatter_tokens(starts, bkt, rnk, slab, n_moe_tiles * MOE_TILE, X_PITCH, init=xs)
        h2s = _moe(expert_tables, xs, post_params[1], wg_all, wu_all, wd_all, i)

        convs_p.append(jnp.concatenate([xbc[(b + 1) * seq_len - tail:(b + 1) * seq_len] for b in range(n_seq)])
                       .reshape(n_seq, tail, conv_dim))
        xbc_s = xbc[n_prompt:].reshape(n_dec, GROUP, conv_dim)
        convs_s.append(xbc_s[:, GROUP - tail:])
        ssms_p.append(ssm_p.reshape(n_seq, SSD_HEADS, SSD_HEAD_DIM, D_STATE))
        vs_s.append(v[n_prompt:].reshape(n_dec, GROUP, d_gmlp)[:, lead:])

        final = i == depth - 1
        outs = _ple(pos, h2s, pp_all, ps_all, ple_params,
                    norm_final.astype(F32).reshape(1, -1) if final else inproj_params, final, i, n_prompt)
        hp, hs = outs[:2]
        if not final:
            z, xbc, dt, u, v = outs[2:]

    y_prompt = hp.reshape(n_seq, seq_len, d_model)
    y_sample = hs.reshape(n_dec, GROUP, d_model)[:, lead:]
    new_ssm_sample = ssm_s.reshape(depth, n_dec, SSD_HEADS, SSD_HEAD_DIM, D_STATE)
    return (y_prompt, y_sample, jnp.stack(convs_p), jnp.stack(ssms_p), jnp.stack(convs_s), new_ssm_sample,
            jnp.stack(vs_s))
```

```python
import functools
import math

import jax
import jax.numpy as jnp
from jax import lax
from jax.experimental import pallas as pl
from jax.experimental.pallas import tpu as pltpu

F32 = jnp.float32
BF16 = jnp.bfloat16

LANES = 128
VMEM_LIMIT_BYTES = 56 * 1024 * 1024

CONV_WIDTH = 4
SSD_HEADS = 8
SSD_HEAD_DIM = 64
SSD_GROUPS = 2
D_STATE = 128
CHUNK = 128
GMLP_HEADS = 8
N_EXPERT_GROUPS = 4
EXPERTS_PER_GROUP = 4
N_PAIRS = 6
N_BUCKETS = N_EXPERT_GROUPS * N_PAIRS
EPS = 1e-6

GROUP = 8
DEC_SEQ_ROWS = 4
TOK_TILE = 512
MOE_TILE = 512
PROMPT_TILES_PER_STEP = 8
ROUTE_ROWS = 32
TOKEN_ROWS = 8
X_PITCH = TOKEN_ROWS + 1
Y_PITCH = TOKEN_ROWS + 1
NEG_BIG = -1e30


def _dot(a, b):
    return jnp.dot(a, b, preferred_element_type=F32)


def _dot_nt(a, b):
    return lax.dot_general(a, b, (((1,), (1,)), ((), ())), preferred_element_type=F32)


def _rms(x, g):
    ms = jnp.mean(x * x, axis=-1, keepdims=True)
    return (x * lax.rsqrt(ms + EPS)) * g


def _gelu(x):
    return 0.5 * x * (1.0 + lax.erf(x * (1.0 / math.sqrt(2.0))))


def _sigmoid(x):
    return 0.5 * jnp.tanh(0.5 * x) + 0.5


def _softplus(x):
    return jnp.maximum(x, 0.0) + jnp.log1p(jnp.exp(-jnp.abs(x)))


def _params(n_grid):
    return pltpu.CompilerParams(dimension_semantics=("arbitrary",) * n_grid,
                                vmem_limit_bytes=VMEM_LIMIT_BYTES)


def _pair_specs(width, n_prompt_tiles, prompt_tile0=0, sample_tile0=0):
    return [pl.BlockSpec((TOK_TILE, width), lambda i: (prompt_tile0 + jnp.minimum(i, n_prompt_tiles - 1), 0)),
            pl.BlockSpec((TOK_TILE, width), lambda i: (sample_tile0 + jnp.maximum(i - n_prompt_tiles, 0), 0))]


def _pair_load(p_ref, s_ref, n_prompt_tiles):
    return jnp.where(pl.program_id(0) < n_prompt_tiles, p_ref[...], s_ref[...])


def _layer_spec(a, layer):
    zeros = (0,) * (a.ndim - 1)
    return pl.BlockSpec((None,) + a.shape[1:], lambda *_: (layer,) + zeros)


def _inproj_tile(h, nm_ref, wz_ref, wx_ref, wdt_ref, wu_ref, wv_ref, dtb_ref, gn_ref,
                 z_ref, xbc_ref, dt_ref, u_ref, v_ref):
    a = _rms(h, nm_ref[...]).astype(BF16)
    z_ref[...] = _dot(a, wz_ref[...])
    xbc_ref[...] = _dot(a, wx_ref[...])
    dt_ref[...] = _softplus(_dot(a, wdt_ref[...]) + dtb_ref[...])
    u_ref[...] = _gelu(_dot(a, wu_ref[...]))
    v_ref[...] = _rms(_gelu(_dot(a, wv_ref[...])), gn_ref[...])


def _inproj_kernel(hp_ref, hs_ref, *refs, n_prompt_tiles):
    _inproj_tile(_pair_load(hp_ref, hs_ref, n_prompt_tiles), *refs)


N_INPROJ_PARAMS = 8


def _inproj_specs(weights, t, layer):
    assert len(weights) == N_INPROJ_PARAMS
    widths = [w.shape[-1] for w in weights[1:6]]
    row = lambda w: pl.BlockSpec((TOK_TILE, w), lambda i: (i, 0))
    return ([_layer_spec(a, layer) for a in weights], [row(w) for w in widths],
            [jax.ShapeDtypeStruct((t, w), F32) for w in widths])


def _inproj(hp, hs, weights, layer):
    t = hp.shape[0] + hs.shape[0]
    n_prompt_tiles = hp.shape[0] // TOK_TILE
    w_specs, out_specs, out_shape = _inproj_specs(weights, t, layer)
    return pl.pallas_call(
        functools.partial(_inproj_kernel, n_prompt_tiles=n_prompt_tiles),
        grid=(t // TOK_TILE,),
        in_specs=_pair_specs(hp.shape[1], n_prompt_tiles) + w_specs,
        out_specs=out_specs,
        out_shape=out_shape,
        compiler_params=_params(1),
        name="inproj",
    )(hp, hs, *weights)


def _seg_cumsum(x, seg, rowmod):
    d = 1
    while d < seg:
        x = x + jnp.where(rowmod >= d, pltpu.roll(x, d, axis=0), 0.0)
        d *= 2
    return x


def _seg_rev_cumsum(x, seg, rowmod):
    n = x.shape[0]
    d = 1
    while d < seg:
        x = x + jnp.where(rowmod + d < seg, pltpu.roll(x, n - d, axis=0), 0.0)
        d *= 2
    return x


def _expand_heads(m, lane_lt_half):
    parts = []
    for j in range(SSD_HEADS // 2):
        parts.append(jnp.where(lane_lt_half, m[:, 2 * j:2 * j + 1], m[:, 2 * j + 1:2 * j + 2]))
    return jnp.concatenate(parts, axis=1)


def _merge_head_pairs(per_head, lane_lt_half):
    parts = [jnp.where(lane_lt_half, per_head[2 * j], per_head[2 * j + 1]) for j in range(len(per_head) // 2)]
    return jnp.concatenate(parts, axis=1)


def _mixer_kernel(*refs, sample, n_inner, n_prev=0):
    n_tok = 6 if sample else 5
    tok_refs, rest = refs[:n_tok], refs[n_tok:]
    if sample:
        if n_prev:
            prev_ref, rest = rest[0], rest[1:]
        (s0_ref, cw_ref, cb_ref, arow_ref, dsk_ref, sn_ref, wsp_ref, bsp_ref,
         y_ref, stack_ref, ext_ref, yoff_ref) = rest
        if n_prev:
            stack_ref[0:n_prev] = prev_ref[...]
        sout_ref = stack_ref.at[n_prev]
        first = pl.program_id(0) == 0
    else:
        (cw_ref, cb_ref, arow_ref, dsk_ref, sn_ref, wsp_ref, bsp_ref, y_ref, sout_ref, ext_ref) = rest
        s0_ref = yoff_ref = None
        first = pl.program_id(1) == 0

    @pl.when(first)
    def _():
        ext_ref[...] = jnp.zeros(ext_ref.shape, F32)
        if not sample:
            sout_ref[...] = jnp.zeros(sout_ref.shape, F32)

    def tile(c, carry):
        r0 = pl.multiple_of(c * CHUNK, CHUNK)
        views = [r.at[pl.ds(r0, CHUNK)] for r in tok_refs + (y_ref,)]
        _mixer_tile(*views[:n_tok], s0_ref, cw_ref, cb_ref, arow_ref, dsk_ref, sn_ref, wsp_ref, bsp_ref,
                    views[n_tok], sout_ref, ext_ref, yoff_ref, sample=sample)
        return carry

    if n_inner == 1:
        tile(0, 0)
    else:
        lax.fori_loop(0, n_inner, tile, 0)


def _mixer_tile(z_ref, xbc_ref, dt_ref, u_ref, v_ref, *rest, sample):
    if sample:
        cs_ref, s0_ref = rest[0], rest[1]
        rest = rest[2:]
    else:
        rest = rest[1:]
    cw_ref, cb_ref, arow_ref, dsk_ref, sn_ref, wsp_ref, bsp_ref, y_ref, sout_ref, ext_ref, yoff_ref = rest
    seg = GROUP if sample else CHUNK
    d_ssd = SSD_HEADS * SSD_HEAD_DIM
    gw = d_ssd // SSD_GROUPS
    hpg = SSD_HEADS // SSD_GROUPS
    cs_first = GROUP - DEC_SEQ_ROWS - (CONV_WIDTH - 1)

    rows = lax.broadcasted_iota(jnp.int32, (CHUNK, LANES), 0)
    cols = lax.broadcasted_iota(jnp.int32, (CHUNK, LANES), 1)
    rowmod = rows & (seg - 1)
    lane_lt_half = cols < SSD_HEAD_DIM

    xbc = xbc_ref[...]
    if sample:
        rm = lax.broadcasted_iota(jnp.int32, xbc.shape, 0) & (GROUP - 1)
        xbc = jnp.where((rm >= cs_first) & (rm < cs_first + CONV_WIDTH - 1), cs_ref[...], xbc)

    tail = ext_ref[...]
    row8 = lax.broadcasted_iota(jnp.int32, tail.shape, 0)
    acc = cb_ref[...] + cw_ref[CONV_WIDTH - 1:CONV_WIDTH, :] * xbc
    for j in range(1, CONV_WIDTH):
        rolled = pltpu.roll(xbc, j, axis=0)
        head = jnp.where(row8 < j, pltpu.roll(tail, j, axis=0), rolled[0:8, :])
        shifted = jnp.concatenate([head, rolled[8:, :]], axis=0)
        acc = acc + cw_ref[CONV_WIDTH - 1 - j:CONV_WIDTH - j, :] * shifted
    if not sample:
        ext_ref[...] = xbc[CHUNK - 8:, :]
    xc = acc * _sigmoid(acc)
    x = xc[:, :d_ssd]
    bb = xc[:, d_ssd:d_ssd + SSD_GROUPS * D_STATE].astype(BF16)
    cm = xc[:, d_ssd + SSD_GROUPS * D_STATE:]
    cbf = cm.astype(BF16)

    dtc = dt_ref[...]
    if sample:
        dtc = jnp.where(rowmod >= GROUP - DEC_SEQ_ROWS, dtc, 0.0)
    da = dtc * arow_ref[...]
    cum = _seg_cumsum(da, seg, rowmod)
    rev = _seg_rev_cumsum(da, seg, rowmod) - da
    cum_t = cum.T
    ecum = jnp.exp(cum)
    dt_e = _expand_heads(dtc, lane_lt_half)
    ecum_e = _expand_heads(ecum, lane_lt_half)
    erev_e = _expand_heads(jnp.exp(rev), lane_lt_half)

    xdt = x * dt_e
    xdt_bf = xdt.astype(BF16)
    causal = rows >= cols
    if sample:
        same_seq = (rows >> 3) == (cols >> 3)
        causal = causal & same_seq

    yd = []
    for g in range(SSD_GROUPS):
        cb_g = _dot_nt(cbf[:, g * D_STATE:(g + 1) * D_STATE], bb[:, g * D_STATE:(g + 1) * D_STATE])
        for hh in range(hpg):
            h = g * hpg + hh
            expo = cum[:, h:h + 1] - cum_t[h:h + 1, :]
            w = (cb_g * jnp.exp(jnp.where(causal, expo, NEG_BIG))).astype(BF16)
            j = h // 2
            yd.append(_dot(w, xdt_bf[:, j * LANES:(j + 1) * LANES]))
    y_diag = _merge_head_pairs(yd, lane_lt_half)

    if sample:
        for i in range(CHUNK // GROUP):
            for g in range(SSD_GROUPS):
                s_g = s0_ref[i, g * gw:(g + 1) * gw, :].astype(BF16)
                yoff_ref[i * GROUP:(i + 1) * GROUP, g * gw:(g + 1) * gw] = _dot_nt(
                    cm[i * GROUP:(i + 1) * GROUP, g * D_STATE:(g + 1) * D_STATE].astype(BF16), s_g)
        y_off = yoff_ref[...]
    else:
        y_off = jnp.concatenate(
            [_dot_nt(cbf[:, g * D_STATE:(g + 1) * D_STATE], sout_ref[0, g * gw:(g + 1) * gw, :].astype(BF16))
             for g in range(SSD_GROUPS)], axis=1)
    y = y_diag + y_off * ecum_e + dsk_ref[...] * x

    xd = xdt * erev_e
    for g in range(SSD_GROUPS):
        xd_t = xd[:, g * gw:(g + 1) * gw].T
        b_g = bb[:, g * D_STATE:(g + 1) * D_STATE]
        if sample:
            tcols = lax.broadcasted_iota(jnp.int32, xd_t.shape, 1) >> 3
            for i in range(CHUNK // GROUP):
                upd = _dot(jnp.where(tcols == i, xd_t, 0.0).astype(BF16), b_g)
                last = i * GROUP + GROUP - 1
                for hh in range(hpg):
                    h = g * hpg + hh
                    r0 = h * SSD_HEAD_DIM
                    sout_ref[i, r0:r0 + SSD_HEAD_DIM, :] = (
                        s0_ref[i, r0:r0 + SSD_HEAD_DIM, :] * ecum[last:last + 1, h:h + 1]
                        + upd[hh * SSD_HEAD_DIM:(hh + 1) * SSD_HEAD_DIM, :])
        else:
            upd = _dot(xd_t.astype(BF16), b_g)
            for hh in range(hpg):
                h = g * hpg + hh
                r0 = h * SSD_HEAD_DIM
                sout_ref[0, r0:r0 + SSD_HEAD_DIM, :] = (
                    sout_ref[0, r0:r0 + SSD_HEAD_DIM, :] * ecum[CHUNK - 1:CHUNK, h:h + 1]
                    + upd[hh * SSD_HEAD_DIM:(hh + 1) * SSD_HEAD_DIM, :])

    zf = z_ref[...]
    yf = y * (zf * _sigmoid(zf))
    parts = []
    for g in range(SSD_GROUPS):
        part = yf[:, g * gw:(g + 1) * gw]
        ms = jnp.mean(part * part, axis=-1, keepdims=True)
        parts.append(part * lax.rsqrt(ms + EPS))
    y_ssd = jnp.concatenate(parts, axis=1) * sn_ref[...]

    vb = v_ref[...].astype(BF16)
    if sample:
        n_grp = CHUNK // GROUP
        wms = [jnp.where(same_seq, jnp.tile(wsp_ref[h], (n_grp, 1)), 0.0).astype(BF16) for h in range(GMLP_HEADS)]
    else:
        wms = [wsp_ref[h] for h in range(GMLP_HEADS)]
    sg = [_dot(wms[h], vb[:, (h // 2) * LANES:(h // 2 + 1) * LANES]) for h in range(GMLP_HEADS)]
    s = _merge_head_pairs(sg, lane_lt_half) + bsp_ref[...]
    y_gm = u_ref[...] * s

    y_ref[:, :d_ssd] = y_ssd.astype(BF16)
    y_ref[:, d_ssd:] = y_gm.astype(BF16)


def _mixer_prompt(z, xbc, dt, u, v, params, n_seq, n_chunk, layer):
    n_inner = math.gcd(n_chunk, PROMPT_TILES_PER_STEP)
    n_outer = n_chunk // n_inner
    tok = lambda w: pl.BlockSpec((n_inner * CHUNK, w), lambda b, c: (b * n_outer + c, 0))
    d_ssd = z.shape[1]
    d_mix = d_ssd + u.shape[1]
    return pl.pallas_call(
        functools.partial(_mixer_kernel, sample=False, n_inner=n_inner),
        grid=(n_seq, n_outer),
        in_specs=[tok(z.shape[1]), tok(xbc.shape[1]), tok(dt.shape[1]), tok(u.shape[1]), tok(v.shape[1])]
        + [_layer_spec(a, layer) for a in params],
        out_specs=[tok(d_mix),
                   pl.BlockSpec((1, d_ssd, D_STATE), lambda b, c: (b, 0, 0))],
        out_shape=[jax.ShapeDtypeStruct((n_seq * n_chunk * CHUNK, d_mix), BF16),
                   jax.ShapeDtypeStruct((n_seq, d_ssd, D_STATE), F32)],
        scratch_shapes=[pltpu.VMEM((8, xbc.shape[1]), F32)],
        compiler_params=_params(2),
        name="mixer_prompt",
    )(z, xbc, dt, u, v, *params)


def _mixer_sample(z, xbc, dt, u, v, cs_all, s0_all, prev, params, row0, n_rows, layer):
    blk0 = row0 // CHUNK
    n_seq_blk = CHUNK // GROUP
    n_blk = n_rows // CHUNK
    tok = lambda w: pl.BlockSpec((CHUNK, w), lambda i: (blk0 + i, 0))
    d_ssd = z.shape[1]
    d_mix = d_ssd + u.shape[1]
    st = (n_seq_blk, d_ssd, D_STATE)
    stack = lambda n: pl.BlockSpec((n,) + st, lambda i: (0, i, 0, 0))
    prev_args = [] if layer == 0 else [prev]
    return pl.pallas_call(
        functools.partial(_mixer_kernel, sample=True, n_inner=1, n_prev=layer),
        grid=(n_blk,),
        in_specs=[tok(z.shape[1]), tok(xbc.shape[1]), tok(dt.shape[1]), tok(u.shape[1]), tok(v.shape[1]),
                  pl.BlockSpec((CHUNK, cs_all.shape[1]), lambda i: (layer * n_blk + i, 0))]
        + [stack(layer) for _ in prev_args]
        + [pl.BlockSpec(st, lambda i: (layer * n_blk + i, 0, 0))]
        + [_layer_spec(a, layer) for a in params],
        out_specs=[pl.BlockSpec((CHUNK, d_mix), lambda i: (i, 0)), stack(layer + 1)],
        out_shape=[jax.ShapeDtypeStruct((n_rows, d_mix), BF16),
                   jax.ShapeDtypeStruct((layer + 1, n_blk * n_seq_blk, d_ssd, D_STATE), F32)],
        scratch_shapes=[pltpu.VMEM((8, xbc.shape[1]), F32), pltpu.VMEM((CHUNK, d_ssd), F32)],
        compiler_params=_params(1),
        name="mixer_sample",
    )(z, xbc, dt, u, v, cs_all, *prev_args, s0_all, *params)


def _first_argmax(vals):
    m = vals[0]
    for v in vals[1:]:
        m = jnp.maximum(m, v)
    idx = jnp.full(m.shape, len(vals) - 1, jnp.int32)
    for k in range(len(vals) - 2, -1, -1):
        idx = jnp.where(vals[k] >= m, k, idx)
    return m, idx


def _slab_store(slab_ref, x, pitch, tok0=0):
    for k in range(x.shape[1] // LANES):
        slab_ref[pl.ds(tok0 * pitch + k, x.shape[0], stride=pitch), :] = x[:, k * LANES:(k + 1) * LANES]


def _slab_load(slab_ref, rows, n_pieces, pitch, tok0=0):
    return jnp.concatenate([slab_ref[pl.ds(tok0 * pitch + k, rows, stride=pitch), :] for k in range(n_pieces)],
                           axis=1)


def _post_kernel(yp_ref, ys_ref, hp_ref, hs_ref, wo_ref, nf_ref, wr_ref, br_ref,
                 slab_ref, bkt_ref, rnk_ref, cnt_ref, carry_ref, earlier_ref, *, n_prompt_tiles):
    i = pl.program_id(0)
    tm, d = hp_ref.shape

    @pl.when(i == 0)
    def _():
        carry_ref[...] = jnp.zeros(carry_ref.shape, F32)
        tr = lax.broadcasted_iota(jnp.int32, (tm, tm), 0)
        tc = lax.broadcasted_iota(jnp.int32, (tm, tm), 1)
        earlier_ref[...] = jnp.where(tr < tc, 1.0, 0.0).astype(BF16)

    ym = _pair_load(yp_ref, ys_ref, n_prompt_tiles)
    h1 = _pair_load(hp_ref, hs_ref, n_prompt_tiles) + _dot(ym, wo_ref[...])
    _slab_store(slab_ref, h1, X_PITCH)

    t = _rms(h1, nf_ref[...])
    wr = wr_ref[...]
    wr_hi = wr.astype(BF16)
    wr_lo = (wr - wr_hi.astype(F32)).astype(BF16)
    t_hi = t.astype(BF16)
    t_lo = (t - t_hi.astype(F32)).astype(BF16)
    by_hi = _dot_nt(jnp.concatenate([wr_hi, wr_lo], axis=0), t_hi)
    logits = (by_hi[:ROUTE_ROWS] + (by_hi[ROUTE_ROWS:] + _dot_nt(wr_hi, t_lo))) + br_ref[...]
    lg = [logits[k:k + 1, :] for k in range(N_EXPERT_GROUPS)]
    m, g = _first_argmax(lg)
    ssum = jnp.exp(lg[0] - m)
    for k in range(1, N_EXPERT_GROUPS):
        ssum = ssum + jnp.exp(lg[k] - m)
    p_sel = 1.0 / ssum
    le = [logits[N_EXPERT_GROUPS + e:N_EXPERT_GROUPS + e + 1, :] for e in range(N_EXPERT_GROUPS * EXPERTS_PER_GROUP)]
    a = []
    for k in range(EXPERTS_PER_GROUP):
        sel = le[(N_EXPERT_GROUPS - 1) * EXPERTS_PER_GROUP + k]
        for gi in range(N_EXPERT_GROUPS - 2, -1, -1):
            sel = jnp.where(g == gi, le[gi * EXPERTS_PER_GROUP + k], sel)
        a.append(sel)
    v1, i1 = _first_argmax(a)
    a2 = [jnp.where(i1 == k, -jnp.inf, a[k]) for k in range(EXPERTS_PER_GROUP)]
    v2, i2 = _first_argmax(a2)
    e2 = jnp.exp(v2 - v1)
    den = 1.0 + e2
    g1 = (1.0 / den) * p_sel
    g2 = (e2 / den) * p_sel
    lo = jnp.minimum(i1, i2)
    hi = jnp.maximum(i1, i2)
    pair = jnp.where(lo == 0, jnp.where(hi == 1, 0, hi), jnp.where(lo == 1, jnp.where(hi == 2, 1, 4), 5))
    first = jnp.where(pair == 0, 0, jnp.where(pair <= 2, 2, 3))
    c_first = jnp.where(first == i1, g1, g2)
    c_second = jnp.where(first == i1, g2, g1)
    bucket = g * N_PAIRS + pair

    brow = lax.broadcasted_iota(jnp.int32, (ROUTE_ROWS, tm), 0)
    onehot = jnp.where(brow == bucket, 1.0, 0.0)
    prefix = _dot(onehot.astype(BF16), earlier_ref[...])
    carry = carry_ref[:, 0:1]
    rank = jnp.sum(onehot * (prefix + carry), axis=0, keepdims=True)
    carry = carry + jnp.sum(onehot, axis=1, keepdims=True)
    carry_b = jnp.broadcast_to(carry, carry_ref.shape)
    carry_ref[...] = carry_b
    cnt_ref[...] = carry_b

    bkt_ref[0] = bucket
    rnk_ref[0] = rank.astype(jnp.int32)
    ar = lax.broadcasted_iota(jnp.int32, (LANES, tm), 0)
    aux = jnp.where(ar == 0, c_first, jnp.where(ar == 1, c_second, 0.0))
    slab_ref[pl.ds(d // LANES, tm, stride=X_PITCH), :] = aux.T


def _post(yp, ys, hp, hs, params, layer):
    d = hp.shape[1]
    t = hp.shape[0] + hs.shape[0]
    n_prompt_tiles = hp.shape[0] // TOK_TILE
    return pl.pallas_call(
        functools.partial(_post_kernel, n_prompt_tiles=n_prompt_tiles),
        grid=(t // TOK_TILE,),
        in_specs=_pair_specs(d, n_prompt_tiles) + _pair_specs(d, n_prompt_tiles)
        + [_layer_spec(a, layer) for a in params],
        out_specs=[pl.BlockSpec((TOK_TILE * X_PITCH, LANES), lambda i: (i, 0)),
                   pl.BlockSpec((1, 1, TOK_TILE), lambda i: (i, 0, 0)),
                   pl.BlockSpec((1, 1, TOK_TILE), lambda i: (i, 0, 0)),
                   pl.BlockSpec((ROUTE_ROWS, LANES), lambda i: (0, 0))],
        out_shape=[jax.ShapeDtypeStruct((t * X_PITCH, LANES), F32),
                   jax.ShapeDtypeStruct((t // TOK_TILE, 1, TOK_TILE), jnp.int32),
                   jax.ShapeDtypeStruct((t // TOK_TILE, 1, TOK_TILE), jnp.int32),
                   jax.ShapeDtypeStruct((ROUTE_ROWS, LANES), F32)],
        scratch_shapes=[pltpu.VMEM((ROUTE_ROWS, LANES), F32), pltpu.VMEM((TOK_TILE, TOK_TILE), BF16)],
        compiler_params=_params(1),
        name="post",
    )(yp, ys, hp, hs, *params)


def _zero_unused_slots(fill_ref, dst_ref, zero_ref, zero_sem, pitch):
    tile_rows = zero_ref.shape[0]
    n_tiles = dst_ref.shape[0] // tile_rows
    n_used = fill_ref[2 * N_BUCKETS]
    pieces = [1 << k for k in range((tile_rows // pitch).bit_length() - 2, -1, -1)]
    zero_ref[...] = jnp.zeros(zero_ref.shape, zero_ref.dtype)

    def piece(first_slot, n_slots):
        return pltpu.make_async_copy(zero_ref.at[pl.ds(0, n_slots * pitch)],
                                     dst_ref.at[pl.ds(first_slot * pitch, n_slots * pitch)], zero_sem)

    def per_piece(act):
        for b in range(N_BUCKETS):
            first, n = fill_ref[b], fill_ref[N_BUCKETS + b]
            for p in pieces:
                @pl.when((n & p) != 0)
                def _(first=first, p=p):
                    act(piece(first, p))
                first = first + (n & p)

        def tile(k, carry):
            act(piece(k * (tile_rows // pitch), tile_rows // pitch))
            return carry

        lax.fori_loop(n_used, n_tiles, tile, 0)

    per_piece(lambda c: c.start())
    per_piece(lambda c: c.wait())


def _scatter_kernel(starts_ref, fill_ref, bkt_ref, rnk_ref, src_ref, *refs, pitch, reuse):
    if reuse:
        _, dst_ref, pos_ref, sem = refs
    else:
        dst_ref, pos_ref, sem, zero_ref, zero_sem = refs

        @pl.when(pl.program_id(0) == 0)
        def _():
            _zero_unused_slots(fill_ref, dst_ref, zero_ref, zero_sem, pitch)

    n_tok = src_ref.shape[0] // pitch
    for r in range(n_tok):
        slot = starts_ref[bkt_ref[0, 0, r]] + rnk_ref[0, 0, r]
        pos_ref[0, 0, r] = slot
        pltpu.make_async_copy(src_ref.at[pl.ds(r * pitch, pitch)],
                              dst_ref.at[pl.ds(slot * pitch, pitch)], sem).start(priority=r % 2)
    pltpu.make_async_copy(src_ref, dst_ref.at[pl.ds(0, n_tok * pitch)], sem).wait()


def _scatter_tokens(starts, fill, bkt, rnk, src, n_out, pitch, reuse=None):
    hbm = pl.BlockSpec(memory_space=pl.ANY)
    smem = pl.BlockSpec(memory_space=pltpu.SMEM)
    tile_smem = pl.BlockSpec((1, 1, TOK_TILE), lambda i: (i, 0, 0), memory_space=pltpu.SMEM)
    in_specs = [smem, smem, tile_smem, tile_smem, pl.BlockSpec((TOK_TILE * pitch, LANES), lambda i: (i, 0))]
    args = [starts, fill, bkt, rnk, src]
    scratch = [pltpu.SemaphoreType.DMA(())]
    aliases = {}
    if reuse is not None:
        in_specs.append(hbm)
        args.append(reuse)
        aliases = {len(args) - 1: 0}
    else:
        scratch += [pltpu.VMEM((MOE_TILE * pitch, LANES), src.dtype), pltpu.SemaphoreType.DMA(())]
    return pl.pallas_call(
        functools.partial(_scatter_kernel, pitch=pitch, reuse=reuse is not None),
        grid=(bkt.shape[0],),
        in_specs=in_specs,
        out_specs=[hbm, tile_smem],
        out_shape=[jax.ShapeDtypeStruct((n_out * pitch, LANES), src.dtype),
                   jax.ShapeDtypeStruct(bkt.shape, jnp.int32)],
        scratch_shapes=scratch,
        input_output_aliases=aliases,
        compiler_params=_params(1),
        name="scatter_tokens",
    )(*args)


def _moe_kernel(first_ref, second_ref, nused_ref, new_first_ref, new_second_ref, x_ref, nf_ref, *refs):
    f32_first, f32_second, o_ref, bf_first, bf_second = refs[0:3], refs[3:6], refs[6], refs[7:10], refs[10:13]
    i = pl.program_id(0)
    n_pieces = nf_ref.shape[1] // LANES

    @pl.when(i < nused_ref[0])
    def _():
        for new_ref, srcs, dsts in ((new_first_ref, f32_first, bf_first), (new_second_ref, f32_second, bf_second)):
            @pl.when(new_ref[i] == 1)
            def _():
                for src, dst in zip(srcs, dsts):
                    dst[...] = src[...].astype(BF16)

        x = _slab_load(x_ref, MOE_TILE, n_pieces, X_PITCH)
        gates = x_ref[pl.ds(n_pieces, MOE_TILE, stride=X_PITCH), :]
        c_first = gates[:, 0:1]
        c_second = gates[:, 1:2]
        t = _rms(x, nf_ref[...]).astype(BF16)

        def expert(wg, wu, wd):
            gate = _dot(t, wg[...])
            hid = (gate * _sigmoid(gate)) * _dot(t, wu[...])
            return _dot(hid.astype(BF16), wd[...])

        y = c_first * expert(*bf_first)
        y = y + c_second * expert(*bf_second)
        _slab_store(o_ref, x + y, Y_PITCH)
        o_ref[pl.ds(TOKEN_ROWS, MOE_TILE, stride=Y_PITCH), :] = jnp.zeros((MOE_TILE, LANES), F32)

    @pl.when(i >= nused_ref[0])
    def _():
        o_ref[...] = jnp.zeros(o_ref.shape, F32)


def _moe(tables, xs, nf, wg, wu, wd, layer):
    n_tiles = tables[0].shape[0]
    d, de = wg.shape[1:]
    first = lambda i, first, second, *_: (first[i], 0, 0)
    second = lambda i, first, second, *_: (second[i], 0, 0)
    wspec = lambda shape, index_map: pl.BlockSpec((None,) + shape, index_map)
    grid_spec = pltpu.PrefetchScalarGridSpec(
        num_scalar_prefetch=len(tables),
        grid=(n_tiles,),
        in_specs=[pl.BlockSpec((MOE_TILE * X_PITCH, LANES),
                               lambda i, first, second, n_used, *_: (jnp.maximum(jnp.minimum(i, n_used[0] - 1), 0), 0)),
                  _layer_spec(nf, layer),
                  wspec((d, de), first), wspec((d, de), first), wspec((de, d), first),
                  wspec((d, de), second), wspec((d, de), second), wspec((de, d), second)],
        out_specs=pl.BlockSpec((MOE_TILE * Y_PITCH, LANES), lambda i, *_: (i, 0)),
        scratch_shapes=[pltpu.VMEM(s, BF16) for s in ((d, de), (d, de), (de, d)) * 2],
    )
    return pl.pallas_call(
        _moe_kernel,
        grid_spec=grid_spec,
        out_shape=jax.ShapeDtypeStruct((n_tiles * MOE_TILE * Y_PITCH, LANES), F32),
        compiler_params=_params(1),
        name="expert_pairs",
    )(*tables, xs, nf, wg, wu, wd, wg, wu, wd)


def _ple_kernel(pos_ref, pos_next_ref, h_hbm, pp_ref, ps_ref, wple_ref, nple_ref, npg_ref, wpg_ref, *refs,
                final, n_prompt_tiles):
    refs, (xbuf, sem) = refs[:-2], refs[-2:]
    i = pl.program_id(0)
    last = pl.num_programs(0) - 1
    slot = i % 2

    def start_tile(p_ref, s):
        for r in range(TOK_TILE):
            pltpu.make_async_copy(h_hbm.at[pl.ds(p_ref[0, 0, r] * Y_PITCH, Y_PITCH)],
                                  xbuf.at[s, pl.ds(r * Y_PITCH, Y_PITCH)], sem.at[s]).start()

    def wait_tile(s):
        pltpu.make_async_copy(h_hbm.at[pl.ds(0, TOK_TILE * Y_PITCH)], xbuf.at[s], sem.at[s]).wait()

    @pl.when(i == 0)
    def _():
        start_tile(pos_ref, 0)

    wait_tile(slot)
    start_tile(pos_next_ref, 1 - slot)
    h2 = _slab_load(xbuf.at[slot], TOK_TILE, TOKEN_ROWS, Y_PITCH)
    p = _pair_load(pp_ref, ps_ref, n_prompt_tiles)
    e = _rms(_dot(p.astype(BF16), wple_ref[...]), nple_ref[...])
    gate = _sigmoid(_dot(_rms(h2, npg_ref[...]).astype(BF16), wpg_ref[...]))
    h3 = h2 + gate * e
    if final:
        nfin_ref, op_ref, os_ref = refs
        out = _rms(h3, nfin_ref[...])
    else:
        op_ref, os_ref = refs[N_INPROJ_PARAMS:N_INPROJ_PARAMS + 2]
        out = h3
        _inproj_tile(h3, *refs[:N_INPROJ_PARAMS], *refs[N_INPROJ_PARAMS + 2:])

    @pl.when(i < n_prompt_tiles)
    def _():
        op_ref[...] = out

    @pl.when(i >= n_prompt_tiles)
    def _():
        os_ref[...] = out

    @pl.when(i == last)
    def _():
        wait_tile(1 - slot)


def _ple(pos, h2s, pp_all, ps_all, params, tail, final, layer, n_prompt):
    n_steps = pos.shape[0]
    t = n_steps * TOK_TILE
    d = TOKEN_ROWS * LANES
    tile_smem = pl.BlockSpec((1, 1, TOK_TILE), lambda i: (i, 0, 0), memory_space=pltpu.SMEM)
    next_smem = pl.BlockSpec((1, 1, TOK_TILE), lambda i: (jnp.minimum(i + 1, n_steps - 1), 0, 0),
                             memory_space=pltpu.SMEM)
    n_prompt_tiles = n_prompt // TOK_TILE
    n_sample_tiles = n_steps - n_prompt_tiles
    out_specs = _pair_specs(d, n_prompt_tiles)
    out_shape = [jax.ShapeDtypeStruct((n_prompt, d), F32), jax.ShapeDtypeStruct((t - n_prompt, d), F32)]
    if final:
        tail = (tail,)
        tail_specs = [pl.BlockSpec(tail[0].shape, lambda i: (0,) * tail[0].ndim)]
    else:
        tail_specs, io, ish = _inproj_specs(tail, t, layer + 1)
        out_specs, out_shape = out_specs + io, out_shape + ish
    return pl.pallas_call(
        functools.partial(_ple_kernel, final=final, n_prompt_tiles=n_prompt_tiles),
        grid=(n_steps,),
        in_specs=[tile_smem, next_smem, pl.BlockSpec(memory_space=pl.ANY)]
        + _pair_specs(pp_all.shape[1], n_prompt_tiles, layer * n_prompt_tiles, layer * n_sample_tiles)
        + [_layer_spec(a, layer) for a in params] + tail_specs,
        out_specs=out_specs,
        out_shape=out_shape,
        scratch_shapes=[pltpu.VMEM((2, TOK_TILE * Y_PITCH, LANES), F32), pltpu.SemaphoreType.DMA((2,))],
        compiler_params=_params(1),
        name="ple_final" if final else "ple_inproj",
    )(pos, pos, h2s, pp_all, ps_all, *params, *tail)


_PAIR_FIRST = (0, 2, 2, 3, 3, 3)
_PAIR_SECOND = (1, 1, 0, 0, 1, 2)


def _routing_tables(cnt, n_tiles, expert0):
    counts = cnt[:N_BUCKETS, 0].astype(jnp.int32)
    padded = ((counts + MOE_TILE - 1) // MOE_TILE) * MOE_TILE
    ends = jnp.cumsum(padded)
    starts = jnp.pad(ends - padded, (0, ROUTE_ROWS - N_BUCKETS))
    n_used = ends[-1] // MOE_TILE
    fill = jnp.pad(jnp.concatenate([ends - padded + counts, padded - counts, n_used.reshape(1)]),
                   (0, 2 * ROUTE_ROWS - 2 * N_BUCKETS - 1))
    tile = jnp.minimum(jnp.arange(n_tiles, dtype=jnp.int32), n_used - 1)
    tb = jnp.sum((ends[None, :] <= (tile * MOE_TILE)[:, None]).astype(jnp.int32), axis=1)
    tb = jnp.minimum(tb, N_BUCKETS - 1)
    grp = tb // N_PAIRS
    pr = tb % N_PAIRS
    e_first = expert0 + grp * EXPERTS_PER_GROUP + jnp.asarray(_PAIR_FIRST, jnp.int32)[pr]
    e_second = expert0 + grp * EXPERTS_PER_GROUP + jnp.asarray(_PAIR_SECOND, jnp.int32)[pr]
    is_new = lambda e: jnp.concatenate([jnp.ones((1,), jnp.int32), (e[1:] != e[:-1]).astype(jnp.int32)])
    return starts, fill, (e_first, e_second, n_used.reshape(1), is_new(e_first), is_new(e_second))


def kernel(x_prompt, x_sample, state_conv, state_ssm, p_prompt, p_sample, norm_mix, w_in, conv_w, conv_b,
           dt_bias, a_log, d_skip, ssd_norm, gmlp_norm, w_spatial, b_spatial, w_out, norm_ffn,
           w_router_group, b_router_group, w_router_expert, b_router_expert, w_gate, w_up, w_down,
           w_ple, norm_ple, norm_pg, w_pg, norm_final):
    n_seq, seq_len, d_model = x_prompt.shape
    n_dec, dec_seq, _ = x_sample.shape
    depth = w_in.shape[0]
    conv_dim = conv_w.shape[2]
    d_ssd = SSD_HEADS * SSD_HEAD_DIM
    d_gmlp = gmlp_norm.shape[1]
    assert dec_seq == DEC_SEQ_ROWS and conv_w.shape[1] == CONV_WIDTH and seq_len % CHUNK == 0
    assert conv_dim == d_ssd + 2 * SSD_GROUPS * D_STATE and w_spatial.shape[1:] == (GMLP_HEADS, CHUNK, CHUNK)
    n_chunk = seq_len // CHUNK
    n_prompt = n_seq * seq_len
    n_srows = n_dec * GROUP
    t_all = n_prompt + n_srows
    assert n_prompt % TOK_TILE == 0 and n_srows % TOK_TILE == 0
    assert d_model == TOKEN_ROWS * LANES
    lead = GROUP - dec_seq
    n_moe_tiles = t_all // MOE_TILE + N_BUCKETS

    def sample_rows(a):
        pad = [(0, 0)] * (a.ndim - 2) + [(lead, 0), (0, 0)]
        return jnp.pad(a, pad).reshape(-1, a.shape[-1])

    hp, hs = x_prompt.reshape(n_prompt, d_model), sample_rows(x_sample)

    vec = lambda a: a.astype(F32).reshape(depth, 1, -1)
    lane_pad = ((0, 0), (0, LANES - SSD_HEADS))
    o_dt = d_ssd + conv_dim
    o_uv = o_dt + SSD_HEADS
    w_cols = lambda a, b: w_in[..., a:b].astype(BF16)
    inproj_params = (vec(norm_mix), w_cols(0, d_ssd), w_cols(d_ssd, o_dt),
                     jnp.pad(w_in[..., o_dt:o_uv], ((0, 0),) + lane_pad).astype(BF16),
                     w_cols(o_uv, o_uv + d_gmlp), w_cols(o_uv + d_gmlp, o_uv + 2 * d_gmlp),
                     jnp.pad(dt_bias.astype(F32), lane_pad).reshape(depth, 1, LANES), vec(gmlp_norm))

    head_cols = jnp.arange(d_ssd) // SSD_HEAD_DIM
    per_head = d_gmlp // GMLP_HEADS
    ws_tril = jnp.where(jnp.tril(jnp.ones((CHUNK, CHUNK), bool)), w_spatial, 0.0)
    bsp_p = jnp.repeat(jnp.swapaxes(b_spatial, 1, 2), per_head, axis=2)
    w8 = jnp.pad(ws_tril[:, :, :dec_seq, :dec_seq], ((0, 0), (0, 0), (lead, 0), (lead, 0)))
    wsp_s = jnp.tile(w8, (1, 1, 1, CHUNK // GROUP))
    b8 =jnp.pad(b_spatial[:, :, :dec_seq], ((0, 0), (0, 0), (lead, 0)))
    bsp_s = jnp.repeat(jnp.swapaxes(jnp.tile(b8, (1, 1, CHUNK // GROUP)), 1, 2), per_head, axis=2)
    mixer_common = (conv_w.astype(F32), vec(conv_b),
                    jnp.pad(-jnp.exp(a_log.astype(F32)), lane_pad).reshape(depth, 1, LANES),
                    d_skip.astype(F32)[:, head_cols].reshape(depth, 1, d_ssd), vec(ssd_norm))
    mixer_p_params = mixer_common + (ws_tril.astype(BF16), bsp_p)
    mixer_s_params = mixer_common + (wsp_s.astype(F32), bsp_s)

    n_route_pad = ROUTE_ROWS - N_EXPERT_GROUPS * (1 + EXPERTS_PER_GROUP)
    wr = jnp.concatenate([jnp.swapaxes(w_router_group, 1, 2), jnp.swapaxes(w_router_expert, 1, 2),
                          jnp.zeros((depth, n_route_pad, d_model), F32)], axis=1).astype(F32)
    br = jnp.concatenate([b_router_group, b_router_expert, jnp.zeros((depth, n_route_pad), F32)],
                         axis=1).astype(F32).reshape(depth, ROUTE_ROWS, 1)
    post_params = (w_out.astype(BF16), vec(norm_ffn), wr, br)
    ple_params = (w_ple.astype(BF16), vec(norm_ple), vec(norm_pg), w_pg.astype(BF16))

    n_exp = w_gate.shape[1]
    wg_all = w_gate.astype(F32).reshape((depth * n_exp,) + w_gate.shape[2:])
    wu_all = w_up.astype(F32).reshape((depth * n_exp,) + w_up.shape[2:])
    wd_all = w_down.astype(F32).reshape((depth * n_exp,) + w_down.shape[2:])
    s0_all = state_ssm.reshape(depth * n_dec, d_ssd, D_STATE)
    tail = CONV_WIDTH - 1
    cs_all = jnp.pad(state_conv, ((0, 0), (0, 0), (lead - tail, dec_seq), (0, 0))).reshape(depth * n_srows, conv_dim)
    pp_all = p_prompt.reshape(depth * n_prompt, p_prompt.shape[-1])
    ps_all = sample_rows(p_sample)

    xs = ssm_s = None
    convs_p, ssms_p, convs_s, vs_s = [], [], [], []
    z, xbc, dt, u, v = _inproj(hp, hs, inproj_params, 0)

    for i in range(depth):
        yp, ssm_p = _mixer_prompt(z, xbc, dt, u, v, mixer_p_params, n_seq, n_chunk, i)
        ys, ssm_s = _mixer_sample(z, xbc, dt, u, v, cs_all, s0_all, ssm_s, mixer_s_params, n_prompt, n_srows, i)

        slab, bkt, rnk, cnt = _post(yp, ys, hp, hs, post_params, i)
        starts, fill, expert_tables = _routing_tables(cnt, n_moe_tiles, i * n_exp)

        xs, pos = _scatter_tokens(starts, fill, bkt, rnk, slab, n_moe_tiles * MOE_TILE, X_PITCH, reuse=xs)
        h2s = _moe(expert_tables, xs, post_params[1], wg_all, wu_all, wd_all, i)

        convs_p.append(jnp.concatenate([xbc[(b + 1) * seq_len - tail:(b + 1) * seq_len] for b in range(n_seq)])
                       .reshape(n_seq, tail, conv_dim))
        xbc_s = xbc[n_prompt:].reshape(n_dec, GROUP, conv_dim)
        convs_s.append(xbc_s[:, GROUP - tail:])
        ssms_p.append(ssm_p.reshape(n_seq, SSD_HEADS, SSD_HEAD_DIM, D_STATE))
        vs_s.append(v[n_prompt:].reshape(n_dec, GROUP, d_gmlp)[:, lead:])

        final = i == depth - 1
        outs = _ple(pos, h2s, pp_all, ps_all, ple_params,
                    norm_final.astype(F32).reshape(1, -1) if final else inproj_params, final, i, n_prompt)
        hp, hs = outs[:2]
        if not final:
            z, xbc, dt, u, v = outs[2:]

    y_prompt = hp.reshape(n_seq, seq_len, d_model)
    y_sample = hs.reshape(n_dec, GROUP, d_model)[:, lead:]
    new_ssm_sample = ssm_s.reshape(depth, n_dec, SSD_HEADS, SSD_HEAD_DIM, D_STATE)
    return (y_prompt, y_sample, jnp.stack(convs_p), jnp.stack(ssms_p), jnp.stack(convs_s), new_ssm_sample,
            jnp.stack(vs_s))
```
